```python
import math
import jax, jax.numpy as jnp
from jax import lax
import numpy as np

D_MODEL = 1024
BATCH = 8
SEQ = 8192
DEPTH = 2

CHUNK = 64
N_MEM = 256
SB_HEADS = 8
SB_HEAD_DIM = 64
SB_WIDTH = SB_HEADS * SB_HEAD_DIM
SB_BLOCK = 128
CONV_CH = 256
CONV_WIDTH = 31
SSM_CH = 256
SSM_GROUP = 16
SSM_GROUPS = SSM_CH // SSM_GROUP
SSM_STATE = 64
MIX_WIDTH = SB_WIDTH + CONV_CH + SSM_CH
IN_PROJ = 3 * SB_WIDTH + 2 * CONV_CH + SSM_CH
XA_HEADS = 4
XA_HEAD_DIM = D_MODEL // XA_HEADS
XA_WIDTH = XA_HEADS * XA_HEAD_DIM
FFN_HIDDEN = ((int(math.ceil(8 * D_MODEL / 3)) + 255) // 256) * 256
EPS = 1e-6

kernel_name = "hybrid_sb_conformer_s5_encoder"


def rms_norm(x, g):
    xf = x.astype(jnp.float32)
    y = xf * lax.rsqrt(jnp.mean(xf * xf, axis=-1, keepdims=True) + EPS)
    return (y * g.astype(jnp.float32)).astype(x.dtype)


def layer_norm(x, g, b):
    xf = x.astype(jnp.float32)
    mu = jnp.mean(xf, axis=-1, keepdims=True)
    var = jnp.mean(jnp.square(xf - mu), axis=-1, keepdims=True)
    y = (xf - mu) * lax.rsqrt(var + EPS)
    return (y * g.astype(jnp.float32) + b.astype(jnp.float32)).astype(x.dtype)


def stick_breaking_attention(q, k, v):
    bsz, L, H, hd = q.shape
    nb = L // SB_BLOCK
    kh = k.transpose(0, 2, 1, 3)
    vh = v.transpose(0, 2, 1, 3)
    q_blocks = q.transpose(0, 2, 1, 3).reshape(bsz, H, nb, SB_BLOCK, hd).transpose(2, 0, 1, 3, 4)
    key_pos = jnp.arange(L)
    scale = hd ** -0.5

    def one_block(args):
        qb, blk = args
        z = jnp.einsum('bhqd,bhkd->bhqk', qb, kh).astype(jnp.float32) * scale
        q_pos = blk * SB_BLOCK + jnp.arange(SB_BLOCK)
        mask = key_pos[None, :] < q_pos[:, None]
        log_1mb = jnp.where(mask, jax.nn.log_sigmoid(-z), 0.0)
        later = lax.cumsum(log_1mb, axis=3, reverse=True) - log_1mb
        w = jnp.where(mask, jnp.exp(jax.nn.log_sigmoid(z) + later), 0.0)
        return jnp.einsum('bhqk,bhkd->bhqd', w.astype(vh.dtype), vh)

    out = lax.map(one_block, (q_blocks, jnp.arange(nb)))
    return out.transpose(1, 0, 3, 2, 4).reshape(bsz, L, H * hd)


def conformer_conv(u2, dw_w, dw_b, ln_g, ln_b, pw2_w):
    a, b = jnp.split(u2, 2, axis=-1)
    h = a * jax.nn.sigmoid(b)
    h = lax.conv_general_dilated(
        h, dw_w[:, None, :].astype(h.dtype), window_strides=(1,),
        padding=((CONV_WIDTH - 1, 0),), dimension_numbers=('NWC', 'WIO', 'NWC'),
        feature_group_count=CONV_CH) + dw_b
    h = jax.nn.silu(layer_norm(h, ln_g, ln_b))
    return h @ pw2_w


def s5_ssm(u, lam_re, lam_im, log_dt, b_re, b_im, c_re, c_im, d, glu_w):
    bsz, L, _ = u.shape
    f32 = jnp.float32
    uf = u.astype(f32).reshape(bsz, L, SSM_GROUPS, SSM_GROUP)
    lr, li = lam_re.astype(f32), lam_im.astype(f32)
    dt = jnp.exp(log_dt.astype(f32))[:, None]
    mag = jnp.exp(lr * dt)
    ar, ai = mag * jnp.cos(li * dt), mag * jnp.sin(li * dt)
    den = lr * lr + li * li
    fr = ((ar - 1.0) * lr + ai * li) / den
    fi = (ai * lr - (ar - 1.0) * li) / den
    br, bi = b_re.astype(f32), b_im.astype(f32)
    bbr = fr[..., None] * br - fi[..., None] * bi
    bbi = fr[..., None] * bi + fi[..., None] * br
    bu_r = jnp.einsum('blgh,gph->blgp', uf, bbr)
    bu_i = jnp.einsum('blgh,gph->blgp', uf, bbi)
    shape = bu_r.shape
    a_r = jnp.broadcast_to(ar, shape)
    a_i = jnp.broadcast_to(ai, shape)

    def combine(e1, e2):
        a1r, a1i, b1r, b1i = e1
        a2r, a2i, b2r, b2i = e2
        return (a2r * a1r - a2i * a1i,
                a2r * a1i + a2i * a1r,
                a2r * b1r - a2i * b1i + b2r,
                a2r * b1i + a2i * b1r + b2i)

    _, _, xr, xi = lax.associative_scan(combine, (a_r, a_i, bu_r, bu_i), axis=1)
    y = (jnp.einsum('blgp,ghp->blgh', xr, c_re.astype(f32))
         - jnp.einsum('blgp,ghp->blgh', xi, c_im.astype(f32)))
    y = y.reshape(bsz, L, SSM_CH) + d.astype(f32) * uf.reshape(bsz, L, SSM_CH)
    y = y.astype(u.dtype)
    ya, yb = jnp.split(y @ glu_w, 2, axis=-1)
    return ya * jax.nn.sigmoid(yb)


def memory_cross_attention(h, m, wq, wk, wv, q_g, k_g, wo):
    bsz, L, _ = h.shape
    q = rms_norm((h @ wq).reshape(bsz, L, XA_HEADS, XA_HEAD_DIM), q_g)
    k = rms_norm((m @ wk).reshape(bsz, N_MEM, XA_HEADS, XA_HEAD_DIM), k_g)
    v = (m @ wv).reshape(bsz, N_MEM, XA_HEADS, XA_HEAD_DIM)
    s = jnp.einsum('blhd,bmhd->bhlm', q, k).astype(jnp.float32) * (XA_HEAD_DIM ** -0.5)
    p = jax.nn.softmax(s, axis=-1).astype(v.dtype)
    o = jnp.einsum('bhlm,bmhd->blhd', p, v).reshape(bsz, L, XA_WIDTH)
    return o @ wo


def _fwd_setup_inputs(seed: int = 0) -> dict:
    key = jax.random.key(seed)
    ks = iter(jax.random.split(key, 40))
    f32 = jnp.float32

    def nrm(shape, scale):
        return jax.random.normal(next(ks), shape, f32) * scale

    def gain(shape):
        return 1.0 + 0.02 * jax.random.normal(next(ks), shape, f32)

    n_idx = jnp.arange(SSM_STATE, dtype=f32)
    return {
        "x": nrm((BATCH, SEQ, D_MODEL), 1.0),
        "mem": nrm((BATCH, N_MEM, D_MODEL), 1.0),
        "norm_mix_g": gain((DEPTH, D_MODEL)),
        "w_in": nrm((DEPTH, D_MODEL, IN_PROJ), D_MODEL ** -0.5),
        "sb_q_norm_g": gain((DEPTH, SB_HEAD_DIM)),
        "sb_k_norm_g": gain((DEPTH, SB_HEAD_DIM)),
        "conv_dw_w": nrm((DEPTH, CONV_WIDTH, CONV_CH), CONV_WIDTH ** -0.5),
        "conv_dw_b": nrm((DEPTH, CONV_CH), 0.02),
        "conv_ln_g": gain((DEPTH, CONV_CH)),
        "conv_ln_b": nrm((DEPTH, CONV_CH), 0.02),
        "conv_pw2_w": nrm((DEPTH, CONV_CH, CONV_CH), CONV_CH ** -0.5),
        "ssm_lam_re": -0.5 * jnp.exp(nrm((DEPTH, SSM_GROUPS, SSM_STATE), 0.05)),
        "ssm_lam_im": jnp.pi * n_idx * jnp.exp(nrm((DEPTH, SSM_GROUPS, SSM_STATE), 0.01)),
        "ssm_log_dt": jax.random.uniform(next(ks), (DEPTH, SSM_GROUPS), f32,
                                         math.log(1e-3), math.log(1e-1)),
        "ssm_b_re": nrm((DEPTH, SSM_GROUPS, SSM_STATE, SSM_GROUP), (2 * SSM_GROUP) ** -0.5),
        "ssm_b_im": nrm((DEPTH, SSM_GROUPS, SSM_STATE, SSM_GROUP), (2 * SSM_GROUP) ** -0.5),
        "ssm_c_re": nrm((DEPTH, SSM_GROUPS, SSM_GROUP, SSM_STATE), (2 * SSM_STATE) ** -0.5),
        "ssm_c_im": nrm((DEPTH, SSM_GROUPS, SSM_GROUP, SSM_STATE), (2 * SSM_STATE) ** -0.5),
        "ssm_d": nrm((DEPTH, SSM_CH), 1.0),
        "ssm_glu_w": nrm((DEPTH, SSM_CH, 2 * SSM_CH), SSM_CH ** -0.5),
        "branch_norm_g": gain((DEPTH, MIX_WIDTH)),
        "w_out": nrm((DEPTH, MIX_WIDTH, D_MODEL), MIX_WIDTH ** -0.5),
        "norm_xa_g": gain((DEPTH, D_MODEL)),
        "norm_mem_g": gain((DEPTH, D_MODEL)),
        "xa_wq": nrm((DEPTH, D_MODEL, XA_WIDTH), D_MODEL ** -0.5),
        "xa_wk": nrm((DEPTH, D_MODEL, XA_WIDTH), D_MODEL ** -0.5),
        "xa_wv": nrm((DEPTH, D_MODEL, XA_WIDTH), D_MODEL ** -0.5),
        "xa_q_norm_g": gain((DEPTH, XA_HEAD_DIM)),
        "xa_k_norm_g": gain((DEPTH, XA_HEAD_DIM)),
        "xa_wo": nrm((DEPTH, XA_WIDTH, D_MODEL), XA_WIDTH ** -0.5),
        "norm_ffn_g": gain((DEPTH, D_MODEL)),
        "ffn_w_in": nrm((DEPTH, D_MODEL, 2 * FFN_HIDDEN), D_MODEL ** -0.5),
        "ffn_w_out": nrm((DEPTH, FFN_HIDDEN, D_MODEL), FFN_HIDDEN ** -0.5),
    }


def _fwd_reference(x, mem, norm_mix_g, w_in, sb_q_norm_g, sb_k_norm_g, conv_dw_w, conv_dw_b,
              conv_ln_g, conv_ln_b, conv_pw2_w, ssm_lam_re, ssm_lam_im, ssm_log_dt,
              ssm_b_re, ssm_b_im, ssm_c_re, ssm_c_im, ssm_d, ssm_glu_w, branch_norm_g,
              w_out, norm_xa_g, norm_mem_g, xa_wq, xa_wk, xa_wv, xa_q_norm_g, xa_k_norm_g,
              xa_wo, norm_ffn_g, ffn_w_in, ffn_w_out):
    bsz, L, _ = x.shape
    s1 = SB_WIDTH
    s2 = 2 * SB_WIDTH
    s3 = 3 * SB_WIDTH
    s4 = s3 + 2 * CONV_CH
    for l in range(DEPTH):
        h = rms_norm(x, norm_mix_g[l])
        p = h @ w_in[l]
        q = rms_norm(p[..., :s1].reshape(bsz, L, SB_HEADS, SB_HEAD_DIM), sb_q_norm_g[l])
        k = rms_norm(p[..., s1:s2].reshape(bsz, L, SB_HEADS, SB_HEAD_DIM), sb_k_norm_g[l])
        v = p[..., s2:s3].reshape(bsz, L, SB_HEADS, SB_HEAD_DIM)
        o_sb = stick_breaking_attention(q, k, v)
        o_conv = conformer_conv(p[..., s3:s4], conv_dw_w[l], conv_dw_b[l],
                                conv_ln_g[l], conv_ln_b[l], conv_pw2_w[l])
        o_ssm = s5_ssm(p[..., s4:], ssm_lam_re[l], ssm_lam_im[l], ssm_log_dt[l],
                       ssm_b_re[l], ssm_b_im[l], ssm_c_re[l], ssm_c_im[l],
                       ssm_d[l], ssm_glu_w[l])
        g = branch_norm_g[l]
        mixed = jnp.concatenate([
            rms_norm(o_sb, g[:SB_WIDTH]),
            rms_norm(o_conv, g[SB_WIDTH:SB_WIDTH + CONV_CH]),
            rms_norm(o_ssm, g[SB_WIDTH + CONV_CH:]),
        ], axis=-1)
        x = x + mixed @ w_out[l]
        hx = rms_norm(x, norm_xa_g[l])
        hm = rms_norm(mem, norm_mem_g[l])
        x = x + memory_cross_attention(hx, hm, xa_wq[l], xa_wk[l], xa_wv[l],
                                       xa_q_norm_g[l], xa_k_norm_g[l], xa_wo[l])
        hf = rms_norm(x, norm_ffn_g[l])
        gate, up = jnp.split(hf @ ffn_w_in[l], 2, axis=-1)
        x = x + (jax.nn.silu(gate) * up) @ ffn_w_out[l]
    return x


import jax as _jax
import jax.numpy as _jnp

TWIN_FORMAT = 'train_step'
FWD_PARAMS = ['x', 'mem', 'norm_mix_g', 'w_in', 'sb_q_norm_g', 'sb_k_norm_g', 'conv_dw_w', 'conv_dw_b', 'conv_ln_g', 'conv_ln_b', 'conv_pw2_w', 'ssm_lam_re', 'ssm_lam_im', 'ssm_log_dt', 'ssm_b_re', 'ssm_b_im', 'ssm_c_re', 'ssm_c_im', 'ssm_d', 'ssm_glu_w', 'branch_norm_g', 'w_out', 'norm_xa_g', 'norm_mem_g', 'xa_wq', 'xa_wk', 'xa_wv', 'xa_q_norm_g', 'xa_k_norm_g', 'xa_wo', 'norm_ffn_g', 'ffn_w_in', 'ffn_w_out']
TWIN_WEIGHTS = ['norm_mix_g', 'w_in', 'sb_q_norm_g', 'sb_k_norm_g', 'conv_dw_w', 'conv_dw_b', 'conv_ln_g', 'conv_ln_b', 'conv_pw2_w', 'ssm_lam_re', 'ssm_lam_im', 'ssm_log_dt', 'ssm_b_re', 'ssm_b_im', 'ssm_c_re', 'ssm_c_im', 'ssm_d', 'ssm_glu_w', 'branch_norm_g', 'w_out', 'norm_xa_g', 'norm_mem_g', 'xa_wq', 'xa_wk', 'xa_wv', 'xa_q_norm_g', 'xa_k_norm_g', 'xa_wo', 'norm_ffn_g', 'ffn_w_in', 'ffn_w_out']
TWIN_DIFF_INPUT = 'x'
TWIN_INPUTS = ['x', 'mem', 'norm_mix_g', 'w_in', 'sb_q_norm_g', 'sb_k_norm_g', 'conv_dw_w', 'conv_dw_b', 'conv_ln_g', 'conv_ln_b', 'conv_pw2_w', 'ssm_lam_re', 'ssm_lam_im', 'ssm_log_dt', 'ssm_b_re', 'ssm_b_im', 'ssm_c_re', 'ssm_c_im', 'ssm_d', 'ssm_glu_w', 'branch_norm_g', 'w_out', 'norm_xa_g', 'norm_mem_g', 'xa_wq', 'xa_wk', 'xa_wv', 'xa_q_norm_g', 'xa_k_norm_g', 'xa_wo', 'norm_ffn_g', 'ffn_w_in', 'ffn_w_out', 'loss_target', 'm_norm_mix_g', 'm_w_in', 'm_sb_q_norm_g', 'm_sb_k_norm_g', 'm_conv_dw_w', 'm_conv_dw_b', 'm_conv_ln_g', 'm_conv_ln_b', 'm_conv_pw2_w', 'm_ssm_lam_re', 'm_ssm_lam_im', 'm_ssm_log_dt', 'm_ssm_b_re', 'm_ssm_b_im', 'm_ssm_c_re', 'm_ssm_c_im', 'm_ssm_d', 'm_ssm_glu_w', 'm_branch_norm_g', 'm_w_out', 'm_norm_xa_g', 'm_norm_mem_g', 'm_xa_wq', 'm_xa_wk', 'm_xa_wv', 'm_xa_q_norm_g', 'm_xa_k_norm_g', 'm_xa_wo', 'm_norm_ffn_g', 'm_ffn_w_in', 'm_ffn_w_out', 'v_norm_mix_g', 'v_w_in', 'v_sb_q_norm_g', 'v_sb_k_norm_g', 'v_conv_dw_w', 'v_conv_dw_b', 'v_conv_ln_g', 'v_conv_ln_b', 'v_conv_pw2_w', 'v_ssm_lam_re', 'v_ssm_lam_im', 'v_ssm_log_dt', 'v_ssm_b_re', 'v_ssm_b_im', 'v_ssm_c_re', 'v_ssm_c_im', 'v_ssm_d', 'v_ssm_glu_w', 'v_branch_norm_g', 'v_w_out', 'v_norm_xa_g', 'v_norm_mem_g', 'v_xa_wq', 'v_xa_wk', 'v_xa_wv', 'v_xa_q_norm_g', 'v_xa_k_norm_g', 'v_xa_wo', 'v_norm_ffn_g', 'v_ffn_w_in', 'v_ffn_w_out']
TWIN_OUTPUTS = ['loss', 'grad_x', 'grad_norm_mix_g', 'grad_w_in', 'grad_sb_q_norm_g', 'grad_sb_k_norm_g', 'grad_conv_dw_w', 'grad_conv_dw_b', 'grad_conv_ln_g', 'grad_conv_ln_b', 'grad_conv_pw2_w', 'grad_ssm_lam_re', 'grad_ssm_lam_im', 'grad_ssm_log_dt', 'grad_ssm_b_re', 'grad_ssm_b_im', 'grad_ssm_c_re', 'grad_ssm_c_im', 'grad_ssm_d', 'grad_ssm_glu_w', 'grad_branch_norm_g', 'grad_w_out', 'grad_norm_xa_g', 'grad_norm_mem_g', 'grad_xa_wq', 'grad_xa_wk', 'grad_xa_wv', 'grad_xa_q_norm_g', 'grad_xa_k_norm_g', 'grad_xa_wo', 'grad_norm_ffn_g', 'grad_ffn_w_in', 'grad_ffn_w_out', 'delta_norm_mix_g', 'delta_w_in', 'delta_sb_q_norm_g', 'delta_sb_k_norm_g', 'delta_conv_dw_w', 'delta_conv_dw_b', 'delta_conv_ln_g', 'delta_conv_ln_b', 'delta_conv_pw2_w', 'delta_ssm_lam_re', 'delta_ssm_lam_im', 'delta_ssm_log_dt', 'delta_ssm_b_re', 'delta_ssm_b_im', 'delta_ssm_c_re', 'delta_ssm_c_im', 'delta_ssm_d', 'delta_ssm_glu_w', 'delta_branch_norm_g', 'delta_w_out', 'delta_norm_xa_g', 'delta_norm_mem_g', 'delta_xa_wq', 'delta_xa_wk', 'delta_xa_wv', 'delta_xa_q_norm_g', 'delta_xa_k_norm_g', 'delta_xa_wo', 'delta_norm_ffn_g', 'delta_ffn_w_in', 'delta_ffn_w_out', 'new_m_norm_mix_g', 'new_m_w_in', 'new_m_sb_q_norm_g', 'new_m_sb_k_norm_g', 'new_m_conv_dw_w', 'new_m_conv_dw_b', 'new_m_conv_ln_g', 'new_m_conv_ln_b', 'new_m_conv_pw2_w', 'new_m_ssm_lam_re', 'new_m_ssm_lam_im', 'new_m_ssm_log_dt', 'new_m_ssm_b_re', 'new_m_ssm_b_im', 'new_m_ssm_c_re', 'new_m_ssm_c_im', 'new_m_ssm_d', 'new_m_ssm_glu_w', 'new_m_branch_norm_g', 'new_m_w_out', 'new_m_norm_xa_g', 'new_m_norm_mem_g', 'new_m_xa_wq', 'new_m_xa_wk', 'new_m_xa_wv', 'new_m_xa_q_norm_g', 'new_m_xa_k_norm_g', 'new_m_xa_wo', 'new_m_norm_ffn_g', 'new_m_ffn_w_in', 'new_m_ffn_w_out', 'new_v_norm_mix_g', 'new_v_w_in', 'new_v_sb_q_norm_g', 'new_v_sb_k_norm_g', 'new_v_conv_dw_w', 'new_v_conv_dw_b', 'new_v_conv_ln_g', 'new_v_conv_ln_b', 'new_v_conv_pw2_w', 'new_v_ssm_lam_re', 'new_v_ssm_lam_im', 'new_v_ssm_log_dt', 'new_v_ssm_b_re', 'new_v_ssm_b_im', 'new_v_ssm_c_re', 'new_v_ssm_c_im', 'new_v_ssm_d', 'new_v_ssm_glu_w', 'new_v_branch_norm_g', 'new_v_w_out', 'new_v_norm_xa_g', 'new_v_norm_mem_g', 'new_v_xa_wq', 'new_v_xa_wk', 'new_v_xa_wv', 'new_v_xa_q_norm_g', 'new_v_xa_k_norm_g', 'new_v_xa_wo', 'new_v_norm_ffn_g', 'new_v_ffn_w_in', 'new_v_ffn_w_out']
TWIN_LEAF_KINDS = {'loss': 'loss', 'grad_x': 'grad_x', 'grad_norm_mix_g': 'grad_w', 'grad_w_in': 'grad_w', 'grad_sb_q_norm_g': 'grad_w', 'grad_sb_k_norm_g': 'grad_w', 'grad_conv_dw_w': 'grad_w', 'grad_conv_dw_b': 'grad_w', 'grad_conv_ln_g': 'grad_w', 'grad_conv_ln_b': 'grad_w', 'grad_conv_pw2_w': 'grad_w', 'grad_ssm_lam_re': 'grad_w', 'grad_ssm_lam_im': 'grad_w', 'grad_ssm_log_dt': 'grad_w', 'grad_ssm_b_re': 'grad_w', 'grad_ssm_b_im': 'grad_w', 'grad_ssm_c_re': 'grad_w', 'grad_ssm_c_im': 'grad_w', 'grad_ssm_d': 'grad_w', 'grad_ssm_glu_w': 'grad_w', 'grad_branch_norm_g': 'grad_w', 'grad_w_out': 'grad_w', 'grad_norm_xa_g': 'grad_w', 'grad_norm_mem_g': 'grad_w', 'grad_xa_wq': 'grad_w', 'grad_xa_wk': 'grad_w', 'grad_xa_wv': 'grad_w', 'grad_xa_q_norm_g': 'grad_w', 'grad_xa_k_norm_g': 'grad_w', 'grad_xa_wo': 'grad_w', 'grad_norm_ffn_g': 'grad_w', 'grad_ffn_w_in': 'grad_w', 'grad_ffn_w_out': 'grad_w', 'delta_norm_mix_g': 'delta_w', 'delta_w_in': 'delta_w', 'delta_sb_q_norm_g': 'delta_w', 'delta_sb_k_norm_g': 'delta_w', 'delta_conv_dw_w': 'delta_w', 'delta_conv_dw_b': 'delta_w', 'delta_conv_ln_g': 'delta_w', 'delta_conv_ln_b': 'delta_w', 'delta_conv_pw2_w': 'delta_w', 'delta_ssm_lam_re': 'delta_w', 'delta_ssm_lam_im': 'delta_w', 'delta_ssm_log_dt': 'delta_w', 'delta_ssm_b_re': 'delta_w', 'delta_ssm_b_im': 'delta_w', 'delta_ssm_c_re': 'delta_w', 'delta_ssm_c_im': 'delta_w', 'delta_ssm_d': 'delta_w', 'delta_ssm_glu_w': 'delta_w', 'delta_branch_norm_g': 'delta_w', 'delta_w_out': 'delta_w', 'delta_norm_xa_g': 'delta_w', 'delta_norm_mem_g': 'delta_w', 'delta_xa_wq': 'delta_w', 'delta_xa_wk': 'delta_w', 'delta_xa_wv': 'delta_w', 'delta_xa_q_norm_g': 'delta_w', 'delta_xa_k_norm_g': 'delta_w', 'delta_xa_wo': 'delta_w', 'delta_norm_ffn_g': 'delta_w', 'delta_ffn_w_in': 'delta_w', 'delta_ffn_w_out': 'delta_w', 'new_m_norm_mix_g': 'new_m', 'new_m_w_in': 'new_m', 'new_m_sb_q_norm_g': 'new_m', 'new_m_sb_k_norm_g': 'new_m', 'new_m_conv_dw_w': 'new_m', 'new_m_conv_dw_b': 'new_m', 'new_m_conv_ln_g': 'new_m', 'new_m_conv_ln_b': 'new_m', 'new_m_conv_pw2_w': 'new_m', 'new_m_ssm_lam_re': 'new_m', 'new_m_ssm_lam_im': 'new_m', 'new_m_ssm_log_dt': 'new_m', 'new_m_ssm_b_re': 'new_m', 'new_m_ssm_b_im': 'new_m', 'new_m_ssm_c_re': 'new_m', 'new_m_ssm_c_im': 'new_m', 'new_m_ssm_d': 'new_m', 'new_m_ssm_glu_w': 'new_m', 'new_m_branch_norm_g': 'new_m', 'new_m_w_out': 'new_m', 'new_m_norm_xa_g': 'new_m', 'new_m_norm_mem_g': 'new_m', 'new_m_xa_wq': 'new_m', 'new_m_xa_wk': 'new_m', 'new_m_xa_wv': 'new_m', 'new_m_xa_q_norm_g': 'new_m', 'new_m_xa_k_norm_g': 'new_m', 'new_m_xa_wo': 'new_m', 'new_m_norm_ffn_g': 'new_m', 'new_m_ffn_w_in': 'new_m', 'new_m_ffn_w_out': 'new_m', 'new_v_norm_mix_g': 'new_v', 'new_v_w_in': 'new_v', 'new_v_sb_q_norm_g': 'new_v', 'new_v_sb_k_norm_g': 'new_v', 'new_v_conv_dw_w': 'new_v', 'new_v_conv_dw_b': 'new_v', 'new_v_conv_ln_g': 'new_v', 'new_v_conv_ln_b': 'new_v', 'new_v_conv_pw2_w': 'new_v', 'new_v_ssm_lam_re': 'new_v', 'new_v_ssm_lam_im': 'new_v', 'new_v_ssm_log_dt': 'new_v', 'new_v_ssm_b_re': 'new_v', 'new_v_ssm_b_im': 'new_v', 'new_v_ssm_c_re': 'new_v', 'new_v_ssm_c_im': 'new_v', 'new_v_ssm_d': 'new_v', 'new_v_ssm_glu_w': 'new_v', 'new_v_branch_norm_g': 'new_v', 'new_v_w_out': 'new_v', 'new_v_norm_xa_g': 'new_v', 'new_v_norm_mem_g': 'new_v', 'new_v_xa_wq': 'new_v', 'new_v_xa_wk': 'new_v', 'new_v_xa_wv': 'new_v', 'new_v_xa_q_norm_g': 'new_v', 'new_v_xa_k_norm_g': 'new_v', 'new_v_xa_wo': 'new_v', 'new_v_norm_ffn_g': 'new_v', 'new_v_ffn_w_in': 'new_v', 'new_v_ffn_w_out': 'new_v'}


def _forward(args):
    return _fwd_reference(*[args[k] for k in FWD_PARAMS])


def _output_shape():
    def fwd():
        inp = _fwd_setup_inputs(0)
        return _fwd_reference(*[inp[k] for k in FWD_PARAMS])
    out = _jax.eval_shape(fwd)
    return out.shape, out.dtype

N_MICROBATCH = 1
ADAM_LR = 0.001
ADAM_B1 = 0.9
ADAM_B2 = 0.999
ADAM_EPS = 1e-08
ADAM_WD = 0.01
ADAM_STEP = 10
PER_EXAMPLE_BATCH_AXIS = {'x': 0, 'mem': 0, 'loss_target': 0}
SHARED_INPUTS = []
_WEIGHT_DTYPES = {'norm_mix_g': _jnp.float32, 'w_in': _jnp.float32, 'sb_q_norm_g': _jnp.float32, 'sb_k_norm_g': _jnp.float32, 'conv_dw_w': _jnp.float32, 'conv_dw_b': _jnp.float32, 'conv_ln_g': _jnp.float32, 'conv_ln_b': _jnp.float32, 'conv_pw2_w': _jnp.float32, 'ssm_lam_re': _jnp.float32, 'ssm_lam_im': _jnp.float32, 'ssm_log_dt': _jnp.float32, 'ssm_b_re': _jnp.float32, 'ssm_b_im': _jnp.float32, 'ssm_c_re': _jnp.float32, 'ssm_c_im': _jnp.float32, 'ssm_d': _jnp.float32, 'ssm_glu_w': _jnp.float32, 'branch_norm_g': _jnp.float32, 'w_out': _jnp.float32, 'norm_xa_g': _jnp.float32, 'norm_mem_g': _jnp.float32, 'xa_wq': _jnp.float32, 'xa_wk': _jnp.float32, 'xa_wv': _jnp.float32, 'xa_q_norm_g': _jnp.float32, 'xa_k_norm_g': _jnp.float32, 'xa_wo': _jnp.float32, 'norm_ffn_g': _jnp.float32, 'ffn_w_in': _jnp.float32, 'ffn_w_out': _jnp.float32}
MOMENT_SCALE = {'norm_mix_g': 2.173492e+00, 'w_in': 1.396477e+00, 'sb_q_norm_g': 1.238739e+00, 'sb_k_norm_g': 1.223705e+00, 'conv_dw_w': 1.987418e+00, 'conv_dw_b': 3.274414e+01, 'conv_ln_g': 1.311839e+01, 'conv_ln_b': 1.916287e+01, 'conv_pw2_w': 8.238201e+00, 'ssm_lam_re': 2.292637e-01, 'ssm_lam_im': 2.430656e-01, 'ssm_log_dt': 5.387536e+01, 'ssm_b_re': 1.082142e-01, 'ssm_b_im': 1.033858e-01, 'ssm_c_re': 1.775095e-01, 'ssm_c_im': 2.060802e-01, 'ssm_d': 3.876765e+00, 'ssm_glu_w': 3.004414e+00, 'branch_norm_g': 6.488291e+01, 'w_out': 4.955316e+00, 'norm_xa_g': 1.591430e-01, 'norm_mem_g': 8.069625e-01, 'xa_wq': 1.544984e-01, 'xa_wk': 1.529816e-01, 'xa_wv': 4.854269e-01, 'xa_q_norm_g': 2.443468e+00, 'xa_k_norm_g': 2.442616e+00, 'xa_wo': 4.685426e-01, 'norm_ffn_g': 4.934843e+01, 'ffn_w_in': 7.265847e-01, 'ffn_w_out': 1.006776e+00}


def _to_microbatches(a, axis):
    t = _jnp.moveaxis(a, axis, 0)
    t = t.reshape((N_MICROBATCH, t.shape[0] // N_MICROBATCH) + t.shape[1:])
    return _jnp.moveaxis(t, 1, axis + 1)


def setup_inputs(seed: int = 0) -> dict:
    inp = _fwd_setup_inputs(seed)
    key = _jax.random.fold_in(_jax.random.key(seed), 7919)
    shape, _ = _output_shape()
    out = dict(inp)
    out["loss_target"] = _jax.random.normal(_jax.random.fold_in(key, 0), shape, _jnp.float32)
    for i, name in enumerate(TWIN_WEIGHTS):
        w = inp[name].astype(_jnp.float32)
        if MOMENT_SCALE is None:
            s = _jnp.sqrt(_jnp.mean(_jnp.square(w)) + 1e-30)
        else:
            s = MOMENT_SCALE[name]
        km, kv = _jax.random.split(_jax.random.fold_in(key, i + 1))
        out[name] = w
        out["m_" + name] = s * _jax.random.normal(km, w.shape, _jnp.float32)
        out["v_" + name] = (s * s) * _jax.random.uniform(kv, w.shape, _jnp.float32, 0.5, 1.5)
    if N_MICROBATCH > 1:
        for name, axis in PER_EXAMPLE_BATCH_AXIS.items():
            out[name] = _to_microbatches(out[name], axis)
    return {'x': out['x'], 'mem': out['mem'], 'norm_mix_g': out['norm_mix_g'], 'w_in': out['w_in'], 'sb_q_norm_g': out['sb_q_norm_g'], 'sb_k_norm_g': out['sb_k_norm_g'], 'conv_dw_w': out['conv_dw_w'], 'conv_dw_b': out['conv_dw_b'], 'conv_ln_g': out['conv_ln_g'], 'conv_ln_b': out['conv_ln_b'], 'conv_pw2_w': out['conv_pw2_w'], 'ssm_lam_re': out['ssm_lam_re'], 'ssm_lam_im': out['ssm_lam_im'], 'ssm_log_dt': out['ssm_log_dt'], 'ssm_b_re': out['ssm_b_re'], 'ssm_b_im': out['ssm_b_im'], 'ssm_c_re': out['ssm_c_re'], 'ssm_c_im': out['ssm_c_im'], 'ssm_d': out['ssm_d'], 'ssm_glu_w': out['ssm_glu_w'], 'branch_norm_g': out['branch_norm_g'], 'w_out': out['w_out'], 'norm_xa_g': out['norm_xa_g'], 'norm_mem_g': out['norm_mem_g'], 'xa_wq': out['xa_wq'], 'xa_wk': out['xa_wk'], 'xa_wv': out['xa_wv'], 'xa_q_norm_g': out['xa_q_norm_g'], 'xa_k_norm_g': out['xa_k_norm_g'], 'xa_wo': out['xa_wo'], 'norm_ffn_g': out['norm_ffn_g'], 'ffn_w_in': out['ffn_w_in'], 'ffn_w_out': out['ffn_w_out'], 'loss_target': out['loss_target'], 'm_norm_mix_g': out['m_norm_mix_g'], 'm_w_in': out['m_w_in'], 'm_sb_q_norm_g': out['m_sb_q_norm_g'], 'm_sb_k_norm_g': out['m_sb_k_norm_g'], 'm_conv_dw_w': out['m_conv_dw_w'], 'm_conv_dw_b': out['m_conv_dw_b'], 'm_conv_ln_g': out['m_conv_ln_g'], 'm_conv_ln_b': out['m_conv_ln_b'], 'm_conv_pw2_w': out['m_conv_pw2_w'], 'm_ssm_lam_re': out['m_ssm_lam_re'], 'm_ssm_lam_im': out['m_ssm_lam_im'], 'm_ssm_log_dt': out['m_ssm_log_dt'], 'm_ssm_b_re': out['m_ssm_b_re'], 'm_ssm_b_im': out['m_ssm_b_im'], 'm_ssm_c_re': out['m_ssm_c_re'], 'm_ssm_c_im': out['m_ssm_c_im'], 'm_ssm_d': out['m_ssm_d'], 'm_ssm_glu_w': out['m_ssm_glu_w'], 'm_branch_norm_g': out['m_branch_norm_g'], 'm_w_out': out['m_w_out'], 'm_norm_xa_g': out['m_norm_xa_g'], 'm_norm_mem_g': out['m_norm_mem_g'], 'm_xa_wq': out['m_xa_wq'], 'm_xa_wk': out['m_xa_wk'], 'm_xa_wv': out['m_xa_wv'], 'm_xa_q_norm_g': out['m_xa_q_norm_g'], 'm_xa_k_norm_g': out['m_xa_k_norm_g'], 'm_xa_wo': out['m_xa_wo'], 'm_norm_ffn_g': out['m_norm_ffn_g'], 'm_ffn_w_in': out['m_ffn_w_in'], 'm_ffn_w_out': out['m_ffn_w_out'], 'v_norm_mix_g': out['v_norm_mix_g'], 'v_w_in': out['v_w_in'], 'v_sb_q_norm_g': out['v_sb_q_norm_g'], 'v_sb_k_norm_g': out['v_sb_k_norm_g'], 'v_conv_dw_w': out['v_conv_dw_w'], 'v_conv_dw_b': out['v_conv_dw_b'], 'v_conv_ln_g': out['v_conv_ln_g'], 'v_conv_ln_b': out['v_conv_ln_b'], 'v_conv_pw2_w': out['v_conv_pw2_w'], 'v_ssm_lam_re': out['v_ssm_lam_re'], 'v_ssm_lam_im': out['v_ssm_lam_im'], 'v_ssm_log_dt': out['v_ssm_log_dt'], 'v_ssm_b_re': out['v_ssm_b_re'], 'v_ssm_b_im': out['v_ssm_b_im'], 'v_ssm_c_re': out['v_ssm_c_re'], 'v_ssm_c_im': out['v_ssm_c_im'], 'v_ssm_d': out['v_ssm_d'], 'v_ssm_glu_w': out['v_ssm_glu_w'], 'v_branch_norm_g': out['v_branch_norm_g'], 'v_w_out': out['v_w_out'], 'v_norm_xa_g': out['v_norm_xa_g'], 'v_norm_mem_g': out['v_norm_mem_g'], 'v_xa_wq': out['v_xa_wq'], 'v_xa_wk': out['v_xa_wk'], 'v_xa_wv': out['v_xa_wv'], 'v_xa_q_norm_g': out['v_xa_q_norm_g'], 'v_xa_k_norm_g': out['v_xa_k_norm_g'], 'v_xa_wo': out['v_xa_wo'], 'v_norm_ffn_g': out['v_norm_ffn_g'], 'v_ffn_w_in': out['v_ffn_w_in'], 'v_ffn_w_out': out['v_ffn_w_out']}


def _loss(weights, diff, rest, loss_target):
    with _jax.named_scope("forward"):
        args = {**rest, TWIN_DIFF_INPUT: diff, **{k: w.astype(_WEIGHT_DTYPES[k]) for k, w in weights.items()}}
        y = _forward(args)
    with _jax.named_scope("loss_head"):
        err = _jnp.square(y.astype(_jnp.float32) - loss_target)
        return 0.5 * _jnp.sum(_jnp.mean(err, axis=-1)) if err.ndim else 0.5 * err


def _adamw(w, g, m, v):
    m = ADAM_B1 * m + (1.0 - ADAM_B1) * g
    v = ADAM_B2 * v + (1.0 - ADAM_B2) * _jnp.square(g)
    m_hat = m / (1.0 - ADAM_B1 ** ADAM_STEP)
    v_hat = v / (1.0 - ADAM_B2 ** ADAM_STEP)
    delta = -ADAM_LR * (m_hat / (_jnp.sqrt(v_hat) + ADAM_EPS) + ADAM_WD * w)
    return delta, m, v


def reference(x, mem, norm_mix_g, w_in, sb_q_norm_g, sb_k_norm_g, conv_dw_w, conv_dw_b, conv_ln_g, conv_ln_b, conv_pw2_w, ssm_lam_re, ssm_lam_im, ssm_log_dt, ssm_b_re, ssm_b_im, ssm_c_re, ssm_c_im, ssm_d, ssm_glu_w, branch_norm_g, w_out, norm_xa_g, norm_mem_g, xa_wq, xa_wk, xa_wv, xa_q_norm_g, xa_k_norm_g, xa_wo, norm_ffn_g, ffn_w_in, ffn_w_out, loss_target, m_norm_mix_g, m_w_in, m_sb_q_norm_g, m_sb_k_norm_g, m_conv_dw_w, m_conv_dw_b, m_conv_ln_g, m_conv_ln_b, m_conv_pw2_w, m_ssm_lam_re, m_ssm_lam_im, m_ssm_log_dt, m_ssm_b_re, m_ssm_b_im, m_ssm_c_re, m_ssm_c_im, m_ssm_d, m_ssm_glu_w, m_branch_norm_g, m_w_out, m_norm_xa_g, m_norm_mem_g, m_xa_wq, m_xa_wk, m_xa_wv, m_xa_q_norm_g, m_xa_k_norm_g, m_xa_wo, m_norm_ffn_g, m_ffn_w_in, m_ffn_w_out, v_norm_mix_g, v_w_in, v_sb_q_norm_g, v_sb_k_norm_g, v_conv_dw_w, v_conv_dw_b, v_conv_ln_g, v_conv_ln_b, v_conv_pw2_w, v_ssm_lam_re, v_ssm_lam_im, v_ssm_log_dt, v_ssm_b_re, v_ssm_b_im, v_ssm_c_re, v_ssm_c_im, v_ssm_d, v_ssm_glu_w, v_branch_norm_g, v_w_out, v_norm_xa_g, v_norm_mem_g, v_xa_wq, v_xa_wk, v_xa_wv, v_xa_q_norm_g, v_xa_k_norm_g, v_xa_wo, v_norm_ffn_g, v_ffn_w_in, v_ffn_w_out):
    given = dict(x=x, mem=mem, norm_mix_g=norm_mix_g, w_in=w_in, sb_q_norm_g=sb_q_norm_g, sb_k_norm_g=sb_k_norm_g, conv_dw_w=conv_dw_w, conv_dw_b=conv_dw_b, conv_ln_g=conv_ln_g, conv_ln_b=conv_ln_b, conv_pw2_w=conv_pw2_w, ssm_lam_re=ssm_lam_re, ssm_lam_im=ssm_lam_im, ssm_log_dt=ssm_log_dt, ssm_b_re=ssm_b_re, ssm_b_im=ssm_b_im, ssm_c_re=ssm_c_re, ssm_c_im=ssm_c_im, ssm_d=ssm_d, ssm_glu_w=ssm_glu_w, branch_norm_g=branch_norm_g, w_out=w_out, norm_xa_g=norm_xa_g, norm_mem_g=norm_mem_g, xa_wq=xa_wq, xa_wk=xa_wk, xa_wv=xa_wv, xa_q_norm_g=xa_q_norm_g, xa_k_norm_g=xa_k_norm_g, xa_wo=xa_wo, norm_ffn_g=norm_ffn_g, ffn_w_in=ffn_w_in, ffn_w_out=ffn_w_out, loss_target=loss_target, m_norm_mix_g=m_norm_mix_g, m_w_in=m_w_in, m_sb_q_norm_g=m_sb_q_norm_g, m_sb_k_norm_g=m_sb_k_norm_g, m_conv_dw_w=m_conv_dw_w, m_conv_dw_b=m_conv_dw_b, m_conv_ln_g=m_conv_ln_g, m_conv_ln_b=m_conv_ln_b, m_conv_pw2_w=m_conv_pw2_w, m_ssm_lam_re=m_ssm_lam_re, m_ssm_lam_im=m_ssm_lam_im, m_ssm_log_dt=m_ssm_log_dt, m_ssm_b_re=m_ssm_b_re, m_ssm_b_im=m_ssm_b_im, m_ssm_c_re=m_ssm_c_re, m_ssm_c_im=m_ssm_c_im, m_ssm_d=m_ssm_d, m_ssm_glu_w=m_ssm_glu_w, m_branch_norm_g=m_branch_norm_g, m_w_out=m_w_out, m_norm_xa_g=m_norm_xa_g, m_norm_mem_g=m_norm_mem_g, m_xa_wq=m_xa_wq, m_xa_wk=m_xa_wk, m_xa_wv=m_xa_wv, m_xa_q_norm_g=m_xa_q_norm_g, m_xa_k_norm_g=m_xa_k_norm_g, m_xa_wo=m_xa_wo, m_norm_ffn_g=m_norm_ffn_g, m_ffn_w_in=m_ffn_w_in, m_ffn_w_out=m_ffn_w_out, v_norm_mix_g=v_norm_mix_g, v_w_in=v_w_in, v_sb_q_norm_g=v_sb_q_norm_g, v_sb_k_norm_g=v_sb_k_norm_g, v_conv_dw_w=v_conv_dw_w, v_conv_dw_b=v_conv_dw_b, v_conv_ln_g=v_conv_ln_g, v_conv_ln_b=v_conv_ln_b, v_conv_pw2_w=v_conv_pw2_w, v_ssm_lam_re=v_ssm_lam_re, v_ssm_lam_im=v_ssm_lam_im, v_ssm_log_dt=v_ssm_log_dt, v_ssm_b_re=v_ssm_b_re, v_ssm_b_im=v_ssm_b_im, v_ssm_c_re=v_ssm_c_re, v_ssm_c_im=v_ssm_c_im, v_ssm_d=v_ssm_d, v_ssm_glu_w=v_ssm_glu_w, v_branch_norm_g=v_branch_norm_g, v_w_out=v_w_out, v_norm_xa_g=v_norm_xa_g, v_norm_mem_g=v_norm_mem_g, v_xa_wq=v_xa_wq, v_xa_wk=v_xa_wk, v_xa_wv=v_xa_wv, v_xa_q_norm_g=v_xa_q_norm_g, v_xa_k_norm_g=v_xa_k_norm_g, v_xa_wo=v_xa_wo, v_norm_ffn_g=v_norm_ffn_g, v_ffn_w_in=v_ffn_w_in, v_ffn_w_out=v_ffn_w_out)
    weights = {n: given[n] for n in TWIN_WEIGHTS}
    shared = {n: given[n] for n in SHARED_INPUTS}
    per_example = {n: given[n] for n in ['x', 'mem']}
    grad_fn = _jax.value_and_grad(_loss, argnums=(0, 1))

    def one_microbatch(ex, loss_target):
        ex = dict(ex)
        diff = ex.pop(TWIN_DIFF_INPUT)
        return grad_fn(weights, diff, {**shared, **ex}, loss_target)

    if N_MICROBATCH == 1:
        loss, (grad_w, grad_x) = one_microbatch(per_example, given["loss_target"])
    else:
        def body(carry, xs):
            loss_sum, grad_sum = carry
            l_k, (gw_k, gx_k) = one_microbatch(xs[0], xs[1])
            with _jax.named_scope("update"):
                return (loss_sum + l_k, _jax.tree.map(_jnp.add, grad_sum, gw_k)), gx_k

        init = (_jnp.zeros((), _jnp.float32), _jax.tree.map(_jnp.zeros_like, weights))
        (loss, grad_w), grad_x = _jax.lax.scan(body, init, (per_example, given["loss_target"]))
    with _jax.named_scope("update"):
        delta_w, new_m, new_v = {}, {}, {}
        for n in TWIN_WEIGHTS:
            delta_w[n], new_m[n], new_v[n] = _adamw(weights[n], grad_w[n], given["m_" + n], given["v_" + n])
    return (loss, grad_x, *[grad_w[n] for n in TWIN_WEIGHTS], *[delta_w[n] for n in TWIN_WEIGHTS],
            *[new_m[n] for n in TWIN_WEIGHTS], *[new_v[n] for n in TWIN_WEIGHTS])
```

```python
import functools

import numpy as np
import jax
import jax.numpy as jnp
from jax import lax
from jax.experimental import pallas as pl
from jax.experimental.pallas import tpu as pltpu

F32 = jnp.float32
BF16 = jnp.bfloat16
EPS = 1e-6
D_MODEL = 1024
DEPTH = 2
N_DEV = 8
SB_WIDTH = 512
SB_HEAD_DIM = 64
CONV_CH = 256
CONV_WIDTH = 31
SSM_CH = 256
SSM_GROUP = 16
SSM_GROUPS = 16
SSM_STATE = 64
SSM_LANES = SSM_GROUPS * SSM_STATE
XA_HEADS = 4
XA_HEAD_DIM = 256
FFN_HIDDEN = 2816
ADAM_LR = 0.001
ADAM_B1 = 0.9
ADAM_B2 = 0.999
ADAM_EPS = 1e-08
ADAM_WD = 0.01
ADAM_STEP = 10

LANES = 128
SUBLANES = 8
TL = 512
SB_TQ = 256
SB_TK = 128
SSM_T = 256
CONV_HALO = 32
CONV_SUB = 64
VMEM_MB = 48

MESH_AXES = ("x", "y", "c")
WEIGHT_NAMES = ['norm_mix_g', 'w_in', 'sb_q_norm_g', 'sb_k_norm_g', 'conv_dw_w', 'conv_dw_b', 'conv_ln_g',
                'conv_ln_b', 'conv_pw2_w', 'ssm_lam_re', 'ssm_lam_im', 'ssm_log_dt', 'ssm_b_re', 'ssm_b_im',
                'ssm_c_re', 'ssm_c_im', 'ssm_d', 'ssm_glu_w', 'branch_norm_g', 'w_out', 'norm_xa_g',
                'norm_mem_g', 'xa_wq', 'xa_wk', 'xa_wv', 'xa_q_norm_g', 'xa_k_norm_g', 'xa_wo', 'norm_ffn_g',
                'ffn_w_in', 'ffn_w_out']
SHARD_AXIS = {'w_in': 2, 'conv_dw_w': 2, 'conv_pw2_w': 1, 'ssm_glu_w': 2, 'w_out': 1, 'xa_wq': 1, 'xa_wk': 1,
              'xa_wv': 1, 'xa_wo': 1, 'ffn_w_in': 2, 'ffn_w_out': 1}
BIG_NAMES = [n for n in WEIGHT_NAMES if n in SHARD_AXIS]
SMALL_NAMES = [n for n in WEIGHT_NAMES if n not in SHARD_AXIS]


def _nn(a, b):
    return jnp.dot(a.astype(BF16), b.astype(BF16), preferred_element_type=F32)


def _nt(a, b):
    return lax.dot_general(a.astype(BF16), b.astype(BF16), (((1,), (1,)), ((), ())), preferred_element_type=F32)


def _tn(a, b):
    return lax.dot_general(a.astype(BF16), b.astype(BF16), (((0,), (0,)), ((), ())), preferred_element_type=F32)


def _rms(x, g):
    return x * lax.rsqrt(jnp.mean(x * x, axis=-1, keepdims=True) + EPS) * g


def _sigmoid(x):
    return 1.0 / (1.0 + jnp.exp(-x))


def _silu(x):
    return x * _sigmoid(x)


def _layer_norm(x, g, b):
    mu = jnp.mean(x, axis=-1, keepdims=True)
    xc = x - mu
    var = jnp.mean(xc * xc, axis=-1, keepdims=True)
    return xc * lax.rsqrt(var + EPS) * g + b


def _head_rms64(p, g, mavg):
    ms = jnp.dot(p * p, mavg, preferred_element_type=F32)
    return p * lax.rsqrt(ms + EPS) * g


def _params(n_grid, vmem_mb=VMEM_MB):
    return pltpu.CompilerParams(dimension_semantics=("arbitrary",) * n_grid, vmem_limit_bytes=vmem_mb << 20)


def _rows(cols, tl=TL):
    return pl.BlockSpec((tl, cols), lambda i: (i, 0))


def _res(shape):
    nd = len(shape)
    return pl.BlockSpec(tuple(shape), lambda *_: (0,) * nd)


def _sds(shape, dtype=F32):
    return jax.ShapeDtypeStruct(tuple(shape), dtype)


def _accumulate(i, ref, val):
    @pl.when(i == 0)
    def _():
        ref[...] = val

    @pl.when(i > 0)
    def _():
        ref[...] += val


def _mixin_post(pq, pk, a, b, gq, gk, mavg):
    return _head_rms64(pq, gq, mavg), _head_rms64(pk, gk, mavg), a * _sigmoid(b)


def _mix_in_fwd(x, g_mix, w_in, gq, gk, mavg, tag):
    L = x.shape[0]

    def body(x_ref, g_ref, w_ref, gq_ref, gk_ref, mavg_ref, q_ref, k_ref, v_ref, hg_ref, u_ref):
        h = _rms(x_ref[...], g_ref[...])
        p = _nn(h, w_ref[...])
        q, k, hg = _mixin_post(p[:, 0:512], p[:, 512:1024], p[:, 1536:1792], p[:, 1792:2048],
                               gq_ref[...], gk_ref[...], mavg_ref[...])
        q_ref[...] = q.astype(BF16)
        k_ref[...] = k.astype(BF16)
        v_ref[...] = p[:, 1024:1536].astype(BF16)
        hg_ref[...] = hg
        u_ref[...] = p[:, 2048:2304]

    return pl.pallas_call(
        body, name=f"mix_in_fwd_{tag}", grid=(L // TL,),
        in_specs=[_rows(D_MODEL), _res((1, D_MODEL)), _res(w_in.shape), _res((1, 512)), _res((1, 512)), _res((512, 512))],
        out_specs=[_rows(512), _rows(512), _rows(512), _rows(256), _rows(256)],
        out_shape=[_sds((L, 512), BF16), _sds((L, 512), BF16), _sds((L, 512), BF16), _sds((L, 256)), _sds((L, 256))],
        compiler_params=_params(1),
    )(x, g_mix, w_in, gq, gk, mavg)


def _mix_in_bwd(x, dres, dq, dk, dv, dhg, du, g_mix, w_in, gq, gk, mavg, tag):
    L = x.shape[0]
    tl = 256

    def body(x_ref, dres_ref, dq_ref, dk_ref, dv_ref, dhg_ref, du_ref, g_ref, w_ref, gq_ref, gk_ref, mavg_ref,
             dx_ref, dw_ref, dg_ref, dgq_ref, dgk_ref):
        i = pl.program_id(0)
        xx = x_ref[...]
        g = g_ref[...]
        mavg_v = mavg_ref[...]
        h, vjp_n = jax.vjp(_rms, xx, g)
        p = _nn(h, w_ref[...])
        _, vjp_p = jax.vjp(lambda pq, pk, a, b, gq_, gk_: _mixin_post(pq, pk, a, b, gq_, gk_, mavg_v),
                           p[:, 0:512], p[:, 512:1024], p[:, 1536:1792], p[:, 1792:2048], gq_ref[...], gk_ref[...])
        dpq, dpk, da, db, dgq, dgk = vjp_p((dq_ref[...], dk_ref[...], dhg_ref[...]))
        dp = jnp.concatenate([dpq, dpk, dv_ref[...], da, db, du_ref[...]], axis=1)
        dh = _nt(dp, w_ref[...])
        dxn, dg = vjp_n(dh)
        dx_ref[...] = dres_ref[...] + dxn
        _accumulate(i, dw_ref, _tn(h, dp))
        _accumulate(i, dg_ref, dg)
        _accumulate(i, dgq_ref, dgq)
        _accumulate(i, dgk_ref, dgk)

    r = lambda c: _rows(c, tl)
    return pl.pallas_call(
        body, name=f"mix_in_bwd_{tag}", grid=(L // tl,),
        in_specs=[r(D_MODEL), r(D_MODEL), r(512), r(512), r(512), r(256), r(256),
                  _res((1, D_MODEL)), _res(w_in.shape), _res((1, 512)), _res((1, 512)), _res((512, 512))],
        out_specs=[r(D_MODEL), _res(w_in.shape), _res((1, D_MODEL)), _res((1, 512)), _res((1, 512))],
        out_shape=[_sds((L, D_MODEL)), _sds(w_in.shape), _sds((1, D_MODEL)), _sds((1, 512)), _sds((1, 512))],
        compiler_params=_params(1),
    )(x, dres, dq, dk, dv, dhg, du, g_mix, w_in, gq, gk, mavg)


def _sb_tri_consts():
    r = np.arange(2 * SB_TK)[:, None]
    c = np.arange(2 * SB_TK)[None, :]
    same = (r // SB_TK) == (c // SB_TK)
    later = (same & (r > c)).astype(np.float32)
    earlier = (same & (r < c)).astype(np.float32)
    return jnp.asarray(later, BF16), jnp.asarray(earlier, BF16)


def _hilo_dot(a, m):
    hi = a.astype(BF16)
    lo = (a - hi.astype(F32)).astype(BF16)
    return jnp.dot(hi, m, preferred_element_type=F32) + jnp.dot(lo, m, preferred_element_type=F32)


def _sb_block(qb, kb, i, j, masked, ul, ra, rb, lane_a):
    zero = jnp.zeros_like(kb)
    kk2 = jnp.concatenate([jnp.where(lane_a, kb, zero), jnp.where(lane_a, zero, kb)], axis=0)
    z = lax.dot_general(qb, kk2, (((1,), (1,)), ((), ())), preferred_element_type=F32) * (SB_HEAD_DIM ** -0.5)
    e = jnp.exp(-jnp.abs(z))
    lm = -(jnp.maximum(z, 0.0) + jnp.log(1.0 + e))
    ls = z + lm
    valid = None
    if masked:
        row = lax.broadcasted_iota(jnp.int32, (SB_TQ, 2 * SB_TK), 0)
        col = lax.broadcasted_iota(jnp.int32, (SB_TQ, 2 * SB_TK), 1) & (SB_TK - 1)
        valid = (j * SB_TK + col) < (i * SB_TQ + row)
        lm = jnp.where(valid, lm, 0.0)
    lb = _hilo_dot(lm, ul)
    rfull = jnp.concatenate([jnp.broadcast_to(ra, (SB_TQ, SB_TK)), jnp.broadcast_to(rb, (SB_TQ, SB_TK))], axis=1)
    w = jnp.exp(ls + lb + rfull)
    if masked:
        w = jnp.where(valid, w, 0.0)
    return kk2, ls, lm, lb, w, valid


def _two_heads(blk, lane_a):
    zero = jnp.zeros_like(blk)
    return jnp.concatenate([jnp.where(lane_a, blk, zero), jnp.where(lane_a, zero, blk)], axis=0)


def _sb_fwd(q, k, v, ul, tag):
    L = q.shape[0]
    nq = L // SB_TQ
    per = SB_TQ // SB_TK

    def body(q_ref, k_ref, v_ref, ul_ref, o_ref, rs_ref):
        i = pl.program_id(1)
        qb = q_ref[...]
        ulv = ul_ref[...]
        lane_a = lax.broadcasted_iota(jnp.int32, (1, LANES), 1) < SB_HEAD_DIM
        lane_q = lax.broadcasted_iota(jnp.int32, (SB_TQ, LANES), 1)

        def step(j, carry, masked):
            ra, rb, acc, rsave = carry
            off = pl.multiple_of(j * SB_TK, SB_TK)
            kb = k_ref[pl.ds(off, SB_TK), :]
            vb = v_ref[pl.ds(off, SB_TK), :]
            _, _, lm, lb, w, _ = _sb_block(qb, kb, i, j, masked, ulv, ra, rb, lane_a)
            acc = acc + jnp.dot(w.astype(BF16), _two_heads(vb, lane_a), preferred_element_type=F32)
            rsave = jnp.where(lane_q == j, ra, jnp.where(lane_q == j + SB_HEAD_DIM, rb, rsave))
            ra = ra + lb[:, 0:1] + lm[:, 0:1]
            rb = rb + lb[:, SB_TK:SB_TK + 1] + lm[:, SB_TK:SB_TK + 1]
            return ra, rb, acc, rsave

        carry = (jnp.zeros((SB_TQ, 1), F32), jnp.zeros((SB_TQ, 1), F32),
                 jnp.zeros((SB_TQ, LANES), F32), jnp.zeros((SB_TQ, LANES), F32))
        for d in range(per):
            carry = step(i * per + (per - 1 - d), carry, True)
        carry = lax.fori_loop(0, i * per, lambda jj, c: step(i * per - 1 - jj, c, False), carry)
        o_ref[...] = carry[2]
        rs_ref[...] = carry[3]

    qspec = pl.BlockSpec((SB_TQ, LANES), lambda g, i: (i, g))
    kspec = pl.BlockSpec((L, LANES), lambda g, i: (0, g))
    return pl.pallas_call(
        body, name=f"sb_fwd_{tag}", grid=(SB_WIDTH // LANES, nq),
        in_specs=[qspec, kspec, kspec, pl.BlockSpec((2 * SB_TK, 2 * SB_TK), lambda g, i: (0, 0))],
        out_specs=[qspec, qspec],
        out_shape=[_sds((L, SB_WIDTH)), _sds((L, SB_WIDTH))],
        compiler_params=_params(2),
    )(q, k, v, ul)


def _sb_bwd(q, k, v, rsave, do, ul, ue, tag):
    L = q.shape[0]
    nq = L // SB_TQ
    per = SB_TQ // SB_TK

    def body(q_ref, k_ref, v_ref, rs_ref, do_ref, ul_ref, ue_ref, dq_ref, dk_ref, dv_ref):
        i = pl.program_id(1)

        @pl.when(i == 0)
        def _():
            dk_ref[...] = jnp.zeros_like(dk_ref)
            dv_ref[...] = jnp.zeros_like(dv_ref)

        qb = q_ref[...]
        dob = do_ref[...].astype(BF16)
        rsv = rs_ref[...]
        ulv = ul_ref[...]
        uev = ue_ref[...]
        lane_a = lax.broadcasted_iota(jnp.int32, (1, LANES), 1) < SB_HEAD_DIM
        lane_q = lax.broadcasted_iota(jnp.int32, (SB_TQ, LANES), 1)

        def step(j, carry, masked):
            pa, pb, dq = carry
            off = pl.multiple_of(j * SB_TK, SB_TK)
            kb = k_ref[pl.ds(off, SB_TK), :]
            vb = v_ref[pl.ds(off, SB_TK), :]
            ra = jnp.sum(jnp.where(lane_q == j, rsv, 0.0), axis=1, keepdims=True)
            rb = jnp.sum(jnp.where(lane_q == j + SB_HEAD_DIM, rsv, 0.0), axis=1, keepdims=True)
            kk2, ls, _, _, w, valid = _sb_block(qb, kb, i, j, masked, ulv, ra, rb, lane_a)
            dw = lax.dot_general(dob, _two_heads(vb, lane_a), (((1,), (1,)), ((), ())), preferred_element_type=F32)
            gg = w * dw
            cb = _hilo_dot(gg, uev)
            pfull = jnp.concatenate([jnp.broadcast_to(pa, (SB_TQ, SB_TK)), jnp.broadcast_to(pb, (SB_TQ, SB_TK))], axis=1)
            beta = jnp.exp(ls)
            dz = (gg * (1.0 - beta) - beta * (cb + pfull)) * (SB_HEAD_DIM ** -0.5)
            if masked:
                dz = jnp.where(valid, dz, 0.0)
            dzb = dz.astype(BF16)
            dq = dq + jnp.dot(dzb, kk2, preferred_element_type=F32)
            dk2 = lax.dot_general(dzb, qb, (((0,), (0,)), ((), ())), preferred_element_type=F32)
            dv2 = lax.dot_general(w.astype(BF16), dob, (((0,), (0,)), ((), ())), preferred_element_type=F32)
            dk_ref[pl.ds(off, SB_TK), :] += jnp.where(lane_a, dk2[0:SB_TK], dk2[SB_TK:2 * SB_TK])
            dv_ref[pl.ds(off, SB_TK), :] += jnp.where(lane_a, dv2[0:SB_TK], dv2[SB_TK:2 * SB_TK])
            pa = pa + cb[:, SB_TK - 1:SB_TK] + gg[:, SB_TK - 1:SB_TK]
            pb = pb + cb[:, 2 * SB_TK - 1:2 * SB_TK] + gg[:, 2 * SB_TK - 1:2 * SB_TK]
            return pa, pb, dq

        carry = (jnp.zeros((SB_TQ, 1), F32), jnp.zeros((SB_TQ, 1), F32), jnp.zeros((SB_TQ, LANES), F32))
        carry = lax.fori_loop(0, i * per, lambda j, c: step(j, c, False), carry)
        for d in range(per):
            carry = step(i * per + d, carry, True)
        dq_ref[...] = carry[2]

    qspec = pl.BlockSpec((SB_TQ, LANES), lambda g, i: (i, g))
    kspec = pl.BlockSpec((L, LANES), lambda g, i: (0, g))
    cspec = pl.BlockSpec((2 * SB_TK, 2 * SB_TK), lambda g, i: (0, 0))
    return pl.pallas_call(
        body, name=f"sb_bwd_{tag}", grid=(SB_WIDTH // LANES, nq),
        in_specs=[qspec, kspec, kspec, qspec, qspec, cspec, cspec],
        out_specs=[qspec, kspec, kspec],
        out_shape=[_sds((L, SB_WIDTH)), _sds((L, SB_WIDTH)), _sds((L, SB_WIDTH))],
        compiler_params=_params(2),
    )(q, k, v, rsave, do, ul, ue)


def _conv_fwd(hp, w, tag):
    L = hp.shape[0] - CONV_HALO
    win_rows = CONV_SUB + CONV_HALO

    def body(hp_ref, w_ref, o_ref):
        i = pl.program_id(0)

        def sub(s, _):
            t0 = pl.multiple_of(i * TL + s * CONV_SUB, CONV_SUB)
            win = hp_ref[pl.ds(t0, win_rows), :]
            acc = jnp.zeros((CONV_SUB, CONV_CH), F32)
            for kk in range(CONV_WIDTH):
                sh = CONV_WIDTH - 1 - kk
                r = win if sh == 0 else pltpu.roll(win, sh, 0)
                acc = acc + w_ref[kk:kk + 1, :] * r[CONV_HALO:, :]
            o_ref[pl.ds(pl.multiple_of(s * CONV_SUB, CONV_SUB), CONV_SUB), :] = acc
            return 0

        lax.fori_loop(0, TL // CONV_SUB, sub, 0)

    return pl.pallas_call(
        body, name=f"conv_fwd_{tag}", grid=(L // TL,),
        in_specs=[_res(hp.shape), _res(w.shape)],
        out_specs=_rows(CONV_CH),
        out_shape=_sds((L, CONV_CH)),
        compiler_params=_params(1),
    )(hp, w)


def _conv_bwd(dpad, hp, w, tag):
    L = hp.shape[0] - CONV_HALO
    win_rows = CONV_SUB + CONV_HALO
    n_tiles = L // TL

    def body(dp_ref, hp_ref, w_ref, dh_ref, dw_ref, acc_ref):
        i = pl.program_id(0)

        @pl.when(i == 0)
        def _():
            acc_ref[...] = jnp.zeros_like(acc_ref)

        def sub(s, _):
            t0 = pl.multiple_of(i * TL + s * CONV_SUB, CONV_SUB)
            wd = dp_ref[pl.ds(t0, win_rows), :]
            wh = hp_ref[pl.ds(t0, win_rows), :]
            dy = wd[0:CONV_SUB, :]
            acc = jnp.zeros((CONV_SUB, CONV_CH), F32)
            for kk in range(CONV_WIDTH):
                sh = CONV_WIDTH - 1 - kk
                rd = wd if sh == 0 else pltpu.roll(wd, win_rows - sh, 0)
                acc = acc + w_ref[kk:kk + 1, :] * rd[0:CONV_SUB, :]
                rh = wh if sh == 0 else pltpu.roll(wh, sh, 0)
                prod = dy * rh[CONV_HALO:, :]
                part = prod[0:SUBLANES]
                for m in range(1, CONV_SUB // SUBLANES):
                    part = part + prod[m * SUBLANES:(m + 1) * SUBLANES]
                acc_ref[kk] += part
            dh_ref[pl.ds(pl.multiple_of(s * CONV_SUB, CONV_SUB), CONV_SUB), :] = acc
            return 0

        lax.fori_loop(0, TL // CONV_SUB, sub, 0)

        @pl.when(i == n_tiles - 1)
        def _():
            for kk in range(CONV_WIDTH):
                dw_ref[kk:kk + 1, :] = jnp.sum(acc_ref[kk], axis=0, keepdims=True)

    return pl.pallas_call(
        body, name=f"conv_bwd_{tag}", grid=(n_tiles,),
        in_specs=[_res(dpad.shape), _res(hp.shape), _res(w.shape)],
        out_specs=[_rows(CONV_CH), _res(w.shape)],
        out_shape=[_sds((L, CONV_CH)), _sds(w.shape)],
        scratch_shapes=[pltpu.VMEM((CONV_WIDTH, SUBLANES, CONV_CH), F32)],
        compiler_params=_params(1),
    )(dpad, hp, w)


def _ssm_mask():
    r = np.arange(SSM_CH)[:, None] // SSM_GROUP
    c = np.arange(SSM_LANES)[None, :] // SSM_STATE
    return jnp.asarray((r == c).astype(np.float32))


def _ssm_discretize(lr, li, ldt, brt, bit):
    dt = jnp.exp(ldt)
    mag = jnp.exp(lr * dt)
    ar = mag * jnp.cos(li * dt)
    ai = mag * jnp.sin(li * dt)
    den = lr * lr + li * li
    fr = ((ar - 1.0) * lr + ai * li) / den
    fi = (ai * lr - (ar - 1.0) * li) / den
    return ar, ai, fr * brt - fi * bit, fr * bit + fi * brt


def _block_diag(rows16, mask):
    return jnp.where(mask > 0.5, jnp.tile(rows16, (SSM_GROUPS, 1)), 0.0)


def _block_diag_t(full, mask):
    m = jnp.where(mask > 0.5, full, 0.0)
    out = m[0:SSM_GROUP]
    for g in range(1, SSM_GROUPS):
        out = out + m[g * SSM_GROUP:(g + 1) * SSM_GROUP]
    return out


def _ssm_prep(lr, li, ldt, brt, bit, crt, cit, mask, tag):
    def body(lr_ref, li_ref, ldt_ref, brt_ref, bit_ref, crt_ref, cit_ref, m_ref,
             ar_ref, ai_ref, bbr_ref, bbi_ref, cbr_ref, cbi_ref):
        ar, ai, bbr, bbi = _ssm_discretize(lr_ref[...], li_ref[...], ldt_ref[...], brt_ref[...], bit_ref[...])
        m = m_ref[...]
        ar_ref[...] = ar
        ai_ref[...] = ai
        bbr_ref[...] = _block_diag(bbr, m).astype(BF16)
        bbi_ref[...] = _block_diag(bbi, m).astype(BF16)
        cbr_ref[...] = _block_diag(crt_ref[...], m).astype(BF16)
        cbi_ref[...] = _block_diag(cit_ref[...], m).astype(BF16)

    row = _sds((1, SSM_LANES))
    blk = _sds((SSM_CH, SSM_LANES), BF16)
    return pl.pallas_call(body, name=f"ssm_prep_{tag}", out_shape=[row, row, blk, blk, blk, blk])(
        lr, li, ldt, brt, bit, crt, cit, mask)


def _ssm_prep_bwd(lr, li, ldt, brt, bit, mask, dar, dai, dbbr, dbbi, dcbr, dcbi, tag):
    def body(lr_ref, li_ref, ldt_ref, brt_ref, bit_ref, m_ref, dar_ref, dai_ref, dbbr_ref, dbbi_ref, dcbr_ref,
             dcbi_ref, dlr_ref, dli_ref, dldt_ref, dbrt_ref, dbit_ref, dcrt_ref, dcit_ref):
        m = m_ref[...]
        _, vjp = jax.vjp(_ssm_discretize, lr_ref[...], li_ref[...], ldt_ref[...], brt_ref[...], bit_ref[...])
        dlr, dli, dldt, dbrt, dbit = vjp((dar_ref[...], dai_ref[...], _block_diag_t(dbbr_ref[...], m),
                                          _block_diag_t(dbbi_ref[...], m)))
        dlr_ref[...] = dlr
        dli_ref[...] = dli
        dldt_ref[...] = dldt
        dbrt_ref[...] = dbrt
        dbit_ref[...] = dbit
        dcrt_ref[...] = _block_diag_t(dcbr_ref[...], m)
        dcit_ref[...] = _block_diag_t(dcbi_ref[...], m)

    row = _sds((1, SSM_LANES))
    r16 = _sds((SSM_GROUP, SSM_LANES))
    return pl.pallas_call(body, name=f"ssm_prep_bwd_{tag}", out_shape=[row, row, row, r16, r16, r16, r16])(
        lr, li, ldt, brt, bit, mask, dar, dai, dbbr, dbbi, dcbr, dcbi)


def _complex_scan(br, bi, ar, ai, reverse):
    n = br.shape[0]
    row = lax.broadcasted_iota(jnp.int32, (n, 1), 0)
    xr, xi, pr, pi = br, bi, ar, ai
    d = 1
    while d < n:
        if reverse:
            sr, si, keep = pltpu.roll(xr, n - d, 0), pltpu.roll(xi, n - d, 0), row < n - d
        else:
            sr, si, keep = pltpu.roll(xr, d, 0), pltpu.roll(xi, d, 0), row >= d
        sr = jnp.where(keep, sr, 0.0)
        si = jnp.where(keep, si, 0.0)
        xr, xi = xr + pr * sr - pi * si, xi + pr * si + pi * sr
        pr, pi = pr * pr - pi * pi, 2.0 * pr * pi
        d *= 2
    return xr, xi


def _ssm_fwd(u, ar, ai, bbr, bbi, cbr, cbi, dvec, tag):
    L = u.shape[0]
    T = SSM_T

    def body(u_ref, ar_ref, ai_ref, bbr_ref, bbi_ref, cbr_ref, cbi_ref, d_ref, y_ref, xr_ref, xi_ref, cr_ref, ci_ref):
        i = pl.program_id(0)

        @pl.when(i == 0)
        def _():
            cr_ref[...] = jnp.zeros_like(cr_ref)
            ci_ref[...] = jnp.zeros_like(ci_ref)

        uu = u_ref[...]
        a_r, a_i = ar_ref[...], ai_ref[...]
        c_r, c_i = cr_ref[...], ci_ref[...]
        first = lax.broadcasted_iota(jnp.int32, (T, 1), 0) == 0
        bur = _nn(uu, bbr_ref[...]) + jnp.where(first, a_r * c_r - a_i * c_i, 0.0)
        bui = _nn(uu, bbi_ref[...]) + jnp.where(first, a_r * c_i + a_i * c_r, 0.0)
        xr, xi = _complex_scan(bur, bui, a_r, a_i, False)
        cr_ref[...] = xr[T - 1:T, :]
        ci_ref[...] = xi[T - 1:T, :]
        xr_ref[...] = xr
        xi_ref[...] = xi
        y_ref[...] = _nt(xr, cbr_ref[...]) - _nt(xi, cbi_ref[...]) + d_ref[...] * uu

    blk = _res((SSM_CH, SSM_LANES))
    row = _res((1, SSM_LANES))
    return pl.pallas_call(
        body, name=f"ssm_fwd_{tag}", grid=(L // T,),
        in_specs=[_rows(SSM_CH, T), row, row, blk, blk, blk, blk, _res((1, SSM_CH))],
        out_specs=[_rows(SSM_CH, T), _rows(SSM_LANES, T), _rows(SSM_LANES, T)],
        out_shape=[_sds((L, SSM_CH)), _sds((L, SSM_LANES)), _sds((L, SSM_LANES))],
        scratch_shapes=[pltpu.VMEM((1, SSM_LANES), F32), pltpu.VMEM((1, SSM_LANES), F32)],
        compiler_params=_params(1),
    )(u, ar, ai, bbr, bbi, cbr, cbi, dvec)


def _ssm_bwd(dy, u, xr, xi, ar, ai, bbr, bbi, cbr, cbi, dvec, tag):
    L = u.shape[0]
    T = SSM_T
    nc = L // T

    def body(dy_ref, u_ref, xr_ref, xi_ref, pr_ref, pi_ref, ar_ref, ai_ref, bbr_ref, bbi_ref, cbr_ref, cbi_ref, d_ref,
             du_ref, dar_ref, dai_ref, dbbr_ref, dbbi_ref, dcbr_ref, dcbi_ref, dd_ref, gr_ref, gi_ref):
        i = pl.program_id(0)

        @pl.when(i == 0)
        def _():
            gr_ref[...] = jnp.zeros_like(gr_ref)
            gi_ref[...] = jnp.zeros_like(gi_ref)

        dyy = dy_ref[...]
        uu = u_ref[...]
        xr, xi = xr_ref[...], xi_ref[...]
        a_r, a_i = ar_ref[...], ai_ref[...]
        g_r, g_i = gr_ref[...], gi_ref[...]
        row = lax.broadcasted_iota(jnp.int32, (T, 1), 0)
        last = row == T - 1
        inr = _nn(dyy, cbr_ref[...]) + jnp.where(last, a_r * g_r + a_i * g_i, 0.0)
        ini = -_nn(dyy, cbi_ref[...]) + jnp.where(last, a_r * g_i - a_i * g_r, 0.0)
        gr, gi = _complex_scan(inr, ini, a_r, -a_i, True)
        gr_ref[...] = gr[0:1, :]
        gi_ref[...] = gi[0:1, :]
        has_prev = (i < nc - 1).astype(F32)
        pr = pr_ref[SUBLANES - 1:SUBLANES, :] * has_prev
        pi = pi_ref[SUBLANES - 1:SUBLANES, :] * has_prev
        sr = jnp.where(row == 0, pr, pltpu.roll(xr, 1, 0))
        si = jnp.where(row == 0, pi, pltpu.roll(xi, 1, 0))
        _accumulate(i, dar_ref, jnp.sum(gr * sr + gi * si, axis=0, keepdims=True))
        _accumulate(i, dai_ref, jnp.sum(gi * sr - gr * si, axis=0, keepdims=True))
        _accumulate(i, dbbr_ref, _tn(uu, gr))
        _accumulate(i, dbbi_ref, _tn(uu, gi))
        _accumulate(i, dcbr_ref, _tn(dyy, xr))
        _accumulate(i, dcbi_ref, -_tn(dyy, xi))
        _accumulate(i, dd_ref, jnp.sum(dyy * uu, axis=0, keepdims=True))
        du_ref[...] = _nt(gr, bbr_ref[...]) + _nt(gi, bbi_ref[...]) + dyy * d_ref[...]

    rev = lambda cols: pl.BlockSpec((T, cols), lambda i: (nc - 1 - i, 0))
    prev = pl.BlockSpec((SUBLANES, SSM_LANES), lambda i: (jnp.maximum((nc - 1 - i) * (T // SUBLANES) - 1, 0), 0))
    blk = _res((SSM_CH, SSM_LANES))
    row = _res((1, SSM_LANES))
    return pl.pallas_call(
        body, name=f"ssm_bwd_{tag}", grid=(nc,),
        in_specs=[rev(SSM_CH), rev(SSM_CH), rev(SSM_LANES), rev(SSM_LANES), prev, prev, row, row, blk, blk, blk, blk,
                  _res((1, SSM_CH))],
        out_specs=[rev(SSM_CH), row, row, blk, blk, blk, blk, _res((1, SSM_CH))],
        out_shape=[_sds((L, SSM_CH)), _sds((1, SSM_LANES)), _sds((1, SSM_LANES)), _sds((SSM_CH, SSM_LANES)),
                   _sds((SSM_CH, SSM_LANES)), _sds((SSM_CH, SSM_LANES)), _sds((SSM_CH, SSM_LANES)), _sds((1, SSM_CH))],
        scratch_shapes=[pltpu.VMEM((1, SSM_LANES), F32), pltpu.VMEM((1, SSM_LANES), F32)],
        compiler_params=_params(1),
    )(dy, u, xr, xi, xr, xi, ar, ai, bbr, bbi, cbr, cbi, dvec)


def _conv_post(hc, dw_b, ln_g, ln_b):
    return _silu(_layer_norm(hc + dw_b, ln_g, ln_b))


def _branch_mix(o_sb, o_conv, t, g1, g2, g3):
    o_ssm = t[:, 0:SSM_CH] * _sigmoid(t[:, SSM_CH:2 * SSM_CH])
    return jnp.concatenate([_rms(o_sb, g1), _rms(o_conv, g2), _rms(o_ssm, g3)], axis=1)


def _branch_mix_split(o_sb, o_conv, ta, tb, g1, g2, g3):
    return jnp.concatenate([_rms(o_sb, g1), _rms(o_conv, g2), _rms(ta * _sigmoid(tb), g3)], axis=1)


def _mix_out_fwd(x, o_sb, hc, y, dw_b, ln_g, ln_b, pw2, glu_w, g1, g2, g3, w_out, tag):
    L = x.shape[0]

    def body(x_ref, o_ref, hc_ref, y_ref, dwb_ref, lng_ref, lnb_ref, pw2_ref, glu_ref, g1_ref, g2_ref, g3_ref, wo_ref,
             out_ref):
        c1 = _conv_post(hc_ref[...], dwb_ref[...], lng_ref[...], lnb_ref[...])
        o_conv = _nn(c1, pw2_ref[...])
        t = _nn(y_ref[...], glu_ref[...])
        mixed = _branch_mix(o_ref[...], o_conv, t, g1_ref[...], g2_ref[...], g3_ref[...])
        out_ref[...] = x_ref[...] + _nn(mixed, wo_ref[...])

    v256 = _res((1, 256))
    return pl.pallas_call(
        body, name=f"mix_out_fwd_{tag}", grid=(L // TL,),
        in_specs=[_rows(D_MODEL), _rows(512), _rows(256), _rows(256), v256, v256, v256, _res(pw2.shape),
                  _res(glu_w.shape), _res((1, 512)), v256, v256, _res(w_out.shape)],
        out_specs=_rows(D_MODEL),
        out_shape=_sds((L, D_MODEL)),
        compiler_params=_params(1),
    )(x, o_sb, hc, y, dw_b, ln_g, ln_b, pw2, glu_w, g1, g2, g3, w_out)


def _mix_out_bwd(dx1, o_sb, hc, y, dw_b, ln_g, ln_b, pw2, glu_w, g1, g2, g3, w_out, tag):
    L = dx1.shape[0]

    def body(dx_ref, o_ref, hc_ref, y_ref, dwb_ref, lng_ref, lnb_ref, pw2_ref, glu_ref, g1_ref, g2_ref, g3_ref, wo_ref,
             do_ref, dhc_ref, dy_ref, ddwb_ref, dlng_ref, dlnb_ref, dpw2_ref, dglu_ref, dg1_ref, dg2_ref, dg3_ref,
             dwo_ref):
        i = pl.program_id(0)
        dxx = dx_ref[...]
        yy = y_ref[...]
        c1, vjp1 = jax.vjp(_conv_post, hc_ref[...], dwb_ref[...], lng_ref[...], lnb_ref[...])
        o_conv = _nn(c1, pw2_ref[...])
        t = _nn(yy, glu_ref[...])
        mixed, vjp2 = jax.vjp(_branch_mix_split, o_ref[...], o_conv, t[:, 0:SSM_CH], t[:, SSM_CH:2 * SSM_CH],
                              g1_ref[...], g2_ref[...], g3_ref[...])
        dmixed = _nt(dxx, wo_ref[...])
        do_sb, do_conv, dta, dtb, dg1, dg2, dg3 = vjp2(dmixed)
        dt = jnp.concatenate([dta, dtb], axis=1)
        dc1 = _nt(do_conv, pw2_ref[...])
        dhc, ddwb, dlng, dlnb = vjp1(dc1)
        do_ref[...] = do_sb
        dhc_ref[...] = dhc
        dy_ref[...] = _nt(dt, glu_ref[...])
        _accumulate(i, dwo_ref, _tn(mixed, dxx))
        _accumulate(i, dglu_ref, _tn(yy, dt))
        _accumulate(i, dpw2_ref, _tn(c1, do_conv))
        _accumulate(i, ddwb_ref, ddwb)
        _accumulate(i, dlng_ref, dlng)
        _accumulate(i, dlnb_ref, dlnb)
        _accumulate(i, dg1_ref, dg1)
        _accumulate(i, dg2_ref, dg2)
        _accumulate(i, dg3_ref, dg3)

    v256 = _res((1, 256))
    return pl.pallas_call(
        body, name=f"mix_out_bwd_{tag}", grid=(L // TL,),
        in_specs=[_rows(D_MODEL), _rows(512), _rows(256), _rows(256), v256, v256, v256, _res(pw2.shape),
                  _res(glu_w.shape), _res((1, 512)), v256, v256, _res(w_out.shape)],
        out_specs=[_rows(512), _rows(256), _rows(256), v256, v256, v256, _res(pw2.shape), _res(glu_w.shape),
                   _res((1, 512)), v256, v256, _res(w_out.shape)],
        out_shape=[_sds((L, 512)), _sds((L, 256)), _sds((L, 256)), _sds((1, 256)), _sds((1, 256)), _sds((1, 256)),
                   _sds(pw2.shape), _sds(glu_w.shape), _sds((1, 512)), _sds((1, 256)), _sds((1, 256)), _sds(w_out.shape)],
        compiler_params=_params(1),
    )(dx1, o_sb, hc, y, dw_b, ln_g, ln_b, pw2, glu_w, g1, g2, g3, w_out)


def _xa_heads_norm(kk, kg):
    return jnp.concatenate([_rms(kk[:, h * XA_HEAD_DIM:(h + 1) * XA_HEAD_DIM], kg) for h in range(XA_HEADS)], axis=1)


def _xa_mem_fwd(mem, g_mem, wk, wv, kg, tag):
    def body(m_ref, g_ref, wk_ref, wv_ref, kg_ref, k_ref, v_ref):
        hm = _rms(m_ref[...], g_ref[...])
        k_ref[...] = _xa_heads_norm(_nn(hm, wk_ref[...]), kg_ref[...])
        v_ref[...] = _nn(hm, wv_ref[...])

    return pl.pallas_call(body, name=f"xa_mem_fwd_{tag}", out_shape=[_sds(mem.shape), _sds(mem.shape)],
                          compiler_params=_params(0))(mem, g_mem, wk, wv, kg)


def _xa_mem_bwd(mem, g_mem, wk, wv, kg, dkx, dvx, tag):
    def body(m_ref, g_ref, wk_ref, wv_ref, kg_ref, dk_ref, dv_ref, dwk_ref, dwv_ref, dg_ref, dkg_ref):
        hm, vjp_n = jax.vjp(_rms, m_ref[...], g_ref[...])
        kk = _nn(hm, wk_ref[...])
        dvv = dv_ref[...]
        dkg = jnp.zeros((1, XA_HEAD_DIM), F32)
        parts = []
        for h in range(XA_HEADS):
            sl = slice(h * XA_HEAD_DIM, (h + 1) * XA_HEAD_DIM)
            _, vjp_h = jax.vjp(_rms, kk[:, sl], kg_ref[...])
            dkh, dgh = vjp_h(dk_ref[:, sl])
            parts.append(dkh)
            dkg = dkg + dgh
        dkk = jnp.concatenate(parts, axis=1)
        dwk_ref[...] = _tn(hm, dkk)
        dwv_ref[...] = _tn(hm, dvv)
        dhm = _nt(dkk, wk_ref[...]) + _nt(dvv, wv_ref[...])
        _, dg = vjp_n(dhm)
        dg_ref[...] = dg
        dkg_ref[...] = dkg

    return pl.pallas_call(
        body, name=f"xa_mem_bwd_{tag}",
        out_shape=[_sds(wk.shape), _sds(wv.shape), _sds((1, D_MODEL)), _sds((1, XA_HEAD_DIM))],
        compiler_params=_params(0))(mem, g_mem, wk, wv, kg, dkx, dvx)


def _xa_fwd(x1, kx, vx, g_xa, wq, qg, wo, tag):
    L = x1.shape[0]

    def body(x_ref, k_ref, v_ref, g_ref, wq_ref, qg_ref, wo_ref, out_ref):
        xx = x_ref[...]
        qp = _nn(_rms(xx, g_ref[...]), wq_ref[...])
        outs = []
        for h in range(XA_HEADS):
            sl = slice(h * XA_HEAD_DIM, (h + 1) * XA_HEAD_DIM)
            qh = _rms(qp[:, sl], qg_ref[...])
            s = _nt(qh, k_ref[:, sl]) * (XA_HEAD_DIM ** -0.5)
            s = s - jnp.max(s, axis=-1, keepdims=True)
            e = jnp.exp(s)
            p = e / jnp.sum(e, axis=-1, keepdims=True)
            outs.append(_nn(p, v_ref[:, sl]))
        out_ref[...] = xx + _nn(jnp.concatenate(outs, axis=1), wo_ref[...])

    return pl.pallas_call(
        body, name=f"xa_fwd_{tag}", grid=(L // TL,),
        in_specs=[_rows(D_MODEL), _res(kx.shape), _res(vx.shape), _res((1, D_MODEL)), _res(wq.shape),
                  _res((1, XA_HEAD_DIM)), _res(wo.shape)],
        out_specs=_rows(D_MODEL),
        out_shape=_sds((L, D_MODEL)),
        compiler_params=_params(1),
    )(x1, kx, vx, g_xa, wq, qg, wo)


def _xa_bwd(x1, dx2, kx, vx, g_xa, wq, qg, wo, tag):
    L = x1.shape[0]
    tl = 256

    def body(x_ref, dx_ref, k_ref, v_ref, g_ref, wq_ref, qg_ref, wo_ref,
             dx1_ref, dk_ref, dv_ref, dwq_ref, dwo_ref, dg_ref, dqg_ref):
        i = pl.program_id(0)
        xx = x_ref[...]
        dxx = dx_ref[...]
        hx, vjp_n = jax.vjp(_rms, xx, g_ref[...])
        qp = _nn(hx, wq_ref[...])
        do = _nt(dxx, wo_ref[...])
        outs, dqps, dks, dvs = [], [], [], []
        dqg = jnp.zeros((1, XA_HEAD_DIM), F32)
        for h in range(XA_HEADS):
            sl = slice(h * XA_HEAD_DIM, (h + 1) * XA_HEAD_DIM)
            kh, vh = k_ref[:, sl], v_ref[:, sl]
            qh, vjp_q = jax.vjp(_rms, qp[:, sl], qg_ref[...])
            s = _nt(qh, kh) * (XA_HEAD_DIM ** -0.5)
            s = s - jnp.max(s, axis=-1, keepdims=True)
            e = jnp.exp(s)
            p = e / jnp.sum(e, axis=-1, keepdims=True)
            outs.append(_nn(p, vh))
            doh = do[:, sl]
            dp = _nt(doh, vh)
            dvs.append(_tn(p, doh))
            ds = p * (dp - jnp.sum(dp * p, axis=-1, keepdims=True)) * (XA_HEAD_DIM ** -0.5)
            dks.append(_tn(ds, qh))
            dqh, dgh = vjp_q(_nn(ds, kh))
            dqps.append(dqh)
            dqg = dqg + dgh
        o = jnp.concatenate(outs, axis=1)
        dqp = jnp.concatenate(dqps, axis=1)
        dxn, dg = vjp_n(_nt(dqp, wq_ref[...]))
        dx1_ref[...] = dxx + dxn
        _accumulate(i, dk_ref, jnp.concatenate(dks, axis=1))
        _accumulate(i, dv_ref, jnp.concatenate(dvs, axis=1))
        _accumulate(i, dwq_ref, _tn(hx, dqp))
        _accumulate(i, dwo_ref, _tn(o, dxx))
        _accumulate(i, dg_ref, dg)
        _accumulate(i, dqg_ref, dqg)

    r = lambda c: _rows(c, tl)
    return pl.pallas_call(
        body, name=f"xa_bwd_{tag}", grid=(L // tl,),
        in_specs=[r(D_MODEL), r(D_MODEL), _res(kx.shape), _res(vx.shape), _res((1, D_MODEL)), _res(wq.shape),
                  _res((1, XA_HEAD_DIM)), _res(wo.shape)],
        out_specs=[r(D_MODEL), _res(kx.shape), _res(vx.shape), _res(wq.shape), _res(wo.shape), _res((1, D_MODEL)),
                   _res((1, XA_HEAD_DIM))],
        out_shape=[_sds((L, D_MODEL)), _sds(kx.shape), _sds(vx.shape), _sds(wq.shape), _sds(wo.shape),
                   _sds((1, D_MODEL)), _sds((1, XA_HEAD_DIM))],
        compiler_params=_params(1),
    )(x1, dx2, kx, vx, g_xa, wq, qg, wo)


def _swiglu(gate, up):
    return _silu(gate) * up


def _ffn_fwd(x2, g, w_in, w_out, tag):
    L = x2.shape[0]
    tl = 256

    def body(x_ref, g_ref, wi_ref, wo_ref, out_ref):
        xx = x_ref[...]
        gu = _nn(_rms(xx, g_ref[...]), wi_ref[...])
        act = _swiglu(gu[:, 0:FFN_HIDDEN], gu[:, FFN_HIDDEN:2 * FFN_HIDDEN])
        out_ref[...] = xx + _nn(act, wo_ref[...])

    return pl.pallas_call(
        body, name=f"ffn_fwd_{tag}", grid=(L // tl,),
        in_specs=[_rows(D_MODEL, tl), _res((1, D_MODEL)), _res(w_in.shape), _res(w_out.shape)],
        out_specs=_rows(D_MODEL, tl),
        out_shape=_sds((L, D_MODEL)),
        compiler_params=_params(1, 56),
    )(x2, g, w_in, w_out)


def _ffn_bwd(x2, dx3, g, w_in, w_out, tag):
    L = x2.shape[0]
    tl = 256

    def body(x_ref, dx_ref, g_ref, wi_ref, wo_ref, dx2_ref, dgu_ref, act_ref, hf_ref, dg_ref):
        i = pl.program_id(0)
        xx = x_ref[...]
        dxx = dx_ref[...]
        hf, vjp_n = jax.vjp(_rms, xx, g_ref[...])
        gu = _nn(hf, wi_ref[...])
        act, vjp_a = jax.vjp(_swiglu, gu[:, 0:FFN_HIDDEN], gu[:, FFN_HIDDEN:2 * FFN_HIDDEN])
        dgate, dup = vjp_a(_nt(dxx, wo_ref[...]))
        dgu = jnp.concatenate([dgate, dup], axis=1).astype(BF16)
        dxn, dg = vjp_n(_nt(dgu, wi_ref[...]))
        dx2_ref[...] = dxx + dxn
        dgu_ref[...] = dgu
        act_ref[...] = act.astype(BF16)
        hf_ref[...] = hf.astype(BF16)
        _accumulate(i, dg_ref, dg)

    r = lambda c: _rows(c, tl)
    return pl.pallas_call(
        body, name=f"ffn_bwd_{tag}", grid=(L // tl,),
        in_specs=[r(D_MODEL), r(D_MODEL), _res((1, D_MODEL)), _res(w_in.shape), _res(w_out.shape)],
        out_specs=[r(D_MODEL), r(2 * FFN_HIDDEN), r(FFN_HIDDEN), r(D_MODEL), _res((1, D_MODEL))],
        out_shape=[_sds((L, D_MODEL)), _sds((L, 2 * FFN_HIDDEN), BF16), _sds((L, FFN_HIDDEN), BF16),
                   _sds((L, D_MODEL), BF16), _sds((1, D_MODEL))],
        compiler_params=_params(1, 56),
    )(x2, dx3, g, w_in, w_out)


def _matmul_tn(a, b, tm, tn, tag):
    L, M = a.shape
    N = b.shape[1]
    tk = 512

    def body(a_ref, b_ref, o_ref):
        _accumulate(pl.program_id(2), o_ref, _tn(a_ref[...], b_ref[...]))

    return pl.pallas_call(
        body, name=f"matmul_tn_{tag}", grid=(M // tm, N // tn, L // tk),
        in_specs=[pl.BlockSpec((tk, tm), lambda m, n, k: (k, m)), pl.BlockSpec((tk, tn), lambda m, n, k: (k, n))],
        out_specs=pl.BlockSpec((tm, tn), lambda m, n, k: (m, n)),
        out_shape=_sds((M, N)),
        compiler_params=_params(3),
    )(a, b)


def _loss_head(y, tgt):
    L = y.shape[0]
    n_tiles = L // TL

    def body(y_ref, t_ref, loss_ref, dy_ref, acc_ref):
        i = pl.program_id(0)
        diff = y_ref[...] - t_ref[...]
        dy_ref[...] = diff * (1.0 / D_MODEL)
        _accumulate(i, acc_ref, jnp.sum(diff * diff, axis=0, keepdims=True))

        @pl.when(i == n_tiles - 1)
        def _():
            loss_ref[...] = jnp.sum(acc_ref[...], axis=1, keepdims=True) * (0.5 / D_MODEL)

    return pl.pallas_call(
        body, name="loss_head", grid=(n_tiles,),
        in_specs=[_rows(D_MODEL), _rows(D_MODEL)],
        out_specs=[_res((1, 1)), _rows(D_MODEL)],
        out_shape=[_sds((1, 1)), _sds((L, D_MODEL))],
        scratch_shapes=[pltpu.VMEM((1, D_MODEL), F32)],
        compiler_params=_params(1),
    )(y, tgt)


def _row(v):
    return v.reshape(1, -1)


def _layer_consts():
    r = np.arange(SB_WIDTH)
    mavg = ((r[:, None] // SB_HEAD_DIM) == (r[None, :] // SB_HEAD_DIM)).astype(np.float32) / SB_HEAD_DIM
    ul, ue = _sb_tri_consts()
    return dict(mavg=jnp.asarray(mavg), ul=ul, ue=ue, mask=_ssm_mask())


def _ssm_rows(P):
    lanes = lambda a: a.reshape(1, SSM_LANES)
    return dict(
        lr=lanes(P['ssm_lam_re']), li=lanes(P['ssm_lam_im']),
        ldt=lanes(jnp.repeat(P['ssm_log_dt'], SSM_STATE)),
        brt=P['ssm_b_re'].transpose(2, 0, 1).reshape(SSM_GROUP, SSM_LANES),
        bit=P['ssm_b_im'].transpose(2, 0, 1).reshape(SSM_GROUP, SSM_LANES),
        crt=P['ssm_c_re'].transpose(1, 0, 2).reshape(SSM_GROUP, SSM_LANES),
        cit=P['ssm_c_im'].transpose(1, 0, 2).reshape(SSM_GROUP, SSM_LANES))


def _layer_fwd(x, mem, P, C, tag):
    gq = _row(jnp.tile(P['sb_q_norm_g'], SB_WIDTH // SB_HEAD_DIM))
    gk = _row(jnp.tile(P['sb_k_norm_g'], SB_WIDTH // SB_HEAD_DIM))
    q, k, v, hg, u = _mix_in_fwd(x, _row(P['norm_mix_g']), P['w_in'], gq, gk, C['mavg'], tag)
    o_sb, rsave = _sb_fwd(q, k, v, C['ul'], tag)
    hp = jnp.pad(hg, ((CONV_HALO, 0), (0, 0)))
    hc = _conv_fwd(hp, P['conv_dw_w'], tag)
    S = _ssm_rows(P)
    ar, ai, bbr, bbi, cbr, cbi = _ssm_prep(S['lr'], S['li'], S['ldt'], S['brt'], S['bit'], S['crt'], S['cit'],
                                           C['mask'], tag)
    y, xr, xi = _ssm_fwd(u, ar, ai, bbr, bbi, cbr, cbi, _row(P['ssm_d']), tag)
    gb = P['branch_norm_g']
    x1 = _mix_out_fwd(x, o_sb, hc, y, _row(P['conv_dw_b']), _row(P['conv_ln_g']), _row(P['conv_ln_b']),
                      P['conv_pw2_w'], P['ssm_glu_w'], _row(gb[0:512]), _row(gb[512:768]), _row(gb[768:1024]),
                      P['w_out'], tag)
    kx, vx = _xa_mem_fwd(mem, _row(P['norm_mem_g']), P['xa_wk'], P['xa_wv'], _row(P['xa_k_norm_g']), tag)
    x2 = _xa_fwd(x1, kx, vx, _row(P['norm_xa_g']), P['xa_wq'], _row(P['xa_q_norm_g']), P['xa_wo'], tag)
    x3 = _ffn_fwd(x2, _row(P['norm_ffn_g']), P['ffn_w_in'], P['ffn_w_out'], tag)
    saved = dict(x=x, q=q, k=k, v=v, rsave=rsave, o_sb=o_sb, hp=hp, hc=hc, u=u, y=y, xr=xr, xi=xi, x1=x1, x2=x2,
                 kx=kx, vx=vx, gq=gq, gk=gk, S=S, ssm=(ar, ai, bbr, bbi, cbr, cbi))
    return x3, saved


def _layer_bwd(dx3, mem, P, C, sv, tag):
    G = {}
    dx2, dgu, act, hf, dg = _ffn_bwd(sv['x2'], dx3, _row(P['norm_ffn_g']), P['ffn_w_in'], P['ffn_w_out'], tag)
    G['norm_ffn_g'] = dg.reshape(-1)
    G['ffn_w_in'] = _matmul_tn(hf, dgu, 512, 2 * FFN_HIDDEN // 4, "ffn_in_" + tag)
    G['ffn_w_out'] = _matmul_tn(act, dx3, FFN_HIDDEN // 2, 512, "ffn_out_" + tag)
    dx1, dkx, dvx, dwq, dwo, dg, dqg = _xa_bwd(sv['x1'], dx2, sv['kx'], sv['vx'], _row(P['norm_xa_g']), P['xa_wq'],
                                               _row(P['xa_q_norm_g']), P['xa_wo'], tag)
    G['xa_wq'], G['xa_wo'], G['norm_xa_g'], G['xa_q_norm_g'] = dwq, dwo, dg.reshape(-1), dqg.reshape(-1)
    dwk, dwv, dg, dkg = _xa_mem_bwd(mem, _row(P['norm_mem_g']), P['xa_wk'], P['xa_wv'], _row(P['xa_k_norm_g']),
                                    dkx, dvx, tag)
    G['xa_wk'], G['xa_wv'], G['norm_mem_g'], G['xa_k_norm_g'] = dwk, dwv, dg.reshape(-1), dkg.reshape(-1)
    gb = P['branch_norm_g']
    (do_sb, dhc, dy, ddwb, dlng, dlnb, dpw2, dglu, dg1, dg2, dg3, dwout) = _mix_out_bwd(
        dx1, sv['o_sb'], sv['hc'], sv['y'], _row(P['conv_dw_b']), _row(P['conv_ln_g']), _row(P['conv_ln_b']),
        P['conv_pw2_w'], P['ssm_glu_w'], _row(gb[0:512]), _row(gb[512:768]), _row(gb[768:1024]), P['w_out'], tag)
    G['conv_dw_b'], G['conv_ln_g'], G['conv_ln_b'] = ddwb.reshape(-1), dlng.reshape(-1), dlnb.reshape(-1)
    G['conv_pw2_w'], G['ssm_glu_w'], G['w_out'] = dpw2, dglu, dwout
    G['branch_norm_g'] = jnp.concatenate([dg1.reshape(-1), dg2.reshape(-1), dg3.reshape(-1)])
    ar, ai, bbr, bbi, cbr, cbi = sv['ssm']
    du, dar, dai, dbbr, dbbi, dcbr, dcbi, dd = _ssm_bwd(dy, sv['u'], sv['xr'], sv['xi'], ar, ai, bbr, bbi, cbr, cbi,
                                                        _row(P['ssm_d']), tag)
    S = sv['S']
    dlr, dli, dldt, dbrt, dbit, dcrt, dcit = _ssm_prep_bwd(S['lr'], S['li'], S['ldt'], S['brt'], S['bit'], C['mask'],
                                                           dar, dai, dbbr, dbbi, dcbr, dcbi, tag)
    G['ssm_lam_re'] = dlr.reshape(SSM_GROUPS, SSM_STATE)
    G['ssm_lam_im'] = dli.reshape(SSM_GROUPS, SSM_STATE)
    G['ssm_log_dt'] = dldt.reshape(SSM_GROUPS, SSM_STATE).sum(axis=1)
    G['ssm_b_re'] = dbrt.reshape(SSM_GROUP, SSM_GROUPS, SSM_STATE).transpose(1, 2, 0)
    G['ssm_b_im'] = dbit.reshape(SSM_GROUP, SSM_GROUPS, SSM_STATE).transpose(1, 2, 0)
    G['ssm_c_re'] = dcrt.reshape(SSM_GROUP, SSM_GROUPS, SSM_STATE).transpose(1, 0, 2)
    G['ssm_c_im'] = dcit.reshape(SSM_GROUP, SSM_GROUPS, SSM_STATE).transpose(1, 0, 2)
    G['ssm_d'] = dd.reshape(-1)
    dpad = jnp.pad(dhc, ((0, CONV_HALO), (0, 0)))
    dhg, ddww = _conv_bwd(dpad, sv['hp'], P['conv_dw_w'], tag)
    G['conv_dw_w'] = ddww
    dq, dk, dv = _sb_bwd(sv['q'], sv['k'], sv['v'], sv['rsave'], do_sb, C['ul'], C['ue'], tag)
    dx, dwin, dg, dgq, dgk = _mix_in_bwd(sv['x'], dx1, dq, dk, dv, dhg, du, _row(P['norm_mix_g']), P['w_in'],
                                         sv['gq'], sv['gk'], C['mavg'], tag)
    G['w_in'], G['norm_mix_g'] = dwin, dg.reshape(-1)
    G['sb_q_norm_g'] = dgq.reshape(SB_WIDTH // SB_HEAD_DIM, SB_HEAD_DIM).sum(axis=0)
    G['sb_k_norm_g'] = dgk.reshape(SB_WIDTH // SB_HEAD_DIM, SB_HEAD_DIM).sum(axis=0)
    return dx, G


def _local_step(x, mem, tgt, W):
    C = _layer_consts()
    layers = [{n: W[n][l] for n in WEIGHT_NAMES} for l in range(DEPTH)]
    saved = []
    h = x
    for l in range(DEPTH):
        h, sv = _layer_fwd(h, mem, layers[l], C, f"l{l}")
        saved.append(sv)
    loss, dh = _loss_head(h, tgt)
    grads = [None] * DEPTH
    for l in reversed(range(DEPTH)):
        dh, grads[l] = _layer_bwd(dh, mem, layers[l], C, saved[l], f"l{l}")
    G = {n: jnp.stack([grads[l][n] for l in range(DEPTH)]) for n in WEIGHT_NAMES}
    return loss, dh, G


def _my_index():
    return lax.axis_index("x") * 4 + lax.axis_index("y") * 2 + lax.axis_index("c")


def _peer(k):
    x, y, c = lax.axis_index("x"), lax.axis_index("y"), lax.axis_index("c")
    return (x ^ ((k >> 2) & 1), y ^ ((k >> 1) & 1), c ^ (k & 1))


def _peer_index(k):
    return _my_index() ^ k


HBM = pl.BlockSpec(memory_space=pltpu.HBM)


def _all_gather(x, name):
    def body(x_ref, out_ref, send_sems, recv_sems, local_sem):
        me = _my_index()
        mine = pltpu.make_async_copy(x_ref, out_ref.at[me], local_sem)
        mine.start()
        sends = []
        for k in range(1, N_DEV):
            cp = pltpu.make_async_remote_copy(src_ref=x_ref, dst_ref=out_ref.at[me], send_sem=send_sems.at[k],
                                              recv_sem=recv_sems.at[k], device_id=_peer(k),
                                              device_id_type=pl.DeviceIdType.MESH)
            cp.start()
            sends.append(cp)
        for k in range(1, N_DEV):
            pltpu.make_async_remote_copy(src_ref=x_ref, dst_ref=out_ref.at[_peer_index(k)], send_sem=send_sems.at[k],
                                         recv_sem=recv_sems.at[k], device_id=_peer(k),
                                         device_id_type=pl.DeviceIdType.MESH).wait_recv()
        for cp in sends:
            cp.wait_send()
        mine.wait()

    return pl.pallas_call(
        body, name=name, in_specs=[HBM], out_specs=HBM,
        out_shape=_sds((N_DEV,) + x.shape, x.dtype),
        scratch_shapes=[pltpu.SemaphoreType.DMA((N_DEV,)), pltpu.SemaphoreType.DMA((N_DEV,)), pltpu.SemaphoreType.DMA],
    )(x)


def _grad_exchange(big, small, name):
    def body(big_ref, small_ref, rbig_ref, rsmall_ref, send_sems, recv_sems, local_sems):
        me = _my_index()
        loc_b = pltpu.make_async_copy(big_ref.at[me], rbig_ref.at[me], local_sems.at[0])
        loc_s = pltpu.make_async_copy(small_ref, rsmall_ref.at[me], local_sems.at[1])
        loc_b.start()
        loc_s.start()
        sends = []
        for k in range(1, N_DEV):
            cb = pltpu.make_async_remote_copy(src_ref=big_ref.at[_peer_index(k)], dst_ref=rbig_ref.at[me],
                                              send_sem=send_sems.at[k], recv_sem=recv_sems.at[k], device_id=_peer(k),
                                              device_id_type=pl.DeviceIdType.MESH)
            cs = pltpu.make_async_remote_copy(src_ref=small_ref, dst_ref=rsmall_ref.at[me],
                                              send_sem=send_sems.at[N_DEV + k], recv_sem=recv_sems.at[N_DEV + k],
                                              device_id=_peer(k), device_id_type=pl.DeviceIdType.MESH)
            cb.start()
            cs.start()
            sends += [cb, cs]
        for k in range(1, N_DEV):
            pk = _peer_index(k)
            pltpu.make_async_remote_copy(src_ref=big_ref.at[pk], dst_ref=rbig_ref.at[pk], send_sem=send_sems.at[k],
                                         recv_sem=recv_sems.at[k], device_id=_peer(k),
                                         device_id_type=pl.DeviceIdType.MESH).wait_recv()
            pltpu.make_async_remote_copy(src_ref=small_ref, dst_ref=rsmall_ref.at[pk], send_sem=send_sems.at[N_DEV + k],
                                         recv_sem=recv_sems.at[N_DEV + k], device_id=_peer(k),
                                         device_id_type=pl.DeviceIdType.MESH).wait_recv()
        for cp in sends:
            cp.wait_send()
        loc_b.wait()
        loc_s.wait()

    return pl.pallas_call(
        body, name=name, in_specs=[HBM, HBM], out_specs=[HBM, HBM],
        out_shape=[_sds(big.shape, big.dtype), _sds((N_DEV,) + small.shape, small.dtype)],
        scratch_shapes=[pltpu.SemaphoreType.DMA((2 * N_DEV,)), pltpu.SemaphoreType.DMA((2 * N_DEV,)),
                        pltpu.SemaphoreType.DMA((2,))],
    )(big, small)


def _adamw(recv, w, m, v, tile, name):
    n_slots, R, _ = recv.shape
    c1 = 1.0 / (1.0 - ADAM_B1 ** ADAM_STEP)
    c2 = 1.0 / (1.0 - ADAM_B2 ** ADAM_STEP)

    def body(r_ref, w_ref, m_ref, v_ref, g_ref, d_ref, nm_ref, nv_ref):
        g = r_ref[0].astype(F32)
        for s in range(1, n_slots):
            g = g + r_ref[s].astype(F32)
        nm = ADAM_B1 * m_ref[...] + (1.0 - ADAM_B1) * g
        nv = ADAM_B2 * v_ref[...] + (1.0 - ADAM_B2) * (g * g)
        g_ref[...] = g
        nm_ref[...] = nm
        nv_ref[...] = nv
        d_ref[...] = -ADAM_LR * ((nm * c1) / (jnp.sqrt(nv * c2) + ADAM_EPS) + ADAM_WD * w_ref[...])

    rows = pl.BlockSpec((tile, LANES), lambda i: (i, 0))
    out = _sds((R, LANES))
    return pl.pallas_call(
        body, name=name, grid=(R // tile,),
        in_specs=[pl.BlockSpec((n_slots, tile, LANES), lambda i: (0, i, 0)), rows, rows, rows],
        out_specs=[rows, rows, rows, rows],
        out_shape=[out, out, out, out],
        compiler_params=_params(1),
    )(recv, w, m, v)


SEG = SUBLANES * LANES


def _pad_to(n, mult):
    return -(-n // mult) * mult


def _pack(arrays, dtype, row_mult):
    parts = []
    for a in arrays:
        flat = a.reshape(-1).astype(dtype)
        parts.append(jnp.pad(flat, (0, _pad_to(flat.shape[0], SEG) - flat.shape[0])))
    flat = jnp.concatenate(parts)
    total = _pad_to(flat.shape[0], row_mult * LANES)
    return jnp.pad(flat, (0, total - flat.shape[0])).reshape(-1, LANES)


def _unpack(buf, shapes):
    lead = buf.shape[:-2]
    flat = buf.reshape(lead + (-1,))
    out, off = [], 0
    for shp in shapes:
        n = int(np.prod(shp))
        out.append(flat[..., off:off + n].reshape(lead + tuple(shp)))
        off += _pad_to(n, SEG)
    return out


def _shards_first(full, axis):
    shp = full.shape
    split = full.reshape(shp[:axis] + (N_DEV, shp[axis] // N_DEV) + shp[axis + 1:])
    return jnp.moveaxis(split, axis, 0)


def _from_shards(sh, axis):
    moved = jnp.moveaxis(sh, 0, axis)
    shp = moved.shape
    return moved.reshape(shp[:axis] + (shp[axis] * shp[axis + 1],) + shp[axis + 2:])


BIG_TILE = 512
SMALL_TILE = 256


def kernel(x, mem, norm_mix_g, w_in, sb_q_norm_g, sb_k_norm_g, conv_dw_w, conv_dw_b, conv_ln_g, conv_ln_b, conv_pw2_w, ssm_lam_re, ssm_lam_im, ssm_log_dt, ssm_b_re, ssm_b_im, ssm_c_re, ssm_c_im, ssm_d, ssm_glu_w, branch_norm_g, w_out, norm_xa_g, norm_mem_g, xa_wq, xa_wk, xa_wv, xa_q_norm_g, xa_k_norm_g, xa_wo, norm_ffn_g, ffn_w_in, ffn_w_out, loss_target, m_norm_mix_g, m_w_in, m_sb_q_norm_g, m_sb_k_norm_g, m_conv_dw_w, m_conv_dw_b, m_conv_ln_g, m_conv_ln_b, m_conv_pw2_w, m_ssm_lam_re, m_ssm_lam_im, m_ssm_log_dt, m_ssm_b_re, m_ssm_b_im, m_ssm_c_re, m_ssm_c_im, m_ssm_d, m_ssm_glu_w, m_branch_norm_g, m_w_out, m_norm_xa_g, m_norm_mem_g, m_xa_wq, m_xa_wk, m_xa_wv, m_xa_q_norm_g, m_xa_k_norm_g, m_xa_wo, m_norm_ffn_g, m_ffn_w_in, m_ffn_w_out, v_norm_mix_g, v_w_in, v_sb_q_norm_g, v_sb_k_norm_g, v_conv_dw_w, v_conv_dw_b, v_conv_ln_g, v_conv_ln_b, v_conv_pw2_w, v_ssm_lam_re, v_ssm_lam_im, v_ssm_log_dt, v_ssm_b_re, v_ssm_b_im, v_ssm_c_re, v_ssm_c_im, v_ssm_d, v_ssm_glu_w, v_branch_norm_g, v_w_out, v_norm_xa_g, v_norm_mem_g, v_xa_wq, v_xa_wk, v_xa_wv, v_xa_q_norm_g, v_xa_k_norm_g, v_xa_wo, v_norm_ffn_g, v_ffn_w_in, v_ffn_w_out):
    args = locals()
    w_loc = {n: args[n] for n in WEIGHT_NAMES}
    m_loc = {n: args["m_" + n] for n in WEIGHT_NAMES}
    v_loc = {n: args["v_" + n] for n in WEIGHT_NAMES}
    big_shapes = [w_loc[n].shape for n in BIG_NAMES]
    small_shapes = [w_loc[n].shape for n in SMALL_NAMES]

    gathered = _all_gather(_pack([w_loc[n] for n in BIG_NAMES], BF16, BIG_TILE), "weights_all_gather")
    W = {n: w_loc[n] for n in SMALL_NAMES}
    for n, sh in zip(BIG_NAMES, _unpack(gathered, big_shapes)):
        W[n] = _from_shards(sh, SHARD_AXIS[n])

    loss_part, grad_x, G = _local_step(x[0], mem[0], loss_target[0], W)
    loss = lax.psum(loss_part[0, 0], MESH_AXES)

    big_parts = [_shards_first(G[n], SHARD_AXIS[n]) for n in BIG_NAMES]
    send_big = jnp.stack([_pack([p[d] for p in big_parts], BF16, BIG_TILE) for d in range(N_DEV)])
    send_small = _pack([G[n] for n in SMALL_NAMES], F32, SMALL_TILE)
    recv_big, recv_small = _grad_exchange(send_big, send_small, "grad_exchange")
    pk = lambda d, names, tile: _pack([d[n] for n in names], F32, tile)
    outs_big = _adamw(recv_big, pk(w_loc, BIG_NAMES, BIG_TILE), pk(m_loc, BIG_NAMES, BIG_TILE),
                      pk(v_loc, BIG_NAMES, BIG_TILE), BIG_TILE, "adamw_sharded")
    outs_small = _adamw(recv_small, pk(w_loc, SMALL_NAMES, SMALL_TILE), pk(m_loc, SMALL_NAMES, SMALL_TILE),
                        pk(v_loc, SMALL_NAMES, SMALL_TILE), SMALL_TILE, "adamw_replicated")
    result = [{}, {}, {}, {}]
    for kind in range(4):
        for n, a in zip(BIG_NAMES, _unpack(outs_big[kind], big_shapes)):
            result[kind][n] = a
        for n, a in zip(SMALL_NAMES, _unpack(outs_small[kind], small_shapes)):
            result[kind][n] = a
    return (loss, grad_x[None], *[result[0][n] for n in WEIGHT_NAMES], *[result[1][n] for n in WEIGHT_NAMES],
            *[result[2][n] for n in WEIGHT_NAMES], *[result[3][n] for n in WEIGHT_NAMES])
```

```python
import functools

import numpy as np
import jax
import jax.numpy as jnp
from jax import lax
from jax.experimental import pallas as pl
from jax.experimental.pallas import tpu as pltpu

F32 = jnp.float32
BF16 = jnp.bfloat16
EPS = 1e-6
D_MODEL = 1024
DEPTH = 2
N_DEV = 8
SB_WIDTH = 512
SB_HEAD_DIM = 64
CONV_CH = 256
CONV_WIDTH = 31
SSM_CH = 256
SSM_GROUP = 16
SSM_GROUPS = 16
SSM_STATE = 64
SSM_LANES = SSM_GROUPS * SSM_STATE
XA_HEADS = 4
XA_HEAD_DIM = 256
FFN_HIDDEN = 2816
ADAM_LR = 0.001
ADAM_B1 = 0.9
ADAM_B2 = 0.999
ADAM_EPS = 1e-08
ADAM_WD = 0.01
ADAM_STEP = 10

LANES = 128
SUBLANES = 8
TL = 512
SB_TQ = 256
SB_TK = 128
SB_PAIRS = 2
SB_SCALE = SB_HEAD_DIM ** -0.5
SSM_T = 256
CONV_HALO = 32
CONV_SUB = 64
VMEM_MB = 48

MESH_AXES = ("x", "y", "c")
WEIGHT_NAMES = ['norm_mix_g', 'w_in', 'sb_q_norm_g', 'sb_k_norm_g', 'conv_dw_w', 'conv_dw_b', 'conv_ln_g',
                'conv_ln_b', 'conv_pw2_w', 'ssm_lam_re', 'ssm_lam_im', 'ssm_log_dt', 'ssm_b_re', 'ssm_b_im',
                'ssm_c_re', 'ssm_c_im', 'ssm_d', 'ssm_glu_w', 'branch_norm_g', 'w_out', 'norm_xa_g',
                'norm_mem_g', 'xa_wq', 'xa_wk', 'xa_wv', 'xa_q_norm_g', 'xa_k_norm_g', 'xa_wo', 'norm_ffn_g',
                'ffn_w_in', 'ffn_w_out']
SHARD_AXIS = {'w_in': 2, 'conv_dw_w': 2, 'conv_pw2_w': 1, 'ssm_glu_w': 2, 'w_out': 1, 'xa_wq': 1, 'xa_wk': 1,
              'xa_wv': 1, 'xa_wo': 1, 'ffn_w_in': 2, 'ffn_w_out': 1}
BIG_NAMES = [n for n in WEIGHT_NAMES if n in SHARD_AXIS]
SMALL_NAMES = [n for n in WEIGHT_NAMES if n not in SHARD_AXIS]


def _nn(a, b):
    return jnp.dot(a.astype(BF16), b.astype(BF16), preferred_element_type=F32)


def _nt(a, b):
    return lax.dot_general(a.astype(BF16), b.astype(BF16), (((1,), (1,)), ((), ())), preferred_element_type=F32)


def _tn(a, b):
    return lax.dot_general(a.astype(BF16), b.astype(BF16), (((0,), (0,)), ((), ())), preferred_element_type=F32)


def _rms(x, g):
    return x * lax.rsqrt(jnp.mean(x * x, axis=-1, keepdims=True) + EPS) * g


def _sigmoid(x):
    return 1.0 / (1.0 + jnp.exp(-x))


def _silu(x):
    return x * _sigmoid(x)


def _layer_norm(x, g, b):
    mu = jnp.mean(x, axis=-1, keepdims=True)
    xc = x - mu
    var = jnp.mean(xc * xc, axis=-1, keepdims=True)
    return xc * lax.rsqrt(var + EPS) * g + b


def _head_rms64(p, g, mavg):
    ms = jnp.dot(p * p, mavg, preferred_element_type=F32)
    return p * lax.rsqrt(ms + EPS) * g


def _params(n_grid, vmem_mb=VMEM_MB):
    return pltpu.CompilerParams(dimension_semantics=("arbitrary",) * n_grid, vmem_limit_bytes=vmem_mb << 20)


def _rows(cols, tl=TL):
    return pl.BlockSpec((tl, cols), lambda i: (i, 0))


def _res(shape):
    nd = len(shape)
    return pl.BlockSpec(tuple(shape), lambda *_: (0,) * nd)


def _sds(shape, dtype=F32):
    return jax.ShapeDtypeStruct(tuple(shape), dtype)


def _accumulate(i, ref, val):
    @pl.when(i == 0)
    def _():
        ref[...] = val

    @pl.when(i > 0)
    def _():
        ref[...] += val


def _mixin_post(pq, pk, a, b, gq, gk, mavg):
    return _head_rms64(pq, gq, mavg), _head_rms64(pk, gk, mavg), a * _sigmoid(b)


def _mix_in_fwd(x, g_mix, w_in, gq, gk, mavg, tag):
    L = x.shape[0]

    def body(x_ref, g_ref, w_ref, gq_ref, gk_ref, mavg_ref, q_ref, k_ref, v_ref, hg_ref, u_ref):
        h = _rms(x_ref[...], g_ref[...])
        p = _nt(h, w_ref[...])
        q, k, hg = _mixin_post(p[:, 0:512], p[:, 512:1024], p[:, 1536:1792], p[:, 1792:2048],
                               gq_ref[...], gk_ref[...], mavg_ref[...])
        q_ref[...] = (q * SB_SCALE).astype(BF16)
        k_ref[...] = k.astype(BF16)
        v_ref[...] = p[:, 1024:1536].astype(BF16)
        hg_ref[...] = hg
        u_ref[...] = p[:, 2048:2304]

    return pl.pallas_call(
        body, name=f"mix_in_fwd_{tag}", grid=(L // TL,),
        in_specs=[_rows(D_MODEL), _res((1, D_MODEL)), _res(w_in.shape), _res((1, 512)), _res((1, 512)), _res((512, 512))],
        out_specs=[_rows(512), _rows(512), _rows(512), _rows(256), _rows(256)],
        out_shape=[_sds((L, 512), BF16), _sds((L, 512), BF16), _sds((L, 512), BF16), _sds((L, 256)), _sds((L, 256))],
        compiler_params=_params(1),
    )(x, g_mix, w_in, gq, gk, mavg)


def _mix_in_bwd(x, dres, dq, dk, dv, dhg, du, g_mix, w_in, gq, gk, mavg, tag):
    L = x.shape[0]
    tl = 256

    def body(x_ref, dres_ref, dq_ref, dk_ref, dv_ref, dhg_ref, du_ref, g_ref, w_ref, gq_ref, gk_ref, mavg_ref,
             dx_ref, dw_ref, dg_ref, dgq_ref, dgk_ref):
        i = pl.program_id(0)
        xx = x_ref[...]
        g = g_ref[...]
        mavg_v = mavg_ref[...]
        h, vjp_n = jax.vjp(_rms, xx, g)
        p = _nt(h, w_ref[...])
        _, vjp_p = jax.vjp(lambda pq, pk, a, b, gq_, gk_: _mixin_post(pq, pk, a, b, gq_, gk_, mavg_v),
                           p[:, 0:512], p[:, 512:1024], p[:, 1536:1792], p[:, 1792:2048], gq_ref[...], gk_ref[...])
        dpq, dpk, da, db, dgq, dgk = vjp_p((dq_ref[...], dk_ref[...], dhg_ref[...]))
        dp = jnp.concatenate([dpq, dpk, dv_ref[...], da, db, du_ref[...]], axis=1)
        dh = _nn(dp, w_ref[...])
        dxn, dg = vjp_n(dh)
        dx_ref[...] = dres_ref[...] + dxn
        _accumulate(i, dw_ref, _tn(dp, h))
        _accumulate(i, dg_ref, dg)
        _accumulate(i, dgq_ref, dgq)
        _accumulate(i, dgk_ref, dgk)

    r = lambda c: _rows(c, tl)
    return pl.pallas_call(
        body, name=f"mix_in_bwd_{tag}", grid=(L // tl,),
        in_specs=[r(D_MODEL), r(D_MODEL), r(512), r(512), r(512), r(256), r(256),
                  _res((1, D_MODEL)), _res(w_in.shape), _res((1, 512)), _res((1, 512)), _res((512, 512))],
        out_specs=[r(D_MODEL), _res(w_in.shape), _res((1, D_MODEL)), _res((1, 512)), _res((1, 512))],
        out_shape=[_sds((L, D_MODEL)), _sds(w_in.shape), _sds((1, D_MODEL)), _sds((1, 512)), _sds((1, 512))],
        compiler_params=_params(1),
    )(x, dres, dq, dk, dv, dhg, du, g_mix, w_in, gq, gk, mavg)


def _sb_tri_consts():
    r = np.arange(2 * SB_TK)[:, None]
    c = np.arange(2 * SB_TK)[None, :]
    same = (r // SB_TK) == (c // SB_TK)
    later = (same & (r > c)).astype(np.float32)
    earlier = (same & (r < c)).astype(np.float32)
    return jnp.asarray(later, BF16), jnp.asarray(earlier, BF16)


def _two_heads(blk, lane_a):
    zero = jnp.zeros_like(blk)
    return jnp.concatenate([jnp.where(lane_a, blk, zero), jnp.where(lane_a, zero, blk)], axis=0)


def _sb_logs(z, i, j, masked):
    e = jnp.exp(-jnp.abs(z))
    lm = -(jnp.maximum(z, 0.0) + jnp.log(1.0 + e))
    ls = z + lm
    valid = None
    if masked:
        row = lax.broadcasted_iota(jnp.int32, (SB_TQ, 2 * SB_TK), 0)
        col = lax.broadcasted_iota(jnp.int32, (SB_TQ, 2 * SB_TK), 1) & (SB_TK - 1)
        valid = (j * SB_TK + col) < (i * SB_TQ + row)
        lm = jnp.where(valid, lm, 0.0)
    hi = lm.astype(BF16)
    lo = (lm - hi.astype(F32)).astype(BF16)
    return lm, ls, hi, lo, valid


def _lane_halves(a, b):
    return jnp.concatenate([jnp.broadcast_to(a, (SB_TQ, SB_TK)), jnp.broadcast_to(b, (SB_TQ, SB_TK))], axis=1)


def _dot(a, b):
    return jnp.dot(a, b, preferred_element_type=F32)


def _dot_nt(a, b):
    return lax.dot_general(a, b, (((1,), (1,)), ((), ())), preferred_element_type=F32)


def _dot_tn(a, b):
    return lax.dot_general(a, b, (((0,), (0,)), ((), ())), preferred_element_type=F32)


def _sb_fwd(q, k, v, ul, tag):
    L = q.shape[0]
    nq = L // SB_TQ
    per = SB_TQ // SB_TK
    wid = SB_PAIRS * LANES

    def body(q_ref, k_ref, v_ref, ul_ref, o_ref, rs_ref):
        i = pl.program_id(1)
        ulv = ul_ref[...]
        lane_a = lax.broadcasted_iota(jnp.int32, (1, LANES), 1) < SB_HEAD_DIM
        lane_q = lax.broadcasted_iota(jnp.int32, (SB_TQ, LANES), 1)
        cols = [slice(p * LANES, (p + 1) * LANES) for p in range(SB_PAIRS)]
        qbs = [q_ref[:, c] for c in cols]

        def double_step(jhi, carry, masked):
            chains = [dict(p=p, j=jhi - d) for d in range(2) for p in range(SB_PAIRS)]
            for c in chains:
                c['off'] = pl.multiple_of(c['j'] * SB_TK, SB_TK)
                kb = k_ref[pl.ds(c['off'], SB_TK), cols[c['p']]]
                c['z'] = _dot_nt(qbs[c['p']], _two_heads(kb, lane_a))
            for c in chains:
                c['lm'], c['ls'], c['hi'], c['lo'], c['valid'] = _sb_logs(c.pop('z'), i, c['j'], masked)
            for c in chains:
                c['lb'] = _dot(c.pop('hi'), ulv) + _dot(c.pop('lo'), ulv)
            state = [list(s) for s in carry]
            for c in chains:
                ra, rb, _, rsave = state[c['p']]
                w = jnp.exp(c['ls'] + c['lb'] + _lane_halves(ra, rb))
                if masked:
                    w = jnp.where(c['valid'], w, 0.0)
                c['w'] = w.astype(BF16)
                lb, lm = c['lb'], c['lm']
                state[c['p']][3] = jnp.where(lane_q == c['j'], ra, jnp.where(lane_q == c['j'] + SB_HEAD_DIM, rb, rsave))
                state[c['p']][0] = ra + lb[:, 0:1] + lm[:, 0:1]
                state[c['p']][1] = rb + lb[:, SB_TK:SB_TK + 1] + lm[:, SB_TK:SB_TK + 1]
            for c in chains:
                vb = v_ref[pl.ds(c['off'], SB_TK), cols[c['p']]]
                state[c['p']][2] = state[c['p']][2] + _dot(c['w'], _two_heads(vb, lane_a))
            return tuple(tuple(s) for s in state)

        assert per == 2
        carry = tuple((jnp.zeros((SB_TQ, 1), F32), jnp.zeros((SB_TQ, 1), F32),
                       jnp.zeros((SB_TQ, LANES), F32), jnp.zeros((SB_TQ, LANES), F32)) for _ in range(SB_PAIRS))
        carry = double_step(i * per + 1, carry, True)
        carry = lax.fori_loop(0, i, lambda jj, c: double_step(i * per - 1 - 2 * jj, c, False), carry)
        o_ref[...] = jnp.concatenate([c[2] for c in carry], axis=1)
        rs_ref[...] = jnp.concatenate([c[3] for c in carry], axis=1)

    qspec = pl.BlockSpec((SB_TQ, wid), lambda g, i: (i, g))
    kspec = pl.BlockSpec((L, wid), lambda g, i: (0, g))
    return pl.pallas_call(
        body, name=f"sb_fwd_{tag}", grid=(SB_WIDTH // wid, nq),
        in_specs=[qspec, kspec, kspec, pl.BlockSpec((2 * SB_TK, 2 * SB_TK), lambda g, i: (0, 0))],
        out_specs=[qspec, qspec],
        out_shape=[_sds((L, SB_WIDTH)), _sds((L, SB_WIDTH))],
        compiler_params=_params(2),
    )(q, k, v, ul)


def _sb_bwd(q, k, v, rsave, do, ul, ue, tag):
    L = q.shape[0]
    nq = L // SB_TQ
    per = SB_TQ // SB_TK
    wid = SB_PAIRS * LANES

    def body(q_ref, k_ref, v_ref, rs_ref, do_ref, ul_ref, ue_ref, dq_ref, dk_ref, dv_ref):
        i = pl.program_id(1)

        @pl.when(i == 0)
        def _():
            dk_ref[...] = jnp.zeros_like(dk_ref)
            dv_ref[...] = jnp.zeros_like(dv_ref)

        ulv = ul_ref[...]
        uev = ue_ref[...]
        lane_a = lax.broadcasted_iota(jnp.int32, (1, LANES), 1) < SB_HEAD_DIM
        lane_q = lax.broadcasted_iota(jnp.int32, (SB_TQ, LANES), 1)
        cols = [slice(p * LANES, (p + 1) * LANES) for p in range(SB_PAIRS)]
        qbs = [q_ref[:, c] for c in cols]
        dobs = [do_ref[:, c].astype(BF16) for c in cols]
        rsvs = [rs_ref[:, c] for c in cols]

        def double_step(jlo, carry, masked):
            chains = [dict(p=p, j=jlo + d) for d in range(2) for p in range(SB_PAIRS)]
            for c in chains:
                p = c['p']
                c['off'] = pl.multiple_of(c['j'] * SB_TK, SB_TK)
                c['kk2'] = _two_heads(k_ref[pl.ds(c['off'], SB_TK), cols[p]], lane_a)
                c['z'] = _dot_nt(qbs[p], c['kk2'])
                c['dw'] = _dot_nt(dobs[p], _two_heads(v_ref[pl.ds(c['off'], SB_TK), cols[p]], lane_a))
            for c in chains:
                c['lm'], c['ls'], c['hi'], c['lo'], c['valid'] = _sb_logs(c.pop('z'), i, c['j'], masked)
                c['ra'] = jnp.sum(jnp.where(lane_q == c['j'], rsvs[c['p']], 0.0), axis=1, keepdims=True)
                c['rb'] = jnp.sum(jnp.where(lane_q == c['j'] + SB_HEAD_DIM, rsvs[c['p']], 0.0), axis=1, keepdims=True)
            for c in chains:
                c['lb'] = _dot(c.pop('hi'), ulv) + _dot(c.pop('lo'), ulv)
            for c in chains:
                w = jnp.exp(c['ls'] + c.pop('lb') + _lane_halves(c['ra'], c['rb']))
                if masked:
                    w = jnp.where(c['valid'], w, 0.0)
                c['wb'] = w.astype(BF16)
                gg = w * c.pop('dw')
                c['gg'] = gg
                c['ghi'] = gg.astype(BF16)
                c['glo'] = (gg - c['ghi'].astype(F32)).astype(BF16)
                c['beta'] = jnp.exp(c['ls'])
            for c in chains:
                c['cb'] = _dot(c.pop('ghi'), uev) + _dot(c.pop('glo'), uev)
                c['dv2'] = _dot_tn(c.pop('wb'), dobs[c['p']])
            state = [list(s) for s in carry]
            for c in chains:
                pa, pb, _ = state[c['p']]
                gg, cb, beta = c['gg'], c['cb'], c['beta']
                dz = gg * (1.0 - beta) - beta * (cb + _lane_halves(pa, pb))
                if masked:
                    dz = jnp.where(c['valid'], dz, 0.0)
                c['dzb'] = dz.astype(BF16)
                state[c['p']][0] = pa + cb[:, SB_TK - 1:SB_TK] + gg[:, SB_TK - 1:SB_TK]
                state[c['p']][1] = pb + cb[:, 2 * SB_TK - 1:2 * SB_TK] + gg[:, 2 * SB_TK - 1:2 * SB_TK]
            for c in chains:
                c['dqc'] = _dot(c['dzb'], c['kk2'])
                c['dk2'] = _dot_tn(c['dzb'], qbs[c['p']])
            for c in chains:
                p, dk2, dv2 = c['p'], c['dk2'], c['dv2']
                state[p][2] = state[p][2] + c['dqc']
                dk_ref[pl.ds(c['off'], SB_TK), cols[p]] += jnp.where(lane_a, dk2[0:SB_TK], dk2[SB_TK:2 * SB_TK])
                dv_ref[pl.ds(c['off'], SB_TK), cols[p]] += jnp.where(lane_a, dv2[0:SB_TK], dv2[SB_TK:2 * SB_TK])
            return tuple(tuple(s) for s in state)

        assert per == 2
        carry = tuple((jnp.zeros((SB_TQ, 1), F32), jnp.zeros((SB_TQ, 1), F32), jnp.zeros((SB_TQ, LANES), F32))
                      for _ in range(SB_PAIRS))
        carry = lax.fori_loop(0, i, lambda jj, c: double_step(2 * jj, c, False), carry)
        carry = double_step(i * per, carry, True)
        dq_ref[...] = jnp.concatenate([c[2] for c in carry], axis=1) * SB_SCALE

    qspec = pl.BlockSpec((SB_TQ, wid), lambda g, i: (i, g))
    kspec = pl.BlockSpec((L, wid), lambda g, i: (0, g))
    kin = pl.BlockSpec((L, wid), lambda g, i: (0, g), pipeline_mode=pl.Buffered(1))
    cspec = pl.BlockSpec((2 * SB_TK, 2 * SB_TK), lambda g, i: (0, 0))
    return pl.pallas_call(
        body, name=f"sb_bwd_{tag}", grid=(SB_WIDTH // wid, nq),
        in_specs=[qspec, kin, kin, qspec, qspec, cspec, cspec],
        out_specs=[qspec, kspec, kspec],
        out_shape=[_sds((L, SB_WIDTH)), _sds((L, SB_WIDTH)), _sds((L, SB_WIDTH))],
        compiler_params=_params(2, 58),
    )(q, k, v, rsave, do, ul, ue)


def _conv_fwd(hp, w, tag):
    L = hp.shape[0] - CONV_HALO
    win_rows = CONV_SUB + CONV_HALO

    def body(hp_ref, w_ref, o_ref):
        i = pl.program_id(0)

        def sub(s, _):
            t0 = pl.multiple_of(i * TL + s * CONV_SUB, CONV_SUB)
            win = hp_ref[pl.ds(t0, win_rows), :]
            acc = jnp.zeros((CONV_SUB, CONV_CH), F32)
            for kk in range(CONV_WIDTH):
                sh = CONV_WIDTH - 1 - kk
                r = win if sh == 0 else pltpu.roll(win, sh, 0)
                acc = acc + w_ref[kk:kk + 1, :] * r[CONV_HALO:, :]
            o_ref[pl.ds(pl.multiple_of(s * CONV_SUB, CONV_SUB), CONV_SUB), :] = acc
            return 0

        lax.fori_loop(0, TL // CONV_SUB, sub, 0)

    return pl.pallas_call(
        body, name=f"conv_fwd_{tag}", grid=(L // TL,),
        in_specs=[_res(hp.shape), _res(w.shape)],
        out_specs=_rows(CONV_CH),
        out_shape=_sds((L, CONV_CH)),
        compiler_params=_params(1),
    )(hp, w)


def _conv_bwd(dpad, hp, w, tag):
    L = hp.shape[0] - CONV_HALO
    win_rows = CONV_SUB + CONV_HALO
    n_tiles = L // TL

    def body(dp_ref, hp_ref, w_ref, dh_ref, dw_ref, acc_ref):
        i = pl.program_id(0)

        @pl.when(i == 0)
        def _():
            acc_ref[...] = jnp.zeros_like(acc_ref)

        def sub(s, _):
            t0 = pl.multiple_of(i * TL + s * CONV_SUB, CONV_SUB)
            wd = dp_ref[pl.ds(t0, win_rows), :]
            wh = hp_ref[pl.ds(t0, win_rows), :]
            dy = wd[0:CONV_SUB, :]
            acc = jnp.zeros((CONV_SUB, CONV_CH), F32)
            for kk in range(CONV_WIDTH):
                sh = CONV_WIDTH - 1 - kk
                rd = wd if sh == 0 else pltpu.roll(wd, win_rows - sh, 0)
                acc = acc + w_ref[kk:kk + 1, :] * rd[0:CONV_SUB, :]
                rh = wh if sh == 0 else pltpu.roll(wh, sh, 0)
                prod = dy * rh[CONV_HALO:, :]
                part = prod[0:SUBLANES]
                for m in range(1, CONV_SUB // SUBLANES):
                    part = part + prod[m * SUBLANES:(m + 1) * SUBLANES]
                acc_ref[kk] += part
            dh_ref[pl.ds(pl.multiple_of(s * CONV_SUB, CONV_SUB), CONV_SUB), :] = acc
            return 0

        lax.fori_loop(0, TL // CONV_SUB, sub, 0)

        @pl.when(i == n_tiles - 1)
        def _():
            for kk in range(CONV_WIDTH):
                dw_ref[kk:kk + 1, :] = jnp.sum(acc_ref[kk], axis=0, keepdims=True)

    return pl.pallas_call(
        body, name=f"conv_bwd_{tag}", grid=(n_tiles,),
        in_specs=[_res(dpad.shape), _res(hp.shape), _res(w.shape)],
        out_specs=[_rows(CONV_CH), _res(w.shape)],
        out_shape=[_sds((L, CONV_CH)), _sds(w.shape)],
        scratch_shapes=[pltpu.VMEM((CONV_WIDTH, SUBLANES, CONV_CH), F32)],
        compiler_params=_params(1),
    )(dpad, hp, w)


def _ssm_mask():
    r = np.arange(SSM_CH)[:, None] // SSM_GROUP
    c = np.arange(SSM_LANES)[None, :] // SSM_STATE
    return jnp.asarray((r == c).astype(np.float32))


def _ssm_discretize(lr, li, ldt, brt, bit):
    dt = jnp.exp(ldt)
    mag = jnp.exp(lr * dt)
    ar = mag * jnp.cos(li * dt)
    ai = mag * jnp.sin(li * dt)
    den = lr * lr + li * li
    fr = ((ar - 1.0) * lr + ai * li) / den
    fi = (ai * lr - (ar - 1.0) * li) / den
    return ar, ai, fr * brt - fi * bit, fr * bit + fi * brt


def _block_diag(rows16, mask):
    return jnp.where(mask > 0.5, jnp.tile(rows16, (SSM_GROUPS, 1)), 0.0)


def _block_diag_t(full, mask):
    m = jnp.where(mask > 0.5, full, 0.0)
    out = m[0:SSM_GROUP]
    for g in range(1, SSM_GROUPS):
        out = out + m[g * SSM_GROUP:(g + 1) * SSM_GROUP]
    return out


def _ssm_prep(lr, li, ldt, brt, bit, crt, cit, mask, tag):
    def body(lr_ref, li_ref, ldt_ref, brt_ref, bit_ref, crt_ref, cit_ref, m_ref,
             ar_ref, ai_ref, bbr_ref, bbi_ref, cbr_ref, cbi_ref):
        ar, ai, bbr, bbi = _ssm_discretize(lr_ref[...], li_ref[...], ldt_ref[...], brt_ref[...], bit_ref[...])
        m = m_ref[...]
        ar_ref[...] = ar
        ai_ref[...] = ai
        bbr_ref[...] = _block_diag(bbr, m).astype(BF16)
        bbi_ref[...] = _block_diag(bbi, m).astype(BF16)
        cbr_ref[...] = _block_diag(crt_ref[...], m).astype(BF16)
        cbi_ref[...] = _block_diag(cit_ref[...], m).astype(BF16)

    row = _sds((1, SSM_LANES))
    blk = _sds((SSM_CH, SSM_LANES), BF16)
    return pl.pallas_call(body, name=f"ssm_prep_{tag}", out_shape=[row, row, blk, blk, blk, blk])(
        lr, li, ldt, brt, bit, crt, cit, mask)


def _ssm_prep_bwd(lr, li, ldt, brt, bit, mask, dar, dai, dbbr, dbbi, dcbr, dcbi, tag):
    def body(lr_ref, li_ref, ldt_ref, brt_ref, bit_ref, m_ref, dar_ref, dai_ref, dbbr_ref, dbbi_ref, dcbr_ref,
             dcbi_ref, dlr_ref, dli_ref, dldt_ref, dbrt_ref, dbit_ref, dcrt_ref, dcit_ref):
        m = m_ref[...]
        _, vjp = jax.vjp(_ssm_discretize, lr_ref[...], li_ref[...], ldt_ref[...], brt_ref[...], bit_ref[...])
        dlr, dli, dldt, dbrt, dbit = vjp((dar_ref[...], dai_ref[...], _block_diag_t(dbbr_ref[...], m),
                                          _block_diag_t(dbbi_ref[...], m)))
        dlr_ref[...] = dlr
        dli_ref[...] = dli
        dldt_ref[...] = dldt
        dbrt_ref[...] = dbrt
        dbit_ref[...] = dbit
        dcrt_ref[...] = _block_diag_t(dcbr_ref[...], m)
        dcit_ref[...] = _block_diag_t(dcbi_ref[...], m)

    row = _sds((1, SSM_LANES))
    r16 = _sds((SSM_GROUP, SSM_LANES))
    return pl.pallas_call(body, name=f"ssm_prep_bwd_{tag}", out_shape=[row, row, row, r16, r16, r16, r16])(
        lr, li, ldt, brt, bit, mask, dar, dai, dbbr, dbbi, dcbr, dcbi)


def _complex_scan(br, bi, ar, ai, reverse):
    n = br.shape[0]
    row = lax.broadcasted_iota(jnp.int32, (n, 1), 0)
    xr, xi, pr, pi = br, bi, ar, ai
    d = 1
    while d < n:
        if reverse:
            sr, si, keep = pltpu.roll(xr, n - d, 0), pltpu.roll(xi, n - d, 0), row < n - d
        else:
            sr, si, keep = pltpu.roll(xr, d, 0), pltpu.roll(xi, d, 0), row >= d
        sr = jnp.where(keep, sr, 0.0)
        si = jnp.where(keep, si, 0.0)
        xr, xi = xr + pr * sr - pi * si, xi + pr * si + pi * sr
        pr, pi = pr * pr - pi * pi, 2.0 * pr * pi
        d *= 2
    return xr, xi


def _ssm_fwd(u, ar, ai, bbr, bbi, cbr, cbi, dvec, tag):
    L = u.shape[0]
    T = SSM_T

    def body(u_ref, ar_ref, ai_ref, bbr_ref, bbi_ref, cbr_ref, cbi_ref, d_ref, y_ref, xr_ref, xi_ref, cr_ref, ci_ref):
        i = pl.program_id(0)

        @pl.when(i == 0)
        def _():
            cr_ref[...] = jnp.zeros_like(cr_ref)
            ci_ref[...] = jnp.zeros_like(ci_ref)

        uu = u_ref[...]
        a_r, a_i = ar_ref[...], ai_ref[...]
        c_r, c_i = cr_ref[...], ci_ref[...]
        first = lax.broadcasted_iota(jnp.int32, (T, 1), 0) == 0
        bur = _nn(uu, bbr_ref[...]) + jnp.where(first, a_r * c_r - a_i * c_i, 0.0)
        bui = _nn(uu, bbi_ref[...]) + jnp.where(first, a_r * c_i + a_i * c_r, 0.0)
        xr, xi = _complex_scan(bur, bui, a_r, a_i, False)
        cr_ref[...] = xr[T - 1:T, :]
        ci_ref[...] = xi[T - 1:T, :]
        xr_ref[...] = xr
        xi_ref[...] = xi
        y_ref[...] = _nt(xr, cbr_ref[...]) - _nt(xi, cbi_ref[...]) + d_ref[...] * uu

    blk = _res((SSM_CH, SSM_LANES))
    row = _res((1, SSM_LANES))
    return pl.pallas_call(
        body, name=f"ssm_fwd_{tag}", grid=(L // T,),
        in_specs=[_rows(SSM_CH, T), row, row, blk, blk, blk, blk, _res((1, SSM_CH))],
        out_specs=[_rows(SSM_CH, T), _rows(SSM_LANES, T), _rows(SSM_LANES, T)],
        out_shape=[_sds((L, SSM_CH)), _sds((L, SSM_LANES)), _sds((L, SSM_LANES))],
        scratch_shapes=[pltpu.VMEM((1, SSM_LANES), F32), pltpu.VMEM((1, SSM_LANES), F32)],
        compiler_params=_params(1),
    )(u, ar, ai, bbr, bbi, cbr, cbi, dvec)


def _ssm_bwd(dy, u, xr, xi, ar, ai, bbr, bbi, cbr, cbi, dvec, tag):
    L = u.shape[0]
    T = SSM_T
    nc = L // T

    def body(dy_ref, u_ref, xr_ref, xi_ref, pr_ref, pi_ref, ar_ref, ai_ref, bbr_ref, bbi_ref, cbr_ref, cbi_ref, d_ref,
             du_ref, dar_ref, dai_ref, dbbr_ref, dbbi_ref, dcbr_ref, dcbi_ref, dd_ref, gr_ref, gi_ref):
        i = pl.program_id(0)

        @pl.when(i == 0)
        def _():
            gr_ref[...] = jnp.zeros_like(gr_ref)
            gi_ref[...] = jnp.zeros_like(gi_ref)

        dyy = dy_ref[...]
        uu = u_ref[...]
        xr, xi = xr_ref[...], xi_ref[...]
        a_r, a_i = ar_ref[...], ai_ref[...]
        g_r, g_i = gr_ref[...], gi_ref[...]
        row = lax.broadcasted_iota(jnp.int32, (T, 1), 0)
        last = row == T - 1
        inr = _nn(dyy, cbr_ref[...]) + jnp.where(last, a_r * g_r + a_i * g_i, 0.0)
        ini = -_nn(dyy, cbi_ref[...]) + jnp.where(last, a_r * g_i - a_i * g_r, 0.0)
        gr, gi = _complex_scan(inr, ini, a_r, -a_i, True)
        gr_ref[...] = gr[0:1, :]
        gi_ref[...] = gi[0:1, :]
        has_prev = (i < nc - 1).astype(F32)
        pr = pr_ref[SUBLANES - 1:SUBLANES, :] * has_prev
        pi = pi_ref[SUBLANES - 1:SUBLANES, :] * has_prev
        sr = jnp.where(row == 0, pr, pltpu.roll(xr, 1, 0))
        si = jnp.where(row == 0, pi, pltpu.roll(xi, 1, 0))
        _accumulate(i, dar_ref, jnp.sum(gr * sr + gi * si, axis=0, keepdims=True))
        _accumulate(i, dai_ref, jnp.sum(gi * sr - gr * si, axis=0, keepdims=True))
        _accumulate(i, dbbr_ref, _tn(uu, gr))
        _accumulate(i, dbbi_ref, _tn(uu, gi))
        _accumulate(i, dcbr_ref, _tn(dyy, xr))
        _accumulate(i, dcbi_ref, -_tn(dyy, xi))
        _accumulate(i, dd_ref, jnp.sum(dyy * uu, axis=0, keepdims=True))
        du_ref[...] = _nt(gr, bbr_ref[...]) + _nt(gi, bbi_ref[...]) + dyy * d_ref[...]

    rev = lambda cols: pl.BlockSpec((T, cols), lambda i: (nc - 1 - i, 0))
    prev = pl.BlockSpec((SUBLANES, SSM_LANES), lambda i: (jnp.maximum((nc - 1 - i) * (T // SUBLANES) - 1, 0), 0))
    blk = _res((SSM_CH, SSM_LANES))
    row = _res((1, SSM_LANES))
    return pl.pallas_call(
        body, name=f"ssm_bwd_{tag}", grid=(nc,),
        in_specs=[rev(SSM_CH), rev(SSM_CH), rev(SSM_LANES), rev(SSM_LANES), prev, prev, row, row, blk, blk, blk, blk,
                  _res((1, SSM_CH))],
        out_specs=[rev(SSM_CH), row, row, blk, blk, blk, blk, _res((1, SSM_CH))],
        out_shape=[_sds((L, SSM_CH)), _sds((1, SSM_LANES)), _sds((1, SSM_LANES)), _sds((SSM_CH, SSM_LANES)),
                   _sds((SSM_CH, SSM_LANES)), _sds((SSM_CH, SSM_LANES)), _sds((SSM_CH, SSM_LANES)), _sds((1, SSM_CH))],
        scratch_shapes=[pltpu.VMEM((1, SSM_LANES), F32), pltpu.VMEM((1, SSM_LANES), F32)],
        compiler_params=_params(1),
    )(dy, u, xr, xi, xr, xi, ar, ai, bbr, bbi, cbr, cbi, dvec)


def _conv_post(hc, dw_b, ln_g, ln_b):
    return _silu(_layer_norm(hc + dw_b, ln_g, ln_b))


def _branch_mix(o_sb, o_conv, t, g1, g2, g3):
    o_ssm = t[:, 0:SSM_CH] * _sigmoid(t[:, SSM_CH:2 * SSM_CH])
    return jnp.concatenate([_rms(o_sb, g1), _rms(o_conv, g2), _rms(o_ssm, g3)], axis=1)


def _branch_mix_split(o_sb, o_conv, ta, tb, g1, g2, g3):
    return jnp.concatenate([_rms(o_sb, g1), _rms(o_conv, g2), _rms(ta * _sigmoid(tb), g3)], axis=1)


def _mix_out_fwd(x, o_sb, hc, y, dw_b, ln_g, ln_b, pw2, glu_w, g1, g2, g3, w_out, tag):
    L = x.shape[0]

    def body(x_ref, o_ref, hc_ref, y_ref, dwb_ref, lng_ref, lnb_ref, pw2_ref, glu_ref, g1_ref, g2_ref, g3_ref, wo_ref,
             out_ref):
        c1 = _conv_post(hc_ref[...], dwb_ref[...], lng_ref[...], lnb_ref[...])
        o_conv = _nn(c1, pw2_ref[...])
        t = _nt(y_ref[...], glu_ref[...])
        mixed = _branch_mix(o_ref[...], o_conv, t, g1_ref[...], g2_ref[...], g3_ref[...])
        out_ref[...] = x_ref[...] + _nn(mixed, wo_ref[...])

    v256 = _res((1, 256))
    return pl.pallas_call(
        body, name=f"mix_out_fwd_{tag}", grid=(L // TL,),
        in_specs=[_rows(D_MODEL), _rows(512), _rows(256), _rows(256), v256, v256, v256, _res(pw2.shape),
                  _res(glu_w.shape), _res((1, 512)), v256, v256, _res(w_out.shape)],
        out_specs=_rows(D_MODEL),
        out_shape=_sds((L, D_MODEL)),
        compiler_params=_params(1),
    )(x, o_sb, hc, y, dw_b, ln_g, ln_b, pw2, glu_w, g1, g2, g3, w_out)


def _mix_out_bwd(dx1, o_sb, hc, y, dw_b, ln_g, ln_b, pw2, glu_w, g1, g2, g3, w_out, tag):
    L = dx1.shape[0]

    def body(dx_ref, o_ref, hc_ref, y_ref, dwb_ref, lng_ref, lnb_ref, pw2_ref, glu_ref, g1_ref, g2_ref, g3_ref, wo_ref,
             do_ref, dhc_ref, dy_ref, ddwb_ref, dlng_ref, dlnb_ref, dpw2_ref, dglu_ref, dg1_ref, dg2_ref, dg3_ref,
             dwo_ref):
        i = pl.program_id(0)
        dxx = dx_ref[...]
        yy = y_ref[...]
        c1, vjp1 = jax.vjp(_conv_post, hc_ref[...], dwb_ref[...], lng_ref[...], lnb_ref[...])
        o_conv = _nn(c1, pw2_ref[...])
        t = _nt(yy, glu_ref[...])
        mixed, vjp2 = jax.vjp(_branch_mix_split, o_ref[...], o_conv, t[:, 0:SSM_CH], t[:, SSM_CH:2 * SSM_CH],
                              g1_ref[...], g2_ref[...], g3_ref[...])
        dmixed = _nt(dxx, wo_ref[...])
        do_sb, do_conv, dta, dtb, dg1, dg2, dg3 = vjp2(dmixed)
        dt = jnp.concatenate([dta, dtb], axis=1)
        dc1 = _nt(do_conv, pw2_ref[...])
        dhc, ddwb, dlng, dlnb = vjp1(dc1)
        do_ref[...] = do_sb
        dhc_ref[...] = dhc
        dy_ref[...] = _nn(dt, glu_ref[...])
        _accumulate(i, dwo_ref, _tn(mixed, dxx))
        _accumulate(i, dglu_ref, _tn(dt, yy))
        _accumulate(i, dpw2_ref, _tn(c1, do_conv))
        _accumulate(i, ddwb_ref, ddwb)
        _accumulate(i, dlng_ref, dlng)
        _accumulate(i, dlnb_ref, dlnb)
        _accumulate(i, dg1_ref, dg1)
        _accumulate(i, dg2_ref, dg2)
        _accumulate(i, dg3_ref, dg3)

    v256 = _res((1, 256))
    return pl.pallas_call(
        body, name=f"mix_out_bwd_{tag}", grid=(L // TL,),
        in_specs=[_rows(D_MODEL), _rows(512), _rows(256), _rows(256), v256, v256, v256, _res(pw2.shape),
                  _res(glu_w.shape), _res((1, 512)), v256, v256, _res(w_out.shape)],
        out_specs=[_rows(512), _rows(256), _rows(256), v256, v256, v256, _res(pw2.shape), _res(glu_w.shape),
                   _res((1, 512)), v256, v256, _res(w_out.shape)],
        out_shape=[_sds((L, 512)), _sds((L, 256)), _sds((L, 256)), _sds((1, 256)), _sds((1, 256)), _sds((1, 256)),
                   _sds(pw2.shape), _sds(glu_w.shape), _sds((1, 512)), _sds((1, 256)), _sds((1, 256)), _sds(w_out.shape)],
        compiler_params=_params(1),
    )(dx1, o_sb, hc, y, dw_b, ln_g, ln_b, pw2, glu_w, g1, g2, g3, w_out)


def _xa_heads_norm(kk, kg):
    return jnp.concatenate([_rms(kk[:, h * XA_HEAD_DIM:(h + 1) * XA_HEAD_DIM], kg) for h in range(XA_HEADS)], axis=1)


def _xa_mem_fwd(mem, g_mem, wk, wv, kg, tag):
    def body(m_ref, g_ref, wk_ref, wv_ref, kg_ref, k_ref, v_ref):
        hm = _rms(m_ref[...], g_ref[...])
        k_ref[...] = _xa_heads_norm(_nn(hm, wk_ref[...]), kg_ref[...])
        v_ref[...] = _nn(hm, wv_ref[...])

    return pl.pallas_call(body, name=f"xa_mem_fwd_{tag}", out_shape=[_sds(mem.shape), _sds(mem.shape)],
                          compiler_params=_params(0))(mem, g_mem, wk, wv, kg)


def _xa_mem_bwd(mem, g_mem, wk, wv, kg, dkx, dvx, tag):
    def body(m_ref, g_ref, wk_ref, wv_ref, kg_ref, dk_ref, dv_ref, dwk_ref, dwv_ref, dg_ref, dkg_ref):
        hm, vjp_n = jax.vjp(_rms, m_ref[...], g_ref[...])
        kk = _nn(hm, wk_ref[...])
        dvv = dv_ref[...]
        dkg = jnp.zeros((1, XA_HEAD_DIM), F32)
        parts = []
        for h in range(XA_HEADS):
            sl = slice(h * XA_HEAD_DIM, (h + 1) * XA_HEAD_DIM)
            _, vjp_h = jax.vjp(_rms, kk[:, sl], kg_ref[...])
            dkh, dgh = vjp_h(dk_ref[:, sl])
            parts.append(dkh)
            dkg = dkg + dgh
        dkk = jnp.concatenate(parts, axis=1)
        dwk_ref[...] = _tn(hm, dkk)
        dwv_ref[...] = _tn(hm, dvv)
        dhm = _nt(dkk, wk_ref[...]) + _nt(dvv, wv_ref[...])
        _, dg = vjp_n(dhm)
        dg_ref[...] = dg
        dkg_ref[...] = dkg

    return pl.pallas_call(
        body, name=f"xa_mem_bwd_{tag}",
        out_shape=[_sds(wk.shape), _sds(wv.shape), _sds((1, D_MODEL)), _sds((1, XA_HEAD_DIM))],
        compiler_params=_params(0))(mem, g_mem, wk, wv, kg, dkx, dvx)


def _xa_fwd(x1, kx, vx, g_xa, wq, qg, wo, tag):
    L = x1.shape[0]

    def body(x_ref, k_ref, v_ref, g_ref, wq_ref, qg_ref, wo_ref, out_ref):
        xx = x_ref[...]
        qp = _nn(_rms(xx, g_ref[...]), wq_ref[...])
        outs = []
        for h in range(XA_HEADS):
            sl = slice(h * XA_HEAD_DIM, (h + 1) * XA_HEAD_DIM)
            qh = _rms(qp[:, sl], qg_ref[...])
            s = _nt(qh, k_ref[:, sl]) * (XA_HEAD_DIM ** -0.5)
            s = s - jnp.max(s, axis=-1, keepdims=True)
            e = jnp.exp(s)
            p = e / jnp.sum(e, axis=-1, keepdims=True)
            outs.append(_nn(p, v_ref[:, sl]))
        out_ref[...] = xx + _nn(jnp.concatenate(outs, axis=1), wo_ref[...])

    return pl.pallas_call(
        body, name=f"xa_fwd_{tag}", grid=(L // TL,),
        in_specs=[_rows(D_MODEL), _res(kx.shape), _res(vx.shape), _res((1, D_MODEL)), _res(wq.shape),
                  _res((1, XA_HEAD_DIM)), _res(wo.shape)],
        out_specs=_rows(D_MODEL),
        out_shape=_sds((L, D_MODEL)),
        compiler_params=_params(1),
    )(x1, kx, vx, g_xa, wq, qg, wo)


def _xa_bwd(x1, dx2, kx, vx, g_xa, wq, qg, wo, tag):
    L = x1.shape[0]
    tl = 256

    def body(x_ref, dx_ref, k_ref, v_ref, g_ref, wq_ref, qg_ref, wo_ref,
             dx1_ref, dk_ref, dv_ref, dwq_ref, dwo_ref, dg_ref, dqg_ref):
        i = pl.program_id(0)
        xx = x_ref[...]
        dxx = dx_ref[...]
        hx, vjp_n = jax.vjp(_rms, xx, g_ref[...])
        qp = _nn(hx, wq_ref[...])
        do = _nt(dxx, wo_ref[...])
        outs, dqps, dks, dvs = [], [], [], []
        dqg = jnp.zeros((1, XA_HEAD_DIM), F32)
        for h in range(XA_HEADS):
            sl = slice(h * XA_HEAD_DIM, (h + 1) * XA_HEAD_DIM)
            kh, vh = k_ref[:, sl], v_ref[:, sl]
            qh, vjp_q = jax.vjp(_rms, qp[:, sl], qg_ref[...])
            s = _nt(qh, kh) * (XA_HEAD_DIM ** -0.5)
            s = s - jnp.max(s, axis=-1, keepdims=True)
            e = jnp.exp(s)
            p = e / jnp.sum(e, axis=-1, keepdims=True)
            outs.append(_nn(p, vh))
            doh = do[:, sl]
            dp = _nt(doh, vh)
            dvs.append(_tn(p, doh))
            ds = p * (dp - jnp.sum(dp * p, axis=-1, keepdims=True)) * (XA_HEAD_DIM ** -0.5)
            dks.append(_tn(ds, qh))
            dqh, dgh = vjp_q(_nn(ds, kh))
            dqps.append(dqh)
            dqg = dqg + dgh
        o = jnp.concatenate(outs, axis=1)
        dqp = jnp.concatenate(dqps, axis=1)
        dxn, dg = vjp_n(_nt(dqp, wq_ref[...]))
        dx1_ref[...] = dxx + dxn
        _accumulate(i, dk_ref, jnp.concatenate(dks, axis=1))
        _accumulate(i, dv_ref, jnp.concatenate(dvs, axis=1))
        _accumulate(i, dwq_ref, _tn(hx, dqp))
        _accumulate(i, dwo_ref, _tn(o, dxx))
        _accumulate(i, dg_ref, dg)
        _accumulate(i, dqg_ref, dqg)

    r = lambda c: _rows(c, tl)
    return pl.pallas_call(
        body, name=f"xa_bwd_{tag}", grid=(L // tl,),
        in_specs=[r(D_MODEL), r(D_MODEL), _res(kx.shape), _res(vx.shape), _res((1, D_MODEL)), _res(wq.shape),
                  _res((1, XA_HEAD_DIM)), _res(wo.shape)],
        out_specs=[r(D_MODEL), _res(kx.shape), _res(vx.shape), _res(wq.shape), _res(wo.shape), _res((1, D_MODEL)),
                   _res((1, XA_HEAD_DIM))],
        out_shape=[_sds((L, D_MODEL)), _sds(kx.shape), _sds(vx.shape), _sds(wq.shape), _sds(wo.shape),
                   _sds((1, D_MODEL)), _sds((1, XA_HEAD_DIM))],
        compiler_params=_params(1),
    )(x1, dx2, kx, vx, g_xa, wq, qg, wo)


def _swiglu(gate, up):
    return _silu(gate) * up


def _ffn_fwd(x2, g, w_in, w_out, tag):
    L = x2.shape[0]
    tl = 256

    def body(x_ref, g_ref, wi_ref, wo_ref, out_ref):
        xx = x_ref[...]
        gu = _nt(_rms(xx, g_ref[...]), wi_ref[...])
        act = _swiglu(gu[:, 0:FFN_HIDDEN], gu[:, FFN_HIDDEN:2 * FFN_HIDDEN])
        out_ref[...] = xx + _nn(act, wo_ref[...])

    return pl.pallas_call(
        body, name=f"ffn_fwd_{tag}", grid=(L // tl,),
        in_specs=[_rows(D_MODEL, tl), _res((1, D_MODEL)), _res(w_in.shape), _res(w_out.shape)],
        out_specs=_rows(D_MODEL, tl),
        out_shape=_sds((L, D_MODEL)),
        compiler_params=_params(1, 56),
    )(x2, g, w_in, w_out)


def _ffn_bwd(x2, dx3, g, w_in, w_out, tag):
    L = x2.shape[0]
    tl = 256

    def body(x_ref, dx_ref, g_ref, wi_ref, wo_ref, dx2_ref, dgu_ref, act_ref, hf_ref, dg_ref):
        i = pl.program_id(0)
        xx = x_ref[...]
        dxx = dx_ref[...]
        hf, vjp_n = jax.vjp(_rms, xx, g_ref[...])
        gu = _nt(hf, wi_ref[...])
        act, vjp_a = jax.vjp(_swiglu, gu[:, 0:FFN_HIDDEN], gu[:, FFN_HIDDEN:2 * FFN_HIDDEN])
        dgate, dup = vjp_a(_nt(dxx, wo_ref[...]))
        dgu = jnp.concatenate([dgate, dup], axis=1).astype(BF16)
        dxn, dg = vjp_n(_nn(dgu, wi_ref[...]))
        dx2_ref[...] = dxx + dxn
        dgu_ref[...] = dgu
        act_ref[...] = act.astype(BF16)
        hf_ref[...] = hf.astype(BF16)
        _accumulate(i, dg_ref, dg)

    r = lambda c: _rows(c, tl)
    return pl.pallas_call(
        body, name=f"ffn_bwd_{tag}", grid=(L // tl,),
        in_specs=[r(D_MODEL), r(D_MODEL), _res((1, D_MODEL)), _res(w_in.shape), _res(w_out.shape)],
        out_specs=[r(D_MODEL), r(2 * FFN_HIDDEN), r(FFN_HIDDEN), r(D_MODEL), _res((1, D_MODEL))],
        out_shape=[_sds((L, D_MODEL)), _sds((L, 2 * FFN_HIDDEN), BF16), _sds((L, FFN_HIDDEN), BF16),
                   _sds((L, D_MODEL), BF16), _sds((1, D_MODEL))],
        compiler_params=_params(1, 56),
    )(x2, dx3, g, w_in, w_out)


def _matmul_tn(a, b, tm, tn, tag):
    L, M = a.shape
    N = b.shape[1]
    tk = 512

    def body(a_ref, b_ref, o_ref):
        _accumulate(pl.program_id(2), o_ref, _tn(a_ref[...], b_ref[...]))

    return pl.pallas_call(
        body, name=f"matmul_tn_{tag}", grid=(M // tm, N // tn, L // tk),
        in_specs=[pl.BlockSpec((tk, tm), lambda m, n, k: (k, m)), pl.BlockSpec((tk, tn), lambda m, n, k: (k, n))],
        out_specs=pl.BlockSpec((tm, tn), lambda m, n, k: (m, n)),
        out_shape=_sds((M, N)),
        compiler_params=_params(3),
    )(a, b)


def _loss_head(y, tgt):
    L = y.shape[0]
    n_tiles = L // TL

    def body(y_ref, t_ref, loss_ref, dy_ref, acc_ref):
        i = pl.program_id(0)
        diff = y_ref[...] - t_ref[...]
        dy_ref[...] = diff * (1.0 / D_MODEL)
        _accumulate(i, acc_ref, jnp.sum(diff * diff, axis=0, keepdims=True))

        @pl.when(i == n_tiles - 1)
        def _():
            loss_ref[...] = jnp.sum(acc_ref[...], axis=1, keepdims=True) * (0.5 / D_MODEL)

    return pl.pallas_call(
        body, name="loss_head", grid=(n_tiles,),
        in_specs=[_rows(D_MODEL), _rows(D_MODEL)],
        out_specs=[_res((1, 1)), _rows(D_MODEL)],
        out_shape=[_sds((1, 1)), _sds((L, D_MODEL))],
        scratch_shapes=[pltpu.VMEM((1, D_MODEL), F32)],
        compiler_params=_params(1),
    )(y, tgt)


def _row(v):
    return v.reshape(1, -1)


def _layer_consts():
    r = np.arange(SB_WIDTH)
    mavg = ((r[:, None] // SB_HEAD_DIM) == (r[None, :] // SB_HEAD_DIM)).astype(np.float32) / SB_HEAD_DIM
    ul, ue = _sb_tri_consts()
    return dict(mavg=jnp.asarray(mavg), ul=ul, ue=ue, mask=_ssm_mask())


def _ssm_rows(P):
    lanes = lambda a: a.reshape(1, SSM_LANES)
    return dict(
        lr=lanes(P['ssm_lam_re']), li=lanes(P['ssm_lam_im']),
        ldt=lanes(jnp.repeat(P['ssm_log_dt'], SSM_STATE)),
        brt=P['ssm_b_re'].transpose(2, 0, 1).reshape(SSM_GROUP, SSM_LANES),
        bit=P['ssm_b_im'].transpose(2, 0, 1).reshape(SSM_GROUP, SSM_LANES),
        crt=P['ssm_c_re'].transpose(1, 0, 2).reshape(SSM_GROUP, SSM_LANES),
        cit=P['ssm_c_im'].transpose(1, 0, 2).reshape(SSM_GROUP, SSM_LANES))


def _layer_fwd(x, mem, P, C, tag):
    gq = _row(jnp.tile(P['sb_q_norm_g'], SB_WIDTH // SB_HEAD_DIM))
    gk = _row(jnp.tile(P['sb_k_norm_g'], SB_WIDTH // SB_HEAD_DIM))
    q, k, v, hg, u = _mix_in_fwd(x, _row(P['norm_mix_g']), P['w_in'], gq, gk, C['mavg'], tag)
    o_sb, rsave = _sb_fwd(q, k, v, C['ul'], tag)
    hp = jnp.pad(hg, ((CONV_HALO, 0), (0, 0)))
    hc = _conv_fwd(hp, P['conv_dw_w'].T.astype(F32), tag)
    S = _ssm_rows(P)
    ar, ai, bbr, bbi, cbr, cbi = _ssm_prep(S['lr'], S['li'], S['ldt'], S['brt'], S['bit'], S['crt'], S['cit'],
                                           C['mask'], tag)
    y, xr, xi = _ssm_fwd(u, ar, ai, bbr, bbi, cbr, cbi, _row(P['ssm_d']), tag)
    gb = P['branch_norm_g']
    x1 = _mix_out_fwd(x, o_sb, hc, y, _row(P['conv_dw_b']), _row(P['conv_ln_g']), _row(P['conv_ln_b']),
                      P['conv_pw2_w'], P['ssm_glu_w'], _row(gb[0:512]), _row(gb[512:768]), _row(gb[768:1024]),
                      P['w_out'], tag)
    kx, vx = _xa_mem_fwd(mem, _row(P['norm_mem_g']), P['xa_wk'], P['xa_wv'], _row(P['xa_k_norm_g']), tag)
    x2 = _xa_fwd(x1, kx, vx, _row(P['norm_xa_g']), P['xa_wq'], _row(P['xa_q_norm_g']), P['xa_wo'], tag)
    x3 = _ffn_fwd(x2, _row(P['norm_ffn_g']), P['ffn_w_in'], P['ffn_w_out'], tag)
    saved = dict(x=x, q=q, k=k, v=v, rsave=rsave, o_sb=o_sb, hp=hp, hc=hc, u=u, y=y, xr=xr, xi=xi, x1=x1, x2=x2,
                 kx=kx, vx=vx, gq=gq, gk=gk, S=S, ssm=(ar, ai, bbr, bbi, cbr, cbi))
    return x3, saved


def _layer_bwd(dx3, mem, P, C, sv, tag):
    G = {}
    dx2, dgu, act, hf, dg = _ffn_bwd(sv['x2'], dx3, _row(P['norm_ffn_g']), P['ffn_w_in'], P['ffn_w_out'], tag)
    G['norm_ffn_g'] = dg.reshape(-1)
    G['ffn_w_in'] = _matmul_tn(dgu, hf, 2 * FFN_HIDDEN // 4, D_MODEL, "ffn_in_" + tag)
    G['ffn_w_out'] = _matmul_tn(act, dx3, FFN_HIDDEN // 2, 512, "ffn_out_" + tag)
    dx1, dkx, dvx, dwq, dwo, dg, dqg = _xa_bwd(sv['x1'], dx2, sv['kx'], sv['vx'], _row(P['norm_xa_g']), P['xa_wq'],
                                               _row(P['xa_q_norm_g']), P['xa_wo'], tag)
    G['xa_wq'], G['xa_wo'], G['norm_xa_g'], G['xa_q_norm_g'] = dwq, dwo, dg.reshape(-1), dqg.reshape(-1)
    dwk, dwv, dg, dkg = _xa_mem_bwd(mem, _row(P['norm_mem_g']), P['xa_wk'], P['xa_wv'], _row(P['xa_k_norm_g']),
                                    dkx, dvx, tag)
    G['xa_wk'], G['xa_wv'], G['norm_mem_g'], G['xa_k_norm_g'] = dwk, dwv, dg.reshape(-1), dkg.reshape(-1)
    gb = P['branch_norm_g']
    (do_sb, dhc, dy, ddwb, dlng, dlnb, dpw2, dglu, dg1, dg2, dg3, dwout) = _mix_out_bwd(
        dx1, sv['o_sb'], sv['hc'], sv['y'], _row(P['conv_dw_b']), _row(P['conv_ln_g']), _row(P['conv_ln_b']),
        P['conv_pw2_w'], P['ssm_glu_w'], _row(gb[0:512]), _row(gb[512:768]), _row(gb[768:1024]), P['w_out'], tag)
    G['conv_dw_b'], G['conv_ln_g'], G['conv_ln_b'] = ddwb.reshape(-1), dlng.reshape(-1), dlnb.reshape(-1)
    G['conv_pw2_w'], G['ssm_glu_w'], G['w_out'] = dpw2, dglu, dwout
    G['branch_norm_g'] = jnp.concatenate([dg1.reshape(-1), dg2.reshape(-1), dg3.reshape(-1)])
    ar, ai, bbr, bbi, cbr, cbi = sv['ssm']
    du, dar, dai, dbbr, dbbi, dcbr, dcbi, dd = _ssm_bwd(dy, sv['u'], sv['xr'], sv['xi'], ar, ai, bbr, bbi, cbr, cbi,
                                                        _row(P['ssm_d']), tag)
    S = sv['S']
    dlr, dli, dldt, dbrt, dbit, dcrt, dcit = _ssm_prep_bwd(S['lr'], S['li'], S['ldt'], S['brt'], S['bit'], C['mask'],
                                                           dar, dai, dbbr, dbbi, dcbr, dcbi, tag)
    G['ssm_lam_re'] = dlr.reshape(SSM_GROUPS, SSM_STATE)
    G['ssm_lam_im'] = dli.reshape(SSM_GROUPS, SSM_STATE)
    G['ssm_log_dt'] = dldt.reshape(SSM_GROUPS, SSM_STATE).sum(axis=1)
    G['ssm_b_re'] = dbrt.reshape(SSM_GROUP, SSM_GROUPS, SSM_STATE).transpose(1, 2, 0)
    G['ssm_b_im'] = dbit.reshape(SSM_GROUP, SSM_GROUPS, SSM_STATE).transpose(1, 2, 0)
    G['ssm_c_re'] = dcrt.reshape(SSM_GROUP, SSM_GROUPS, SSM_STATE).transpose(1, 0, 2)
    G['ssm_c_im'] = dcit.reshape(SSM_GROUP, SSM_GROUPS, SSM_STATE).transpose(1, 0, 2)
    G['ssm_d'] = dd.reshape(-1)
    dpad = jnp.pad(dhc, ((0, CONV_HALO), (0, 0)))
    dhg, ddww = _conv_bwd(dpad, sv['hp'], P['conv_dw_w'].T.astype(F32), tag)
    G['conv_dw_w'] = ddww.T
    dq, dk, dv = _sb_bwd(sv['q'], sv['k'], sv['v'], sv['rsave'], do_sb, C['ul'], C['ue'], tag)
    dx, dwin, dg, dgq, dgk = _mix_in_bwd(sv['x'], dx1, dq, dk, dv, dhg, du, _row(P['norm_mix_g']), P['w_in'],
                                         sv['gq'], sv['gk'], C['mavg'], tag)
    G['w_in'], G['norm_mix_g'] = dwin, dg.reshape(-1)
    G['sb_q_norm_g'] = dgq.reshape(SB_WIDTH // SB_HEAD_DIM, SB_HEAD_DIM).sum(axis=0)
    G['sb_k_norm_g'] = dgk.reshape(SB_WIDTH // SB_HEAD_DIM, SB_HEAD_DIM).sum(axis=0)
    return dx, G


def _local_step(x, mem, tgt, layers):
    C = _layer_consts()
    saved = []
    h = x
    for l in range(DEPTH):
        h, sv = _layer_fwd(h, mem, layers[l], C, f"l{l}")
        saved.append(sv)
    loss, dh = _loss_head(h, tgt)
    grads = [None] * DEPTH
    for l in reversed(range(DEPTH)):
        dh, grads[l] = _layer_bwd(dh, mem, layers[l], C, saved[l], f"l{l}")
    return loss, dh, grads


def _my_index():
    return lax.axis_index("x") * 4 + lax.axis_index("y") * 2 + lax.axis_index("c")


def _peer(k):
    x, y, c = lax.axis_index("x"), lax.axis_index("y"), lax.axis_index("c")
    return (x ^ ((k >> 2) & 1), y ^ ((k >> 1) & 1), c ^ (k & 1))


def _peer_index(k):
    return _my_index() ^ k


HBM = pl.BlockSpec(memory_space=pltpu.HBM)


def _all_gather(x, name):
    def body(x_ref, out_ref, send_sems, recv_sems, local_sem):
        me = _my_index()
        mine = pltpu.make_async_copy(x_ref, out_ref.at[me], local_sem)
        mine.start()
        sends = []
        for k in range(1, N_DEV):
            cp = pltpu.make_async_remote_copy(src_ref=x_ref, dst_ref=out_ref.at[me], send_sem=send_sems.at[k],
                                              recv_sem=recv_sems.at[k], device_id=_peer(k),
                                              device_id_type=pl.DeviceIdType.MESH)
            cp.start()
            sends.append(cp)
        for k in range(1, N_DEV):
            pltpu.make_async_remote_copy(src_ref=x_ref, dst_ref=out_ref.at[_peer_index(k)], send_sem=send_sems.at[k],
                                         recv_sem=recv_sems.at[k], device_id=_peer(k),
                                         device_id_type=pl.DeviceIdType.MESH).wait_recv()
        for cp in sends:
            cp.wait_send()
        mine.wait()

    return pl.pallas_call(
        body, name=name, in_specs=[HBM], out_specs=HBM,
        out_shape=_sds((N_DEV,) + x.shape, x.dtype),
        scratch_shapes=[pltpu.SemaphoreType.DMA((N_DEV,)), pltpu.SemaphoreType.DMA((N_DEV,)), pltpu.SemaphoreType.DMA],
    )(x)


def _grad_exchange(big, small, name):
    def body(big_ref, small_ref, rbig_ref, rsmall_ref, send_sems, recv_sems, local_sems):
        me = _my_index()
        loc_b = pltpu.make_async_copy(big_ref.at[me], rbig_ref.at[me], local_sems.at[0])
        loc_s = pltpu.make_async_copy(small_ref, rsmall_ref.at[me], local_sems.at[1])
        loc_b.start()
        loc_s.start()
        sends = []
        for k in range(1, N_DEV):
            cb = pltpu.make_async_remote_copy(src_ref=big_ref.at[_peer_index(k)], dst_ref=rbig_ref.at[me],
                                              send_sem=send_sems.at[k], recv_sem=recv_sems.at[k], device_id=_peer(k),
                                              device_id_type=pl.DeviceIdType.MESH)
            cs = pltpu.make_async_remote_copy(src_ref=small_ref, dst_ref=rsmall_ref.at[me],
                                              send_sem=send_sems.at[N_DEV + k], recv_sem=recv_sems.at[N_DEV + k],
                                              device_id=_peer(k), device_id_type=pl.DeviceIdType.MESH)
            cb.start()
            cs.start()
            sends += [cb, cs]
        for k in range(1, N_DEV):
            pk = _peer_index(k)
            pltpu.make_async_remote_copy(src_ref=big_ref.at[pk], dst_ref=rbig_ref.at[pk], send_sem=send_sems.at[k],
                                         recv_sem=recv_sems.at[k], device_id=_peer(k),
                                         device_id_type=pl.DeviceIdType.MESH).wait_recv()
            pltpu.make_async_remote_copy(src_ref=small_ref, dst_ref=rsmall_ref.at[pk], send_sem=send_sems.at[N_DEV + k],
                                         recv_sem=recv_sems.at[N_DEV + k], device_id=_peer(k),
                                         device_id_type=pl.DeviceIdType.MESH).wait_recv()
        for cp in sends:
            cp.wait_send()
        loc_b.wait()
        loc_s.wait()

    return pl.pallas_call(
        body, name=name, in_specs=[HBM, HBM], out_specs=[HBM, HBM],
        out_shape=[_sds(big.shape, big.dtype), _sds((N_DEV,) + small.shape, small.dtype)],
        scratch_shapes=[pltpu.SemaphoreType.DMA((2 * N_DEV,)), pltpu.SemaphoreType.DMA((2 * N_DEV,)),
                        pltpu.SemaphoreType.DMA((2,))],
    )(big, small)


def _adamw(recv, w, m, v, tile, name):
    n_slots, R, _ = recv.shape
    c1 = 1.0 / (1.0 - ADAM_B1 ** ADAM_STEP)
    c2 = 1.0 / (1.0 - ADAM_B2 ** ADAM_STEP)

    def body(r_ref, w_ref, m_ref, v_ref, g_ref, d_ref, nm_ref, nv_ref):
        g = r_ref[0].astype(F32)
        for s in range(1, n_slots):
            g = g + r_ref[s].astype(F32)
        nm = ADAM_B1 * m_ref[...] + (1.0 - ADAM_B1) * g
        nv = ADAM_B2 * v_ref[...] + (1.0 - ADAM_B2) * (g * g)
        g_ref[...] = g
        nm_ref[...] = nm
        nv_ref[...] = nv
        d_ref[...] = -ADAM_LR * ((nm * c1) / (jnp.sqrt(nv * c2) + ADAM_EPS) + ADAM_WD * w_ref[...])

    rows = pl.BlockSpec((tile, LANES), lambda i: (i, 0))
    out = _sds((R, LANES))
    return pl.pallas_call(
        body, name=name, grid=(R // tile,),
        in_specs=[pl.BlockSpec((n_slots, tile, LANES), lambda i: (0, i, 0)), rows, rows, rows],
        out_specs=[rows, rows, rows, rows],
        out_shape=[out, out, out, out],
        compiler_params=_params(1),
    )(recv, w, m, v)


def _reduce_slots(recv, tile, name):
    n_slots, R, _ = recv.shape

    def body(r_ref, g_ref):
        g = r_ref[0].astype(F32)
        for s in range(1, n_slots):
            g = g + r_ref[s].astype(F32)
        g_ref[...] = g

    return pl.pallas_call(
        body, name=name, grid=(R // tile,),
        in_specs=[pl.BlockSpec((n_slots, tile, LANES), lambda i: (0, i, 0))],
        out_specs=pl.BlockSpec((tile, LANES), lambda i: (i, 0)),
        out_shape=_sds((R, LANES)),
        compiler_params=_params(1),
    )(recv)


SEG = SUBLANES * LANES


def _pad_to(n, mult):
    return -(-n // mult) * mult


def _pack(arrays, dtype, row_mult, lead=0):
    keep = [(0, 0)] * lead
    parts = []
    for a in arrays:
        flat = a.reshape(a.shape[:lead] + (-1,)).astype(dtype)
        n = flat.shape[-1]
        parts.append(jnp.pad(flat, keep + [(0, _pad_to(n, SEG) - n)]))
    flat = jnp.concatenate(parts, axis=-1)
    n = flat.shape[-1]
    flat = jnp.pad(flat, keep + [(0, _pad_to(n, row_mult * LANES) - n)])
    return flat.reshape(flat.shape[:lead] + (-1, LANES))


def _unpack(buf, shapes):
    lead = buf.shape[:-2]
    flat = buf.reshape(lead + (-1,))
    out, off = [], 0
    for shp in shapes:
        n = int(np.prod(shp))
        out.append(flat[..., off:off + n].reshape(lead + tuple(shp)))
        off += _pad_to(n, SEG)
    return out


def _rows_first(a, name):
    return a.transpose(0, 2, 1) if SHARD_AXIS[name] == 2 else a


BIG_TILE = 512
SMALL_TILE = 256


def kernel(x, mem, norm_mix_g, w_in, sb_q_norm_g, sb_k_norm_g, conv_dw_w, conv_dw_b, conv_ln_g, conv_ln_b, conv_pw2_w, ssm_lam_re, ssm_lam_im, ssm_log_dt, ssm_b_re, ssm_b_im, ssm_c_re, ssm_c_im, ssm_d, ssm_glu_w, branch_norm_g, w_out, norm_xa_g, norm_mem_g, xa_wq, xa_wk, xa_wv, xa_q_norm_g, xa_k_norm_g, xa_wo, norm_ffn_g, ffn_w_in, ffn_w_out, loss_target, m_norm_mix_g, m_w_in, m_sb_q_norm_g, m_sb_k_norm_g, m_conv_dw_w, m_conv_dw_b, m_conv_ln_g, m_conv_ln_b, m_conv_pw2_w, m_ssm_lam_re, m_ssm_lam_im, m_ssm_log_dt, m_ssm_b_re, m_ssm_b_im, m_ssm_c_re, m_ssm_c_im, m_ssm_d, m_ssm_glu_w, m_branch_norm_g, m_w_out, m_norm_xa_g, m_norm_mem_g, m_xa_wq, m_xa_wk, m_xa_wv, m_xa_q_norm_g, m_xa_k_norm_g, m_xa_wo, m_norm_ffn_g, m_ffn_w_in, m_ffn_w_out, v_norm_mix_g, v_w_in, v_sb_q_norm_g, v_sb_k_norm_g, v_conv_dw_w, v_conv_dw_b, v_conv_ln_g, v_conv_ln_b, v_conv_pw2_w, v_ssm_lam_re, v_ssm_lam_im, v_ssm_log_dt, v_ssm_b_re, v_ssm_b_im, v_ssm_c_re, v_ssm_c_im, v_ssm_d, v_ssm_glu_w, v_branch_norm_g, v_w_out, v_norm_xa_g, v_norm_mem_g, v_xa_wq, v_xa_wk, v_xa_wv, v_xa_q_norm_g, v_xa_k_norm_g, v_xa_wo, v_norm_ffn_g, v_ffn_w_in, v_ffn_w_out):
    args = locals()
    w_loc = {n: args[n] for n in WEIGHT_NAMES}
    m_loc = {n: args["m_" + n] for n in WEIGHT_NAMES}
    v_loc = {n: args["v_" + n] for n in WEIGHT_NAMES}
    big_shapes = [w_loc[n].shape for n in BIG_NAMES]
    small_shapes = [w_loc[n].shape for n in SMALL_NAMES]

    rows_first = {n: _rows_first(w_loc[n], n) for n in BIG_NAMES}
    segs = [(l, n) for l in range(DEPTH) for n in BIG_NAMES]
    seg_shapes = [rows_first[n].shape[1:] for (_, n) in segs]
    gathered = _all_gather(_pack([rows_first[n][l] for (l, n) in segs], BF16, BIG_TILE), "weights_all_gather")
    layers = [{n: w_loc[n][l] for n in SMALL_NAMES} for l in range(DEPTH)]
    for (l, n), sh in zip(segs, _unpack(gathered, seg_shapes)):
        layers[l][n] = sh.reshape((N_DEV * sh.shape[1],) + sh.shape[2:])

    loss_part, grad_x, grads = _local_step(x[0], mem[0], loss_target[0], layers)
    loss = lax.psum(loss_part[0, 0], MESH_AXES)

    send_big = _pack([grads[l][n].reshape(N_DEV, -1) for (l, n) in segs], BF16, BIG_TILE, lead=1)
    send_small = _pack([jnp.stack([grads[l][n] for l in range(DEPTH)]) for n in SMALL_NAMES], F32, SMALL_TILE)
    recv_big, recv_small = _grad_exchange(send_big, send_small, "grad_exchange")
    g_rows_first = dict(zip(segs, _unpack(_reduce_slots(recv_big, BIG_TILE, "grad_reduce"), seg_shapes)))
    g_big = []
    for n in BIG_NAMES:
        g = jnp.stack([g_rows_first[(l, n)] for l in range(DEPTH)])
        g_big.append(_rows_first(g, n))
    pk = lambda d, names, tile: _pack([d[n] for n in names], F32, tile)
    outs_big = _adamw(_pack(g_big, F32, BIG_TILE)[None], pk(w_loc, BIG_NAMES, BIG_TILE), pk(m_loc, BIG_NAMES, BIG_TILE),
                      pk(v_loc, BIG_NAMES, BIG_TILE), BIG_TILE, "adamw_sharded")
    outs_small = _adamw(recv_small, pk(w_loc, SMALL_NAMES, SMALL_TILE), pk(m_loc, SMALL_NAMES, SMALL_TILE),
                        pk(v_loc, SMALL_NAMES, SMALL_TILE), SMALL_TILE, "adamw_replicated")
    result = [{}, {}, {}, {}]
    for kind in range(4):
        for n, a in zip(BIG_NAMES, _unpack(outs_big[kind], big_shapes)):
            result[kind][n] = a
        for n, a in zip(SMALL_NAMES, _unpack(outs_small[kind], small_shapes)):
            result[kind][n] = a
    return (loss, grad_x[None], *[result[0][n] for n in WEIGHT_NAMES], *[result[1][n] for n in WEIGHT_NAMES],
            *[result[2][n] for n in WEIGHT_NAMES], *[result[3][n] for n in WEIGHT_NAMES])
```

```python
import functools

import numpy as np
import jax
import jax.numpy as jnp
from jax import lax
from jax.experimental import pallas as pl
from jax.experimental.pallas import tpu as pltpu

F32 = jnp.float32
BF16 = jnp.bfloat16
EPS = 1e-6
D_MODEL = 1024
DEPTH = 2
N_DEV = 8
SB_WIDTH = 512
SB_HEAD_DIM = 64
CONV_CH = 256
CONV_WIDTH = 31
SSM_CH = 256
SSM_GROUP = 16
SSM_GROUPS = 16
SSM_STATE = 64
SSM_LANES = SSM_GROUPS * SSM_STATE
XA_HEADS = 4
XA_HEAD_DIM = 256
FFN_HIDDEN = 2816
ADAM_LR = 0.001
ADAM_B1 = 0.9
ADAM_B2 = 0.999
ADAM_EPS = 1e-08
ADAM_WD = 0.01
ADAM_STEP = 10

LANES = 128
SUBLANES = 8
TL = 512
SB_TQ = 256
SB_TK = 128
SB_PAIRS = 2
SB_SCALE = SB_HEAD_DIM ** -0.5
SB_DEAD = -120.0
SSM_T = 256
CONV_HALO = 32
CONV_SUB = 64
VMEM_MB = 48

MESH_AXES = ("x", "y", "c")
WEIGHT_NAMES = ['norm_mix_g', 'w_in', 'sb_q_norm_g', 'sb_k_norm_g', 'conv_dw_w', 'conv_dw_b', 'conv_ln_g',
                'conv_ln_b', 'conv_pw2_w', 'ssm_lam_re', 'ssm_lam_im', 'ssm_log_dt', 'ssm_b_re', 'ssm_b_im',
                'ssm_c_re', 'ssm_c_im', 'ssm_d', 'ssm_glu_w', 'branch_norm_g', 'w_out', 'norm_xa_g',
                'norm_mem_g', 'xa_wq', 'xa_wk', 'xa_wv', 'xa_q_norm_g', 'xa_k_norm_g', 'xa_wo', 'norm_ffn_g',
                'ffn_w_in', 'ffn_w_out']
SHARD_AXIS = {'w_in': 2, 'conv_dw_w': 2, 'conv_pw2_w': 1, 'ssm_glu_w': 2, 'w_out': 1, 'xa_wq': 1, 'xa_wk': 1,
              'xa_wv': 1, 'xa_wo': 1, 'ffn_w_in': 2, 'ffn_w_out': 1}
BIG_NAMES = [n for n in WEIGHT_NAMES if n in SHARD_AXIS]
SMALL_NAMES = [n for n in WEIGHT_NAMES if n not in SHARD_AXIS]


def _nn(a, b):
    return jnp.dot(a.astype(BF16), b.astype(BF16), preferred_element_type=F32)


def _nt(a, b):
    return lax.dot_general(a.astype(BF16), b.astype(BF16), (((1,), (1,)), ((), ())), preferred_element_type=F32)


def _tn(a, b):
    return lax.dot_general(a.astype(BF16), b.astype(BF16), (((0,), (0,)), ((), ())), preferred_element_type=F32)


def _rms(x, g):
    return x * lax.rsqrt(jnp.mean(x * x, axis=-1, keepdims=True) + EPS) * g


def _sigmoid(x):
    return 1.0 / (1.0 + jnp.exp(-x))


def _silu(x):
    return x * _sigmoid(x)


def _layer_norm(x, g, b):
    mu = jnp.mean(x, axis=-1, keepdims=True)
    xc = x - mu
    var = jnp.mean(xc * xc, axis=-1, keepdims=True)
    return xc * lax.rsqrt(var + EPS) * g + b


def _head_rms64(p, g, mavg):
    ms = jnp.dot(p * p, mavg, preferred_element_type=F32)
    return p * lax.rsqrt(ms + EPS) * g


def _params(n_grid, vmem_mb=VMEM_MB):
    return pltpu.CompilerParams(dimension_semantics=("arbitrary",) * n_grid, vmem_limit_bytes=vmem_mb << 20)


def _rows(cols, tl=TL):
    return pl.BlockSpec((tl, cols), lambda i: (i, 0))


def _res(shape):
    nd = len(shape)
    return pl.BlockSpec(tuple(shape), lambda *_: (0,) * nd)


def _sds(shape, dtype=F32):
    return jax.ShapeDtypeStruct(tuple(shape), dtype)


def _accumulate(i, ref, val):
    @pl.when(i == 0)
    def _():
        ref[...] = val

    @pl.when(i > 0)
    def _():
        ref[...] += val


def _mixin_post(pq, pk, a, b, gq, gk, mavg):
    return _head_rms64(pq, gq, mavg), _head_rms64(pk, gk, mavg), a * _sigmoid(b)


def _mix_in_fwd(x, g_mix, w_in, gq, gk, mavg, tag):
    L = x.shape[0]

    def body(x_ref, g_ref, w_ref, gq_ref, gk_ref, mavg_ref, q_ref, k_ref, v_ref, hg_ref, u_ref):
        h = _rms(x_ref[...], g_ref[...])
        p = _nt(h, w_ref[...])
        q, k, hg = _mixin_post(p[:, 0:512], p[:, 512:1024], p[:, 1536:1792], p[:, 1792:2048],
                               gq_ref[...], gk_ref[...], mavg_ref[...])
        q_ref[...] = (q * SB_SCALE).astype(BF16)
        k_ref[...] = k.astype(BF16)
        v_ref[...] = p[:, 1024:1536].astype(BF16)
        hg_ref[...] = hg
        u_ref[...] = p[:, 2048:2304]

    return pl.pallas_call(
        body, name=f"mix_in_fwd_{tag}", grid=(L // TL,),
        in_specs=[_rows(D_MODEL), _res((1, D_MODEL)), _res(w_in.shape), _res((1, 512)), _res((1, 512)), _res((512, 512))],
        out_specs=[_rows(512), _rows(512), _rows(512), _rows(256), _rows(256)],
        out_shape=[_sds((L, 512), BF16), _sds((L, 512), BF16), _sds((L, 512), BF16), _sds((L, 256)), _sds((L, 256))],
        compiler_params=_params(1),
    )(x, g_mix, w_in, gq, gk, mavg)


def _mix_in_bwd(x, dres, dq, dk, dv, dhg, du, g_mix, w_in, gq, gk, mavg, tag):
    L = x.shape[0]
    tl = 256

    def body(x_ref, dres_ref, dq_ref, dk_ref, dv_ref, dhg_ref, du_ref, g_ref, w_ref, gq_ref, gk_ref, mavg_ref,
             dx_ref, dw_ref, dg_ref, dgq_ref, dgk_ref):
        i = pl.program_id(0)
        xx = x_ref[...]
        g = g_ref[...]
        mavg_v = mavg_ref[...]
        h, vjp_n = jax.vjp(_rms, xx, g)
        p = _nt(h, w_ref[...])
        _, vjp_p = jax.vjp(lambda pq, pk, a, b, gq_, gk_: _mixin_post(pq, pk, a, b, gq_, gk_, mavg_v),
                           p[:, 0:512], p[:, 512:1024], p[:, 1536:1792], p[:, 1792:2048], gq_ref[...], gk_ref[...])
        dpq, dpk, da, db, dgq, dgk = vjp_p((dq_ref[...], dk_ref[...], dhg_ref[...]))
        dp = jnp.concatenate([dpq, dpk, dv_ref[...], da, db, du_ref[...]], axis=1)
        dh = _nn(dp, w_ref[...])
        dxn, dg = vjp_n(dh)
        dx_ref[...] = dres_ref[...] + dxn
        _accumulate(i, dw_ref, _tn(dp, h))
        _accumulate(i, dg_ref, dg)
        _accumulate(i, dgq_ref, dgq)
        _accumulate(i, dgk_ref, dgk)

    r = lambda c: _rows(c, tl)
    return pl.pallas_call(
        body, name=f"mix_in_bwd_{tag}", grid=(L // tl,),
        in_specs=[r(D_MODEL), r(D_MODEL), r(512), r(512), r(512), r(256), r(256),
                  _res((1, D_MODEL)), _res(w_in.shape), _res((1, 512)), _res((1, 512)), _res((512, 512))],
        out_specs=[r(D_MODEL), _res(w_in.shape), _res((1, D_MODEL)), _res((1, 512)), _res((1, 512))],
        out_shape=[_sds((L, D_MODEL)), _sds(w_in.shape), _sds((1, D_MODEL)), _sds((1, 512)), _sds((1, 512))],
        compiler_params=_params(1),
    )(x, dres, dq, dk, dv, dhg, du, g_mix, w_in, gq, gk, mavg)


def _sb_tri_consts():
    r = np.arange(2 * SB_TK)[:, None]
    c = np.arange(2 * SB_TK)[None, :]
    same = (r // SB_TK) == (c // SB_TK)
    later = (same & (r > c)).astype(np.float32)
    earlier = (same & (r < c)).astype(np.float32)
    return jnp.asarray(later, BF16), jnp.asarray(earlier, BF16)


def _two_heads(blk, lane_a):
    zero = jnp.zeros_like(blk)
    return jnp.concatenate([jnp.where(lane_a, blk, zero), jnp.where(lane_a, zero, blk)], axis=0)


def _sb_logs(z, i, j, masked):
    e = jnp.exp(-jnp.abs(z))
    lm = -(jnp.maximum(z, 0.0) + jnp.log(1.0 + e))
    ls = z + lm
    valid = None
    if masked:
        row = lax.broadcasted_iota(jnp.int32, (SB_TQ, 2 * SB_TK), 0)
        col = lax.broadcasted_iota(jnp.int32, (SB_TQ, 2 * SB_TK), 1) & (SB_TK - 1)
        valid = (j * SB_TK + col) < (i * SB_TQ + row)
        lm = jnp.where(valid, lm, 0.0)
    hi = lm.astype(BF16)
    lo = (lm - hi.astype(F32)).astype(BF16)
    return lm, ls, hi, lo, valid


def _lane_halves(a, b):
    return jnp.concatenate([jnp.broadcast_to(a, (SB_TQ, SB_TK)), jnp.broadcast_to(b, (SB_TQ, SB_TK))], axis=1)


def _dot(a, b):
    return jnp.dot(a, b, preferred_element_type=F32)


def _dot_nt(a, b):
    return lax.dot_general(a, b, (((1,), (1,)), ((), ())), preferred_element_type=F32)


def _dot_tn(a, b):
    return lax.dot_general(a, b, (((0,), (0,)), ((), ())), preferred_element_type=F32)


def _sb_fwd(q, k, v, ul, tag):
    L = q.shape[0]
    nq = L // SB_TQ
    per = SB_TQ // SB_TK
    wid = SB_PAIRS * LANES

    def body(q_ref, k_ref, v_ref, ul_ref, o_ref, rs_ref, n_ref):
        i = pl.program_id(1)
        ulv = ul_ref[...]
        lane_a = lax.broadcasted_iota(jnp.int32, (1, LANES), 1) < SB_HEAD_DIM
        lane_q = lax.broadcasted_iota(jnp.int32, (SB_TQ, LANES), 1)
        cols = [slice(p * LANES, (p + 1) * LANES) for p in range(SB_PAIRS)]
        qbs = [q_ref[:, c] for c in cols]

        def double_step(jhi, carry, masked):
            chains = [dict(p=p, j=jhi - d) for d in range(2) for p in range(SB_PAIRS)]
            for c in chains:
                c['off'] = pl.multiple_of(c['j'] * SB_TK, SB_TK)
                kb = k_ref[pl.ds(c['off'], SB_TK), cols[c['p']]]
                c['z'] = _dot_nt(qbs[c['p']], _two_heads(kb, lane_a))
            for c in chains:
                c['lm'], c['ls'], c['hi'], c['lo'], c['valid'] = _sb_logs(c.pop('z'), i, c['j'], masked)
            for c in chains:
                c['lb'] = _dot(c.pop('hi'), ulv) + _dot(c.pop('lo'), ulv)
            state = [list(s) for s in carry]
            for c in chains:
                ra, rb, _, rsave = state[c['p']]
                w = jnp.exp(c['ls'] + c['lb'] + _lane_halves(ra, rb))
                if masked:
                    w = jnp.where(c['valid'], w, 0.0)
                c['w'] = w.astype(BF16)
                lb, lm = c['lb'], c['lm']
                state[c['p']][3] = jnp.where(lane_q == c['j'], ra, jnp.where(lane_q == c['j'] + SB_HEAD_DIM, rb, rsave))
                state[c['p']][0] = ra + lb[:, 0:1] + lm[:, 0:1]
                state[c['p']][1] = rb + lb[:, SB_TK:SB_TK + 1] + lm[:, SB_TK:SB_TK + 1]
            for c in chains:
                vb = v_ref[pl.ds(c['off'], SB_TK), cols[c['p']]]
                state[c['p']][2] = state[c['p']][2] + _dot(c['w'], _two_heads(vb, lane_a))
            return tuple(tuple(s) for s in state)

        assert per == 2
        carry = tuple((jnp.zeros((SB_TQ, 1), F32), jnp.zeros((SB_TQ, 1), F32),
                       jnp.zeros((SB_TQ, LANES), F32), jnp.zeros((SB_TQ, LANES), F32)) for _ in range(SB_PAIRS))
        def alive(carry):
            m = carry[0][0]
            for c in carry:
                m = jnp.maximum(m, jnp.maximum(c[0], c[1]))
            return jnp.max(m) > SB_DEAD

        carry = double_step(i * per + 1, carry, True)
        n_done, _, carry = lax.while_loop(
            lambda st: jnp.logical_and(st[0] < i, st[1]),
            lambda st: (lambda c: (st[0] + 1, alive(c), c))(double_step(i * per - 1 - 2 * st[0], st[2], False)),
            (jnp.int32(0), alive(carry), carry))
        o_ref[...] = jnp.concatenate([c[2] for c in carry], axis=1)
        rs_ref[...] = jnp.concatenate([c[3] for c in carry], axis=1)
        n_ref[pl.program_id(0), i] = n_done

    qspec = pl.BlockSpec((SB_TQ, wid), lambda g, i: (i, g))
    kspec = pl.BlockSpec((L, wid), lambda g, i: (0, g))
    return pl.pallas_call(
        body, name=f"sb_fwd_{tag}", grid=(SB_WIDTH // wid, nq),
        in_specs=[qspec, kspec, kspec, pl.BlockSpec((2 * SB_TK, 2 * SB_TK), lambda g, i: (0, 0))],
        out_specs=[qspec, qspec, pl.BlockSpec(memory_space=pltpu.SMEM)],
        out_shape=[_sds((L, SB_WIDTH)), _sds((L, SB_WIDTH)), _sds((SB_WIDTH // wid, nq), jnp.int32)],
        compiler_params=_params(2),
    )(q, k, v, ul)


def _sb_bwd(n_done, q, k, v, rsave, do, ul, ue, tag):
    L = q.shape[0]
    nq = L // SB_TQ
    per = SB_TQ // SB_TK
    wid = SB_PAIRS * LANES

    def body(n_ref, q_ref, k_ref, v_ref, rs_ref, do_ref, ul_ref, ue_ref, dq_ref, dk_ref, dv_ref):
        i = pl.program_id(1)

        @pl.when(i == 0)
        def _():
            dk_ref[...] = jnp.zeros_like(dk_ref)
            dv_ref[...] = jnp.zeros_like(dv_ref)

        ulv = ul_ref[...]
        uev = ue_ref[...]
        lane_a = lax.broadcasted_iota(jnp.int32, (1, LANES), 1) < SB_HEAD_DIM
        lane_q = lax.broadcasted_iota(jnp.int32, (SB_TQ, LANES), 1)
        cols = [slice(p * LANES, (p + 1) * LANES) for p in range(SB_PAIRS)]
        qbs = [q_ref[:, c] for c in cols]
        dobs = [do_ref[:, c].astype(BF16) for c in cols]
        rsvs = [rs_ref[:, c] for c in cols]

        def double_step(jlo, carry, masked):
            chains = [dict(p=p, j=jlo + d) for d in range(2) for p in range(SB_PAIRS)]
            for c in chains:
                p = c['p']
                c['off'] = pl.multiple_of(c['j'] * SB_TK, SB_TK)
                c['kk2'] = _two_heads(k_ref[pl.ds(c['off'], SB_TK), cols[p]], lane_a)
                c['z'] = _dot_nt(qbs[p], c['kk2'])
                c['dw'] = _dot_nt(dobs[p], _two_heads(v_ref[pl.ds(c['off'], SB_TK), cols[p]], lane_a))
            for c in chains:
                c['lm'], c['ls'], c['hi'], c['lo'], c['valid'] = _sb_logs(c.pop('z'), i, c['j'], masked)
                c['ra'] = jnp.sum(jnp.where(lane_q == c['j'], rsvs[c['p']], 0.0), axis=1, keepdims=True)
                c['rb'] = jnp.sum(jnp.where(lane_q == c['j'] + SB_HEAD_DIM, rsvs[c['p']], 0.0), axis=1, keepdims=True)
            for c in chains:
                c['lb'] = _dot(c.pop('hi'), ulv) + _dot(c.pop('lo'), ulv)
            for c in chains:
                w = jnp.exp(c['ls'] + c.pop('lb') + _lane_halves(c['ra'], c['rb']))
                if masked:
                    w = jnp.where(c['valid'], w, 0.0)
                c['wb'] = w.astype(BF16)
                gg = w * c.pop('dw')
                c['gg'] = gg
                c['ghi'] = gg.astype(BF16)
                c['glo'] = (gg - c['ghi'].astype(F32)).astype(BF16)
                c['beta'] = jnp.exp(c['ls'])
            for c in chains:
                c['cb'] = _dot(c.pop('ghi'), uev) + _dot(c.pop('glo'), uev)
                c['dv2'] = _dot_tn(c.pop('wb'), dobs[c['p']])
            state = [list(s) for s in carry]
            for c in chains:
                pa, pb, _ = state[c['p']]
                gg, cb, beta = c['gg'], c['cb'], c['beta']
                dz = gg * (1.0 - beta) - beta * (cb + _lane_halves(pa, pb))
                if masked:
                    dz = jnp.where(c['valid'], dz, 0.0)
                c['dzb'] = dz.astype(BF16)
                state[c['p']][0] = pa + cb[:, SB_TK - 1:SB_TK] + gg[:, SB_TK - 1:SB_TK]
                state[c['p']][1] = pb + cb[:, 2 * SB_TK - 1:2 * SB_TK] + gg[:, 2 * SB_TK - 1:2 * SB_TK]
            for c in chains:
                c['dqc'] = _dot(c['dzb'], c['kk2'])
                c['dk2'] = _dot_tn(c['dzb'], qbs[c['p']])
            for c in chains:
                p, dk2, dv2 = c['p'], c['dk2'], c['dv2']
                state[p][2] = state[p][2] + c['dqc']
                dk_ref[pl.ds(c['off'], SB_TK), cols[p]] += jnp.where(lane_a, dk2[0:SB_TK], dk2[SB_TK:2 * SB_TK])
                dv_ref[pl.ds(c['off'], SB_TK), cols[p]] += jnp.where(lane_a, dv2[0:SB_TK], dv2[SB_TK:2 * SB_TK])
            return tuple(tuple(s) for s in state)

        assert per == 2
        carry = tuple((jnp.zeros((SB_TQ, 1), F32), jnp.zeros((SB_TQ, 1), F32), jnp.zeros((SB_TQ, LANES), F32))
                      for _ in range(SB_PAIRS))
        first = i - n_ref[pl.program_id(0), i]
        carry = lax.fori_loop(first, i, lambda jj, c: double_step(2 * jj, c, False), carry)
        carry = double_step(i * per, carry, True)
        dq_ref[...] = jnp.concatenate([c[2] for c in carry], axis=1) * SB_SCALE

    qspec = pl.BlockSpec((SB_TQ, wid), lambda g, i: (i, g))
    kspec = pl.BlockSpec((L, wid), lambda g, i: (0, g))
    kin = pl.BlockSpec((L, wid), lambda g, i: (0, g), pipeline_mode=pl.Buffered(1))
    cspec = pl.BlockSpec((2 * SB_TK, 2 * SB_TK), lambda g, i: (0, 0))
    return pl.pallas_call(
        body, name=f"sb_bwd_{tag}", grid=(SB_WIDTH // wid, nq),
        in_specs=[pl.BlockSpec(memory_space=pltpu.SMEM), qspec, kin, kin, qspec, qspec, cspec, cspec],
        out_specs=[qspec, kspec, kspec],
        out_shape=[_sds((L, SB_WIDTH)), _sds((L, SB_WIDTH)), _sds((L, SB_WIDTH))],
        compiler_params=_params(2, 58),
    )(n_done, q, k, v, rsave, do, ul, ue)


def _conv_fwd(hp, w, tag):
    L = hp.shape[0] - CONV_HALO
    win_rows = CONV_SUB + CONV_HALO

    def body(hp_ref, w_ref, o_ref):
        i = pl.program_id(0)

        def sub(s, _):
            t0 = pl.multiple_of(i * TL + s * CONV_SUB, CONV_SUB)
            win = hp_ref[pl.ds(t0, win_rows), :]
            acc = jnp.zeros((CONV_SUB, CONV_CH), F32)
            for kk in range(CONV_WIDTH):
                sh = CONV_WIDTH - 1 - kk
                r = win if sh == 0 else pltpu.roll(win, sh, 0)
                acc = acc + w_ref[kk:kk + 1, :] * r[CONV_HALO:, :]
            o_ref[pl.ds(pl.multiple_of(s * CONV_SUB, CONV_SUB), CONV_SUB), :] = acc
            return 0

        lax.fori_loop(0, TL // CONV_SUB, sub, 0)

    return pl.pallas_call(
        body, name=f"conv_fwd_{tag}", grid=(L // TL,),
        in_specs=[_res(hp.shape), _res(w.shape)],
        out_specs=_rows(CONV_CH),
        out_shape=_sds((L, CONV_CH)),
        compiler_params=_params(1),
    )(hp, w)


def _conv_bwd(dpad, hp, w, tag):
    L = hp.shape[0] - CONV_HALO
    win_rows = CONV_SUB + CONV_HALO
    n_tiles = L // TL

    def body(dp_ref, hp_ref, w_ref, dh_ref, dw_ref, acc_ref):
        i = pl.program_id(0)

        @pl.when(i == 0)
        def _():
            acc_ref[...] = jnp.zeros_like(acc_ref)

        def sub(s, _):
            t0 = pl.multiple_of(i * TL + s * CONV_SUB, CONV_SUB)
            wd = dp_ref[pl.ds(t0, win_rows), :]
            wh = hp_ref[pl.ds(t0, win_rows), :]
            dy = wd[0:CONV_SUB, :]
            acc = jnp.zeros((CONV_SUB, CONV_CH), F32)
            for kk in range(CONV_WIDTH):
                sh = CONV_WIDTH - 1 - kk
                rd = wd if sh == 0 else pltpu.roll(wd, win_rows - sh, 0)
                acc = acc + w_ref[kk:kk + 1, :] * rd[0:CONV_SUB, :]
                rh = wh if sh == 0 else pltpu.roll(wh, sh, 0)
                prod = dy * rh[CONV_HALO:, :]
                part = prod[0:SUBLANES]
                for m in range(1, CONV_SUB // SUBLANES):
                    part = part + prod[m * SUBLANES:(m + 1) * SUBLANES]
                acc_ref[kk] += part
            dh_ref[pl.ds(pl.multiple_of(s * CONV_SUB, CONV_SUB), CONV_SUB), :] = acc
            return 0

        lax.fori_loop(0, TL // CONV_SUB, sub, 0)

        @pl.when(i == n_tiles - 1)
        def _():
            for kk in range(CONV_WIDTH):
                dw_ref[kk:kk + 1, :] = jnp.sum(acc_ref[kk], axis=0, keepdims=True)

    return pl.pallas_call(
        body, name=f"conv_bwd_{tag}", grid=(n_tiles,),
        in_specs=[_res(dpad.shape), _res(hp.shape), _res(w.shape)],
        out_specs=[_rows(CONV_CH), _res(w.shape)],
        out_shape=[_sds((L, CONV_CH)), _sds(w.shape)],
        scratch_shapes=[pltpu.VMEM((CONV_WIDTH, SUBLANES, CONV_CH), F32)],
        compiler_params=_params(1),
    )(dpad, hp, w)


def _ssm_mask():
    r = np.arange(SSM_CH)[:, None] // SSM_GROUP
    c = np.arange(SSM_LANES)[None, :] // SSM_STATE
    return jnp.asarray((r == c).astype(np.float32))


def _ssm_discretize(lr, li, ldt, brt, bit):
    dt = jnp.exp(ldt)
    mag = jnp.exp(lr * dt)
    ar = mag * jnp.cos(li * dt)
    ai = mag * jnp.sin(li * dt)
    den = lr * lr + li * li
    fr = ((ar - 1.0) * lr + ai * li) / den
    fi = (ai * lr - (ar - 1.0) * li) / den
    return ar, ai, fr * brt - fi * bit, fr * bit + fi * brt


def _block_diag(rows16, mask):
    return jnp.where(mask > 0.5, jnp.tile(rows16, (SSM_GROUPS, 1)), 0.0)


def _block_diag_t(full, mask):
    m = jnp.where(mask > 0.5, full, 0.0)
    out = m[0:SSM_GROUP]
    for g in range(1, SSM_GROUPS):
        out = out + m[g * SSM_GROUP:(g + 1) * SSM_GROUP]
    return out


def _ssm_prep(lr, li, ldt, brt, bit, crt, cit, mask, tag):
    def body(lr_ref, li_ref, ldt_ref, brt_ref, bit_ref, crt_ref, cit_ref, m_ref,
             ar_ref, ai_ref, bbr_ref, bbi_ref, cbr_ref, cbi_ref):
        ar, ai, bbr, bbi = _ssm_discretize(lr_ref[...], li_ref[...], ldt_ref[...], brt_ref[...], bit_ref[...])
        m = m_ref[...]
        ar_ref[...] = ar
        ai_ref[...] = ai
        bbr_ref[...] = _block_diag(bbr, m).astype(BF16)
        bbi_ref[...] = _block_diag(bbi, m).astype(BF16)
        cbr_ref[...] = _block_diag(crt_ref[...], m).astype(BF16)
        cbi_ref[...] = _block_diag(cit_ref[...], m).astype(BF16)

    row = _sds((1, SSM_LANES))
    blk = _sds((SSM_CH, SSM_LANES), BF16)
    return pl.pallas_call(body, name=f"ssm_prep_{tag}", out_shape=[row, row, blk, blk, blk, blk])(
        lr, li, ldt, brt, bit, crt, cit, mask)


def _ssm_prep_bwd(lr, li, ldt, brt, bit, mask, dar, dai, dbbr, dbbi, dcbr, dcbi, tag):
    def body(lr_ref, li_ref, ldt_ref, brt_ref, bit_ref, m_ref, dar_ref, dai_ref, dbbr_ref, dbbi_ref, dcbr_ref,
             dcbi_ref, dlr_ref, dli_ref, dldt_ref, dbrt_ref, dbit_ref, dcrt_ref, dcit_ref):
        m = m_ref[...]
        _, vjp = jax.vjp(_ssm_discretize, lr_ref[...], li_ref[...], ldt_ref[...], brt_ref[...], bit_ref[...])
        dlr, dli, dldt, dbrt, dbit = vjp((dar_ref[...], dai_ref[...], _block_diag_t(dbbr_ref[...], m),
                                          _block_diag_t(dbbi_ref[...], m)))
        dlr_ref[...] = dlr
        dli_ref[...] = dli
        dldt_ref[...] = dldt
        dbrt_ref[...] = dbrt
        dbit_ref[...] = dbit
        dcrt_ref[...] = _block_diag_t(dcbr_ref[...], m)
        dcit_ref[...] = _block_diag_t(dcbi_ref[...], m)

    row = _sds((1, SSM_LANES))
    r16 = _sds((SSM_GROUP, SSM_LANES))
    return pl.pallas_call(body, name=f"ssm_prep_bwd_{tag}", out_shape=[row, row, row, r16, r16, r16, r16])(
        lr, li, ldt, brt, bit, mask, dar, dai, dbbr, dbbi, dcbr, dcbi)


def _complex_scan(br, bi, ar, ai, reverse):
    n = br.shape[0]
    row = lax.broadcasted_iota(jnp.int32, (n, 1), 0)
    xr, xi, pr, pi = br, bi, ar, ai
    d = 1
    while d < n:
        if reverse:
            sr, si, keep = pltpu.roll(xr, n - d, 0), pltpu.roll(xi, n - d, 0), row < n - d
        else:
            sr, si, keep = pltpu.roll(xr, d, 0), pltpu.roll(xi, d, 0), row >= d
        sr = jnp.where(keep, sr, 0.0)
        si = jnp.where(keep, si, 0.0)
        xr, xi = xr + pr * sr - pi * si, xi + pr * si + pi * sr
        pr, pi = pr * pr - pi * pi, 2.0 * pr * pi
        d *= 2
    return xr, xi


def _ssm_fwd(u, ar, ai, bbr, bbi, cbr, cbi, dvec, tag):
    L = u.shape[0]
    T = SSM_T

    def body(u_ref, ar_ref, ai_ref, bbr_ref, bbi_ref, cbr_ref, cbi_ref, d_ref, y_ref, xr_ref, xi_ref, cr_ref, ci_ref):
        i = pl.program_id(0)

        @pl.when(i == 0)
        def _():
            cr_ref[...] = jnp.zeros_like(cr_ref)
            ci_ref[...] = jnp.zeros_like(ci_ref)

        uu = u_ref[...]
        a_r, a_i = ar_ref[...], ai_ref[...]
        c_r, c_i = cr_ref[...], ci_ref[...]
        first = lax.broadcasted_iota(jnp.int32, (T, 1), 0) == 0
        bur = _nn(uu, bbr_ref[...]) + jnp.where(first, a_r * c_r - a_i * c_i, 0.0)
        bui = _nn(uu, bbi_ref[...]) + jnp.where(first, a_r * c_i + a_i * c_r, 0.0)
        xr, xi = _complex_scan(bur, bui, a_r, a_i, False)
        cr_ref[...] = xr[T - 1:T, :]
        ci_ref[...] = xi[T - 1:T, :]
        xr_ref[...] = xr
        xi_ref[...] = xi
        y_ref[...] = _nt(xr, cbr_ref[...]) - _nt(xi, cbi_ref[...]) + d_ref[...] * uu

    blk = _res((SSM_CH, SSM_LANES))
    row = _res((1, SSM_LANES))
    return pl.pallas_call(
        body, name=f"ssm_fwd_{tag}", grid=(L // T,),
        in_specs=[_rows(SSM_CH, T), row, row, blk, blk, blk, blk, _res((1, SSM_CH))],
        out_specs=[_rows(SSM_CH, T), _rows(SSM_LANES, T), _rows(SSM_LANES, T)],
        out_shape=[_sds((L, SSM_CH)), _sds((L, SSM_LANES)), _sds((L, SSM_LANES))],
        scratch_shapes=[pltpu.VMEM((1, SSM_LANES), F32), pltpu.VMEM((1, SSM_LANES), F32)],
        compiler_params=_params(1),
    )(u, ar, ai, bbr, bbi, cbr, cbi, dvec)


def _ssm_bwd(dy, u, xr, xi, ar, ai, bbr, bbi, cbr, cbi, dvec, tag):
    L = u.shape[0]
    T = SSM_T
    nc = L // T

    def body(dy_ref, u_ref, xr_ref, xi_ref, pr_ref, pi_ref, ar_ref, ai_ref, bbr_ref, bbi_ref, cbr_ref, cbi_ref, d_ref,
             du_ref, dar_ref, dai_ref, dbbr_ref, dbbi_ref, dcbr_ref, dcbi_ref, dd_ref, gr_ref, gi_ref):
        i = pl.program_id(0)

        @pl.when(i == 0)
        def _():
            gr_ref[...] = jnp.zeros_like(gr_ref)
            gi_ref[...] = jnp.zeros_like(gi_ref)

        dyy = dy_ref[...]
        uu = u_ref[...]
        xr, xi = xr_ref[...], xi_ref[...]
        a_r, a_i = ar_ref[...], ai_ref[...]
        g_r, g_i = gr_ref[...], gi_ref[...]
        row = lax.broadcasted_iota(jnp.int32, (T, 1), 0)
        last = row == T - 1
        inr = _nn(dyy, cbr_ref[...]) + jnp.where(last, a_r * g_r + a_i * g_i, 0.0)
        ini = -_nn(dyy, cbi_ref[...]) + jnp.where(last, a_r * g_i - a_i * g_r, 0.0)
        gr, gi = _complex_scan(inr, ini, a_r, -a_i, True)
        gr_ref[...] = gr[0:1, :]
        gi_ref[...] = gi[0:1, :]
        has_prev = (i < nc - 1).astype(F32)
        pr = pr_ref[SUBLANES - 1:SUBLANES, :] * has_prev
        pi = pi_ref[SUBLANES - 1:SUBLANES, :] * has_prev
        sr = jnp.where(row == 0, pr, pltpu.roll(xr, 1, 0))
        si = jnp.where(row == 0, pi, pltpu.roll(xi, 1, 0))
        _accumulate(i, dar_ref, jnp.sum(gr * sr + gi * si, axis=0, keepdims=True))
        _accumulate(i, dai_ref, jnp.sum(gi * sr - gr * si, axis=0, keepdims=True))
        _accumulate(i, dbbr_ref, _tn(uu, gr))
        _accumulate(i, dbbi_ref, _tn(uu, gi))
        _accumulate(i, dcbr_ref, _tn(dyy, xr))
        _accumulate(i, dcbi_ref, -_tn(dyy, xi))
        _accumulate(i, dd_ref, jnp.sum(dyy * uu, axis=0, keepdims=True))
        du_ref[...] = _nt(gr, bbr_ref[...]) + _nt(gi, bbi_ref[...]) + dyy * d_ref[...]

    rev = lambda cols: pl.BlockSpec((T, cols), lambda i: (nc - 1 - i, 0))
    prev = pl.BlockSpec((SUBLANES, SSM_LANES), lambda i: (jnp.maximum((nc - 1 - i) * (T // SUBLANES) - 1, 0), 0))
    blk = _res((SSM_CH, SSM_LANES))
    row = _res((1, SSM_LANES))
    return pl.pallas_call(
        body, name=f"ssm_bwd_{tag}", grid=(nc,),
        in_specs=[rev(SSM_CH), rev(SSM_CH), rev(SSM_LANES), rev(SSM_LANES), prev, prev, row, row, blk, blk, blk, blk,
                  _res((1, SSM_CH))],
        out_specs=[rev(SSM_CH), row, row, blk, blk, blk, blk, _res((1, SSM_CH))],
        out_shape=[_sds((L, SSM_CH)), _sds((1, SSM_LANES)), _sds((1, SSM_LANES)), _sds((SSM_CH, SSM_LANES)),
                   _sds((SSM_CH, SSM_LANES)), _sds((SSM_CH, SSM_LANES)), _sds((SSM_CH, SSM_LANES)), _sds((1, SSM_CH))],
        scratch_shapes=[pltpu.VMEM((1, SSM_LANES), F32), pltpu.VMEM((1, SSM_LANES), F32)],
        compiler_params=_params(1),
    )(dy, u, xr, xi, xr, xi, ar, ai, bbr, bbi, cbr, cbi, dvec)


def _conv_post(hc, dw_b, ln_g, ln_b):
    return _silu(_layer_norm(hc + dw_b, ln_g, ln_b))


def _branch_mix(o_sb, o_conv, t, g1, g2, g3):
    o_ssm = t[:, 0:SSM_CH] * _sigmoid(t[:, SSM_CH:2 * SSM_CH])
    return jnp.concatenate([_rms(o_sb, g1), _rms(o_conv, g2), _rms(o_ssm, g3)], axis=1)


def _branch_mix_split(o_sb, o_conv, ta, tb, g1, g2, g3):
    return jnp.concatenate([_rms(o_sb, g1), _rms(o_conv, g2), _rms(ta * _sigmoid(tb), g3)], axis=1)


def _mix_out_fwd(x, o_sb, hc, y, dw_b, ln_g, ln_b, pw2, glu_w, g1, g2, g3, w_out, tag):
    L = x.shape[0]

    def body(x_ref, o_ref, hc_ref, y_ref, dwb_ref, lng_ref, lnb_ref, pw2_ref, glu_ref, g1_ref, g2_ref, g3_ref, wo_ref,
             out_ref):
        c1 = _conv_post(hc_ref[...], dwb_ref[...], lng_ref[...], lnb_ref[...])
        o_conv = _nn(c1, pw2_ref[...])
        t = _nt(y_ref[...], glu_ref[...])
        mixed = _branch_mix(o_ref[...], o_conv, t, g1_ref[...], g2_ref[...], g3_ref[...])
        out_ref[...] = x_ref[...] + _nn(mixed, wo_ref[...])

    v256 = _res((1, 256))
    return pl.pallas_call(
        body, name=f"mix_out_fwd_{tag}", grid=(L // TL,),
        in_specs=[_rows(D_MODEL), _rows(512), _rows(256), _rows(256), v256, v256, v256, _res(pw2.shape),
                  _res(glu_w.shape), _res((1, 512)), v256, v256, _res(w_out.shape)],
        out_specs=_rows(D_MODEL),
        out_shape=_sds((L, D_MODEL)),
        compiler_params=_params(1),
    )(x, o_sb, hc, y, dw_b, ln_g, ln_b, pw2, glu_w, g1, g2, g3, w_out)


def _mix_out_bwd(dx1, o_sb, hc, y, dw_b, ln_g, ln_b, pw2, glu_w, g1, g2, g3, w_out, tag):
    L = dx1.shape[0]

    def body(dx_ref, o_ref, hc_ref, y_ref, dwb_ref, lng_ref, lnb_ref, pw2_ref, glu_ref, g1_ref, g2_ref, g3_ref, wo_ref,
             do_ref, dhc_ref, dy_ref, ddwb_ref, dlng_ref, dlnb_ref, dpw2_ref, dglu_ref, dg1_ref, dg2_ref, dg3_ref,
             dwo_ref):
        i = pl.program_id(0)
        dxx = dx_ref[...]
        yy = y_ref[...]
        c1, vjp1 = jax.vjp(_conv_post, hc_ref[...], dwb_ref[...], lng_ref[...], lnb_ref[...])
        o_conv = _nn(c1, pw2_ref[...])
        t = _nt(yy, glu_ref[...])
        mixed, vjp2 = jax.vjp(_branch_mix_split, o_ref[...], o_conv, t[:, 0:SSM_CH], t[:, SSM_CH:2 * SSM_CH],
                              g1_ref[...], g2_ref[...], g3_ref[...])
        dmixed = _nt(dxx, wo_ref[...])
        do_sb, do_conv, dta, dtb, dg1, dg2, dg3 = vjp2(dmixed)
        dt = jnp.concatenate([dta, dtb], axis=1)
        dc1 = _nt(do_conv, pw2_ref[...])
        dhc, ddwb, dlng, dlnb = vjp1(dc1)
        do_ref[...] = do_sb
        dhc_ref[...] = dhc
        dy_ref[...] = _nn(dt, glu_ref[...])
        _accumulate(i, dwo_ref, _tn(mixed, dxx))
        _accumulate(i, dglu_ref, _tn(dt, yy))
        _accumulate(i, dpw2_ref, _tn(c1, do_conv))
        _accumulate(i, ddwb_ref, ddwb)
        _accumulate(i, dlng_ref, dlng)
        _accumulate(i, dlnb_ref, dlnb)
        _accumulate(i, dg1_ref, dg1)
        _accumulate(i, dg2_ref, dg2)
        _accumulate(i, dg3_ref, dg3)

    v256 = _res((1, 256))
    return pl.pallas_call(
        body, name=f"mix_out_bwd_{tag}", grid=(L // TL,),
        in_specs=[_rows(D_MODEL), _rows(512), _rows(256), _rows(256), v256, v256, v256, _res(pw2.shape),
                  _res(glu_w.shape), _res((1, 512)), v256, v256, _res(w_out.shape)],
        out_specs=[_rows(512), _rows(256), _rows(256), v256, v256, v256, _res(pw2.shape), _res(glu_w.shape),
                   _res((1, 512)), v256, v256, _res(w_out.shape)],
        out_shape=[_sds((L, 512)), _sds((L, 256)), _sds((L, 256)), _sds((1, 256)), _sds((1, 256)), _sds((1, 256)),
                   _sds(pw2.shape), _sds(glu_w.shape), _sds((1, 512)), _sds((1, 256)), _sds((1, 256)), _sds(w_out.shape)],
        compiler_params=_params(1),
    )(dx1, o_sb, hc, y, dw_b, ln_g, ln_b, pw2, glu_w, g1, g2, g3, w_out)


def _xa_heads_norm(kk, kg):
    return jnp.concatenate([_rms(kk[:, h * XA_HEAD_DIM:(h + 1) * XA_HEAD_DIM], kg) for h in range(XA_HEADS)], axis=1)


def _xa_mem_fwd(mem, g_mem, wk, wv, kg, tag):
    def body(m_ref, g_ref, wk_ref, wv_ref, kg_ref, k_ref, v_ref):
        hm = _rms(m_ref[...], g_ref[...])
        k_ref[...] = _xa_heads_norm(_nn(hm, wk_ref[...]), kg_ref[...])
        v_ref[...] = _nn(hm, wv_ref[...])

    return pl.pallas_call(body, name=f"xa_mem_fwd_{tag}", out_shape=[_sds(mem.shape), _sds(mem.shape)],
                          compiler_params=_params(0))(mem, g_mem, wk, wv, kg)


def _xa_mem_bwd(mem, g_mem, wk, wv, kg, dkx, dvx, tag):
    def body(m_ref, g_ref, wk_ref, wv_ref, kg_ref, dk_ref, dv_ref, dwk_ref, dwv_ref, dg_ref, dkg_ref):
        hm, vjp_n = jax.vjp(_rms, m_ref[...], g_ref[...])
        kk = _nn(hm, wk_ref[...])
        dvv = dv_ref[...]
        dkg = jnp.zeros((1, XA_HEAD_DIM), F32)
        parts = []
        for h in range(XA_HEADS):
            sl = slice(h * XA_HEAD_DIM, (h + 1) * XA_HEAD_DIM)
            _, vjp_h = jax.vjp(_rms, kk[:, sl], kg_ref[...])
            dkh, dgh = vjp_h(dk_ref[:, sl])
            parts.append(dkh)
            dkg = dkg + dgh
        dkk = jnp.concatenate(parts, axis=1)
        dwk_ref[...] = _tn(hm, dkk)
        dwv_ref[...] = _tn(hm, dvv)
        dhm = _nt(dkk, wk_ref[...]) + _nt(dvv, wv_ref[...])
        _, dg = vjp_n(dhm)
        dg_ref[...] = dg
        dkg_ref[...] = dkg

    return pl.pallas_call(
        body, name=f"xa_mem_bwd_{tag}",
        out_shape=[_sds(wk.shape), _sds(wv.shape), _sds((1, D_MODEL)), _sds((1, XA_HEAD_DIM))],
        compiler_params=_params(0))(mem, g_mem, wk, wv, kg, dkx, dvx)


def _xa_fwd(x1, kx, vx, g_xa, wq, qg, wo, tag):
    L = x1.shape[0]

    def body(x_ref, k_ref, v_ref, g_ref, wq_ref, qg_ref, wo_ref, out_ref):
        xx = x_ref[...]
        qp = _nn(_rms(xx, g_ref[...]), wq_ref[...])
        outs = []
        for h in range(XA_HEADS):
            sl = slice(h * XA_HEAD_DIM, (h + 1) * XA_HEAD_DIM)
            qh = _rms(qp[:, sl], qg_ref[...])
            s = _nt(qh, k_ref[:, sl]) * (XA_HEAD_DIM ** -0.5)
            s = s - jnp.max(s, axis=-1, keepdims=True)
            e = jnp.exp(s)
            p = e / jnp.sum(e, axis=-1, keepdims=True)
            outs.append(_nn(p, v_ref[:, sl]))
        out_ref[...] = xx + _nn(jnp.concatenate(outs, axis=1), wo_ref[...])

    return pl.pallas_call(
        body, name=f"xa_fwd_{tag}", grid=(L // TL,),
        in_specs=[_rows(D_MODEL), _res(kx.shape), _res(vx.shape), _res((1, D_MODEL)), _res(wq.shape),
                  _res((1, XA_HEAD_DIM)), _res(wo.shape)],
        out_specs=_rows(D_MODEL),
        out_shape=_sds((L, D_MODEL)),
        compiler_params=_params(1),
    )(x1, kx, vx, g_xa, wq, qg, wo)


def _xa_bwd(x1, dx2, kx, vx, g_xa, wq, qg, wo, tag):
    L = x1.shape[0]
    tl = 256

    def body(x_ref, dx_ref, k_ref, v_ref, g_ref, wq_ref, qg_ref, wo_ref,
             dx1_ref, dk_ref, dv_ref, dwq_ref, dwo_ref, dg_ref, dqg_ref):
        i = pl.program_id(0)
        xx = x_ref[...]
        dxx = dx_ref[...]
        hx, vjp_n = jax.vjp(_rms, xx, g_ref[...])
        qp = _nn(hx, wq_ref[...])
        do = _nt(dxx, wo_ref[...])
        outs, dqps, dks, dvs = [], [], [], []
        dqg = jnp.zeros((1, XA_HEAD_DIM), F32)
        for h in range(XA_HEADS):
            sl = slice(h * XA_HEAD_DIM, (h + 1) * XA_HEAD_DIM)
            kh, vh = k_ref[:, sl], v_ref[:, sl]
            qh, vjp_q = jax.vjp(_rms, qp[:, sl], qg_ref[...])
            s = _nt(qh, kh) * (XA_HEAD_DIM ** -0.5)
            s = s - jnp.max(s, axis=-1, keepdims=True)
            e = jnp.exp(s)
            p = e / jnp.sum(e, axis=-1, keepdims=True)
            outs.append(_nn(p, vh))
            doh = do[:, sl]
            dp = _nt(doh, vh)
            dvs.append(_tn(p, doh))
            ds = p * (dp - jnp.sum(dp * p, axis=-1, keepdims=True)) * (XA_HEAD_DIM ** -0.5)
            dks.append(_tn(ds, qh))
            dqh, dgh = vjp_q(_nn(ds, kh))
            dqps.append(dqh)
            dqg = dqg + dgh
        o = jnp.concatenate(outs, axis=1)
        dqp = jnp.concatenate(dqps, axis=1)
        dxn, dg = vjp_n(_nt(dqp, wq_ref[...]))
        dx1_ref[...] = dxx + dxn
        _accumulate(i, dk_ref, jnp.concatenate(dks, axis=1))
        _accumulate(i, dv_ref, jnp.concatenate(dvs, axis=1))
        _accumulate(i, dwq_ref, _tn(hx, dqp))
        _accumulate(i, dwo_ref, _tn(o, dxx))
        _accumulate(i, dg_ref, dg)
        _accumulate(i, dqg_ref, dqg)

    r = lambda c: _rows(c, tl)
    return pl.pallas_call(
        body, name=f"xa_bwd_{tag}", grid=(L // tl,),
        in_specs=[r(D_MODEL), r(D_MODEL), _res(kx.shape), _res(vx.shape), _res((1, D_MODEL)), _res(wq.shape),
                  _res((1, XA_HEAD_DIM)), _res(wo.shape)],
        out_specs=[r(D_MODEL), _res(kx.shape), _res(vx.shape), _res(wq.shape), _res(wo.shape), _res((1, D_MODEL)),
                   _res((1, XA_HEAD_DIM))],
        out_shape=[_sds((L, D_MODEL)), _sds(kx.shape), _sds(vx.shape), _sds(wq.shape), _sds(wo.shape),
                   _sds((1, D_MODEL)), _sds((1, XA_HEAD_DIM))],
        compiler_params=_params(1),
    )(x1, dx2, kx, vx, g_xa, wq, qg, wo)


def _swiglu(gate, up):
    return _silu(gate) * up


def _ffn_fwd(x2, g, w_in, w_out, tag):
    L = x2.shape[0]
    tl = 256

    def body(x_ref, g_ref, wi_ref, wo_ref, out_ref):
        xx = x_ref[...]
        gu = _nt(_rms(xx, g_ref[...]), wi_ref[...])
        act = _swiglu(gu[:, 0:FFN_HIDDEN], gu[:, FFN_HIDDEN:2 * FFN_HIDDEN])
        out_ref[...] = xx + _nn(act, wo_ref[...])

    return pl.pallas_call(
        body, name=f"ffn_fwd_{tag}", grid=(L // tl,),
        in_specs=[_rows(D_MODEL, tl), _res((1, D_MODEL)), _res(w_in.shape), _res(w_out.shape)],
        out_specs=_rows(D_MODEL, tl),
        out_shape=_sds((L, D_MODEL)),
        compiler_params=_params(1, 56),
    )(x2, g, w_in, w_out)


def _ffn_bwd(x2, dx3, g, w_in, w_out, tag):
    L = x2.shape[0]
    tl = 256

    def body(x_ref, dx_ref, g_ref, wi_ref, wo_ref, dx2_ref, dgu_ref, act_ref, hf_ref, dg_ref):
        i = pl.program_id(0)
        xx = x_ref[...]
        dxx = dx_ref[...]
        hf, vjp_n = jax.vjp(_rms, xx, g_ref[...])
        gu = _nt(hf, wi_ref[...])
        act, vjp_a = jax.vjp(_swiglu, gu[:, 0:FFN_HIDDEN], gu[:, FFN_HIDDEN:2 * FFN_HIDDEN])
        dgate, dup = vjp_a(_nt(dxx, wo_ref[...]))
        dgu = jnp.concatenate([dgate, dup], axis=1).astype(BF16)
        dxn, dg = vjp_n(_nn(dgu, wi_ref[...]))
        dx2_ref[...] = dxx + dxn
        dgu_ref[...] = dgu
        act_ref[...] = act.astype(BF16)
        hf_ref[...] = hf.astype(BF16)
        _accumulate(i, dg_ref, dg)

    r = lambda c: _rows(c, tl)
    return pl.pallas_call(
        body, name=f"ffn_bwd_{tag}", grid=(L // tl,),
        in_specs=[r(D_MODEL), r(D_MODEL), _res((1, D_MODEL)), _res(w_in.shape), _res(w_out.shape)],
        out_specs=[r(D_MODEL), r(2 * FFN_HIDDEN), r(FFN_HIDDEN), r(D_MODEL), _res((1, D_MODEL))],
        out_shape=[_sds((L, D_MODEL)), _sds((L, 2 * FFN_HIDDEN), BF16), _sds((L, FFN_HIDDEN), BF16),
                   _sds((L, D_MODEL), BF16), _sds((1, D_MODEL))],
        compiler_params=_params(1, 56),
    )(x2, dx3, g, w_in, w_out)


def _matmul_tn(a, b, tm, tn, tag):
    L, M = a.shape
    N = b.shape[1]
    tk = 512

    def body(a_ref, b_ref, o_ref):
        _accumulate(pl.program_id(2), o_ref, _tn(a_ref[...], b_ref[...]))

    return pl.pallas_call(
        body, name=f"matmul_tn_{tag}", grid=(M // tm, N // tn, L // tk),
        in_specs=[pl.BlockSpec((tk, tm), lambda m, n, k: (k, m)), pl.BlockSpec((tk, tn), lambda m, n, k: (k, n))],
        out_specs=pl.BlockSpec((tm, tn), lambda m, n, k: (m, n)),
        out_shape=_sds((M, N)),
        compiler_params=_params(3),
    )(a, b)


def _loss_head(y, tgt):
    L = y.shape[0]
    n_tiles = L // TL

    def body(y_ref, t_ref, loss_ref, dy_ref, acc_ref):
        i = pl.program_id(0)
        diff = y_ref[...] - t_ref[...]
        dy_ref[...] = diff * (1.0 / D_MODEL)
        _accumulate(i, acc_ref, jnp.sum(diff * diff, axis=0, keepdims=True))

        @pl.when(i == n_tiles - 1)
        def _():
            loss_ref[...] = jnp.sum(acc_ref[...], axis=1, keepdims=True) * (0.5 / D_MODEL)

    return pl.pallas_call(
        body, name="loss_head", grid=(n_tiles,),
        in_specs=[_rows(D_MODEL), _rows(D_MODEL)],
        out_specs=[_res((1, 1)), _rows(D_MODEL)],
        out_shape=[_sds((1, 1)), _sds((L, D_MODEL))],
        scratch_shapes=[pltpu.VMEM((1, D_MODEL), F32)],
        compiler_params=_params(1),
    )(y, tgt)


def _row(v):
    return v.reshape(1, -1)


def _layer_consts():
    r = np.arange(SB_WIDTH)
    mavg = ((r[:, None] // SB_HEAD_DIM) == (r[None, :] // SB_HEAD_DIM)).astype(np.float32) / SB_HEAD_DIM
    ul, ue = _sb_tri_consts()
    return dict(mavg=jnp.asarray(mavg), ul=ul, ue=ue, mask=_ssm_mask())


def _ssm_rows(P):
    lanes = lambda a: a.reshape(1, SSM_LANES)
    return dict(
        lr=lanes(P['ssm_lam_re']), li=lanes(P['ssm_lam_im']),
        ldt=lanes(jnp.repeat(P['ssm_log_dt'], SSM_STATE)),
        brt=P['ssm_b_re'].transpose(2, 0, 1).reshape(SSM_GROUP, SSM_LANES),
        bit=P['ssm_b_im'].transpose(2, 0, 1).reshape(SSM_GROUP, SSM_LANES),
        crt=P['ssm_c_re'].transpose(1, 0, 2).reshape(SSM_GROUP, SSM_LANES),
        cit=P['ssm_c_im'].transpose(1, 0, 2).reshape(SSM_GROUP, SSM_LANES))


def _layer_fwd(x, mem, P, C, tag):
    gq = _row(jnp.tile(P['sb_q_norm_g'], SB_WIDTH // SB_HEAD_DIM))
    gk = _row(jnp.tile(P['sb_k_norm_g'], SB_WIDTH // SB_HEAD_DIM))
    q, k, v, hg, u = _mix_in_fwd(x, _row(P['norm_mix_g']), P['w_in'], gq, gk, C['mavg'], tag)
    o_sb, rsave, n_done = _sb_fwd(q, k, v, C['ul'], tag)
    hp = jnp.pad(hg, ((CONV_HALO, 0), (0, 0)))
    hc = _conv_fwd(hp, P['conv_dw_w'].T.astype(F32), tag)
    S = _ssm_rows(P)
    ar, ai, bbr, bbi, cbr, cbi = _ssm_prep(S['lr'], S['li'], S['ldt'], S['brt'], S['bit'], S['crt'], S['cit'],
                                           C['mask'], tag)
    y, xr, xi = _ssm_fwd(u, ar, ai, bbr, bbi, cbr, cbi, _row(P['ssm_d']), tag)
    gb = P['branch_norm_g']
    x1 = _mix_out_fwd(x, o_sb, hc, y, _row(P['conv_dw_b']), _row(P['conv_ln_g']), _row(P['conv_ln_b']),
                      P['conv_pw2_w'], P['ssm_glu_w'], _row(gb[0:512]), _row(gb[512:768]), _row(gb[768:1024]),
                      P['w_out'], tag)
    kx, vx = _xa_mem_fwd(mem, _row(P['norm_mem_g']), P['xa_wk'], P['xa_wv'], _row(P['xa_k_norm_g']), tag)
    x2 = _xa_fwd(x1, kx, vx, _row(P['norm_xa_g']), P['xa_wq'], _row(P['xa_q_norm_g']), P['xa_wo'], tag)
    x3 = _ffn_fwd(x2, _row(P['norm_ffn_g']), P['ffn_w_in'], P['ffn_w_out'], tag)
    saved = dict(x=x, q=q, k=k, v=v, rsave=rsave, n_done=n_done, o_sb=o_sb, hp=hp, hc=hc, u=u, y=y, xr=xr, xi=xi, x1=x1, x2=x2,
                 kx=kx, vx=vx, gq=gq, gk=gk, S=S, ssm=(ar, ai, bbr, bbi, cbr, cbi))
    return x3, saved


def _layer_bwd(dx3, mem, P, C, sv, tag):
    G = {}
    dx2, dgu, act, hf, dg = _ffn_bwd(sv['x2'], dx3, _row(P['norm_ffn_g']), P['ffn_w_in'], P['ffn_w_out'], tag)
    G['norm_ffn_g'] = dg.reshape(-1)
    G['ffn_w_in'] = _matmul_tn(dgu, hf, 2 * FFN_HIDDEN // 4, D_MODEL, "ffn_in_" + tag)
    G['ffn_w_out'] = _matmul_tn(act, dx3, FFN_HIDDEN // 2, 512, "ffn_out_" + tag)
    dx1, dkx, dvx, dwq, dwo, dg, dqg = _xa_bwd(sv['x1'], dx2, sv['kx'], sv['vx'], _row(P['norm_xa_g']), P['xa_wq'],
                                               _row(P['xa_q_norm_g']), P['xa_wo'], tag)
    G['xa_wq'], G['xa_wo'], G['norm_xa_g'], G['xa_q_norm_g'] = dwq, dwo, dg.reshape(-1), dqg.reshape(-1)
    dwk, dwv, dg, dkg = _xa_mem_bwd(mem, _row(P['norm_mem_g']), P['xa_wk'], P['xa_wv'], _row(P['xa_k_norm_g']),
                                    dkx, dvx, tag)
    G['xa_wk'], G['xa_wv'], G['norm_mem_g'], G['xa_k_norm_g'] = dwk, dwv, dg.reshape(-1), dkg.reshape(-1)
    gb = P['branch_norm_g']
    (do_sb, dhc, dy, ddwb, dlng, dlnb, dpw2, dglu, dg1, dg2, dg3, dwout) = _mix_out_bwd(
        dx1, sv['o_sb'], sv['hc'], sv['y'], _row(P['conv_dw_b']), _row(P['conv_ln_g']), _row(P['conv_ln_b']),
        P['conv_pw2_w'], P['ssm_glu_w'], _row(gb[0:512]), _row(gb[512:768]), _row(gb[768:1024]), P['w_out'], tag)
    G['conv_dw_b'], G['conv_ln_g'], G['conv_ln_b'] = ddwb.reshape(-1), dlng.reshape(-1), dlnb.reshape(-1)
    G['conv_pw2_w'], G['ssm_glu_w'], G['w_out'] = dpw2, dglu, dwout
    G['branch_norm_g'] = jnp.concatenate([dg1.reshape(-1), dg2.reshape(-1), dg3.reshape(-1)])
    ar, ai, bbr, bbi, cbr, cbi = sv['ssm']
    du, dar, dai, dbbr, dbbi, dcbr, dcbi, dd = _ssm_bwd(dy, sv['u'], sv['xr'], sv['xi'], ar, ai, bbr, bbi, cbr, cbi,
                                                        _row(P['ssm_d']), tag)
    S = sv['S']
    dlr, dli, dldt, dbrt, dbit, dcrt, dcit = _ssm_prep_bwd(S['lr'], S['li'], S['ldt'], S['brt'], S['bit'], C['mask'],
                                                           dar, dai, dbbr, dbbi, dcbr, dcbi, tag)
    G['ssm_lam_re'] = dlr.reshape(SSM_GROUPS, SSM_STATE)
    G['ssm_lam_im'] = dli.reshape(SSM_GROUPS, SSM_STATE)
    G['ssm_log_dt'] = dldt.reshape(SSM_GROUPS, SSM_STATE).sum(axis=1)
    G['ssm_b_re'] = dbrt.reshape(SSM_GROUP, SSM_GROUPS, SSM_STATE).transpose(1, 2, 0)
    G['ssm_b_im'] = dbit.reshape(SSM_GROUP, SSM_GROUPS, SSM_STATE).transpose(1, 2, 0)
    G['ssm_c_re'] = dcrt.reshape(SSM_GROUP, SSM_GROUPS, SSM_STATE).transpose(1, 0, 2)
    G['ssm_c_im'] = dcit.reshape(SSM_GROUP, SSM_GROUPS, SSM_STATE).transpose(1, 0, 2)
    G['ssm_d'] = dd.reshape(-1)
    dpad = jnp.pad(dhc, ((0, CONV_HALO), (0, 0)))
    dhg, ddww = _conv_bwd(dpad, sv['hp'], P['conv_dw_w'].T.astype(F32), tag)
    G['conv_dw_w'] = ddww.T
    dq, dk, dv = _sb_bwd(sv['n_done'], sv['q'], sv['k'], sv['v'], sv['rsave'], do_sb, C['ul'], C['ue'], tag)
    dx, dwin, dg, dgq, dgk = _mix_in_bwd(sv['x'], dx1, dq, dk, dv, dhg, du, _row(P['norm_mix_g']), P['w_in'],
                                         sv['gq'], sv['gk'], C['mavg'], tag)
    G['w_in'], G['norm_mix_g'] = dwin, dg.reshape(-1)
    G['sb_q_norm_g'] = dgq.reshape(SB_WIDTH // SB_HEAD_DIM, SB_HEAD_DIM).sum(axis=0)
    G['sb_k_norm_g'] = dgk.reshape(SB_WIDTH // SB_HEAD_DIM, SB_HEAD_DIM).sum(axis=0)
    return dx, G


def _local_step(x, mem, tgt, layers):
    C = _layer_consts()
    saved = []
    h = x
    for l in range(DEPTH):
        h, sv = _layer_fwd(h, mem, layers[l], C, f"l{l}")
        saved.append(sv)
    loss, dh = _loss_head(h, tgt)
    grads = [None] * DEPTH
    for l in reversed(range(DEPTH)):
        dh, grads[l] = _layer_bwd(dh, mem, layers[l], C, saved[l], f"l{l}")
    return loss, dh, grads


def _my_index():
    return lax.axis_index("x") * 4 + lax.axis_index("y") * 2 + lax.axis_index("c")


def _peer(k):
    x, y, c = lax.axis_index("x"), lax.axis_index("y"), lax.axis_index("c")
    return (x ^ ((k >> 2) & 1), y ^ ((k >> 1) & 1), c ^ (k & 1))


def _peer_index(k):
    return _my_index() ^ k


HBM = pl.BlockSpec(memory_space=pltpu.HBM)


def _all_gather(x, name):
    def body(x_ref, out_ref, send_sems, recv_sems, local_sem):
        me = _my_index()
        mine = pltpu.make_async_copy(x_ref, out_ref.at[me], local_sem)
        mine.start()
        sends = []
        for k in range(1, N_DEV):
            cp = pltpu.make_async_remote_copy(src_ref=x_ref, dst_ref=out_ref.at[me], send_sem=send_sems.at[k],
                                              recv_sem=recv_sems.at[k], device_id=_peer(k),
                                              device_id_type=pl.DeviceIdType.MESH)
            cp.start()
            sends.append(cp)
        for k in range(1, N_DEV):
            pltpu.make_async_remote_copy(src_ref=x_ref, dst_ref=out_ref.at[_peer_index(k)], send_sem=send_sems.at[k],
                                         recv_sem=recv_sems.at[k], device_id=_peer(k),
                                         device_id_type=pl.DeviceIdType.MESH).wait_recv()
        for cp in sends:
            cp.wait_send()
        mine.wait()

    return pl.pallas_call(
        body, name=name, in_specs=[HBM], out_specs=HBM,
        out_shape=_sds((N_DEV,) + x.shape, x.dtype),
        scratch_shapes=[pltpu.SemaphoreType.DMA((N_DEV,)), pltpu.SemaphoreType.DMA((N_DEV,)), pltpu.SemaphoreType.DMA],
    )(x)


def _grad_exchange(big, small, name):
    def body(big_ref, small_ref, rbig_ref, rsmall_ref, send_sems, recv_sems, local_sems):
        me = _my_index()
        loc_b = pltpu.make_async_copy(big_ref.at[me], rbig_ref.at[me], local_sems.at[0])
        loc_s = pltpu.make_async_copy(small_ref, rsmall_ref.at[me], local_sems.at[1])
        loc_b.start()
        loc_s.start()
        sends = []
        for k in range(1, N_DEV):
            cb = pltpu.make_async_remote_copy(src_ref=big_ref.at[_peer_index(k)], dst_ref=rbig_ref.at[me],
                                              send_sem=send_sems.at[k], recv_sem=recv_sems.at[k], device_id=_peer(k),
                                              device_id_type=pl.DeviceIdType.MESH)
            cs = pltpu.make_async_remote_copy(src_ref=small_ref, dst_ref=rsmall_ref.at[me],
                                              send_sem=send_sems.at[N_DEV + k], recv_sem=recv_sems.at[N_DEV + k],
                                              device_id=_peer(k), device_id_type=pl.DeviceIdType.MESH)
            cb.start()
            cs.start()
            sends += [cb, cs]
        for k in range(1, N_DEV):
            pk = _peer_index(k)
            pltpu.make_async_remote_copy(src_ref=big_ref.at[pk], dst_ref=rbig_ref.at[pk], send_sem=send_sems.at[k],
                                         recv_sem=recv_sems.at[k], device_id=_peer(k),
                                         device_id_type=pl.DeviceIdType.MESH).wait_recv()
            pltpu.make_async_remote_copy(src_ref=small_ref, dst_ref=rsmall_ref.at[pk], send_sem=send_sems.at[N_DEV + k],
                                         recv_sem=recv_sems.at[N_DEV + k], device_id=_peer(k),
                                         device_id_type=pl.DeviceIdType.MESH).wait_recv()
        for cp in sends:
            cp.wait_send()
        loc_b.wait()
        loc_s.wait()

    return pl.pallas_call(
        body, name=name, in_specs=[HBM, HBM], out_specs=[HBM, HBM],
        out_shape=[_sds(big.shape, big.dtype), _sds((N_DEV,) + small.shape, small.dtype)],
        scratch_shapes=[pltpu.SemaphoreType.DMA((2 * N_DEV,)), pltpu.SemaphoreType.DMA((2 * N_DEV,)),
                        pltpu.SemaphoreType.DMA((2,))],
    )(big, small)


def _adamw(recv, w, m, v, tile, name):
    n_slots, R, _ = recv.shape
    c1 = 1.0 / (1.0 - ADAM_B1 ** ADAM_STEP)
    c2 = 1.0 / (1.0 - ADAM_B2 ** ADAM_STEP)

    def body(r_ref, w_ref, m_ref, v_ref, g_ref, d_ref, nm_ref, nv_ref):
        g = r_ref[0].astype(F32)
        for s in range(1, n_slots):
            g = g + r_ref[s].astype(F32)
        nm = ADAM_B1 * m_ref[...] + (1.0 - ADAM_B1) * g
        nv = ADAM_B2 * v_ref[...] + (1.0 - ADAM_B2) * (g * g)
        g_ref[...] = g
        nm_ref[...] = nm
        nv_ref[...] = nv
        d_ref[...] = -ADAM_LR * ((nm * c1) / (jnp.sqrt(nv * c2) + ADAM_EPS) + ADAM_WD * w_ref[...])

    rows = pl.BlockSpec((tile, LANES), lambda i: (i, 0))
    out = _sds((R, LANES))
    return pl.pallas_call(
        body, name=name, grid=(R // tile,),
        in_specs=[pl.BlockSpec((n_slots, tile, LANES), lambda i: (0, i, 0)), rows, rows, rows],
        out_specs=[rows, rows, rows, rows],
        out_shape=[out, out, out, out],
        compiler_params=_params(1),
    )(recv, w, m, v)


def _reduce_slots(recv, tile, name):
    n_slots, R, _ = recv.shape

    def body(r_ref, g_ref):
        g = r_ref[0].astype(F32)
        for s in range(1, n_slots):
            g = g + r_ref[s].astype(F32)
        g_ref[...] = g

    return pl.pallas_call(
        body, name=name, grid=(R // tile,),
        in_specs=[pl.BlockSpec((n_slots, tile, LANES), lambda i: (0, i, 0))],
        out_specs=pl.BlockSpec((tile, LANES), lambda i: (i, 0)),
        out_shape=_sds((R, LANES)),
        compiler_params=_params(1),
    )(recv)


SEG = SUBLANES * LANES


def _pad_to(n, mult):
    return -(-n // mult) * mult


def _pack(arrays, dtype, row_mult, lead=0):
    keep = [(0, 0)] * lead
    parts = []
    for a in arrays:
        flat = a.reshape(a.shape[:lead] + (-1,)).astype(dtype)
        n = flat.shape[-1]
        parts.append(jnp.pad(flat, keep + [(0, _pad_to(n, SEG) - n)]))
    flat = jnp.concatenate(parts, axis=-1)
    n = flat.shape[-1]
    flat = jnp.pad(flat, keep + [(0, _pad_to(n, row_mult * LANES) - n)])
    return flat.reshape(flat.shape[:lead] + (-1, LANES))


def _unpack(buf, shapes):
    lead = buf.shape[:-2]
    flat = buf.reshape(lead + (-1,))
    out, off = [], 0
    for shp in shapes:
        n = int(np.prod(shp))
        out.append(flat[..., off:off + n].reshape(lead + tuple(shp)))
        off += _pad_to(n, SEG)
    return out


def _rows_first(a, name):
    return a.transpose(0, 2, 1) if SHARD_AXIS[name] == 2 else a


BIG_TILE = 512
SMALL_TILE = 256


def kernel(x, mem, norm_mix_g, w_in, sb_q_norm_g, sb_k_norm_g, conv_dw_w, conv_dw_b, conv_ln_g, conv_ln_b, conv_pw2_w, ssm_lam_re, ssm_lam_im, ssm_log_dt, ssm_b_re, ssm_b_im, ssm_c_re, ssm_c_im, ssm_d, ssm_glu_w, branch_norm_g, w_out, norm_xa_g, norm_mem_g, xa_wq, xa_wk, xa_wv, xa_q_norm_g, xa_k_norm_g, xa_wo, norm_ffn_g, ffn_w_in, ffn_w_out, loss_target, m_norm_mix_g, m_w_in, m_sb_q_norm_g, m_sb_k_norm_g, m_conv_dw_w, m_conv_dw_b, m_conv_ln_g, m_conv_ln_b, m_conv_pw2_w, m_ssm_lam_re, m_ssm_lam_im, m_ssm_log_dt, m_ssm_b_re, m_ssm_b_im, m_ssm_c_re, m_ssm_c_im, m_ssm_d, m_ssm_glu_w, m_branch_norm_g, m_w_out, m_norm_xa_g, m_norm_mem_g, m_xa_wq, m_xa_wk, m_xa_wv, m_xa_q_norm_g, m_xa_k_norm_g, m_xa_wo, m_norm_ffn_g, m_ffn_w_in, m_ffn_w_out, v_norm_mix_g, v_w_in, v_sb_q_norm_g, v_sb_k_norm_g, v_conv_dw_w, v_conv_dw_b, v_conv_ln_g, v_conv_ln_b, v_conv_pw2_w, v_ssm_lam_re, v_ssm_lam_im, v_ssm_log_dt, v_ssm_b_re, v_ssm_b_im, v_ssm_c_re, v_ssm_c_im, v_ssm_d, v_ssm_glu_w, v_branch_norm_g, v_w_out, v_norm_xa_g, v_norm_mem_g, v_xa_wq, v_xa_wk, v_xa_wv, v_xa_q_norm_g, v_xa_k_norm_g, v_xa_wo, v_norm_ffn_g, v_ffn_w_in, v_ffn_w_out):
    args = locals()
    w_loc = {n: args[n] for n in WEIGHT_NAMES}
    m_loc = {n: args["m_" + n] for n in WEIGHT_NAMES}
    v_loc = {n: args["v_" + n] for n in WEIGHT_NAMES}
    big_shapes = [w_loc[n].shape for n in BIG_NAMES]
    small_shapes = [w_loc[n].shape for n in SMALL_NAMES]

    rows_first = {n: _rows_first(w_loc[n], n) for n in BIG_NAMES}
    segs = [(l, n) for l in range(DEPTH) for n in BIG_NAMES]
    seg_shapes = [rows_first[n].shape[1:] for (_, n) in segs]
    gathered = _all_gather(_pack([rows_first[n][l] for (l, n) in segs], BF16, BIG_TILE), "weights_all_gather")
    layers = [{n: w_loc[n][l] for n in SMALL_NAMES} for l in range(DEPTH)]
    for (l, n), sh in zip(segs, _unpack(gathered, seg_shapes)):
        layers[l][n] = sh.reshape((N_DEV * sh.shape[1],) + sh.shape[2:])

    loss_part, grad_x, grads = _local_step(x[0], mem[0], loss_target[0], layers)
    loss = lax.psum(loss_part[0, 0], MESH_AXES)

    send_big = _pack([grads[l][n].reshape(N_DEV, -1) for (l, n) in segs], BF16, BIG_TILE, lead=1)
    send_small = _pack([jnp.stack([grads[l][n] for l in range(DEPTH)]) for n in SMALL_NAMES], F32, SMALL_TILE)
    recv_big, recv_small = _grad_exchange(send_big, send_small, "grad_exchange")
    g_rows_first = dict(zip(segs, _unpack(_reduce_slots(recv_big, BIG_TILE, "grad_reduce"), seg_shapes)))
    g_big = []
    for n in BIG_NAMES:
        g = jnp.stack([g_rows_first[(l, n)] for l in range(DEPTH)])
        g_big.append(_rows_first(g, n))
    pk = lambda d, names, tile: _pack([d[n] for n in names], F32, tile)
    outs_big = _adamw(_pack(g_big, F32, BIG_TILE)[None], pk(w_loc, BIG_NAMES, BIG_TILE), pk(m_loc, BIG_NAMES, BIG_TILE),
                      pk(v_loc, BIG_NAMES, BIG_TILE), BIG_TILE, "adamw_sharded")
    outs_small = _adamw(recv_small, pk(w_loc, SMALL_NAMES, SMALL_TILE), pk(m_loc, SMALL_NAMES, SMALL_TILE),
                        pk(v_loc, SMALL_NAMES, SMALL_TILE), SMALL_TILE, "adamw_replicated")
    result = [{}, {}, {}, {}]
    for kind in range(4):
        for n, a in zip(BIG_NAMES, _unpack(outs_big[kind], big_shapes)):
            result[kind][n] = a
        for n, a in zip(SMALL_NAMES, _unpack(outs_small[kind], small_shapes)):
            result[kind][n] = a
    return (loss, grad_x[None], *[result[0][n] for n in WEIGHT_NAMES], *[result[1][n] for n in WEIGHT_NAMES],
            *[result[2][n] for n in WEIGHT_NAMES], *[result[3][n] for n in WEIGHT_NAMES])
```

```python
import functools

import numpy as np
import jax
import jax.numpy as jnp
from jax import lax
from jax.experimental import pallas as pl
from jax.experimental.pallas import tpu as pltpu

F32 = jnp.float32
BF16 = jnp.bfloat16
EPS = 1e-6
D_MODEL = 1024
DEPTH = 2
N_DEV = 8
SB_WIDTH = 512
SB_HEAD_DIM = 64
CONV_CH = 256
CONV_WIDTH = 31
SSM_CH = 256
SSM_GROUP = 16
SSM_GROUPS = 16
SSM_STATE = 64
SSM_LANES = SSM_GROUPS * SSM_STATE
XA_HEADS = 4
XA_HEAD_DIM = 256
FFN_HIDDEN = 2816
ADAM_LR = 0.001
ADAM_B1 = 0.9
ADAM_B2 = 0.999
ADAM_EPS = 1e-08
ADAM_WD = 0.01
ADAM_STEP = 10

LANES = 128
SUBLANES = 8
TL = 512
SB_TQ = 256
SB_TK = 128
SB_PAIRS = 2
SB_SCALE = SB_HEAD_DIM ** -0.5
SB_DEAD = -120.0
SSM_T = 256
CONV_HALO = 32
CONV_SUB = 64
VMEM_MB = 48

MESH_AXES = ("x", "y", "c")
WEIGHT_NAMES = ['norm_mix_g', 'w_in', 'sb_q_norm_g', 'sb_k_norm_g', 'conv_dw_w', 'conv_dw_b', 'conv_ln_g',
                'conv_ln_b', 'conv_pw2_w', 'ssm_lam_re', 'ssm_lam_im', 'ssm_log_dt', 'ssm_b_re', 'ssm_b_im',
                'ssm_c_re', 'ssm_c_im', 'ssm_d', 'ssm_glu_w', 'branch_norm_g', 'w_out', 'norm_xa_g',
                'norm_mem_g', 'xa_wq', 'xa_wk', 'xa_wv', 'xa_q_norm_g', 'xa_k_norm_g', 'xa_wo', 'norm_ffn_g',
                'ffn_w_in', 'ffn_w_out']
SHARD_AXIS = {'w_in': 2, 'conv_dw_w': 2, 'conv_pw2_w': 1, 'ssm_glu_w': 2, 'w_out': 1, 'xa_wq': 1, 'xa_wk': 1,
              'xa_wv': 1, 'xa_wo': 1, 'ffn_w_in': 2, 'ffn_w_out': 1}
BIG_NAMES = [n for n in WEIGHT_NAMES if n in SHARD_AXIS]
SMALL_NAMES = [n for n in WEIGHT_NAMES if n not in SHARD_AXIS]


def _nn(a, b):
    return jnp.dot(a.astype(BF16), b.astype(BF16), preferred_element_type=F32)


def _nt(a, b):
    return lax.dot_general(a.astype(BF16), b.astype(BF16), (((1,), (1,)), ((), ())), preferred_element_type=F32)


def _tn(a, b):
    return lax.dot_general(a.astype(BF16), b.astype(BF16), (((0,), (0,)), ((), ())), preferred_element_type=F32)


def _rms(x, g):
    return x * lax.rsqrt(jnp.mean(x * x, axis=-1, keepdims=True) + EPS) * g


def _sigmoid(x):
    return 1.0 / (1.0 + jnp.exp(-x))


def _silu(x):
    return x * _sigmoid(x)


def _layer_norm(x, g, b):
    mu = jnp.mean(x, axis=-1, keepdims=True)
    xc = x - mu
    var = jnp.mean(xc * xc, axis=-1, keepdims=True)
    return xc * lax.rsqrt(var + EPS) * g + b


def _head_rms64(p, g, mavg):
    ms = jnp.dot(p * p, mavg, preferred_element_type=F32)
    return p * lax.rsqrt(ms + EPS) * g


def _params(n_grid, vmem_mb=VMEM_MB):
    return pltpu.CompilerParams(dimension_semantics=("arbitrary",) * n_grid, vmem_limit_bytes=vmem_mb << 20)


def _rows(cols, tl=TL):
    return pl.BlockSpec((tl, cols), lambda i: (i, 0))


def _res(shape):
    nd = len(shape)
    return pl.BlockSpec(tuple(shape), lambda *_: (0,) * nd)


def _sds(shape, dtype=F32):
    return jax.ShapeDtypeStruct(tuple(shape), dtype)


def _accumulate(i, ref, val):
    @pl.when(i == 0)
    def _():
        ref[...] = val

    @pl.when(i > 0)
    def _():
        ref[...] += val


HBM = pl.BlockSpec(memory_space=pltpu.HBM)
ROW_ALIGN = 16


def _my_index():
    return lax.axis_index("x") * 4 + lax.axis_index("y") * 2 + lax.axis_index("c")


def _peer(k):
    x, y, c = lax.axis_index("x"), lax.axis_index("y"), lax.axis_index("c")
    return (x ^ ((k >> 2) & 1), y ^ ((k >> 1) & 1), c ^ (k & 1))


class _Rider:
    def __init__(self, gathers=(), scatters=(), done=None):
        self.gathers, self.scatters, self.done = list(gathers), list(scatters), done

    @property
    def inputs(self):
        return self.gathers + self.scatters

    def out_shapes(self):
        return ([_sds((N_DEV * a.shape[0],) + a.shape[1:], a.dtype) for a in self.gathers]
                + [_sds((N_DEV, a.shape[0] // N_DEV) + a.shape[1:], a.dtype) for a in self.scatters])


def _rider_copies(rider, in_refs, out_refs, send_sems, recv_sems, local_sems):
    me = _my_index()
    local, sends, recvs = [], [], []
    for t, (src, dst) in enumerate(zip(in_refs, out_refs)):
        gather = t < len(rider.gathers)
        rows = src.shape[0] if gather else src.shape[0] // N_DEV

        def block(ref, d, rows=rows):
            return ref.at[pl.ds(pl.multiple_of(d * rows, ROW_ALIGN), rows)]

        src_for = (lambda p, src=src: src) if gather else (lambda p, src=src: block(src, p))
        dst_for = (lambda d, dst=dst: block(dst, d)) if gather else (lambda d, dst=dst: dst.at[d])
        local.append(pltpu.make_async_copy(src_for(me), dst_for(me), local_sems.at[t]))
        for k in range(1, N_DEV):
            args = dict(send_sem=send_sems.at[t * N_DEV + k], recv_sem=recv_sems.at[t * N_DEV + k], device_id=_peer(k),
                        device_id_type=pl.DeviceIdType.MESH)
            sends.append(pltpu.make_async_remote_copy(src_ref=src_for(me ^ k), dst_ref=dst_for(me), **args))
            recvs.append(pltpu.make_async_remote_copy(src_ref=src_for(me ^ k), dst_ref=dst_for(me ^ k), **args))
    return local, sends, recvs


def _pcall(body, *, name, out_shape, grid=(), in_specs=None, out_specs=None, scratch_shapes=(), compiler_params=None,
           rider=None):
    if rider is None or not rider.inputs:
        return pl.pallas_call(body, name=name, grid=grid, in_specs=in_specs, out_specs=out_specs, out_shape=out_shape,
                              scratch_shapes=list(scratch_shapes), compiler_params=compiler_params)
    single = not isinstance(out_shape, (list, tuple))
    outs = [out_shape] if single else list(out_shape)
    ospecs = [out_specs] if single else list(out_specs)
    n_in, n_out, n_scr, n_r = len(in_specs), len(outs), len(scratch_shapes), len(rider.inputs)

    def wrapped(*refs):
        ins, rin = refs[:n_in], refs[n_in:n_in + n_r]
        own_out = refs[n_in + n_r:n_in + n_r + n_out]
        rout = refs[n_in + n_r + n_out:n_in + 2 * n_r + n_out]
        scratch = refs[n_in + 2 * n_r + n_out:n_in + 2 * n_r + n_out + n_scr]
        local, sends, recvs = _rider_copies(rider, rin, rout, *refs[-3:])

        def start():
            for cp in local + sends:
                cp.start()

        def wait():
            for cp in recvs:
                cp.wait_recv()
            for cp in sends:
                cp.wait_send()
            for cp in local:
                cp.wait()

        if grid:
            ids = [pl.program_id(a) for a in range(len(grid))]
            first = functools.reduce(jnp.logical_and, [i == 0 for i in ids])
            last = functools.reduce(jnp.logical_and, [i == n - 1 for i, n in zip(ids, grid)])
            pl.when(first)(start)
            body(*ins, *own_out, *scratch)
            pl.when(last)(wait)
        else:
            start()
            body(*ins, *own_out, *scratch)
            wait()

    call = pl.pallas_call(
        wrapped, name=name, grid=grid, in_specs=list(in_specs) + [HBM] * n_r, out_specs=ospecs + [HBM] * n_r,
        out_shape=outs + rider.out_shapes(),
        scratch_shapes=list(scratch_shapes) + [pltpu.SemaphoreType.DMA((n_r * N_DEV,)),
                                               pltpu.SemaphoreType.DMA((n_r * N_DEV,)), pltpu.SemaphoreType.DMA((n_r,))],
        compiler_params=compiler_params)

    def run(*args):
        res = call(*args, *rider.inputs)
        if rider.done is not None:
            rider.done(list(res[n_out:]))
        return res[0] if single else list(res[:n_out])

    return run


def _exchange(rider, name):
    def body():
        pass

    _pcall(body, name=name, out_shape=[], in_specs=[], out_specs=[], rider=rider)()


def _mixin_post(pq, pk, a, b, gq, gk, mavg):
    return _head_rms64(pq, gq, mavg), _head_rms64(pk, gk, mavg), a * _sigmoid(b)


def _mix_in_fwd(x, g_mix, w_in, gq, gk, mavg, tag, rider=None):
    L = x.shape[0]

    def body(x_ref, g_ref, w_ref, gq_ref, gk_ref, mavg_ref, q_ref, k_ref, v_ref, hg_ref, u_ref):
        h = _rms(x_ref[...], g_ref[...])
        p = _nt(h, w_ref[...])
        q, k, hg = _mixin_post(p[:, 0:512], p[:, 512:1024], p[:, 1536:1792], p[:, 1792:2048],
                               gq_ref[...], gk_ref[...], mavg_ref[...])
        q_ref[...] = (q * SB_SCALE).astype(BF16)
        k_ref[...] = k.astype(BF16)
        v_ref[...] = p[:, 1024:1536].astype(BF16)
        hg_ref[...] = hg
        u_ref[...] = p[:, 2048:2304]

    return _pcall(
        body, name=f"mix_in_fwd_{tag}", grid=(L // TL,),
        in_specs=[_rows(D_MODEL), _res((1, D_MODEL)), _res(w_in.shape), _res((1, 512)), _res((1, 512)), _res((512, 512))],
        out_specs=[_rows(512), _rows(512), _rows(512), _rows(256), _rows(256)],
        out_shape=[_sds((L, 512), BF16), _sds((L, 512), BF16), _sds((L, 512), BF16), _sds((L, 256)), _sds((L, 256))],
        compiler_params=_params(1), rider=rider,
    )(x, g_mix, w_in, gq, gk, mavg)


def _mix_in_bwd(x, dres, dq, dk, dv, dhg, du, g_mix, w_in, gq, gk, mavg, tag):
    L = x.shape[0]
    tl = 256

    def body(x_ref, dres_ref, dq_ref, dk_ref, dv_ref, dhg_ref, du_ref, g_ref, w_ref, gq_ref, gk_ref, mavg_ref,
             dx_ref, dw_ref, dg_ref, dgq_ref, dgk_ref):
        i = pl.program_id(0)
        xx = x_ref[...]
        g = g_ref[...]
        mavg_v = mavg_ref[...]
        h, vjp_n = jax.vjp(_rms, xx, g)
        p = _nt(h, w_ref[...])
        _, vjp_p = jax.vjp(lambda pq, pk, a, b, gq_, gk_: _mixin_post(pq, pk, a, b, gq_, gk_, mavg_v),
                           p[:, 0:512], p[:, 512:1024], p[:, 1536:1792], p[:, 1792:2048], gq_ref[...], gk_ref[...])
        dpq, dpk, da, db, dgq, dgk = vjp_p((dq_ref[...], dk_ref[...], dhg_ref[...]))
        dp = jnp.concatenate([dpq, dpk, dv_ref[...], da, db, du_ref[...]], axis=1)
        dh = _nn(dp, w_ref[...])
        dxn, dg = vjp_n(dh)
        dx_ref[...] = dres_ref[...] + dxn
        _accumulate(i, dw_ref, _tn(dp, h))
        _accumulate(i, dg_ref, dg)
        _accumulate(i, dgq_ref, dgq)
        _accumulate(i, dgk_ref, dgk)

    r = lambda c: _rows(c, tl)
    return pl.pallas_call(
        body, name=f"mix_in_bwd_{tag}", grid=(L // tl,),
        in_specs=[r(D_MODEL), r(D_MODEL), r(512), r(512), r(512), r(256), r(256),
                  _res((1, D_MODEL)), _res(w_in.shape), _res((1, 512)), _res((1, 512)), _res((512, 512))],
        out_specs=[r(D_MODEL), _res(w_in.shape), _res((1, D_MODEL)), _res((1, 512)), _res((1, 512))],
        out_shape=[_sds((L, D_MODEL)), _sds(w_in.shape), _sds((1, D_MODEL)), _sds((1, 512)), _sds((1, 512))],
        compiler_params=_params(1),
    )(x, dres, dq, dk, dv, dhg, du, g_mix, w_in, gq, gk, mavg)


def _sb_tri_consts():
    r = np.arange(2 * SB_TK)[:, None]
    c = np.arange(2 * SB_TK)[None, :]
    same = (r // SB_TK) == (c // SB_TK)
    later = (same & (r > c)).astype(np.float32)
    earlier = (same & (r < c)).astype(np.float32)
    return jnp.asarray(later, BF16), jnp.asarray(earlier, BF16)


def _two_heads(blk, lane_a):
    zero = jnp.zeros_like(blk)
    return jnp.concatenate([jnp.where(lane_a, blk, zero), jnp.where(lane_a, zero, blk)], axis=0)


def _sb_logs(z, i, j, masked):
    e = jnp.exp(-jnp.abs(z))
    lm = -(jnp.maximum(z, 0.0) + jnp.log(1.0 + e))
    ls = z + lm
    valid = None
    if masked:
        row = lax.broadcasted_iota(jnp.int32, (SB_TQ, 2 * SB_TK), 0)
        col = lax.broadcasted_iota(jnp.int32, (SB_TQ, 2 * SB_TK), 1) & (SB_TK - 1)
        valid = (j * SB_TK + col) < (i * SB_TQ + row)
        lm = jnp.where(valid, lm, 0.0)
    hi = lm.astype(BF16)
    lo = (lm - hi.astype(F32)).astype(BF16)
    return lm, ls, hi, lo, valid


def _lane_halves(a, b):
    return jnp.concatenate([jnp.broadcast_to(a, (SB_TQ, SB_TK)), jnp.broadcast_to(b, (SB_TQ, SB_TK))], axis=1)


def _dot(a, b):
    return jnp.dot(a, b, preferred_element_type=F32)


def _dot_nt(a, b):
    return lax.dot_general(a, b, (((1,), (1,)), ((), ())), preferred_element_type=F32)


def _dot_tn(a, b):
    return lax.dot_general(a, b, (((0,), (0,)), ((), ())), preferred_element_type=F32)


def _sb_fwd(q, k, v, ul, tag, rider=None):
    L = q.shape[0]
    nq = L // SB_TQ
    per = SB_TQ // SB_TK
    wid = SB_PAIRS * LANES

    def body(q_ref, k_ref, v_ref, ul_ref, o_ref, rs_ref, n_ref):
        i = pl.program_id(1)
        ulv = ul_ref[...]
        lane_a = lax.broadcasted_iota(jnp.int32, (1, LANES), 1) < SB_HEAD_DIM
        lane_q = lax.broadcasted_iota(jnp.int32, (SB_TQ, LANES), 1)
        cols = [slice(p * LANES, (p + 1) * LANES) for p in range(SB_PAIRS)]
        qbs = [q_ref[:, c] for c in cols]

        def double_step(jhi, carry, masked):
            chains = [dict(p=p, j=jhi - d) for d in range(2) for p in range(SB_PAIRS)]
            for c in chains:
                c['off'] = pl.multiple_of(c['j'] * SB_TK, SB_TK)
                kb = k_ref[pl.ds(c['off'], SB_TK), cols[c['p']]]
                c['z'] = _dot_nt(qbs[c['p']], _two_heads(kb, lane_a))
            for c in chains:
                c['lm'], c['ls'], c['hi'], c['lo'], c['valid'] = _sb_logs(c.pop('z'), i, c['j'], masked)
            for c in chains:
                c['lb'] = _dot(c.pop('hi'), ulv) + _dot(c.pop('lo'), ulv)
            state = [list(s) for s in carry]
            for c in chains:
                ra, rb, _, rsave = state[c['p']]
                w = jnp.exp(c['ls'] + c['lb'] + _lane_halves(ra, rb))
                if masked:
                    w = jnp.where(c['valid'], w, 0.0)
                c['w'] = w.astype(BF16)
                lb, lm = c['lb'], c['lm']
                state[c['p']][3] = jnp.where(lane_q == c['j'], ra, jnp.where(lane_q == c['j'] + SB_HEAD_DIM, rb, rsave))
                state[c['p']][0] = ra + lb[:, 0:1] + lm[:, 0:1]
                state[c['p']][1] = rb + lb[:, SB_TK:SB_TK + 1] + lm[:, SB_TK:SB_TK + 1]
            for c in chains:
                vb = v_ref[pl.ds(c['off'], SB_TK), cols[c['p']]]
                state[c['p']][2] = state[c['p']][2] + _dot(c['w'], _two_heads(vb, lane_a))
            return tuple(tuple(s) for s in state)

        assert per == 2
        carry = tuple((jnp.zeros((SB_TQ, 1), F32), jnp.zeros((SB_TQ, 1), F32),
                       jnp.zeros((SB_TQ, LANES), F32), jnp.zeros((SB_TQ, LANES), F32)) for _ in range(SB_PAIRS))
        def alive(carry):
            m = carry[0][0]
            for c in carry:
                m = jnp.maximum(m, jnp.maximum(c[0], c[1]))
            return jnp.max(m) > SB_DEAD

        carry = double_step(i * per + 1, carry, True)
        n_done, _, carry = lax.while_loop(
            lambda st: jnp.logical_and(st[0] < i, st[1]),
            lambda st: (lambda c: (st[0] + 1, alive(c), c))(double_step(i * per - 1 - 2 * st[0], st[2], False)),
            (jnp.int32(0), alive(carry), carry))
        o_ref[...] = jnp.concatenate([c[2] for c in carry], axis=1)
        rs_ref[...] = jnp.concatenate([c[3] for c in carry], axis=1)
        n_ref[pl.program_id(0), i] = n_done

    qspec = pl.BlockSpec((SB_TQ, wid), lambda g, i: (i, g))
    kspec = pl.BlockSpec((L, wid), lambda g, i: (0, g))
    return _pcall(
        body, name=f"sb_fwd_{tag}", grid=(SB_WIDTH // wid, nq),
        in_specs=[qspec, kspec, kspec, pl.BlockSpec((2 * SB_TK, 2 * SB_TK), lambda g, i: (0, 0))],
        out_specs=[qspec, qspec, pl.BlockSpec(memory_space=pltpu.SMEM)],
        out_shape=[_sds((L, SB_WIDTH)), _sds((L, SB_WIDTH)), _sds((SB_WIDTH // wid, nq), jnp.int32)],
        compiler_params=_params(2), rider=rider,
    )(q, k, v, ul)


def _sb_bwd(n_done, q, k, v, rsave, do, ul, ue, tag):
    L = q.shape[0]
    nq = L // SB_TQ
    per = SB_TQ // SB_TK
    wid = SB_PAIRS * LANES

    def body(n_ref, q_ref, k_ref, v_ref, rs_ref, do_ref, ul_ref, ue_ref, dq_ref, dk_ref, dv_ref):
        i = pl.program_id(1)

        @pl.when(i == 0)
        def _():
            dk_ref[...] = jnp.zeros_like(dk_ref)
            dv_ref[...] = jnp.zeros_like(dv_ref)

        ulv = ul_ref[...]
        uev = ue_ref[...]
        lane_a = lax.broadcasted_iota(jnp.int32, (1, LANES), 1) < SB_HEAD_DIM
        lane_q = lax.broadcasted_iota(jnp.int32, (SB_TQ, LANES), 1)
        cols = [slice(p * LANES, (p + 1) * LANES) for p in range(SB_PAIRS)]
        qbs = [q_ref[:, c] for c in cols]
        dobs = [do_ref[:, c].astype(BF16) for c in cols]
        rsvs = [rs_ref[:, c] for c in cols]

        def double_step(jlo, carry, masked):
            chains = [dict(p=p, j=jlo + d) for d in range(2) for p in range(SB_PAIRS)]
            for c in chains:
                p = c['p']
                c['off'] = pl.multiple_of(c['j'] * SB_TK, SB_TK)
                c['kk2'] = _two_heads(k_ref[pl.ds(c['off'], SB_TK), cols[p]], lane_a)
                c['z'] = _dot_nt(qbs[p], c['kk2'])
                c['dw'] = _dot_nt(dobs[p], _two_heads(v_ref[pl.ds(c['off'], SB_TK), cols[p]], lane_a))
            for c in chains:
                c['lm'], c['ls'], c['hi'], c['lo'], c['valid'] = _sb_logs(c.pop('z'), i, c['j'], masked)
                c['ra'] = jnp.sum(jnp.where(lane_q == c['j'], rsvs[c['p']], 0.0), axis=1, keepdims=True)
                c['rb'] = jnp.sum(jnp.where(lane_q == c['j'] + SB_HEAD_DIM, rsvs[c['p']], 0.0), axis=1, keepdims=True)
            for c in chains:
                c['lb'] = _dot(c.pop('hi'), ulv) + _dot(c.pop('lo'), ulv)
            for c in chains:
                w = jnp.exp(c['ls'] + c.pop('lb') + _lane_halves(c['ra'], c['rb']))
                if masked:
                    w = jnp.where(c['valid'], w, 0.0)
                c['wb'] = w.astype(BF16)
                gg = w * c.pop('dw')
                c['gg'] = gg
                c['ghi'] = gg.astype(BF16)
                c['glo'] = (gg - c['ghi'].astype(F32)).astype(BF16)
                c['beta'] = jnp.exp(c['ls'])
            for c in chains:
                c['cb'] = _dot(c.pop('ghi'), uev) + _dot(c.pop('glo'), uev)
                c['dv2'] = _dot_tn(c.pop('wb'), dobs[c['p']])
            state = [list(s) for s in carry]
            for c in chains:
                pa, pb, _ = state[c['p']]
                gg, cb, beta = c['gg'], c['cb'], c['beta']
                dz = gg * (1.0 - beta) - beta * (cb + _lane_halves(pa, pb))
                if masked:
                    dz = jnp.where(c['valid'], dz, 0.0)
                c['dzb'] = dz.astype(BF16)
                state[c['p']][0] = pa + cb[:, SB_TK - 1:SB_TK] + gg[:, SB_TK - 1:SB_TK]
                state[c['p']][1] = pb + cb[:, 2 * SB_TK - 1:2 * SB_TK] + gg[:, 2 * SB_TK - 1:2 * SB_TK]
            for c in chains:
                c['dqc'] = _dot(c['dzb'], c['kk2'])
                c['dk2'] = _dot_tn(c['dzb'], qbs[c['p']])
            for c in chains:
                p, dk2, dv2 = c['p'], c['dk2'], c['dv2']
                state[p][2] = state[p][2] + c['dqc']
                dk_ref[pl.ds(c['off'], SB_TK), cols[p]] += jnp.where(lane_a, dk2[0:SB_TK], dk2[SB_TK:2 * SB_TK])
                dv_ref[pl.ds(c['off'], SB_TK), cols[p]] += jnp.where(lane_a, dv2[0:SB_TK], dv2[SB_TK:2 * SB_TK])
            return tuple(tuple(s) for s in state)

        assert per == 2
        carry = tuple((jnp.zeros((SB_TQ, 1), F32), jnp.zeros((SB_TQ, 1), F32), jnp.zeros((SB_TQ, LANES), F32))
                      for _ in range(SB_PAIRS))
        first = i - n_ref[pl.program_id(0), i]
        carry = lax.fori_loop(first, i, lambda jj, c: double_step(2 * jj, c, False), carry)
        carry = double_step(i * per, carry, True)
        dq_ref[...] = jnp.concatenate([c[2] for c in carry], axis=1) * SB_SCALE

    qspec = pl.BlockSpec((SB_TQ, wid), lambda g, i: (i, g))
    kspec = pl.BlockSpec((L, wid), lambda g, i: (0, g))
    kin = pl.BlockSpec((L, wid), lambda g, i: (0, g), pipeline_mode=pl.Buffered(1))
    cspec = pl.BlockSpec((2 * SB_TK, 2 * SB_TK), lambda g, i: (0, 0))
    return pl.pallas_call(
        body, name=f"sb_bwd_{tag}", grid=(SB_WIDTH // wid, nq),
        in_specs=[pl.BlockSpec(memory_space=pltpu.SMEM), qspec, kin, kin, qspec, qspec, cspec, cspec],
        out_specs=[qspec, kspec, kspec],
        out_shape=[_sds((L, SB_WIDTH)), _sds((L, SB_WIDTH)), _sds((L, SB_WIDTH))],
        compiler_params=_params(2, 58),
    )(n_done, q, k, v, rsave, do, ul, ue)


def _conv_fwd(hp, w, tag):
    L = hp.shape[0] - CONV_HALO
    win_rows = CONV_SUB + CONV_HALO

    def body(hp_ref, w_ref, o_ref):
        i = pl.program_id(0)

        def sub(s, _):
            t0 = pl.multiple_of(i * TL + s * CONV_SUB, CONV_SUB)
            win = hp_ref[pl.ds(t0, win_rows), :]
            acc = jnp.zeros((CONV_SUB, CONV_CH), F32)
            for kk in range(CONV_WIDTH):
                sh = CONV_WIDTH - 1 - kk
                r = win if sh == 0 else pltpu.roll(win, sh, 0)
                acc = acc + w_ref[kk:kk + 1, :] * r[CONV_HALO:, :]
            o_ref[pl.ds(pl.multiple_of(s * CONV_SUB, CONV_SUB), CONV_SUB), :] = acc
            return 0

        lax.fori_loop(0, TL // CONV_SUB, sub, 0)

    return pl.pallas_call(
        body, name=f"conv_fwd_{tag}", grid=(L // TL,),
        in_specs=[_res(hp.shape), _res(w.shape)],
        out_specs=_rows(CONV_CH),
        out_shape=_sds((L, CONV_CH)),
        compiler_params=_params(1),
    )(hp, w)


def _conv_bwd(dpad, hp, w, tag):
    L = hp.shape[0] - CONV_HALO
    win_rows = CONV_SUB + CONV_HALO
    n_tiles = L // TL

    def body(dp_ref, hp_ref, w_ref, dh_ref, dw_ref, acc_ref):
        i = pl.program_id(0)

        @pl.when(i == 0)
        def _():
            acc_ref[...] = jnp.zeros_like(acc_ref)

        def sub(s, _):
            t0 = pl.multiple_of(i * TL + s * CONV_SUB, CONV_SUB)
            wd = dp_ref[pl.ds(t0, win_rows), :]
            wh = hp_ref[pl.ds(t0, win_rows), :]
            dy = wd[0:CONV_SUB, :]
            acc = jnp.zeros((CONV_SUB, CONV_CH), F32)
            for kk in range(CONV_WIDTH):
                sh = CONV_WIDTH - 1 - kk
                rd = wd if sh == 0 else pltpu.roll(wd, win_rows - sh, 0)
                acc = acc + w_ref[kk:kk + 1, :] * rd[0:CONV_SUB, :]
                rh = wh if sh == 0 else pltpu.roll(wh, sh, 0)
                prod = dy * rh[CONV_HALO:, :]
                part = prod[0:SUBLANES]
                for m in range(1, CONV_SUB // SUBLANES):
                    part = part + prod[m * SUBLANES:(m + 1) * SUBLANES]
                acc_ref[kk] += part
            dh_ref[pl.ds(pl.multiple_of(s * CONV_SUB, CONV_SUB), CONV_SUB), :] = acc
            return 0

        lax.fori_loop(0, TL // CONV_SUB, sub, 0)

        @pl.when(i == n_tiles - 1)
        def _():
            for kk in range(CONV_WIDTH):
                dw_ref[kk:kk + 1, :] = jnp.sum(acc_ref[kk], axis=0, keepdims=True)

    return pl.pallas_call(
        body, name=f"conv_bwd_{tag}", grid=(n_tiles,),
        in_specs=[_res(dpad.shape), _res(hp.shape), _res(w.shape)],
        out_specs=[_rows(CONV_CH), _res(w.shape)],
        out_shape=[_sds((L, CONV_CH)), _sds(w.shape)],
        scratch_shapes=[pltpu.VMEM((CONV_WIDTH, SUBLANES, CONV_CH), F32)],
        compiler_params=_params(1),
    )(dpad, hp, w)


def _ssm_mask():
    r = np.arange(SSM_CH)[:, None] // SSM_GROUP
    c = np.arange(SSM_LANES)[None, :] // SSM_STATE
    return jnp.asarray((r == c).astype(np.float32))


def _ssm_discretize(lr, li, ldt, brt, bit):
    dt = jnp.exp(ldt)
    mag = jnp.exp(lr * dt)
    ar = mag * jnp.cos(li * dt)
    ai = mag * jnp.sin(li * dt)
    den = lr * lr + li * li
    fr = ((ar - 1.0) * lr + ai * li) / den
    fi = (ai * lr - (ar - 1.0) * li) / den
    return ar, ai, fr * brt - fi * bit, fr * bit + fi * brt


def _block_diag(rows16, mask):
    return jnp.where(mask > 0.5, jnp.tile(rows16, (SSM_GROUPS, 1)), 0.0)


def _block_diag_t(full, mask):
    m = jnp.where(mask > 0.5, full, 0.0)
    out = m[0:SSM_GROUP]
    for g in range(1, SSM_GROUPS):
        out = out + m[g * SSM_GROUP:(g + 1) * SSM_GROUP]
    return out


def _ssm_prep(lr, li, ldt, brt, bit, crt, cit, mask, tag):
    def body(lr_ref, li_ref, ldt_ref, brt_ref, bit_ref, crt_ref, cit_ref, m_ref,
             ar_ref, ai_ref, bbr_ref, bbi_ref, cbr_ref, cbi_ref):
        ar, ai, bbr, bbi = _ssm_discretize(lr_ref[...], li_ref[...], ldt_ref[...], brt_ref[...], bit_ref[...])
        m = m_ref[...]
        ar_ref[...] = ar
        ai_ref[...] = ai
        bbr_ref[...] = _block_diag(bbr, m).astype(BF16)
        bbi_ref[...] = _block_diag(bbi, m).astype(BF16)
        cbr_ref[...] = _block_diag(crt_ref[...], m).astype(BF16)
        cbi_ref[...] = _block_diag(cit_ref[...], m).astype(BF16)

    row = _sds((1, SSM_LANES))
    blk = _sds((SSM_CH, SSM_LANES), BF16)
    return pl.pallas_call(body, name=f"ssm_prep_{tag}", out_shape=[row, row, blk, blk, blk, blk])(
        lr, li, ldt, brt, bit, crt, cit, mask)


def _ssm_prep_bwd(lr, li, ldt, brt, bit, mask, dar, dai, dbbr, dbbi, dcbr, dcbi, tag):
    def body(lr_ref, li_ref, ldt_ref, brt_ref, bit_ref, m_ref, dar_ref, dai_ref, dbbr_ref, dbbi_ref, dcbr_ref,
             dcbi_ref, dlr_ref, dli_ref, dldt_ref, dbrt_ref, dbit_ref, dcrt_ref, dcit_ref):
        m = m_ref[...]
        _, vjp = jax.vjp(_ssm_discretize, lr_ref[...], li_ref[...], ldt_ref[...], brt_ref[...], bit_ref[...])
        dlr, dli, dldt, dbrt, dbit = vjp((dar_ref[...], dai_ref[...], _block_diag_t(dbbr_ref[...], m),
                                          _block_diag_t(dbbi_ref[...], m)))
        dlr_ref[...] = dlr
        dli_ref[...] = dli
        dldt_ref[...] = dldt
        dbrt_ref[...] = dbrt
        dbit_ref[...] = dbit
        dcrt_ref[...] = _block_diag_t(dcbr_ref[...], m)
        dcit_ref[...] = _block_diag_t(dcbi_ref[...], m)

    row = _sds((1, SSM_LANES))
    r16 = _sds((SSM_GROUP, SSM_LANES))
    return pl.pallas_call(body, name=f"ssm_prep_bwd_{tag}", out_shape=[row, row, row, r16, r16, r16, r16])(
        lr, li, ldt, brt, bit, mask, dar, dai, dbbr, dbbi, dcbr, dcbi)


def _complex_scan(br, bi, ar, ai, reverse):
    n = br.shape[0]
    row = lax.broadcasted_iota(jnp.int32, (n, 1), 0)
    xr, xi, pr, pi = br, bi, ar, ai
    d = 1
    while d < n:
        if reverse:
            sr, si, keep = pltpu.roll(xr, n - d, 0), pltpu.roll(xi, n - d, 0), row < n - d
        else:
            sr, si, keep = pltpu.roll(xr, d, 0), pltpu.roll(xi, d, 0), row >= d
        sr = jnp.where(keep, sr, 0.0)
        si = jnp.where(keep, si, 0.0)
        xr, xi = xr + pr * sr - pi * si, xi + pr * si + pi * sr
        pr, pi = pr * pr - pi * pi, 2.0 * pr * pi
        d *= 2
    return xr, xi


def _ssm_fwd(u, ar, ai, bbr, bbi, cbr, cbi, dvec, tag, rider=None):
    L = u.shape[0]
    T = SSM_T

    def body(u_ref, ar_ref, ai_ref, bbr_ref, bbi_ref, cbr_ref, cbi_ref, d_ref, y_ref, xr_ref, xi_ref, cr_ref, ci_ref):
        i = pl.program_id(0)

        @pl.when(i == 0)
        def _():
            cr_ref[...] = jnp.zeros_like(cr_ref)
            ci_ref[...] = jnp.zeros_like(ci_ref)

        uu = u_ref[...]
        a_r, a_i = ar_ref[...], ai_ref[...]
        c_r, c_i = cr_ref[...], ci_ref[...]
        first = lax.broadcasted_iota(jnp.int32, (T, 1), 0) == 0
        bur = _nn(uu, bbr_ref[...]) + jnp.where(first, a_r * c_r - a_i * c_i, 0.0)
        bui = _nn(uu, bbi_ref[...]) + jnp.where(first, a_r * c_i + a_i * c_r, 0.0)
        xr, xi = _complex_scan(bur, bui, a_r, a_i, False)
        cr_ref[...] = xr[T - 1:T, :]
        ci_ref[...] = xi[T - 1:T, :]
        xr_ref[...] = xr
        xi_ref[...] = xi
        y_ref[...] = _nt(xr, cbr_ref[...]) - _nt(xi, cbi_ref[...]) + d_ref[...] * uu

    blk = _res((SSM_CH, SSM_LANES))
    row = _res((1, SSM_LANES))
    return _pcall(
        body, name=f"ssm_fwd_{tag}", grid=(L // T,), rider=rider,
        in_specs=[_rows(SSM_CH, T), row, row, blk, blk, blk, blk, _res((1, SSM_CH))],
        out_specs=[_rows(SSM_CH, T), _rows(SSM_LANES, T), _rows(SSM_LANES, T)],
        out_shape=[_sds((L, SSM_CH)), _sds((L, SSM_LANES)), _sds((L, SSM_LANES))],
        scratch_shapes=[pltpu.VMEM((1, SSM_LANES), F32), pltpu.VMEM((1, SSM_LANES), F32)],
        compiler_params=_params(1),
    )(u, ar, ai, bbr, bbi, cbr, cbi, dvec)


def _ssm_bwd(dy, u, xr, xi, ar, ai, bbr, bbi, cbr, cbi, dvec, tag, rider=None):
    L = u.shape[0]
    T = SSM_T
    nc = L // T

    def body(dy_ref, u_ref, xr_ref, xi_ref, pr_ref, pi_ref, ar_ref, ai_ref, bbr_ref, bbi_ref, cbr_ref, cbi_ref, d_ref,
             du_ref, dar_ref, dai_ref, dbbr_ref, dbbi_ref, dcbr_ref, dcbi_ref, dd_ref, gr_ref, gi_ref):
        i = pl.program_id(0)

        @pl.when(i == 0)
        def _():
            gr_ref[...] = jnp.zeros_like(gr_ref)
            gi_ref[...] = jnp.zeros_like(gi_ref)

        dyy = dy_ref[...]
        uu = u_ref[...]
        xr, xi = xr_ref[...], xi_ref[...]
        a_r, a_i = ar_ref[...], ai_ref[...]
        g_r, g_i = gr_ref[...], gi_ref[...]
        row = lax.broadcasted_iota(jnp.int32, (T, 1), 0)
        last = row == T - 1
        inr = _nn(dyy, cbr_ref[...]) + jnp.where(last, a_r * g_r + a_i * g_i, 0.0)
        ini = -_nn(dyy, cbi_ref[...]) + jnp.where(last, a_r * g_i - a_i * g_r, 0.0)
        gr, gi = _complex_scan(inr, ini, a_r, -a_i, True)
        gr_ref[...] = gr[0:1, :]
        gi_ref[...] = gi[0:1, :]
        has_prev = (i < nc - 1).astype(F32)
        pr = pr_ref[SUBLANES - 1:SUBLANES, :] * has_prev
        pi = pi_ref[SUBLANES - 1:SUBLANES, :] * has_prev
        sr = jnp.where(row == 0, pr, pltpu.roll(xr, 1, 0))
        si = jnp.where(row == 0, pi, pltpu.roll(xi, 1, 0))
        _accumulate(i, dar_ref, jnp.sum(gr * sr + gi * si, axis=0, keepdims=True))
        _accumulate(i, dai_ref, jnp.sum(gi * sr - gr * si, axis=0, keepdims=True))
        _accumulate(i, dbbr_ref, _tn(uu, gr))
        _accumulate(i, dbbi_ref, _tn(uu, gi))
        _accumulate(i, dcbr_ref, _tn(dyy, xr))
        _accumulate(i, dcbi_ref, -_tn(dyy, xi))
        _accumulate(i, dd_ref, jnp.sum(dyy * uu, axis=0, keepdims=True))
        du_ref[...] = _nt(gr, bbr_ref[...]) + _nt(gi, bbi_ref[...]) + dyy * d_ref[...]

    rev = lambda cols: pl.BlockSpec((T, cols), lambda i: (nc - 1 - i, 0))
    prev = pl.BlockSpec((SUBLANES, SSM_LANES), lambda i: (jnp.maximum((nc - 1 - i) * (T // SUBLANES) - 1, 0), 0))
    blk = _res((SSM_CH, SSM_LANES))
    row = _res((1, SSM_LANES))
    return _pcall(
        body, name=f"ssm_bwd_{tag}", grid=(nc,), rider=rider,
        in_specs=[rev(SSM_CH), rev(SSM_CH), rev(SSM_LANES), rev(SSM_LANES), prev, prev, row, row, blk, blk, blk, blk,
                  _res((1, SSM_CH))],
        out_specs=[rev(SSM_CH), row, row, blk, blk, blk, blk, _res((1, SSM_CH))],
        out_shape=[_sds((L, SSM_CH)), _sds((1, SSM_LANES)), _sds((1, SSM_LANES)), _sds((SSM_CH, SSM_LANES)),
                   _sds((SSM_CH, SSM_LANES)), _sds((SSM_CH, SSM_LANES)), _sds((SSM_CH, SSM_LANES)), _sds((1, SSM_CH))],
        scratch_shapes=[pltpu.VMEM((1, SSM_LANES), F32), pltpu.VMEM((1, SSM_LANES), F32)],
        compiler_params=_params(1),
    )(dy, u, xr, xi, xr, xi, ar, ai, bbr, bbi, cbr, cbi, dvec)


def _conv_post(hc, dw_b, ln_g, ln_b):
    return _silu(_layer_norm(hc + dw_b, ln_g, ln_b))


def _branch_mix(o_sb, o_conv, t, g1, g2, g3):
    o_ssm = t[:, 0:SSM_CH] * _sigmoid(t[:, SSM_CH:2 * SSM_CH])
    return jnp.concatenate([_rms(o_sb, g1), _rms(o_conv, g2), _rms(o_ssm, g3)], axis=1)


def _branch_mix_split(o_sb, o_conv, ta, tb, g1, g2, g3):
    return jnp.concatenate([_rms(o_sb, g1), _rms(o_conv, g2), _rms(ta * _sigmoid(tb), g3)], axis=1)


def _mix_out_fwd(x, o_sb, hc, y, dw_b, ln_g, ln_b, pw2, glu_w, g1, g2, g3, w_out, tag):
    L = x.shape[0]

    def body(x_ref, o_ref, hc_ref, y_ref, dwb_ref, lng_ref, lnb_ref, pw2_ref, glu_ref, g1_ref, g2_ref, g3_ref, wo_ref,
             out_ref):
        c1 = _conv_post(hc_ref[...], dwb_ref[...], lng_ref[...], lnb_ref[...])
        o_conv = _nn(c1, pw2_ref[...])
        t = _nt(y_ref[...], glu_ref[...])
        mixed = _branch_mix(o_ref[...], o_conv, t, g1_ref[...], g2_ref[...], g3_ref[...])
        out_ref[...] = x_ref[...] + _nn(mixed, wo_ref[...])

    v256 = _res((1, 256))
    return pl.pallas_call(
        body, name=f"mix_out_fwd_{tag}", grid=(L // TL,),
        in_specs=[_rows(D_MODEL), _rows(512), _rows(256), _rows(256), v256, v256, v256, _res(pw2.shape),
                  _res(glu_w.shape), _res((1, 512)), v256, v256, _res(w_out.shape)],
        out_specs=_rows(D_MODEL),
        out_shape=_sds((L, D_MODEL)),
        compiler_params=_params(1),
    )(x, o_sb, hc, y, dw_b, ln_g, ln_b, pw2, glu_w, g1, g2, g3, w_out)


def _mix_out_bwd(dx1, o_sb, hc, y, dw_b, ln_g, ln_b, pw2, glu_w, g1, g2, g3, w_out, tag):
    L = dx1.shape[0]

    def body(dx_ref, o_ref, hc_ref, y_ref, dwb_ref, lng_ref, lnb_ref, pw2_ref, glu_ref, g1_ref, g2_ref, g3_ref, wo_ref,
             do_ref, dhc_ref, dy_ref, ddwb_ref, dlng_ref, dlnb_ref, dpw2_ref, dglu_ref, dg1_ref, dg2_ref, dg3_ref,
             dwo_ref):
        i = pl.program_id(0)
        dxx = dx_ref[...]
        yy = y_ref[...]
        c1, vjp1 = jax.vjp(_conv_post, hc_ref[...], dwb_ref[...], lng_ref[...], lnb_ref[...])
        o_conv = _nn(c1, pw2_ref[...])
        t = _nt(yy, glu_ref[...])
        mixed, vjp2 = jax.vjp(_branch_mix_split, o_ref[...], o_conv, t[:, 0:SSM_CH], t[:, SSM_CH:2 * SSM_CH],
                              g1_ref[...], g2_ref[...], g3_ref[...])
        dmixed = _nt(dxx, wo_ref[...])
        do_sb, do_conv, dta, dtb, dg1, dg2, dg3 = vjp2(dmixed)
        dt = jnp.concatenate([dta, dtb], axis=1)
        dc1 = _nt(do_conv, pw2_ref[...])
        dhc, ddwb, dlng, dlnb = vjp1(dc1)
        do_ref[...] = do_sb
        dhc_ref[...] = dhc
        dy_ref[...] = _nn(dt, glu_ref[...])
        _accumulate(i, dwo_ref, _tn(mixed, dxx))
        _accumulate(i, dglu_ref, _tn(dt, yy))
        _accumulate(i, dpw2_ref, _tn(c1, do_conv))
        _accumulate(i, ddwb_ref, ddwb)
        _accumulate(i, dlng_ref, dlng)
        _accumulate(i, dlnb_ref, dlnb)
        _accumulate(i, dg1_ref, dg1)
        _accumulate(i, dg2_ref, dg2)
        _accumulate(i, dg3_ref, dg3)

    v256 = _res((1, 256))
    return pl.pallas_call(
        body, name=f"mix_out_bwd_{tag}", grid=(L // TL,),
        in_specs=[_rows(D_MODEL), _rows(512), _rows(256), _rows(256), v256, v256, v256, _res(pw2.shape),
                  _res(glu_w.shape), _res((1, 512)), v256, v256, _res(w_out.shape)],
        out_specs=[_rows(512), _rows(256), _rows(256), v256, v256, v256, _res(pw2.shape), _res(glu_w.shape),
                   _res((1, 512)), v256, v256, _res(w_out.shape)],
        out_shape=[_sds((L, 512)), _sds((L, 256)), _sds((L, 256)), _sds((1, 256)), _sds((1, 256)), _sds((1, 256)),
                   _sds(pw2.shape), _sds(glu_w.shape), _sds((1, 512)), _sds((1, 256)), _sds((1, 256)), _sds(w_out.shape)],
        compiler_params=_params(1),
    )(dx1, o_sb, hc, y, dw_b, ln_g, ln_b, pw2, glu_w, g1, g2, g3, w_out)


def _xa_heads_norm(kk, kg):
    return jnp.concatenate([_rms(kk[:, h * XA_HEAD_DIM:(h + 1) * XA_HEAD_DIM], kg) for h in range(XA_HEADS)], axis=1)


def _xa_mem_fwd(mem, g_mem, wk, wv, kg, tag):
    def body(m_ref, g_ref, wk_ref, wv_ref, kg_ref, k_ref, v_ref):
        hm = _rms(m_ref[...], g_ref[...])
        k_ref[...] = _xa_heads_norm(_nn(hm, wk_ref[...]), kg_ref[...])
        v_ref[...] = _nn(hm, wv_ref[...])

    return pl.pallas_call(body, name=f"xa_mem_fwd_{tag}", out_shape=[_sds(mem.shape), _sds(mem.shape)],
                          compiler_params=_params(0))(mem, g_mem, wk, wv, kg)


def _xa_mem_bwd(mem, g_mem, wk, wv, kg, dkx, dvx, tag):
    def body(m_ref, g_ref, wk_ref, wv_ref, kg_ref, dk_ref, dv_ref, dwk_ref, dwv_ref, dg_ref, dkg_ref):
        hm, vjp_n = jax.vjp(_rms, m_ref[...], g_ref[...])
        kk = _nn(hm, wk_ref[...])
        dvv = dv_ref[...]
        dkg = jnp.zeros((1, XA_HEAD_DIM), F32)
        parts = []
        for h in range(XA_HEADS):
            sl = slice(h * XA_HEAD_DIM, (h + 1) * XA_HEAD_DIM)
            _, vjp_h = jax.vjp(_rms, kk[:, sl], kg_ref[...])
            dkh, dgh = vjp_h(dk_ref[:, sl])
            parts.append(dkh)
            dkg = dkg + dgh
        dkk = jnp.concatenate(parts, axis=1)
        dwk_ref[...] = _tn(hm, dkk)
        dwv_ref[...] = _tn(hm, dvv)
        dhm = _nt(dkk, wk_ref[...]) + _nt(dvv, wv_ref[...])
        _, dg = vjp_n(dhm)
        dg_ref[...] = dg
        dkg_ref[...] = dkg

    return pl.pallas_call(
        body, name=f"xa_mem_bwd_{tag}",
        out_shape=[_sds(wk.shape), _sds(wv.shape), _sds((1, D_MODEL)), _sds((1, XA_HEAD_DIM))],
        compiler_params=_params(0))(mem, g_mem, wk, wv, kg, dkx, dvx)


def _xa_fwd(x1, kx, vx, g_xa, wq, qg, wo, tag, rider=None):
    L = x1.shape[0]

    def body(x_ref, k_ref, v_ref, g_ref, wq_ref, qg_ref, wo_ref, out_ref):
        xx = x_ref[...]
        qp = _nn(_rms(xx, g_ref[...]), wq_ref[...])
        outs = []
        for h in range(XA_HEADS):
            sl = slice(h * XA_HEAD_DIM, (h + 1) * XA_HEAD_DIM)
            qh = _rms(qp[:, sl], qg_ref[...])
            s = _nt(qh, k_ref[:, sl]) * (XA_HEAD_DIM ** -0.5)
            s = s - jnp.max(s, axis=-1, keepdims=True)
            e = jnp.exp(s)
            p = e / jnp.sum(e, axis=-1, keepdims=True)
            outs.append(_nn(p, v_ref[:, sl]))
        out_ref[...] = xx + _nn(jnp.concatenate(outs, axis=1), wo_ref[...])

    return _pcall(
        body, name=f"xa_fwd_{tag}", grid=(L // TL,), rider=rider,
        in_specs=[_rows(D_MODEL), _res(kx.shape), _res(vx.shape), _res((1, D_MODEL)), _res(wq.shape),
                  _res((1, XA_HEAD_DIM)), _res(wo.shape)],
        out_specs=_rows(D_MODEL),
        out_shape=_sds((L, D_MODEL)),
        compiler_params=_params(1),
    )(x1, kx, vx, g_xa, wq, qg, wo)


def _xa_bwd(x1, dx2, kx, vx, g_xa, wq, qg, wo, tag, rider=None):
    L = x1.shape[0]
    tl = 256

    def body(x_ref, dx_ref, k_ref, v_ref, g_ref, wq_ref, qg_ref, wo_ref,
             dx1_ref, dk_ref, dv_ref, dwq_ref, dwo_ref, dg_ref, dqg_ref):
        i = pl.program_id(0)
        xx = x_ref[...]
        dxx = dx_ref[...]
        hx, vjp_n = jax.vjp(_rms, xx, g_ref[...])
        qp = _nn(hx, wq_ref[...])
        do = _nt(dxx, wo_ref[...])
        outs, dqps, dks, dvs = [], [], [], []
        dqg = jnp.zeros((1, XA_HEAD_DIM), F32)
        for h in range(XA_HEADS):
            sl = slice(h * XA_HEAD_DIM, (h + 1) * XA_HEAD_DIM)
            kh, vh = k_ref[:, sl], v_ref[:, sl]
            qh, vjp_q = jax.vjp(_rms, qp[:, sl], qg_ref[...])
            s = _nt(qh, kh) * (XA_HEAD_DIM ** -0.5)
            s = s - jnp.max(s, axis=-1, keepdims=True)
            e = jnp.exp(s)
            p = e / jnp.sum(e, axis=-1, keepdims=True)
            outs.append(_nn(p, vh))
            doh = do[:, sl]
            dp = _nt(doh, vh)
            dvs.append(_tn(p, doh))
            ds = p * (dp - jnp.sum(dp * p, axis=-1, keepdims=True)) * (XA_HEAD_DIM ** -0.5)
            dks.append(_tn(ds, qh))
            dqh, dgh = vjp_q(_nn(ds, kh))
            dqps.append(dqh)
            dqg = dqg + dgh
        o = jnp.concatenate(outs, axis=1)
        dqp = jnp.concatenate(dqps, axis=1)
        dxn, dg = vjp_n(_nt(dqp, wq_ref[...]))
        dx1_ref[...] = dxx + dxn
        _accumulate(i, dk_ref, jnp.concatenate(dks, axis=1))
        _accumulate(i, dv_ref, jnp.concatenate(dvs, axis=1))
        _accumulate(i, dwq_ref, _tn(hx, dqp))
        _accumulate(i, dwo_ref, _tn(o, dxx))
        _accumulate(i, dg_ref, dg)
        _accumulate(i, dqg_ref, dqg)

    r = lambda c: _rows(c, tl)
    return _pcall(
        body, name=f"xa_bwd_{tag}", grid=(L // tl,), rider=rider,
        in_specs=[r(D_MODEL), r(D_MODEL), _res(kx.shape), _res(vx.shape), _res((1, D_MODEL)), _res(wq.shape),
                  _res((1, XA_HEAD_DIM)), _res(wo.shape)],
        out_specs=[r(D_MODEL), _res(kx.shape), _res(vx.shape), _res(wq.shape), _res(wo.shape), _res((1, D_MODEL)),
                   _res((1, XA_HEAD_DIM))],
        out_shape=[_sds((L, D_MODEL)), _sds(kx.shape), _sds(vx.shape), _sds(wq.shape), _sds(wo.shape),
                   _sds((1, D_MODEL)), _sds((1, XA_HEAD_DIM))],
        compiler_params=_params(1),
    )(x1, dx2, kx, vx, g_xa, wq, qg, wo)


def _swiglu(gate, up):
    return _silu(gate) * up


def _ffn_fwd(x2, g, w_in, w_out, tag, rider=None):
    L = x2.shape[0]
    tl = 256

    def body(x_ref, g_ref, wi_ref, wo_ref, out_ref):
        xx = x_ref[...]
        gu = _nt(_rms(xx, g_ref[...]), wi_ref[...])
        act = _swiglu(gu[:, 0:FFN_HIDDEN], gu[:, FFN_HIDDEN:2 * FFN_HIDDEN])
        out_ref[...] = xx + _nn(act, wo_ref[...])

    return _pcall(
        body, name=f"ffn_fwd_{tag}", grid=(L // tl,), rider=rider,
        in_specs=[_rows(D_MODEL, tl), _res((1, D_MODEL)), _res(w_in.shape), _res(w_out.shape)],
        out_specs=_rows(D_MODEL, tl),
        out_shape=_sds((L, D_MODEL)),
        compiler_params=_params(1, 56),
    )(x2, g, w_in, w_out)


def _ffn_bwd(x2, dx3, g, w_in, w_out, tag, rider=None):
    L = x2.shape[0]
    tl = 256

    def body(x_ref, dx_ref, g_ref, wi_ref, wo_ref, dx2_ref, dgu_ref, act_ref, hf_ref, dg_ref):
        i = pl.program_id(0)
        xx = x_ref[...]
        dxx = dx_ref[...]
        hf, vjp_n = jax.vjp(_rms, xx, g_ref[...])
        gu = _nt(hf, wi_ref[...])
        act, vjp_a = jax.vjp(_swiglu, gu[:, 0:FFN_HIDDEN], gu[:, FFN_HIDDEN:2 * FFN_HIDDEN])
        dgate, dup = vjp_a(_nt(dxx, wo_ref[...]))
        dgu = jnp.concatenate([dgate, dup], axis=1).astype(BF16)
        dxn, dg = vjp_n(_nn(dgu, wi_ref[...]))
        dx2_ref[...] = dxx + dxn
        dgu_ref[...] = dgu
        act_ref[...] = act.astype(BF16)
        hf_ref[...] = hf.astype(BF16)
        _accumulate(i, dg_ref, dg)

    r = lambda c: _rows(c, tl)
    return _pcall(
        body, name=f"ffn_bwd_{tag}", grid=(L // tl,), rider=rider,
        in_specs=[r(D_MODEL), r(D_MODEL), _res((1, D_MODEL)), _res(w_in.shape), _res(w_out.shape)],
        out_specs=[r(D_MODEL), r(2 * FFN_HIDDEN), r(FFN_HIDDEN), r(D_MODEL), _res((1, D_MODEL))],
        out_shape=[_sds((L, D_MODEL)), _sds((L, 2 * FFN_HIDDEN), BF16), _sds((L, FFN_HIDDEN), BF16),
                   _sds((L, D_MODEL), BF16), _sds((1, D_MODEL))],
        compiler_params=_params(1, 56),
    )(x2, dx3, g, w_in, w_out)


def _matmul_tn(a, b, tm, tn, tag):
    L, M = a.shape
    N = b.shape[1]
    tk = 512

    def body(a_ref, b_ref, o_ref):
        _accumulate(pl.program_id(2), o_ref, _tn(a_ref[...], b_ref[...]))

    return pl.pallas_call(
        body, name=f"matmul_tn_{tag}", grid=(M // tm, N // tn, L // tk),
        in_specs=[pl.BlockSpec((tk, tm), lambda m, n, k: (k, m)), pl.BlockSpec((tk, tn), lambda m, n, k: (k, n))],
        out_specs=pl.BlockSpec((tm, tn), lambda m, n, k: (m, n)),
        out_shape=_sds((M, N)),
        compiler_params=_params(3),
    )(a, b)


def _loss_head(y, tgt):
    L = y.shape[0]
    n_tiles = L // TL

    def body(y_ref, t_ref, loss_ref, dy_ref, acc_ref):
        i = pl.program_id(0)
        diff = y_ref[...] - t_ref[...]
        dy_ref[...] = diff * (1.0 / D_MODEL)
        _accumulate(i, acc_ref, jnp.sum(diff * diff, axis=0, keepdims=True))

        @pl.when(i == n_tiles - 1)
        def _():
            loss_ref[...] = jnp.sum(acc_ref[...], axis=1, keepdims=True) * (0.5 / D_MODEL)

    return pl.pallas_call(
        body, name="loss_head", grid=(n_tiles,),
        in_specs=[_rows(D_MODEL), _rows(D_MODEL)],
        out_specs=[_res((1, 1)), _rows(D_MODEL)],
        out_shape=[_sds((1, 1)), _sds((L, D_MODEL))],
        scratch_shapes=[pltpu.VMEM((1, D_MODEL), F32)],
        compiler_params=_params(1),
    )(y, tgt)


def _row(v):
    return v.reshape(1, -1)


def _layer_consts():
    r = np.arange(SB_WIDTH)
    mavg = ((r[:, None] // SB_HEAD_DIM) == (r[None, :] // SB_HEAD_DIM)).astype(np.float32) / SB_HEAD_DIM
    ul, ue = _sb_tri_consts()
    return dict(mavg=jnp.asarray(mavg), ul=ul, ue=ue, mask=_ssm_mask())


def _ssm_rows(P):
    lanes = lambda a: a.reshape(1, SSM_LANES)
    return dict(
        lr=lanes(P['ssm_lam_re']), li=lanes(P['ssm_lam_im']),
        ldt=lanes(jnp.repeat(P['ssm_log_dt'], SSM_STATE)),
        brt=P['ssm_b_re'].transpose(2, 0, 1).reshape(SSM_GROUP, SSM_LANES),
        bit=P['ssm_b_im'].transpose(2, 0, 1).reshape(SSM_GROUP, SSM_LANES),
        crt=P['ssm_c_re'].transpose(1, 0, 2).reshape(SSM_GROUP, SSM_LANES),
        cit=P['ssm_c_im'].transpose(1, 0, 2).reshape(SSM_GROUP, SSM_LANES))


def _layer_fwd(x, mem, P, C, tag, ride):
    gq = _row(jnp.tile(P['sb_q_norm_g'], SB_WIDTH // SB_HEAD_DIM))
    gk = _row(jnp.tile(P['sb_k_norm_g'], SB_WIDTH // SB_HEAD_DIM))
    q, k, v, hg, u = _mix_in_fwd(x, _row(P['norm_mix_g']), P['w_in'], gq, gk, C['mavg'], tag,
                                 rider=ride("mix_in_fwd_" + tag))
    o_sb, rsave, n_done = _sb_fwd(q, k, v, C['ul'], tag, rider=ride("sb_fwd_" + tag))
    hp = jnp.pad(hg, ((CONV_HALO, 0), (0, 0)))
    hc = _conv_fwd(hp, P['conv_dw_w'].T, tag)
    S = _ssm_rows(P)
    ar, ai, bbr, bbi, cbr, cbi = _ssm_prep(S['lr'], S['li'], S['ldt'], S['brt'], S['bit'], S['crt'], S['cit'],
                                           C['mask'], tag)
    y, xr, xi = _ssm_fwd(u, ar, ai, bbr, bbi, cbr, cbi, _row(P['ssm_d']), tag, rider=ride("ssm_fwd_" + tag))
    gb = P['branch_norm_g']
    x1 = _mix_out_fwd(x, o_sb, hc, y, _row(P['conv_dw_b']), _row(P['conv_ln_g']), _row(P['conv_ln_b']),
                      P['conv_pw2_w'], P['ssm_glu_w'], _row(gb[0:512]), _row(gb[512:768]), _row(gb[768:1024]),
                      P['w_out'], tag)
    kx, vx = _xa_mem_fwd(mem, _row(P['norm_mem_g']), P['xa_wk'], P['xa_wv'], _row(P['xa_k_norm_g']), tag)
    x2 = _xa_fwd(x1, kx, vx, _row(P['norm_xa_g']), P['xa_wq'], _row(P['xa_q_norm_g']), P['xa_wo'], tag,
                 rider=ride("xa_fwd_" + tag))
    x3 = _ffn_fwd(x2, _row(P['norm_ffn_g']), P['ffn_w_in'], P['ffn_w_out'], tag, rider=ride("ffn_fwd_" + tag))
    saved = dict(x=x, q=q, k=k, v=v, rsave=rsave, n_done=n_done, o_sb=o_sb, hp=hp, hc=hc, u=u, y=y, xr=xr, xi=xi, x1=x1, x2=x2,
                 kx=kx, vx=vx, gq=gq, gk=gk, S=S, ssm=(ar, ai, bbr, bbi, cbr, cbi))
    return x3, saved


def _layer_bwd(dx3, mem, P, C, sv, tag, ride, G):
    dx2, dgu, act, hf, dg = _ffn_bwd(sv['x2'], dx3, _row(P['norm_ffn_g']), P['ffn_w_in'], P['ffn_w_out'], tag,
                                     rider=ride("ffn_bwd_" + tag))
    G['norm_ffn_g'] = dg.reshape(-1)
    G['ffn_w_in'] = _matmul_tn(dgu, hf, 2 * FFN_HIDDEN // 4, D_MODEL, "ffn_in_" + tag)
    G['ffn_w_out'] = _matmul_tn(act, dx3, FFN_HIDDEN // 2, 512, "ffn_out_" + tag)
    dx1, dkx, dvx, dwq, dwo, dg, dqg = _xa_bwd(sv['x1'], dx2, sv['kx'], sv['vx'], _row(P['norm_xa_g']), P['xa_wq'],
                                               _row(P['xa_q_norm_g']), P['xa_wo'], tag, rider=ride("xa_bwd_" + tag))
    G['xa_wq'], G['xa_wo'], G['norm_xa_g'], G['xa_q_norm_g'] = dwq, dwo, dg.reshape(-1), dqg.reshape(-1)
    dwk, dwv, dg, dkg = _xa_mem_bwd(mem, _row(P['norm_mem_g']), P['xa_wk'], P['xa_wv'], _row(P['xa_k_norm_g']),
                                    dkx, dvx, tag)
    G['xa_wk'], G['xa_wv'], G['norm_mem_g'], G['xa_k_norm_g'] = dwk, dwv, dg.reshape(-1), dkg.reshape(-1)
    gb = P['branch_norm_g']
    (do_sb, dhc, dy, ddwb, dlng, dlnb, dpw2, dglu, dg1, dg2, dg3, dwout) = _mix_out_bwd(
        dx1, sv['o_sb'], sv['hc'], sv['y'], _row(P['conv_dw_b']), _row(P['conv_ln_g']), _row(P['conv_ln_b']),
        P['conv_pw2_w'], P['ssm_glu_w'], _row(gb[0:512]), _row(gb[512:768]), _row(gb[768:1024]), P['w_out'], tag)
    G['conv_dw_b'], G['conv_ln_g'], G['conv_ln_b'] = ddwb.reshape(-1), dlng.reshape(-1), dlnb.reshape(-1)
    G['conv_pw2_w'], G['ssm_glu_w'], G['w_out'] = dpw2, dglu, dwout
    G['branch_norm_g'] = jnp.concatenate([dg1.reshape(-1), dg2.reshape(-1), dg3.reshape(-1)])
    ar, ai, bbr, bbi, cbr, cbi = sv['ssm']
    du, dar, dai, dbbr, dbbi, dcbr, dcbi, dd = _ssm_bwd(dy, sv['u'], sv['xr'], sv['xi'], ar, ai, bbr, bbi, cbr, cbi,
                                                        _row(P['ssm_d']), tag, rider=ride("ssm_bwd_" + tag))
    S = sv['S']
    dlr, dli, dldt, dbrt, dbit, dcrt, dcit = _ssm_prep_bwd(S['lr'], S['li'], S['ldt'], S['brt'], S['bit'], C['mask'],
                                                           dar, dai, dbbr, dbbi, dcbr, dcbi, tag)
    G['ssm_lam_re'] = dlr.reshape(SSM_GROUPS, SSM_STATE)
    G['ssm_lam_im'] = dli.reshape(SSM_GROUPS, SSM_STATE)
    G['ssm_log_dt'] = dldt.reshape(SSM_GROUPS, SSM_STATE).sum(axis=1)
    G['ssm_b_re'] = dbrt.reshape(SSM_GROUP, SSM_GROUPS, SSM_STATE).transpose(1, 2, 0)
    G['ssm_b_im'] = dbit.reshape(SSM_GROUP, SSM_GROUPS, SSM_STATE).transpose(1, 2, 0)
    G['ssm_c_re'] = dcrt.reshape(SSM_GROUP, SSM_GROUPS, SSM_STATE).transpose(1, 0, 2)
    G['ssm_c_im'] = dcit.reshape(SSM_GROUP, SSM_GROUPS, SSM_STATE).transpose(1, 0, 2)
    G['ssm_d'] = dd.reshape(-1)
    dpad = jnp.pad(dhc, ((0, CONV_HALO), (0, 0)))
    dhg, ddww = _conv_bwd(dpad, sv['hp'], P['conv_dw_w'].T, tag)
    G['conv_dw_w'] = ddww.T
    dq, dk, dv = _sb_bwd(sv['n_done'], sv['q'], sv['k'], sv['v'], sv['rsave'], do_sb, C['ul'], C['ue'], tag)
    dx, dwin, dg, dgq, dgk = _mix_in_bwd(sv['x'], dx1, dq, dk, dv, dhg, du, _row(P['norm_mix_g']), P['w_in'],
                                         sv['gq'], sv['gk'], C['mavg'], tag)
    G['w_in'], G['norm_mix_g'] = dwin, dg.reshape(-1)
    G['sb_q_norm_g'] = dgq.reshape(SB_WIDTH // SB_HEAD_DIM, SB_HEAD_DIM).sum(axis=0)
    G['sb_k_norm_g'] = dgk.reshape(SB_WIDTH // SB_HEAD_DIM, SB_HEAD_DIM).sum(axis=0)
    return dx, G


def _slot_sum(r_ref):
    g = r_ref[0].astype(F32)
    for s in range(1, r_ref.shape[0]):
        g = g + r_ref[s].astype(F32)
    return g


def _adam_update(g, w, m, v):
    nm = ADAM_B1 * m + (1.0 - ADAM_B1) * g
    nv = ADAM_B2 * v + (1.0 - ADAM_B2) * (g * g)
    m_hat = nm * (1.0 / (1.0 - ADAM_B1 ** ADAM_STEP))
    v_hat = nv * (1.0 / (1.0 - ADAM_B2 ** ADAM_STEP))
    return -ADAM_LR * (m_hat / (jnp.sqrt(v_hat) + ADAM_EPS) + ADAM_WD * w), nm, nv


def _adamw(recv, w, m, v, tile, name):
    n_slots, R, C = recv.shape

    def body(r_ref, w_ref, m_ref, v_ref, g_ref, d_ref, nm_ref, nv_ref):
        g = _slot_sum(r_ref)
        g_ref[...] = g
        d_ref[...], nm_ref[...], nv_ref[...] = _adam_update(g, w_ref[...], m_ref[...], v_ref[...])

    rows = pl.BlockSpec((tile, C), lambda i: (i, 0))
    out = _sds((R, C))
    return pl.pallas_call(
        body, name=name, grid=(R // tile,),
        in_specs=[pl.BlockSpec((n_slots, tile, C), lambda i: (0, i, 0)), rows, rows, rows],
        out_specs=[rows, rows, rows, rows],
        out_shape=[out, out, out, out],
        compiler_params=_params(1),
    )(recv, w, m, v)


def _adamw_layers(recvs, w, m, v, tile, name):
    n_slots, R, C = recvs[0].shape

    def body(*refs):
        r_refs = refs[:DEPTH]
        w_ref, m_ref, v_ref, g_ref, d_ref, nm_ref, nv_ref = refs[DEPTH:]
        for l in range(DEPTH):
            @pl.when(pl.program_id(0) == l)
            def _(l=l):
                g = _slot_sum(r_refs[l])
                g_ref[0] = g
                d_ref[0], nm_ref[0], nv_ref[0] = _adam_update(g, w_ref[0], m_ref[0], v_ref[0])

    rspec = lambda l: pl.BlockSpec((n_slots, tile, C), lambda ll, i: (0, jnp.where(ll == l, i, 0), 0))
    rows = pl.BlockSpec((1, tile, C), lambda ll, i: (ll, i, 0))
    out = _sds((DEPTH, R, C))
    return pl.pallas_call(
        body, name=name, grid=(DEPTH, R // tile),
        in_specs=[rspec(l) for l in range(DEPTH)] + [rows, rows, rows],
        out_specs=[rows, rows, rows, rows],
        out_shape=[out, out, out, out],
        compiler_params=_params(2),
    )(*recvs, w, m, v)


def _reduce_slots(recv, tile, name):
    n_slots, R, C = recv.shape

    def body(r_ref, g_ref):
        g_ref[...] = _slot_sum(r_ref)

    return pl.pallas_call(
        body, name=name, grid=(R // tile,),
        in_specs=[pl.BlockSpec((n_slots, tile, C), lambda i: (0, i, 0))],
        out_specs=pl.BlockSpec((tile, C), lambda i: (i, 0)),
        out_shape=_sds((R, C)),
        compiler_params=_params(1),
    )(recv)


SEG = SUBLANES * LANES


def _pad_to(n, mult):
    return -(-n // mult) * mult


def _pack(arrays, dtype, row_mult, lead=0):
    keep = [(0, 0)] * lead
    parts = []
    for a in arrays:
        flat = a.reshape(a.shape[:lead] + (-1,)).astype(dtype)
        n = flat.shape[-1]
        parts.append(jnp.pad(flat, keep + [(0, _pad_to(n, SEG) - n)]))
    flat = jnp.concatenate(parts, axis=-1)
    n = flat.shape[-1]
    flat = jnp.pad(flat, keep + [(0, _pad_to(n, row_mult * LANES) - n)])
    return flat.reshape(flat.shape[:lead] + (-1, LANES))


def _unpack(buf, shapes):
    lead = buf.shape[:-2]
    flat = buf.reshape(lead + (-1,))
    out, off = [], 0
    for shp in shapes:
        n = int(np.prod(shp))
        out.append(flat[..., off:off + n].reshape(lead + tuple(shp)))
        off += _pad_to(n, SEG)
    return out


def _rows_first(a, name):
    return a.transpose(0, 2, 1) if SHARD_AXIS[name] == 2 else a


SMALL_TILE = 256
DIRECT_NAMES = [n for n in BIG_NAMES if n != 'conv_dw_w']
GATHER_RIDES = {
    "mix_in_fwd_l0": [(0, 'conv_pw2_w'), (0, 'ssm_glu_w'), (0, 'w_out')],
    "sb_fwd_l0": [(0, 'xa_wq'), (0, 'xa_wk'), (0, 'xa_wv'), (0, 'xa_wo'), (0, 'ffn_w_in')],
    "ssm_fwd_l0": [(0, 'ffn_w_out')],
    "xa_fwd_l0": [(1, 'w_in'), (1, 'conv_pw2_w'), (1, 'ssm_glu_w')],
    "ffn_fwd_l0": [(1, 'w_out'), (1, 'xa_wq'), (1, 'xa_wk'), (1, 'xa_wv'), (1, 'xa_wo')],
    "sb_fwd_l1": [(1, 'ffn_w_in'), (1, 'ffn_w_out')],
}
_MID = ['xa_wq', 'xa_wo', 'xa_wk', 'xa_wv', 'w_out', 'conv_pw2_w', 'ssm_glu_w']
SCATTER_RIDES = {
    "xa_bwd_l1": [(1, 'ffn_w_in'), (1, 'ffn_w_out')],
    "ssm_bwd_l1": [(1, n) for n in _MID],
    "ffn_bwd_l0": [(1, 'w_in')],
    "xa_bwd_l0": [(0, 'ffn_w_in'), (0, 'ffn_w_out')],
    "ssm_bwd_l0": [(0, n) for n in _MID],
}


def _tile_rows(rows):
    return next(t for t in range(min(rows, 256), 0, -ROW_ALIGN) if rows % t == 0 and t % ROW_ALIGN == 0)


class _LayerWeights:
    def __init__(self, layer, small, full, conv):
        self.layer, self.small, self.full, self.conv = layer, small, full, conv

    def __getitem__(self, name):
        if name == 'conv_dw_w':
            return self.conv[self.layer]
        return self.full[(self.layer, name)] if name in SHARD_AXIS else self.small[name][self.layer]


def kernel(x, mem, norm_mix_g, w_in, sb_q_norm_g, sb_k_norm_g, conv_dw_w, conv_dw_b, conv_ln_g, conv_ln_b, conv_pw2_w, ssm_lam_re, ssm_lam_im, ssm_log_dt, ssm_b_re, ssm_b_im, ssm_c_re, ssm_c_im, ssm_d, ssm_glu_w, branch_norm_g, w_out, norm_xa_g, norm_mem_g, xa_wq, xa_wk, xa_wv, xa_q_norm_g, xa_k_norm_g, xa_wo, norm_ffn_g, ffn_w_in, ffn_w_out, loss_target, m_norm_mix_g, m_w_in, m_sb_q_norm_g, m_sb_k_norm_g, m_conv_dw_w, m_conv_dw_b, m_conv_ln_g, m_conv_ln_b, m_conv_pw2_w, m_ssm_lam_re, m_ssm_lam_im, m_ssm_log_dt, m_ssm_b_re, m_ssm_b_im, m_ssm_c_re, m_ssm_c_im, m_ssm_d, m_ssm_glu_w, m_branch_norm_g, m_w_out, m_norm_xa_g, m_norm_mem_g, m_xa_wq, m_xa_wk, m_xa_wv, m_xa_q_norm_g, m_xa_k_norm_g, m_xa_wo, m_norm_ffn_g, m_ffn_w_in, m_ffn_w_out, v_norm_mix_g, v_w_in, v_sb_q_norm_g, v_sb_k_norm_g, v_conv_dw_w, v_conv_dw_b, v_conv_ln_g, v_conv_ln_b, v_conv_pw2_w, v_ssm_lam_re, v_ssm_lam_im, v_ssm_log_dt, v_ssm_b_re, v_ssm_b_im, v_ssm_c_re, v_ssm_c_im, v_ssm_d, v_ssm_glu_w, v_branch_norm_g, v_w_out, v_norm_xa_g, v_norm_mem_g, v_xa_wq, v_xa_wk, v_xa_wv, v_xa_q_norm_g, v_xa_k_norm_g, v_xa_wo, v_norm_ffn_g, v_ffn_w_in, v_ffn_w_out):
    args = locals()
    w_loc = {n: args[n] for n in WEIGHT_NAMES}
    m_loc = {n: args["m_" + n] for n in WEIGHT_NAMES}
    v_loc = {n: args["v_" + n] for n in WEIGHT_NAMES}
    me = _my_index()
    shard = {(l, n): _rows_first(w_loc[n], n)[l].astype(BF16) for l in range(DEPTH) for n in DIRECT_NAMES}
    conv_shape = _rows_first(w_loc['conv_dw_w'], 'conv_dw_w').shape
    conv_rows = _pack([_rows_first(w_loc['conv_dw_w'], 'conv_dw_w')], F32, SUBLANES)
    full = {}
    recv = {}
    grads = [dict() for _ in range(DEPTH)]

    def ride(kernel_name):
        if kernel_name in GATHER_RIDES:
            keys = GATHER_RIDES[kernel_name]
            return _Rider(gathers=[shard[k] for k in keys], done=lambda res: full.update(zip(keys, res)))
        if kernel_name in SCATTER_RIDES:
            keys = SCATTER_RIDES[kernel_name]
            return _Rider(scatters=[grads[l][n].astype(BF16) for (l, n) in keys],
                          done=lambda res: recv.update(zip(keys, res)))
        return None

    first = []
    _exchange(_Rider(gathers=[shard[(0, 'w_in')], conv_rows], done=first.extend), "gather_first")
    full[(0, 'w_in')] = first[0]
    conv_all = first[1].reshape(N_DEV, -1)[:, :int(np.prod(conv_shape))].reshape((N_DEV,) + conv_shape)
    conv_full = conv_all.transpose(1, 0, 2, 3).reshape(DEPTH, N_DEV * conv_shape[1], conv_shape[2])
    weights = [_LayerWeights(l, w_loc, full, conv_full) for l in range(DEPTH)]

    consts = _layer_consts()
    h, saved = x[0], []
    for l in range(DEPTH):
        h, sv = _layer_fwd(h, mem[0], weights[l], consts, f"l{l}", ride)
        saved.append(sv)
    loss_part, dh = _loss_head(h, loss_target[0])
    for l in reversed(range(DEPTH)):
        dh, _ = _layer_bwd(dh, mem[0], weights[l], consts, saved[l], f"l{l}", ride, grads[l])
    grad_x = dh
    loss = lax.psum(loss_part[0, 0], MESH_AXES)

    small_shapes = [w_loc[n].shape for n in SMALL_NAMES]
    conv_nat = (DEPTH,) + grads[0]['conv_dw_w'].shape[::-1]
    small_send = _pack([jnp.stack([grads[l][n] for l in range(DEPTH)]) for n in SMALL_NAMES]
                       + [jnp.stack([grads[l]['conv_dw_w'].T for l in range(DEPTH)])], F32, SMALL_TILE)
    last = []
    _exchange(_Rider(gathers=[small_send], scatters=[grads[0]['w_in'].astype(BF16)], done=last.extend), "exchange_last")
    recv[(0, 'w_in')] = last[1]
    small_sum = _reduce_slots(last[0].reshape((N_DEV,) + small_send.shape), SMALL_TILE, "reduce_replicated")
    small_g = _unpack(small_sum, small_shapes + [conv_nat])
    conv_cols = w_loc['conv_dw_w'].shape[2]
    conv_g = lax.dynamic_slice_in_dim(small_g[-1], me * conv_cols, conv_cols, axis=2)

    result = [{}, {}, {}, {}]
    for n in DIRECT_NAMES:
        parts = [recv[(l, n)] for l in range(DEPTH)]
        if SHARD_AXIS[n] == 2:
            parts = [_reduce_slots(p, _tile_rows(p.shape[1]), f"reduce_{n}_l{l}").T[None] for l, p in enumerate(parts)]
        outs = _adamw_layers(parts, w_loc[n], m_loc[n], v_loc[n], _tile_rows(w_loc[n].shape[1]), f"adamw_{n}")
        for kind in range(4):
            result[kind][n] = outs[kind]
    packed_names = SMALL_NAMES + ['conv_dw_w']
    pk = lambda d: _pack([d[n] for n in packed_names], F32, SMALL_TILE)
    outs = _adamw(_pack(small_g[:-1] + [conv_g], F32, SMALL_TILE)[None], pk(w_loc), pk(m_loc), pk(v_loc), SMALL_TILE,
                  "adamw_packed")
    for kind in range(4):
        for n, a in zip(packed_names, _unpack(outs[kind], [w_loc[n].shape for n in packed_names])):
            result[kind][n] = a
    return (loss, grad_x[None], *[result[0][n] for n in WEIGHT_NAMES], *[result[1][n] for n in WEIGHT_NAMES],
            *[result[2][n] for n in WEIGHT_NAMES], *[result[3][n] for n in WEIGHT_NAMES])
```

```python
import functools

import numpy as np
import jax
import jax.numpy as jnp
from jax import lax
from jax.experimental import pallas as pl
from jax.experimental.pallas import tpu as pltpu

F32 = jnp.float32
BF16 = jnp.bfloat16
EPS = 1e-6
D_MODEL = 1024
DEPTH = 2
N_DEV = 8
SB_WIDTH = 512
SB_HEAD_DIM = 64
CONV_CH = 256
CONV_WIDTH = 31
SSM_CH = 256
SSM_GROUP = 16
SSM_GROUPS = 16
SSM_STATE = 64
SSM_LANES = SSM_GROUPS * SSM_STATE
XA_HEADS = 4
XA_HEAD_DIM = 256
FFN_HIDDEN = 2816
ADAM_LR = 0.001
ADAM_B1 = 0.9
ADAM_B2 = 0.999
ADAM_EPS = 1e-08
ADAM_WD = 0.01
ADAM_STEP = 10

LANES = 128
SUBLANES = 8
TL = 512
SB_TQ = 256
SB_TK = 128
SB_PAIRS = 2
SB_SCALE = SB_HEAD_DIM ** -0.5
SB_DEAD = -120.0
SSM_T = 256
CONV_HALO = 32
CONV_SUB = 64
VMEM_MB = 48

MESH_AXES = ("x", "y", "c")
WEIGHT_NAMES = ['norm_mix_g', 'w_in', 'sb_q_norm_g', 'sb_k_norm_g', 'conv_dw_w', 'conv_dw_b', 'conv_ln_g',
                'conv_ln_b', 'conv_pw2_w', 'ssm_lam_re', 'ssm_lam_im', 'ssm_log_dt', 'ssm_b_re', 'ssm_b_im',
                'ssm_c_re', 'ssm_c_im', 'ssm_d', 'ssm_glu_w', 'branch_norm_g', 'w_out', 'norm_xa_g',
                'norm_mem_g', 'xa_wq', 'xa_wk', 'xa_wv', 'xa_q_norm_g', 'xa_k_norm_g', 'xa_wo', 'norm_ffn_g',
                'ffn_w_in', 'ffn_w_out']
SHARD_AXIS = {'w_in': 2, 'conv_dw_w': 2, 'conv_pw2_w': 1, 'ssm_glu_w': 2, 'w_out': 1, 'xa_wq': 1, 'xa_wk': 1,
              'xa_wv': 1, 'xa_wo': 1, 'ffn_w_in': 2, 'ffn_w_out': 1}
BIG_NAMES = [n for n in WEIGHT_NAMES if n in SHARD_AXIS]
SMALL_NAMES = [n for n in WEIGHT_NAMES if n not in SHARD_AXIS]


def _nn(a, b):
    return jnp.dot(a.astype(BF16), b.astype(BF16), preferred_element_type=F32)


def _nt(a, b):
    return lax.dot_general(a.astype(BF16), b.astype(BF16), (((1,), (1,)), ((), ())), preferred_element_type=F32)


def _tn(a, b):
    return lax.dot_general(a.astype(BF16), b.astype(BF16), (((0,), (0,)), ((), ())), preferred_element_type=F32)


def _rms(x, g):
    return x * lax.rsqrt(jnp.mean(x * x, axis=-1, keepdims=True) + EPS) * g


def _sigmoid(x):
    return 1.0 / (1.0 + jnp.exp(-x))


def _silu(x):
    return x * _sigmoid(x)


def _layer_norm(x, g, b):
    mu = jnp.mean(x, axis=-1, keepdims=True)
    xc = x - mu
    var = jnp.mean(xc * xc, axis=-1, keepdims=True)
    return xc * lax.rsqrt(var + EPS) * g + b


def _head_rms64(p, g, mavg):
    ms = jnp.dot(p * p, mavg, preferred_element_type=F32)
    return p * lax.rsqrt(ms + EPS) * g


def _params(n_grid, vmem_mb=VMEM_MB):
    return pltpu.CompilerParams(dimension_semantics=("arbitrary",) * n_grid, vmem_limit_bytes=vmem_mb << 20)


def _rows(cols, tl=TL):
    return pl.BlockSpec((tl, cols), lambda i: (i, 0))


def _res(shape):
    nd = len(shape)
    return pl.BlockSpec(tuple(shape), lambda *_: (0,) * nd)


def _sds(shape, dtype=F32):
    return jax.ShapeDtypeStruct(tuple(shape), dtype)


def _accumulate(i, ref, val):
    @pl.when(i == 0)
    def _():
        ref[...] = val

    @pl.when(i > 0)
    def _():
        ref[...] += val


HBM = pl.BlockSpec(memory_space=pltpu.HBM)
ROW_ALIGN = 16


def _my_index():
    return lax.axis_index("x") * 4 + lax.axis_index("y") * 2 + lax.axis_index("c")


def _peer(k):
    x, y, c = lax.axis_index("x"), lax.axis_index("y"), lax.axis_index("c")
    return (x ^ ((k >> 2) & 1), y ^ ((k >> 1) & 1), c ^ (k & 1))


class _Rider:
    def __init__(self, gathers=(), scatters=(), done=None):
        self.gathers, self.scatters, self.done = list(gathers), list(scatters), done

    @property
    def inputs(self):
        return self.gathers + self.scatters

    def out_shapes(self):
        return ([_sds((N_DEV * a.shape[0],) + a.shape[1:], a.dtype) for a in self.gathers]
                + [_sds((N_DEV, a.shape[0] // N_DEV) + a.shape[1:], a.dtype) for a in self.scatters])


def _rider_copies(rider, in_refs, out_refs, send_sems, recv_sems, local_sems):
    me = _my_index()
    local, sends, recvs = [], [], []
    for t, (src, dst) in enumerate(zip(in_refs, out_refs)):
        gather = t < len(rider.gathers)
        rows = src.shape[0] if gather else src.shape[0] // N_DEV

        def block(ref, d, rows=rows):
            return ref.at[pl.ds(pl.multiple_of(d * rows, ROW_ALIGN), rows)]

        src_for = (lambda p, src=src: src) if gather else (lambda p, src=src: block(src, p))
        dst_for = (lambda d, dst=dst: block(dst, d)) if gather else (lambda d, dst=dst: dst.at[d])
        local.append(pltpu.make_async_copy(src_for(me), dst_for(me), local_sems.at[t]))
        for k in range(1, N_DEV):
            args = dict(send_sem=send_sems.at[t * N_DEV + k], recv_sem=recv_sems.at[t * N_DEV + k], device_id=_peer(k),
                        device_id_type=pl.DeviceIdType.MESH)
            sends.append(pltpu.make_async_remote_copy(src_ref=src_for(me ^ k), dst_ref=dst_for(me), **args))
            recvs.append(pltpu.make_async_remote_copy(src_ref=src_for(me ^ k), dst_ref=dst_for(me ^ k), **args))
    return local, sends, recvs


def _pcall(body, *, name, out_shape, grid=(), in_specs=None, out_specs=None, scratch_shapes=(), compiler_params=None,
           rider=None):
    if rider is None or not rider.inputs:
        return pl.pallas_call(body, name=name, grid=grid, in_specs=in_specs, out_specs=out_specs, out_shape=out_shape,
                              scratch_shapes=list(scratch_shapes), compiler_params=compiler_params)
    single = not isinstance(out_shape, (list, tuple))
    outs = [out_shape] if single else list(out_shape)
    ospecs = [out_specs] if single else list(out_specs)
    n_in, n_out, n_scr, n_r = len(in_specs), len(outs), len(scratch_shapes), len(rider.inputs)

    def wrapped(*refs):
        ins, rin = refs[:n_in], refs[n_in:n_in + n_r]
        own_out = refs[n_in + n_r:n_in + n_r + n_out]
        rout = refs[n_in + n_r + n_out:n_in + 2 * n_r + n_out]
        scratch = refs[n_in + 2 * n_r + n_out:n_in + 2 * n_r + n_out + n_scr]
        local, sends, recvs = _rider_copies(rider, rin, rout, *refs[-3:])

        def start():
            for cp in local + sends:
                cp.start()

        def wait():
            for cp in recvs:
                cp.wait_recv()
            for cp in sends:
                cp.wait_send()
            for cp in local:
                cp.wait()

        if grid:
            ids = [pl.program_id(a) for a in range(len(grid))]
            first = functools.reduce(jnp.logical_and, [i == 0 for i in ids])
            last = functools.reduce(jnp.logical_and, [i == n - 1 for i, n in zip(ids, grid)])
            pl.when(first)(start)
            body(*ins, *own_out, *scratch)
            pl.when(last)(wait)
        else:
            start()
            body(*ins, *own_out, *scratch)
            wait()

    call = pl.pallas_call(
        wrapped, name=name, grid=grid, in_specs=list(in_specs) + [HBM] * n_r, out_specs=ospecs + [HBM] * n_r,
        out_shape=outs + rider.out_shapes(),
        scratch_shapes=list(scratch_shapes) + [pltpu.SemaphoreType.DMA((n_r * N_DEV,)),
                                               pltpu.SemaphoreType.DMA((n_r * N_DEV,)), pltpu.SemaphoreType.DMA((n_r,))],
        compiler_params=compiler_params)

    def run(*args):
        res = call(*args, *rider.inputs)
        if rider.done is not None:
            rider.done(list(res[n_out:]))
        return res[0] if single else list(res[:n_out])

    return run


def _exchange(rider, name):
    def body():
        pass

    _pcall(body, name=name, out_shape=[], in_specs=[], out_specs=[], rider=rider)()


def _mixin_post(pq, pk, a, b, gq, gk, mavg):
    return _head_rms64(pq, gq, mavg), _head_rms64(pk, gk, mavg), a * _sigmoid(b)


def _mix_in_fwd(x, g_mix, w_in, gq, gk, mavg, tag, rider=None):
    L = x.shape[0]

    def body(x_ref, g_ref, w_ref, gq_ref, gk_ref, mavg_ref, q_ref, k_ref, v_ref, hg_ref, u_ref):
        h = _rms(x_ref[...], g_ref[...])
        p = _nt(h, w_ref[...])
        q, k, hg = _mixin_post(p[:, 0:512], p[:, 512:1024], p[:, 1536:1792], p[:, 1792:2048],
                               gq_ref[...], gk_ref[...], mavg_ref[...])
        q_ref[...] = (q * SB_SCALE).astype(BF16)
        k_ref[...] = k.astype(BF16)
        v_ref[...] = p[:, 1024:1536].astype(BF16)
        hg_ref[...] = hg
        u_ref[...] = p[:, 2048:2304]

    return _pcall(
        body, name=f"mix_in_fwd_{tag}", grid=(L // TL,),
        in_specs=[_rows(D_MODEL), _res((1, D_MODEL)), _res(w_in.shape), _res((1, 512)), _res((1, 512)), _res((512, 512))],
        out_specs=[_rows(512), _rows(512), _rows(512), _rows(256), _rows(256)],
        out_shape=[_sds((L, 512), BF16), _sds((L, 512), BF16), _sds((L, 512), BF16), _sds((L, 256)), _sds((L, 256))],
        compiler_params=_params(1), rider=rider,
    )(x, g_mix, w_in, gq, gk, mavg)


def _mix_in_bwd(x, dres, dq, dk, dv, dhg, du, g_mix, w_in, gq, gk, mavg, tag):
    L = x.shape[0]
    tl = 256

    def body(x_ref, dres_ref, dq_ref, dk_ref, dv_ref, dhg_ref, du_ref, g_ref, w_ref, gq_ref, gk_ref, mavg_ref,
             dx_ref, dw_ref, dg_ref, dgq_ref, dgk_ref):
        i = pl.program_id(0)
        xx = x_ref[...]
        g = g_ref[...]
        mavg_v = mavg_ref[...]
        h, vjp_n = jax.vjp(_rms, xx, g)
        p = _nt(h, w_ref[...])
        _, vjp_p = jax.vjp(lambda pq, pk, a, b, gq_, gk_: _mixin_post(pq, pk, a, b, gq_, gk_, mavg_v),
                           p[:, 0:512], p[:, 512:1024], p[:, 1536:1792], p[:, 1792:2048], gq_ref[...], gk_ref[...])
        dpq, dpk, da, db, dgq, dgk = vjp_p((dq_ref[...], dk_ref[...], dhg_ref[...]))
        dp = jnp.concatenate([dpq, dpk, dv_ref[...], da, db, du_ref[...]], axis=1)
        dh = _nn(dp, w_ref[...])
        dxn, dg = vjp_n(dh)
        dx_ref[...] = dres_ref[...] + dxn
        _accumulate(i, dw_ref, _tn(dp, h))
        _accumulate(i, dg_ref, dg)
        _accumulate(i, dgq_ref, dgq)
        _accumulate(i, dgk_ref, dgk)

    r = lambda c: _rows(c, tl)
    return pl.pallas_call(
        body, name=f"mix_in_bwd_{tag}", grid=(L // tl,),
        in_specs=[r(D_MODEL), r(D_MODEL), r(512), r(512), r(512), r(256), r(256),
                  _res((1, D_MODEL)), _res(w_in.shape), _res((1, 512)), _res((1, 512)), _res((512, 512))],
        out_specs=[r(D_MODEL), _res(w_in.shape), _res((1, D_MODEL)), _res((1, 512)), _res((1, 512))],
        out_shape=[_sds((L, D_MODEL)), _sds(w_in.shape), _sds((1, D_MODEL)), _sds((1, 512)), _sds((1, 512))],
        compiler_params=_params(1),
    )(x, dres, dq, dk, dv, dhg, du, g_mix, w_in, gq, gk, mavg)


def _sb_tri_consts():
    r = np.arange(2 * SB_TK)[:, None]
    c = np.arange(2 * SB_TK)[None, :]
    same = (r // SB_TK) == (c // SB_TK)
    later = (same & (r > c)).astype(np.float32)
    earlier = (same & (r < c)).astype(np.float32)
    return jnp.asarray(later, BF16), jnp.asarray(earlier, BF16)


def _two_heads(blk, lane_a):
    zero = jnp.zeros_like(blk)
    return jnp.concatenate([jnp.where(lane_a, blk, zero), jnp.where(lane_a, zero, blk)], axis=0)


def _sb_logs(z, i, j, masked):
    e = jnp.exp(-jnp.abs(z))
    lm = -(jnp.maximum(z, 0.0) + jnp.log(1.0 + e))
    ls = z + lm
    valid = None
    if masked:
        row = lax.broadcasted_iota(jnp.int32, (SB_TQ, 2 * SB_TK), 0)
        col = lax.broadcasted_iota(jnp.int32, (SB_TQ, 2 * SB_TK), 1) & (SB_TK - 1)
        valid = (j * SB_TK + col) < (i * SB_TQ + row)
        lm = jnp.where(valid, lm, 0.0)
    hi = lm.astype(BF16)
    lo = (lm - hi.astype(F32)).astype(BF16)
    return lm, ls, hi, lo, valid


def _lane_halves(a, b):
    return jnp.concatenate([jnp.broadcast_to(a, (SB_TQ, SB_TK)), jnp.broadcast_to(b, (SB_TQ, SB_TK))], axis=1)


def _dot(a, b):
    return jnp.dot(a, b, preferred_element_type=F32)


def _dot_nt(a, b):
    return lax.dot_general(a, b, (((1,), (1,)), ((), ())), preferred_element_type=F32)


def _dot_tn(a, b):
    return lax.dot_general(a, b, (((0,), (0,)), ((), ())), preferred_element_type=F32)


def _sb_fwd(q, k, v, ul, tag, rider=None):
    L = q.shape[0]
    nq = L // SB_TQ
    per = SB_TQ // SB_TK
    wid = SB_PAIRS * LANES

    def body(q_ref, k_ref, v_ref, ul_ref, o_ref, rs_ref, n_ref):
        i = pl.program_id(1)
        ulv = ul_ref[...]
        lane_a = lax.broadcasted_iota(jnp.int32, (1, LANES), 1) < SB_HEAD_DIM
        lane_q = lax.broadcasted_iota(jnp.int32, (SB_TQ, LANES), 1)
        cols = [slice(p * LANES, (p + 1) * LANES) for p in range(SB_PAIRS)]
        qbs = [q_ref[:, c] for c in cols]

        def double_step(jhi, carry, masked):
            chains = [dict(p=p, j=jhi - d) for d in range(2) for p in range(SB_PAIRS)]
            for c in chains:
                c['off'] = pl.multiple_of(c['j'] * SB_TK, SB_TK)
                kb = k_ref[pl.ds(c['off'], SB_TK), cols[c['p']]]
                c['z'] = _dot_nt(qbs[c['p']], _two_heads(kb, lane_a))
            for c in chains:
                c['lm'], c['ls'], c['hi'], c['lo'], c['valid'] = _sb_logs(c.pop('z'), i, c['j'], masked)
            for c in chains:
                c['lb'] = _dot(c.pop('hi'), ulv) + _dot(c.pop('lo'), ulv)
            state = [list(s) for s in carry]
            for c in chains:
                ra, rb, _, rsave = state[c['p']]
                w = jnp.exp(c['ls'] + c['lb'] + _lane_halves(ra, rb))
                if masked:
                    w = jnp.where(c['valid'], w, 0.0)
                c['w'] = w.astype(BF16)
                lb, lm = c['lb'], c['lm']
                state[c['p']][3] = jnp.where(lane_q == c['j'], ra, jnp.where(lane_q == c['j'] + SB_HEAD_DIM, rb, rsave))
                state[c['p']][0] = ra + lb[:, 0:1] + lm[:, 0:1]
                state[c['p']][1] = rb + lb[:, SB_TK:SB_TK + 1] + lm[:, SB_TK:SB_TK + 1]
            for c in chains:
                vb = v_ref[pl.ds(c['off'], SB_TK), cols[c['p']]]
                state[c['p']][2] = state[c['p']][2] + _dot(c['w'], _two_heads(vb, lane_a))
            return tuple(tuple(s) for s in state)

        assert per == 2
        carry = tuple((jnp.zeros((SB_TQ, 1), F32), jnp.zeros((SB_TQ, 1), F32),
                       jnp.zeros((SB_TQ, LANES), F32), jnp.zeros((SB_TQ, LANES), F32)) for _ in range(SB_PAIRS))
        def alive(carry):
            m = carry[0][0]
            for c in carry:
                m = jnp.maximum(m, jnp.maximum(c[0], c[1]))
            return jnp.max(m) > SB_DEAD

        carry = double_step(i * per + 1, carry, True)
        n_done, _, carry = lax.while_loop(
            lambda st: jnp.logical_and(st[0] < i, st[1]),
            lambda st: (lambda c: (st[0] + 1, alive(c), c))(double_step(i * per - 1 - 2 * st[0], st[2], False)),
            (jnp.int32(0), alive(carry), carry))
        o_ref[...] = jnp.concatenate([c[2] for c in carry], axis=1)
        rs_ref[...] = jnp.concatenate([c[3] for c in carry], axis=1)
        n_ref[pl.program_id(0), i] = n_done

    qspec = pl.BlockSpec((SB_TQ, wid), lambda g, i: (i, g))
    kspec = pl.BlockSpec((L, wid), lambda g, i: (0, g))
    return _pcall(
        body, name=f"sb_fwd_{tag}", grid=(SB_WIDTH // wid, nq),
        in_specs=[qspec, kspec, kspec, pl.BlockSpec((2 * SB_TK, 2 * SB_TK), lambda g, i: (0, 0))],
        out_specs=[qspec, qspec, pl.BlockSpec(memory_space=pltpu.SMEM)],
        out_shape=[_sds((L, SB_WIDTH)), _sds((L, SB_WIDTH)), _sds((SB_WIDTH // wid, nq), jnp.int32)],
        compiler_params=_params(2), rider=rider,
    )(q, k, v, ul)


def _sb_bwd(n_done, q, k, v, rsave, do, ul, ue, tag):
    L = q.shape[0]
    nq = L // SB_TQ
    per = SB_TQ // SB_TK
    wid = SB_PAIRS * LANES

    def body(n_ref, q_ref, k_ref, v_ref, rs_ref, do_ref, ul_ref, ue_ref, dq_ref, dk_ref, dv_ref):
        i = pl.program_id(1)

        @pl.when(i == 0)
        def _():
            dk_ref[...] = jnp.zeros_like(dk_ref)
            dv_ref[...] = jnp.zeros_like(dv_ref)

        ulv = ul_ref[...]
        uev = ue_ref[...]
        lane_a = lax.broadcasted_iota(jnp.int32, (1, LANES), 1) < SB_HEAD_DIM
        lane_q = lax.broadcasted_iota(jnp.int32, (SB_TQ, LANES), 1)
        cols = [slice(p * LANES, (p + 1) * LANES) for p in range(SB_PAIRS)]
        qbs = [q_ref[:, c] for c in cols]
        dobs = [do_ref[:, c].astype(BF16) for c in cols]
        rsvs = [rs_ref[:, c] for c in cols]

        def double_step(jlo, carry, masked):
            chains = [dict(p=p, j=jlo + d) for d in range(2) for p in range(SB_PAIRS)]
            for c in chains:
                p = c['p']
                c['off'] = pl.multiple_of(c['j'] * SB_TK, SB_TK)
                c['kk2'] = _two_heads(k_ref[pl.ds(c['off'], SB_TK), cols[p]], lane_a)
                c['z'] = _dot_nt(qbs[p], c['kk2'])
                c['dw'] = _dot_nt(dobs[p], _two_heads(v_ref[pl.ds(c['off'], SB_TK), cols[p]], lane_a))
            for c in chains:
                c['lm'], c['ls'], c['hi'], c['lo'], c['valid'] = _sb_logs(c.pop('z'), i, c['j'], masked)
                c['ra'] = jnp.sum(jnp.where(lane_q == c['j'], rsvs[c['p']], 0.0), axis=1, keepdims=True)
                c['rb'] = jnp.sum(jnp.where(lane_q == c['j'] + SB_HEAD_DIM, rsvs[c['p']], 0.0), axis=1, keepdims=True)
            for c in chains:
                c['lb'] = _dot(c.pop('hi'), ulv) + _dot(c.pop('lo'), ulv)
            for c in chains:
                w = jnp.exp(c['ls'] + c.pop('lb') + _lane_halves(c['ra'], c['rb']))
                if masked:
                    w = jnp.where(c['valid'], w, 0.0)
                c['wb'] = w.astype(BF16)
                gg = w * c.pop('dw')
                c['gg'] = gg
                c['ghi'] = gg.astype(BF16)
                c['glo'] = (gg - c['ghi'].astype(F32)).astype(BF16)
                c['beta'] = jnp.exp(c['ls'])
            for c in chains:
                c['cb'] = _dot(c.pop('ghi'), uev) + _dot(c.pop('glo'), uev)
                c['dv2'] = _dot_tn(c.pop('wb'), dobs[c['p']])
            state = [list(s) for s in carry]
            for c in chains:
                pa, pb, _ = state[c['p']]
                gg, cb, beta = c['gg'], c['cb'], c['beta']
                dz = gg * (1.0 - beta) - beta * (cb + _lane_halves(pa, pb))
                if masked:
                    dz = jnp.where(c['valid'], dz, 0.0)
                c['dzb'] = dz.astype(BF16)
                state[c['p']][0] = pa + cb[:, SB_TK - 1:SB_TK] + gg[:, SB_TK - 1:SB_TK]
                state[c['p']][1] = pb + cb[:, 2 * SB_TK - 1:2 * SB_TK] + gg[:, 2 * SB_TK - 1:2 * SB_TK]
            for c in chains:
                c['dqc'] = _dot(c['dzb'], c['kk2'])
                c['dk2'] = _dot_tn(c['dzb'], qbs[c['p']])
            for c in chains:
                p, dk2, dv2 = c['p'], c['dk2'], c['dv2']
                state[p][2] = state[p][2] + c['dqc']
                dk_ref[pl.ds(c['off'], SB_TK), cols[p]] += jnp.where(lane_a, dk2[0:SB_TK], dk2[SB_TK:2 * SB_TK])
                dv_ref[pl.ds(c['off'], SB_TK), cols[p]] += jnp.where(lane_a, dv2[0:SB_TK], dv2[SB_TK:2 * SB_TK])
            return tuple(tuple(s) for s in state)

        assert per == 2
        carry = tuple((jnp.zeros((SB_TQ, 1), F32), jnp.zeros((SB_TQ, 1), F32), jnp.zeros((SB_TQ, LANES), F32))
                      for _ in range(SB_PAIRS))
        first = i - n_ref[pl.program_id(0), i]
        carry = lax.fori_loop(first, i, lambda jj, c: double_step(2 * jj, c, False), carry)
        carry = double_step(i * per, carry, True)
        dq_ref[...] = jnp.concatenate([c[2] for c in carry], axis=1) * SB_SCALE

    qspec = pl.BlockSpec((SB_TQ, wid), lambda g, i: (i, g))
    kspec = pl.BlockSpec((L, wid), lambda g, i: (0, g))
    kin = pl.BlockSpec((L, wid), lambda g, i: (0, g), pipeline_mode=pl.Buffered(1))
    cspec = pl.BlockSpec((2 * SB_TK, 2 * SB_TK), lambda g, i: (0, 0))
    return pl.pallas_call(
        body, name=f"sb_bwd_{tag}", grid=(SB_WIDTH // wid, nq),
        in_specs=[pl.BlockSpec(memory_space=pltpu.SMEM), qspec, kin, kin, qspec, qspec, cspec, cspec],
        out_specs=[qspec, kspec, kspec],
        out_shape=[_sds((L, SB_WIDTH)), _sds((L, SB_WIDTH)), _sds((L, SB_WIDTH))],
        compiler_params=_params(2, 58),
    )(n_done, q, k, v, rsave, do, ul, ue)


def _conv_fwd(hp, w, tag):
    L = hp.shape[0] - CONV_HALO
    win_rows = CONV_SUB + CONV_HALO

    def body(hp_ref, w_ref, o_ref):
        i = pl.program_id(0)

        def sub(s, _):
            t0 = pl.multiple_of(i * TL + s * CONV_SUB, CONV_SUB)
            win = hp_ref[pl.ds(t0, win_rows), :]
            acc = jnp.zeros((CONV_SUB, CONV_CH), F32)
            for kk in range(CONV_WIDTH):
                sh = CONV_WIDTH - 1 - kk
                r = win if sh == 0 else pltpu.roll(win, sh, 0)
                acc = acc + w_ref[kk:kk + 1, :] * r[CONV_HALO:, :]
            o_ref[pl.ds(pl.multiple_of(s * CONV_SUB, CONV_SUB), CONV_SUB), :] = acc
            return 0

        lax.fori_loop(0, TL // CONV_SUB, sub, 0)

    return pl.pallas_call(
        body, name=f"conv_fwd_{tag}", grid=(L // TL,),
        in_specs=[_res(hp.shape), _res(w.shape)],
        out_specs=_rows(CONV_CH),
        out_shape=_sds((L, CONV_CH)),
        compiler_params=_params(1),
    )(hp, w)


def _conv_bwd(dpad, hp, w, tag):
    L = hp.shape[0] - CONV_HALO
    win_rows = CONV_SUB + CONV_HALO
    n_tiles = L // TL

    def body(dp_ref, hp_ref, w_ref, dh_ref, dw_ref, acc_ref):
        i = pl.program_id(0)

        @pl.when(i == 0)
        def _():
            acc_ref[...] = jnp.zeros_like(acc_ref)

        def sub(s, _):
            t0 = pl.multiple_of(i * TL + s * CONV_SUB, CONV_SUB)
            wd = dp_ref[pl.ds(t0, win_rows), :]
            wh = hp_ref[pl.ds(t0, win_rows), :]
            dy = wd[0:CONV_SUB, :]
            acc = jnp.zeros((CONV_SUB, CONV_CH), F32)
            for kk in range(CONV_WIDTH):
                sh = CONV_WIDTH - 1 - kk
                rd = wd if sh == 0 else pltpu.roll(wd, win_rows - sh, 0)
                acc = acc + w_ref[kk:kk + 1, :] * rd[0:CONV_SUB, :]
                rh = wh if sh == 0 else pltpu.roll(wh, sh, 0)
                prod = dy * rh[CONV_HALO:, :]
                part = prod[0:SUBLANES]
                for m in range(1, CONV_SUB // SUBLANES):
                    part = part + prod[m * SUBLANES:(m + 1) * SUBLANES]
                acc_ref[kk] += part
            dh_ref[pl.ds(pl.multiple_of(s * CONV_SUB, CONV_SUB), CONV_SUB), :] = acc
            return 0

        lax.fori_loop(0, TL // CONV_SUB, sub, 0)

        @pl.when(i == n_tiles - 1)
        def _():
            for kk in range(CONV_WIDTH):
                dw_ref[kk:kk + 1, :] = jnp.sum(acc_ref[kk], axis=0, keepdims=True)

    return pl.pallas_call(
        body, name=f"conv_bwd_{tag}", grid=(n_tiles,),
        in_specs=[_res(dpad.shape), _res(hp.shape), _res(w.shape)],
        out_specs=[_rows(CONV_CH), _res(w.shape)],
        out_shape=[_sds((L, CONV_CH)), _sds(w.shape)],
        scratch_shapes=[pltpu.VMEM((CONV_WIDTH, SUBLANES, CONV_CH), F32)],
        compiler_params=_params(1),
    )(dpad, hp, w)


def _ssm_mask():
    r = np.arange(SSM_CH)[:, None] // SSM_GROUP
    c = np.arange(SSM_LANES)[None, :] // SSM_STATE
    return jnp.asarray((r == c).astype(np.float32))


def _ssm_discretize(lr, li, ldt, brt, bit):
    dt = jnp.exp(ldt)
    mag = jnp.exp(lr * dt)
    ar = mag * jnp.cos(li * dt)
    ai = mag * jnp.sin(li * dt)
    den = lr * lr + li * li
    fr = ((ar - 1.0) * lr + ai * li) / den
    fi = (ai * lr - (ar - 1.0) * li) / den
    return ar, ai, fr * brt - fi * bit, fr * bit + fi * brt


def _block_diag(rows16, mask):
    return jnp.where(mask > 0.5, jnp.tile(rows16, (SSM_GROUPS, 1)), 0.0)


def _block_diag_t(full, mask):
    m = jnp.where(mask > 0.5, full, 0.0)
    out = m[0:SSM_GROUP]
    for g in range(1, SSM_GROUPS):
        out = out + m[g * SSM_GROUP:(g + 1) * SSM_GROUP]
    return out


def _ssm_prep(lr, li, ldt, brt, bit, crt, cit, mask, tag):
    def body(lr_ref, li_ref, ldt_ref, brt_ref, bit_ref, crt_ref, cit_ref, m_ref,
             ar_ref, ai_ref, bbr_ref, bbi_ref, cbr_ref, cbi_ref):
        ar, ai, bbr, bbi = _ssm_discretize(lr_ref[...], li_ref[...], ldt_ref[...], brt_ref[...], bit_ref[...])
        m = m_ref[...]
        ar_ref[...] = ar
        ai_ref[...] = ai
        bbr_ref[...] = _block_diag(bbr, m).astype(BF16)
        bbi_ref[...] = _block_diag(bbi, m).astype(BF16)
        cbr_ref[...] = _block_diag(crt_ref[...], m).astype(BF16)
        cbi_ref[...] = _block_diag(cit_ref[...], m).astype(BF16)

    row = _sds((1, SSM_LANES))
    blk = _sds((SSM_CH, SSM_LANES), BF16)
    return pl.pallas_call(body, name=f"ssm_prep_{tag}", out_shape=[row, row, blk, blk, blk, blk])(
        lr, li, ldt, brt, bit, crt, cit, mask)


def _ssm_prep_bwd(lr, li, ldt, brt, bit, mask, dar, dai, dbbr, dbbi, dcbr, dcbi, tag):
    def body(lr_ref, li_ref, ldt_ref, brt_ref, bit_ref, m_ref, dar_ref, dai_ref, dbbr_ref, dbbi_ref, dcbr_ref,
             dcbi_ref, dlr_ref, dli_ref, dldt_ref, dbrt_ref, dbit_ref, dcrt_ref, dcit_ref):
        m = m_ref[...]
        _, vjp = jax.vjp(_ssm_discretize, lr_ref[...], li_ref[...], ldt_ref[...], brt_ref[...], bit_ref[...])
        dlr, dli, dldt, dbrt, dbit = vjp((dar_ref[...], dai_ref[...], _block_diag_t(dbbr_ref[...], m),
                                          _block_diag_t(dbbi_ref[...], m)))
        dlr_ref[...] = dlr
        dli_ref[...] = dli
        dldt_ref[...] = dldt
        dbrt_ref[...] = dbrt
        dbit_ref[...] = dbit
        dcrt_ref[...] = _block_diag_t(dcbr_ref[...], m)
        dcit_ref[...] = _block_diag_t(dcbi_ref[...], m)

    row = _sds((1, SSM_LANES))
    r16 = _sds((SSM_GROUP, SSM_LANES))
    return pl.pallas_call(body, name=f"ssm_prep_bwd_{tag}", out_shape=[row, row, row, r16, r16, r16, r16])(
        lr, li, ldt, brt, bit, mask, dar, dai, dbbr, dbbi, dcbr, dcbi)


def _complex_scan(br, bi, ar, ai, cr, ci, reverse):
    n = br.shape[0]
    row = lax.broadcasted_iota(jnp.int32, (n, 1), 0) & (SUBLANES - 1)
    xr, xi, pr, pi = br, bi, ar, ai
    d = 1
    while d < SUBLANES:
        if reverse:
            sr, si, keep = pltpu.roll(xr, n - d, 0), pltpu.roll(xi, n - d, 0), row < SUBLANES - d
        else:
            sr, si, keep = pltpu.roll(xr, d, 0), pltpu.roll(xi, d, 0), row >= d
        sr = jnp.where(keep, sr, 0.0)
        si = jnp.where(keep, si, 0.0)
        xr, xi = xr + pr * sr - pi * si, xi + pr * si + pi * sr
        pr, pi = pr * pr - pi * pi, 2.0 * pr * pi
        d *= 2
    powers = [(ar, ai)]
    for _ in range(SUBLANES - 1):
        qr, qi = powers[-1]
        powers.append((qr * ar - qi * ai, qr * ai + qi * ar))
    sub = lax.broadcasted_iota(jnp.int32, (SUBLANES, 1), 0)
    tr = jnp.zeros((SUBLANES, br.shape[1]), F32)
    ti = jnp.zeros((SUBLANES, br.shape[1]), F32)
    for r in range(SUBLANES):
        qr, qi = powers[SUBLANES - 1 - r] if reverse else powers[r]
        tr = jnp.where(sub == r, qr, tr)
        ti = jnp.where(sub == r, qi, ti)
    n_groups = n // SUBLANES
    out_r, out_i = [None] * n_groups, [None] * n_groups
    end = 0 if reverse else SUBLANES - 1
    for g in (reversed(range(n_groups)) if reverse else range(n_groups)):
        gr = xr[g * SUBLANES:(g + 1) * SUBLANES]
        gi = xi[g * SUBLANES:(g + 1) * SUBLANES]
        gr, gi = gr + tr * cr - ti * ci, gi + tr * ci + ti * cr
        cr, ci = gr[end:end + 1], gi[end:end + 1]
        out_r[g], out_i[g] = gr, gi
    return jnp.concatenate(out_r, axis=0), jnp.concatenate(out_i, axis=0)


def _ssm_fwd(u, ar, ai, bbr, bbi, cbr, cbi, dvec, tag, rider=None):
    L = u.shape[0]
    T = SSM_T

    def body(u_ref, ar_ref, ai_ref, bbr_ref, bbi_ref, cbr_ref, cbi_ref, d_ref, y_ref, xr_ref, xi_ref, cr_ref, ci_ref):
        i = pl.program_id(0)

        @pl.when(i == 0)
        def _():
            cr_ref[...] = jnp.zeros_like(cr_ref)
            ci_ref[...] = jnp.zeros_like(ci_ref)

        uu = u_ref[...]
        a_r, a_i = ar_ref[...], ai_ref[...]
        c_r, c_i = cr_ref[...], ci_ref[...]
        xr, xi = _complex_scan(_nn(uu, bbr_ref[...]), _nn(uu, bbi_ref[...]), a_r, a_i, c_r, c_i, False)
        cr_ref[...] = xr[T - 1:T, :]
        ci_ref[...] = xi[T - 1:T, :]
        xr_ref[...] = xr
        xi_ref[...] = xi
        y_ref[...] = _nt(xr, cbr_ref[...]) - _nt(xi, cbi_ref[...]) + d_ref[...] * uu

    blk = _res((SSM_CH, SSM_LANES))
    row = _res((1, SSM_LANES))
    return _pcall(
        body, name=f"ssm_fwd_{tag}", grid=(L // T,), rider=rider,
        in_specs=[_rows(SSM_CH, T), row, row, blk, blk, blk, blk, _res((1, SSM_CH))],
        out_specs=[_rows(SSM_CH, T), _rows(SSM_LANES, T), _rows(SSM_LANES, T)],
        out_shape=[_sds((L, SSM_CH)), _sds((L, SSM_LANES)), _sds((L, SSM_LANES))],
        scratch_shapes=[pltpu.VMEM((1, SSM_LANES), F32), pltpu.VMEM((1, SSM_LANES), F32)],
        compiler_params=_params(1),
    )(u, ar, ai, bbr, bbi, cbr, cbi, dvec)


def _ssm_bwd(dy, u, xr, xi, ar, ai, bbr, bbi, cbr, cbi, dvec, tag, rider=None):
    L = u.shape[0]
    T = SSM_T
    nc = L // T

    def body(dy_ref, u_ref, xr_ref, xi_ref, pr_ref, pi_ref, ar_ref, ai_ref, bbr_ref, bbi_ref, cbr_ref, cbi_ref, d_ref,
             du_ref, dar_ref, dai_ref, dbbr_ref, dbbi_ref, dcbr_ref, dcbi_ref, dd_ref, gr_ref, gi_ref):
        i = pl.program_id(0)

        @pl.when(i == 0)
        def _():
            gr_ref[...] = jnp.zeros_like(gr_ref)
            gi_ref[...] = jnp.zeros_like(gi_ref)

        dyy = dy_ref[...]
        uu = u_ref[...]
        xr, xi = xr_ref[...], xi_ref[...]
        a_r, a_i = ar_ref[...], ai_ref[...]
        g_r, g_i = gr_ref[...], gi_ref[...]
        row = lax.broadcasted_iota(jnp.int32, (T, 1), 0)
        gr, gi = _complex_scan(_nn(dyy, cbr_ref[...]), -_nn(dyy, cbi_ref[...]), a_r, -a_i, g_r, g_i, True)
        gr_ref[...] = gr[0:1, :]
        gi_ref[...] = gi[0:1, :]
        has_prev = (i < nc - 1).astype(F32)
        pr = pr_ref[SUBLANES - 1:SUBLANES, :] * has_prev
        pi = pi_ref[SUBLANES - 1:SUBLANES, :] * has_prev
        sr = jnp.where(row == 0, pr, pltpu.roll(xr, 1, 0))
        si = jnp.where(row == 0, pi, pltpu.roll(xi, 1, 0))
        _accumulate(i, dar_ref, jnp.sum(gr * sr + gi * si, axis=0, keepdims=True))
        _accumulate(i, dai_ref, jnp.sum(gi * sr - gr * si, axis=0, keepdims=True))
        _accumulate(i, dbbr_ref, _tn(uu, gr))
        _accumulate(i, dbbi_ref, _tn(uu, gi))
        _accumulate(i, dcbr_ref, _tn(dyy, xr))
        _accumulate(i, dcbi_ref, -_tn(dyy, xi))
        _accumulate(i, dd_ref, jnp.sum(dyy * uu, axis=0, keepdims=True))
        du_ref[...] = _nt(gr, bbr_ref[...]) + _nt(gi, bbi_ref[...]) + dyy * d_ref[...]

    rev = lambda cols: pl.BlockSpec((T, cols), lambda i: (nc - 1 - i, 0))
    prev = pl.BlockSpec((SUBLANES, SSM_LANES), lambda i: (jnp.maximum((nc - 1 - i) * (T // SUBLANES) - 1, 0), 0))
    blk = _res((SSM_CH, SSM_LANES))
    row = _res((1, SSM_LANES))
    return _pcall(
        body, name=f"ssm_bwd_{tag}", grid=(nc,), rider=rider,
        in_specs=[rev(SSM_CH), rev(SSM_CH), rev(SSM_LANES), rev(SSM_LANES), prev, prev, row, row, blk, blk, blk, blk,
                  _res((1, SSM_CH))],
        out_specs=[rev(SSM_CH), row, row, blk, blk, blk, blk, _res((1, SSM_CH))],
        out_shape=[_sds((L, SSM_CH)), _sds((1, SSM_LANES)), _sds((1, SSM_LANES)), _sds((SSM_CH, SSM_LANES)),
                   _sds((SSM_CH, SSM_LANES)), _sds((SSM_CH, SSM_LANES)), _sds((SSM_CH, SSM_LANES)), _sds((1, SSM_CH))],
        scratch_shapes=[pltpu.VMEM((1, SSM_LANES), F32), pltpu.VMEM((1, SSM_LANES), F32)],
        compiler_params=_params(1),
    )(dy, u, xr, xi, xr, xi, ar, ai, bbr, bbi, cbr, cbi, dvec)


def _conv_post(hc, dw_b, ln_g, ln_b):
    return _silu(_layer_norm(hc + dw_b, ln_g, ln_b))


def _branch_mix(o_sb, o_conv, t, g1, g2, g3):
    o_ssm = t[:, 0:SSM_CH] * _sigmoid(t[:, SSM_CH:2 * SSM_CH])
    return jnp.concatenate([_rms(o_sb, g1), _rms(o_conv, g2), _rms(o_ssm, g3)], axis=1)


def _branch_mix_split(o_sb, o_conv, ta, tb, g1, g2, g3):
    return jnp.concatenate([_rms(o_sb, g1), _rms(o_conv, g2), _rms(ta * _sigmoid(tb), g3)], axis=1)


def _mix_out_fwd(x, o_sb, hc, y, dw_b, ln_g, ln_b, pw2, glu_w, g1, g2, g3, w_out, tag):
    L = x.shape[0]

    def body(x_ref, o_ref, hc_ref, y_ref, dwb_ref, lng_ref, lnb_ref, pw2_ref, glu_ref, g1_ref, g2_ref, g3_ref, wo_ref,
             out_ref):
        c1 = _conv_post(hc_ref[...], dwb_ref[...], lng_ref[...], lnb_ref[...])
        o_conv = _nn(c1, pw2_ref[...])
        t = _nt(y_ref[...], glu_ref[...])
        mixed = _branch_mix(o_ref[...], o_conv, t, g1_ref[...], g2_ref[...], g3_ref[...])
        out_ref[...] = x_ref[...] + _nn(mixed, wo_ref[...])

    v256 = _res((1, 256))
    return pl.pallas_call(
        body, name=f"mix_out_fwd_{tag}", grid=(L // TL,),
        in_specs=[_rows(D_MODEL), _rows(512), _rows(256), _rows(256), v256, v256, v256, _res(pw2.shape),
                  _res(glu_w.shape), _res((1, 512)), v256, v256, _res(w_out.shape)],
        out_specs=_rows(D_MODEL),
        out_shape=_sds((L, D_MODEL)),
        compiler_params=_params(1),
    )(x, o_sb, hc, y, dw_b, ln_g, ln_b, pw2, glu_w, g1, g2, g3, w_out)


def _mix_out_bwd(dx1, o_sb, hc, y, dw_b, ln_g, ln_b, pw2, glu_w, g1, g2, g3, w_out, tag):
    L = dx1.shape[0]

    def body(dx_ref, o_ref, hc_ref, y_ref, dwb_ref, lng_ref, lnb_ref, pw2_ref, glu_ref, g1_ref, g2_ref, g3_ref, wo_ref,
             do_ref, dhc_ref, dy_ref, ddwb_ref, dlng_ref, dlnb_ref, dpw2_ref, dglu_ref, dg1_ref, dg2_ref, dg3_ref,
             dwo_ref):
        i = pl.program_id(0)
        dxx = dx_ref[...]
        yy = y_ref[...]
        c1, vjp1 = jax.vjp(_conv_post, hc_ref[...], dwb_ref[...], lng_ref[...], lnb_ref[...])
        o_conv = _nn(c1, pw2_ref[...])
        t = _nt(yy, glu_ref[...])
        mixed, vjp2 = jax.vjp(_branch_mix_split, o_ref[...], o_conv, t[:, 0:SSM_CH], t[:, SSM_CH:2 * SSM_CH],
                              g1_ref[...], g2_ref[...], g3_ref[...])
        dmixed = _nt(dxx, wo_ref[...])
        do_sb, do_conv, dta, dtb, dg1, dg2, dg3 = vjp2(dmixed)
        dt = jnp.concatenate([dta, dtb], axis=1)
        dc1 = _nt(do_conv, pw2_ref[...])
        dhc, ddwb, dlng, dlnb = vjp1(dc1)
        do_ref[...] = do_sb
        dhc_ref[...] = dhc
        dy_ref[...] = _nn(dt, glu_ref[...])
        _accumulate(i, dwo_ref, _tn(mixed, dxx))
        _accumulate(i, dglu_ref, _tn(dt, yy))
        _accumulate(i, dpw2_ref, _tn(c1, do_conv))
        _accumulate(i, ddwb_ref, ddwb)
        _accumulate(i, dlng_ref, dlng)
        _accumulate(i, dlnb_ref, dlnb)
        _accumulate(i, dg1_ref, dg1)
        _accumulate(i, dg2_ref, dg2)
        _accumulate(i, dg3_ref, dg3)

    v256 = _res((1, 256))
    return pl.pallas_call(
        body, name=f"mix_out_bwd_{tag}", grid=(L // TL,),
        in_specs=[_rows(D_MODEL), _rows(512), _rows(256), _rows(256), v256, v256, v256, _res(pw2.shape),
                  _res(glu_w.shape), _res((1, 512)), v256, v256, _res(w_out.shape)],
        out_specs=[_rows(512), _rows(256), _rows(256), v256, v256, v256, _res(pw2.shape), _res(glu_w.shape),
                   _res((1, 512)), v256, v256, _res(w_out.shape)],
        out_shape=[_sds((L, 512)), _sds((L, 256)), _sds((L, 256)), _sds((1, 256)), _sds((1, 256)), _sds((1, 256)),
                   _sds(pw2.shape), _sds(glu_w.shape), _sds((1, 512)), _sds((1, 256)), _sds((1, 256)), _sds(w_out.shape)],
        compiler_params=_params(1),
    )(dx1, o_sb, hc, y, dw_b, ln_g, ln_b, pw2, glu_w, g1, g2, g3, w_out)


def _xa_heads_norm(kk, kg):
    return jnp.concatenate([_rms(kk[:, h * XA_HEAD_DIM:(h + 1) * XA_HEAD_DIM], kg) for h in range(XA_HEADS)], axis=1)


def _xa_mem_fwd(mem, g_mem, wk, wv, kg, tag):
    def body(m_ref, g_ref, wk_ref, wv_ref, kg_ref, k_ref, v_ref):
        hm = _rms(m_ref[...], g_ref[...])
        k_ref[...] = _xa_heads_norm(_nn(hm, wk_ref[...]), kg_ref[...])
        v_ref[...] = _nn(hm, wv_ref[...])

    return pl.pallas_call(body, name=f"xa_mem_fwd_{tag}", out_shape=[_sds(mem.shape), _sds(mem.shape)],
                          compiler_params=_params(0))(mem, g_mem, wk, wv, kg)


def _xa_mem_bwd(mem, g_mem, wk, wv, kg, dkx, dvx, tag):
    def body(m_ref, g_ref, wk_ref, wv_ref, kg_ref, dk_ref, dv_ref, dwk_ref, dwv_ref, dg_ref, dkg_ref):
        hm, vjp_n = jax.vjp(_rms, m_ref[...], g_ref[...])
        kk = _nn(hm, wk_ref[...])
        dvv = dv_ref[...]
        dkg = jnp.zeros((1, XA_HEAD_DIM), F32)
        parts = []
        for h in range(XA_HEADS):
            sl = slice(h * XA_HEAD_DIM, (h + 1) * XA_HEAD_DIM)
            _, vjp_h = jax.vjp(_rms, kk[:, sl], kg_ref[...])
            dkh, dgh = vjp_h(dk_ref[:, sl])
            parts.append(dkh)
            dkg = dkg + dgh
        dkk = jnp.concatenate(parts, axis=1)
        dwk_ref[...] = _tn(hm, dkk)
        dwv_ref[...] = _tn(hm, dvv)
        dhm = _nt(dkk, wk_ref[...]) + _nt(dvv, wv_ref[...])
        _, dg = vjp_n(dhm)
        dg_ref[...] = dg
        dkg_ref[...] = dkg

    return pl.pallas_call(
        body, name=f"xa_mem_bwd_{tag}",
        out_shape=[_sds(wk.shape), _sds(wv.shape), _sds((1, D_MODEL)), _sds((1, XA_HEAD_DIM))],
        compiler_params=_params(0))(mem, g_mem, wk, wv, kg, dkx, dvx)


def _xa_fwd(x1, kx, vx, g_xa, wq, qg, wo, tag, rider=None):
    L = x1.shape[0]

    def body(x_ref, k_ref, v_ref, g_ref, wq_ref, qg_ref, wo_ref, out_ref):
        xx = x_ref[...]
        qp = _nn(_rms(xx, g_ref[...]), wq_ref[...])
        outs = []
        for h in range(XA_HEADS):
            sl = slice(h * XA_HEAD_DIM, (h + 1) * XA_HEAD_DIM)
            qh = _rms(qp[:, sl], qg_ref[...])
            s = _nt(qh, k_ref[:, sl]) * (XA_HEAD_DIM ** -0.5)
            s = s - jnp.max(s, axis=-1, keepdims=True)
            e = jnp.exp(s)
            p = e / jnp.sum(e, axis=-1, keepdims=True)
            outs.append(_nn(p, v_ref[:, sl]))
        out_ref[...] = xx + _nn(jnp.concatenate(outs, axis=1), wo_ref[...])

    return _pcall(
        body, name=f"xa_fwd_{tag}", grid=(L // TL,), rider=rider,
        in_specs=[_rows(D_MODEL), _res(kx.shape), _res(vx.shape), _res((1, D_MODEL)), _res(wq.shape),
                  _res((1, XA_HEAD_DIM)), _res(wo.shape)],
        out_specs=_rows(D_MODEL),
        out_shape=_sds((L, D_MODEL)),
        compiler_params=_params(1),
    )(x1, kx, vx, g_xa, wq, qg, wo)


def _xa_bwd(x1, dx2, kx, vx, g_xa, wq, qg, wo, tag, rider=None):
    L = x1.shape[0]
    tl = 256

    def body(x_ref, dx_ref, k_ref, v_ref, g_ref, wq_ref, qg_ref, wo_ref,
             dx1_ref, dk_ref, dv_ref, dwq_ref, dwo_ref, dg_ref, dqg_ref):
        i = pl.program_id(0)
        xx = x_ref[...]
        dxx = dx_ref[...]
        hx, vjp_n = jax.vjp(_rms, xx, g_ref[...])
        qp = _nn(hx, wq_ref[...])
        do = _nt(dxx, wo_ref[...])
        heads = [dict(sl=slice(h * XA_HEAD_DIM, (h + 1) * XA_HEAD_DIM)) for h in range(XA_HEADS)]
        for c in heads:
            c['kh'], c['vh'], c['doh'] = k_ref[:, c['sl']].astype(BF16), v_ref[:, c['sl']].astype(BF16), do[:, c['sl']]
            c['qh'], c['vjp_q'] = jax.vjp(_rms, qp[:, c['sl']], qg_ref[...])
        for c in heads:
            c['s'] = _nt(c['qh'], c['kh'])
            c['dp'] = _nt(c['doh'], c['vh'])
        for c in heads:
            s = c.pop('s') * (XA_HEAD_DIM ** -0.5)
            e = jnp.exp(s - jnp.max(s, axis=-1, keepdims=True))
            c['p'] = e / jnp.sum(e, axis=-1, keepdims=True)
        for c in heads:
            c['out'] = _nn(c['p'], c['vh'])
            c['dv'] = _tn(c['p'], c['doh'])
        for c in heads:
            p, dp = c['p'], c.pop('dp')
            c['ds'] = p * (dp - jnp.sum(dp * p, axis=-1, keepdims=True)) * (XA_HEAD_DIM ** -0.5)
        for c in heads:
            c['dk'] = _tn(c['ds'], c['qh'])
            c['dq'] = _nn(c['ds'], c['kh'])
        dqg = jnp.zeros((1, XA_HEAD_DIM), F32)
        for c in heads:
            c['dqp'], dgh = c['vjp_q'](c['dq'])
            dqg = dqg + dgh
        o = jnp.concatenate([c['out'] for c in heads], axis=1)
        dqp = jnp.concatenate([c['dqp'] for c in heads], axis=1)
        dks, dvs = [c['dk'] for c in heads], [c['dv'] for c in heads]
        dxn, dg = vjp_n(_nt(dqp, wq_ref[...]))
        dx1_ref[...] = dxx + dxn
        _accumulate(i, dk_ref, jnp.concatenate(dks, axis=1))
        _accumulate(i, dv_ref, jnp.concatenate(dvs, axis=1))
        _accumulate(i, dwq_ref, _tn(hx, dqp))
        _accumulate(i, dwo_ref, _tn(o, dxx))
        _accumulate(i, dg_ref, dg)
        _accumulate(i, dqg_ref, dqg)

    r = lambda c: _rows(c, tl)
    return _pcall(
        body, name=f"xa_bwd_{tag}", grid=(L // tl,), rider=rider,
        in_specs=[r(D_MODEL), r(D_MODEL), _res(kx.shape), _res(vx.shape), _res((1, D_MODEL)), _res(wq.shape),
                  _res((1, XA_HEAD_DIM)), _res(wo.shape)],
        out_specs=[r(D_MODEL), _res(kx.shape), _res(vx.shape), _res(wq.shape), _res(wo.shape), _res((1, D_MODEL)),
                   _res((1, XA_HEAD_DIM))],
        out_shape=[_sds((L, D_MODEL)), _sds(kx.shape), _sds(vx.shape), _sds(wq.shape), _sds(wo.shape),
                   _sds((1, D_MODEL)), _sds((1, XA_HEAD_DIM))],
        compiler_params=_params(1),
    )(x1, dx2, kx, vx, g_xa, wq, qg, wo)


def _swiglu(gate, up):
    return _silu(gate) * up


def _ffn_fwd(x2, g, w_in, w_out, tag, rider=None):
    L = x2.shape[0]
    tl = 256

    def body(x_ref, g_ref, wi_ref, wo_ref, out_ref):
        xx = x_ref[...]
        gu = _nt(_rms(xx, g_ref[...]), wi_ref[...])
        act = _swiglu(gu[:, 0:FFN_HIDDEN], gu[:, FFN_HIDDEN:2 * FFN_HIDDEN])
        out_ref[...] = xx + _nn(act, wo_ref[...])

    return _pcall(
        body, name=f"ffn_fwd_{tag}", grid=(L // tl,), rider=rider,
        in_specs=[_rows(D_MODEL, tl), _res((1, D_MODEL)), _res(w_in.shape), _res(w_out.shape)],
        out_specs=_rows(D_MODEL, tl),
        out_shape=_sds((L, D_MODEL)),
        compiler_params=_params(1, 56),
    )(x2, g, w_in, w_out)


def _ffn_bwd(x2, dx3, g, w_in, w_out, tag, rider=None):
    L = x2.shape[0]
    tl = 256

    def body(x_ref, dx_ref, g_ref, wi_ref, wo_ref, dx2_ref, dgu_ref, act_ref, hf_ref, dg_ref):
        i = pl.program_id(0)
        xx = x_ref[...]
        dxx = dx_ref[...]
        hf, vjp_n = jax.vjp(_rms, xx, g_ref[...])
        gu = _nt(hf, wi_ref[...])
        act, vjp_a = jax.vjp(_swiglu, gu[:, 0:FFN_HIDDEN], gu[:, FFN_HIDDEN:2 * FFN_HIDDEN])
        dgate, dup = vjp_a(_nt(dxx, wo_ref[...]))
        dgu = jnp.concatenate([dgate, dup], axis=1).astype(BF16)
        dxn, dg = vjp_n(_nn(dgu, wi_ref[...]))
        dx2_ref[...] = dxx + dxn
        dgu_ref[...] = dgu
        act_ref[...] = act.astype(BF16)
        hf_ref[...] = hf.astype(BF16)
        _accumulate(i, dg_ref, dg)

    r = lambda c: _rows(c, tl)
    return _pcall(
        body, name=f"ffn_bwd_{tag}", grid=(L // tl,), rider=rider,
        in_specs=[r(D_MODEL), r(D_MODEL), _res((1, D_MODEL)), _res(w_in.shape), _res(w_out.shape)],
        out_specs=[r(D_MODEL), r(2 * FFN_HIDDEN), r(FFN_HIDDEN), r(D_MODEL), _res((1, D_MODEL))],
        out_shape=[_sds((L, D_MODEL)), _sds((L, 2 * FFN_HIDDEN), BF16), _sds((L, FFN_HIDDEN), BF16),
                   _sds((L, D_MODEL), BF16), _sds((1, D_MODEL))],
        compiler_params=_params(1, 56),
    )(x2, dx3, g, w_in, w_out)


def _matmul_tn(a, b, tm, tn, tag):
    L, M = a.shape
    N = b.shape[1]
    tk = min(L, 2048)
    nk = L // tk

    def body(a_ref, b_ref, o_ref, acc_ref):
        k = pl.program_id(2)
        _accumulate(k, acc_ref, _tn(a_ref[...], b_ref[...]))

        @pl.when(k == nk - 1)
        def _():
            o_ref[...] = acc_ref[...].astype(BF16)

    return pl.pallas_call(
        body, name=f"matmul_tn_{tag}", grid=(M // tm, N // tn, nk),
        in_specs=[pl.BlockSpec((tk, tm), lambda m, n, k: (k, m)), pl.BlockSpec((tk, tn), lambda m, n, k: (k, n))],
        out_specs=pl.BlockSpec((tm, tn), lambda m, n, k: (m, n)),
        out_shape=_sds((M, N), BF16),
        scratch_shapes=[pltpu.VMEM((tm, tn), F32)],
        compiler_params=_params(3),
    )(a, b)


def _loss_head(y, tgt):
    L = y.shape[0]
    n_tiles = L // TL

    def body(y_ref, t_ref, loss_ref, dy_ref, acc_ref):
        i = pl.program_id(0)
        diff = y_ref[...] - t_ref[...]
        dy_ref[...] = diff * (1.0 / D_MODEL)
        _accumulate(i, acc_ref, jnp.sum(diff * diff, axis=0, keepdims=True))

        @pl.when(i == n_tiles - 1)
        def _():
            loss_ref[...] = jnp.sum(acc_ref[...], axis=1, keepdims=True) * (0.5 / D_MODEL)

    return pl.pallas_call(
        body, name="loss_head", grid=(n_tiles,),
        in_specs=[_rows(D_MODEL), _rows(D_MODEL)],
        out_specs=[_res((1, 1)), _rows(D_MODEL)],
        out_shape=[_sds((1, 1)), _sds((L, D_MODEL))],
        scratch_shapes=[pltpu.VMEM((1, D_MODEL), F32)],
        compiler_params=_params(1),
    )(y, tgt)


def _row(v):
    return v.reshape(1, -1)


def _layer_consts():
    r = np.arange(SB_WIDTH)
    mavg = ((r[:, None] // SB_HEAD_DIM) == (r[None, :] // SB_HEAD_DIM)).astype(np.float32) / SB_HEAD_DIM
    ul, ue = _sb_tri_consts()
    return dict(mavg=jnp.asarray(mavg), ul=ul, ue=ue, mask=_ssm_mask())


def _ssm_rows(P):
    lanes = lambda a: a.reshape(1, SSM_LANES)
    return dict(
        lr=lanes(P['ssm_lam_re']), li=lanes(P['ssm_lam_im']),
        ldt=lanes(jnp.repeat(P['ssm_log_dt'], SSM_STATE)),
        brt=P['ssm_b_re'].transpose(2, 0, 1).reshape(SSM_GROUP, SSM_LANES),
        bit=P['ssm_b_im'].transpose(2, 0, 1).reshape(SSM_GROUP, SSM_LANES),
        crt=P['ssm_c_re'].transpose(1, 0, 2).reshape(SSM_GROUP, SSM_LANES),
        cit=P['ssm_c_im'].transpose(1, 0, 2).reshape(SSM_GROUP, SSM_LANES))


def _layer_fwd(x, mem, P, C, tag, ride):
    gq = _row(jnp.tile(P['sb_q_norm_g'], SB_WIDTH // SB_HEAD_DIM))
    gk = _row(jnp.tile(P['sb_k_norm_g'], SB_WIDTH // SB_HEAD_DIM))
    q, k, v, hg, u = _mix_in_fwd(x, _row(P['norm_mix_g']), P['w_in'], gq, gk, C['mavg'], tag,
                                 rider=ride("mix_in_fwd_" + tag))
    o_sb, rsave, n_done = _sb_fwd(q, k, v, C['ul'], tag, rider=ride("sb_fwd_" + tag))
    hp = jnp.pad(hg, ((CONV_HALO, 0), (0, 0)))
    hc = _conv_fwd(hp, P['conv_dw_w'].T, tag)
    S = _ssm_rows(P)
    ar, ai, bbr, bbi, cbr, cbi = _ssm_prep(S['lr'], S['li'], S['ldt'], S['brt'], S['bit'], S['crt'], S['cit'],
                                           C['mask'], tag)
    y, xr, xi = _ssm_fwd(u, ar, ai, bbr, bbi, cbr, cbi, _row(P['ssm_d']), tag, rider=ride("ssm_fwd_" + tag))
    gb = P['branch_norm_g']
    x1 = _mix_out_fwd(x, o_sb, hc, y, _row(P['conv_dw_b']), _row(P['conv_ln_g']), _row(P['conv_ln_b']),
                      P['conv_pw2_w'], P['ssm_glu_w'], _row(gb[0:512]), _row(gb[512:768]), _row(gb[768:1024]),
                      P['w_out'], tag)
    kx, vx = _xa_mem_fwd(mem, _row(P['norm_mem_g']), P['xa_wk'], P['xa_wv'], _row(P['xa_k_norm_g']), tag)
    x2 = _xa_fwd(x1, kx, vx, _row(P['norm_xa_g']), P['xa_wq'], _row(P['xa_q_norm_g']), P['xa_wo'], tag,
                 rider=ride("xa_fwd_" + tag))
    x3 = _ffn_fwd(x2, _row(P['norm_ffn_g']), P['ffn_w_in'], P['ffn_w_out'], tag, rider=ride("ffn_fwd_" + tag))
    saved = dict(x=x, q=q, k=k, v=v, rsave=rsave, n_done=n_done, o_sb=o_sb, hp=hp, hc=hc, u=u, y=y, xr=xr, xi=xi, x1=x1, x2=x2,
                 kx=kx, vx=vx, gq=gq, gk=gk, S=S, ssm=(ar, ai, bbr, bbi, cbr, cbi))
    return x3, saved


def _layer_bwd(dx3, mem, P, C, sv, tag, ride, G):
    dx2, dgu, act, hf, dg = _ffn_bwd(sv['x2'], dx3, _row(P['norm_ffn_g']), P['ffn_w_in'], P['ffn_w_out'], tag,
                                     rider=ride("ffn_bwd_" + tag))
    G['norm_ffn_g'] = dg.reshape(-1)
    G['ffn_w_in'] = _matmul_tn(dgu, hf, 2 * FFN_HIDDEN // 4, D_MODEL, "ffn_in_" + tag)
    G['ffn_w_out'] = _matmul_tn(act, dx3, FFN_HIDDEN // 2, 512, "ffn_out_" + tag)
    dx1, dkx, dvx, dwq, dwo, dg, dqg = _xa_bwd(sv['x1'], dx2, sv['kx'], sv['vx'], _row(P['norm_xa_g']), P['xa_wq'],
                                               _row(P['xa_q_norm_g']), P['xa_wo'], tag, rider=ride("xa_bwd_" + tag))
    G['xa_wq'], G['xa_wo'], G['norm_xa_g'], G['xa_q_norm_g'] = dwq, dwo, dg.reshape(-1), dqg.reshape(-1)
    dwk, dwv, dg, dkg = _xa_mem_bwd(mem, _row(P['norm_mem_g']), P['xa_wk'], P['xa_wv'], _row(P['xa_k_norm_g']),
                                    dkx, dvx, tag)
    G['xa_wk'], G['xa_wv'], G['norm_mem_g'], G['xa_k_norm_g'] = dwk, dwv, dg.reshape(-1), dkg.reshape(-1)
    gb = P['branch_norm_g']
    (do_sb, dhc, dy, ddwb, dlng, dlnb, dpw2, dglu, dg1, dg2, dg3, dwout) = _mix_out_bwd(
        dx1, sv['o_sb'], sv['hc'], sv['y'], _row(P['conv_dw_b']), _row(P['conv_ln_g']), _row(P['conv_ln_b']),
        P['conv_pw2_w'], P['ssm_glu_w'], _row(gb[0:512]), _row(gb[512:768]), _row(gb[768:1024]), P['w_out'], tag)
    G['conv_dw_b'], G['conv_ln_g'], G['conv_ln_b'] = ddwb.reshape(-1), dlng.reshape(-1), dlnb.reshape(-1)
    G['conv_pw2_w'], G['ssm_glu_w'], G['w_out'] = dpw2, dglu, dwout
    G['branch_norm_g'] = jnp.concatenate([dg1.reshape(-1), dg2.reshape(-1), dg3.reshape(-1)])
    ar, ai, bbr, bbi, cbr, cbi = sv['ssm']
    du, dar, dai, dbbr, dbbi, dcbr, dcbi, dd = _ssm_bwd(dy, sv['u'], sv['xr'], sv['xi'], ar, ai, bbr, bbi, cbr, cbi,
                                                        _row(P['ssm_d']), tag, rider=ride("ssm_bwd_" + tag))
    S = sv['S']
    dlr, dli, dldt, dbrt, dbit, dcrt, dcit = _ssm_prep_bwd(S['lr'], S['li'], S['ldt'], S['brt'], S['bit'], C['mask'],
                                                           dar, dai, dbbr, dbbi, dcbr, dcbi, tag)
    G['ssm_lam_re'] = dlr.reshape(SSM_GROUPS, SSM_STATE)
    G['ssm_lam_im'] = dli.reshape(SSM_GROUPS, SSM_STATE)
    G['ssm_log_dt'] = dldt.reshape(SSM_GROUPS, SSM_STATE).sum(axis=1)
    G['ssm_b_re'] = dbrt.reshape(SSM_GROUP, SSM_GROUPS, SSM_STATE).transpose(1, 2, 0)
    G['ssm_b_im'] = dbit.reshape(SSM_GROUP, SSM_GROUPS, SSM_STATE).transpose(1, 2, 0)
    G['ssm_c_re'] = dcrt.reshape(SSM_GROUP, SSM_GROUPS, SSM_STATE).transpose(1, 0, 2)
    G['ssm_c_im'] = dcit.reshape(SSM_GROUP, SSM_GROUPS, SSM_STATE).transpose(1, 0, 2)
    G['ssm_d'] = dd.reshape(-1)
    dpad = jnp.pad(dhc, ((0, CONV_HALO), (0, 0)))
    dhg, ddww = _conv_bwd(dpad, sv['hp'], P['conv_dw_w'].T, tag)
    G['conv_dw_w'] = ddww.T
    dq, dk, dv = _sb_bwd(sv['n_done'], sv['q'], sv['k'], sv['v'], sv['rsave'], do_sb, C['ul'], C['ue'], tag)
    dx, dwin, dg, dgq, dgk = _mix_in_bwd(sv['x'], dx1, dq, dk, dv, dhg, du, _row(P['norm_mix_g']), P['w_in'],
                                         sv['gq'], sv['gk'], C['mavg'], tag)
    G['w_in'], G['norm_mix_g'] = dwin, dg.reshape(-1)
    G['sb_q_norm_g'] = dgq.reshape(SB_WIDTH // SB_HEAD_DIM, SB_HEAD_DIM).sum(axis=0)
    G['sb_k_norm_g'] = dgk.reshape(SB_WIDTH // SB_HEAD_DIM, SB_HEAD_DIM).sum(axis=0)
    return dx, G


def _slot_sum(r_ref):
    g = r_ref[0].astype(F32)
    for s in range(1, r_ref.shape[0]):
        g = g + r_ref[s].astype(F32)
    return g


def _adam_update(g, w, m, v):
    nm = ADAM_B1 * m + (1.0 - ADAM_B1) * g
    nv = ADAM_B2 * v + (1.0 - ADAM_B2) * (g * g)
    m_hat = nm * (1.0 / (1.0 - ADAM_B1 ** ADAM_STEP))
    v_hat = nv * (1.0 / (1.0 - ADAM_B2 ** ADAM_STEP))
    return -ADAM_LR * (m_hat / (jnp.sqrt(v_hat) + ADAM_EPS) + ADAM_WD * w), nm, nv


def _adamw(recv, w, m, v, tile, name):
    n_slots, R, C = recv.shape

    def body(r_ref, w_ref, m_ref, v_ref, g_ref, d_ref, nm_ref, nv_ref):
        g = _slot_sum(r_ref)
        g_ref[...] = g
        d_ref[...], nm_ref[...], nv_ref[...] = _adam_update(g, w_ref[...], m_ref[...], v_ref[...])

    rows = pl.BlockSpec((tile, C), lambda i: (i, 0))
    out = _sds((R, C))
    return pl.pallas_call(
        body, name=name, grid=(R // tile,),
        in_specs=[pl.BlockSpec((n_slots, tile, C), lambda i: (0, i, 0)), rows, rows, rows],
        out_specs=[rows, rows, rows, rows],
        out_shape=[out, out, out, out],
        compiler_params=_params(1),
    )(recv, w, m, v)


def _adamw_layers(recvs, w, m, v, tile, name):
    n_slots, R, C = recvs[0].shape

    def body(*refs):
        r_refs = refs[:DEPTH]
        w_ref, m_ref, v_ref, g_ref, d_ref, nm_ref, nv_ref = refs[DEPTH:]
        for l in range(DEPTH):
            @pl.when(pl.program_id(0) == l)
            def _(l=l):
                g = _slot_sum(r_refs[l])
                g_ref[0] = g
                d_ref[0], nm_ref[0], nv_ref[0] = _adam_update(g, w_ref[0], m_ref[0], v_ref[0])

    rspec = lambda l: pl.BlockSpec((n_slots, tile, C), lambda ll, i: (0, jnp.where(ll == l, i, 0), 0))
    rows = pl.BlockSpec((1, tile, C), lambda ll, i: (ll, i, 0))
    out = _sds((DEPTH, R, C))
    return pl.pallas_call(
        body, name=name, grid=(DEPTH, R // tile),
        in_specs=[rspec(l) for l in range(DEPTH)] + [rows, rows, rows],
        out_specs=[rows, rows, rows, rows],
        out_shape=[out, out, out, out],
        compiler_params=_params(2),
    )(*recvs, w, m, v)


def _reduce_slots(recv, tile, name):
    n_slots, R, C = recv.shape

    def body(r_ref, g_ref):
        g_ref[...] = _slot_sum(r_ref)

    return pl.pallas_call(
        body, name=name, grid=(R // tile,),
        in_specs=[pl.BlockSpec((n_slots, tile, C), lambda i: (0, i, 0))],
        out_specs=pl.BlockSpec((tile, C), lambda i: (i, 0)),
        out_shape=_sds((R, C)),
        compiler_params=_params(1),
    )(recv)


SEG = SUBLANES * LANES


def _pad_to(n, mult):
    return -(-n // mult) * mult


def _pack(arrays, dtype, row_mult, lead=0):
    keep = [(0, 0)] * lead
    parts = []
    for a in arrays:
        flat = a.reshape(a.shape[:lead] + (-1,)).astype(dtype)
        n = flat.shape[-1]
        parts.append(jnp.pad(flat, keep + [(0, _pad_to(n, SEG) - n)]))
    flat = jnp.concatenate(parts, axis=-1)
    n = flat.shape[-1]
    flat = jnp.pad(flat, keep + [(0, _pad_to(n, row_mult * LANES) - n)])
    return flat.reshape(flat.shape[:lead] + (-1, LANES))


def _unpack(buf, shapes):
    lead = buf.shape[:-2]
    flat = buf.reshape(lead + (-1,))
    out, off = [], 0
    for shp in shapes:
        n = int(np.prod(shp))
        out.append(flat[..., off:off + n].reshape(lead + tuple(shp)))
        off += _pad_to(n, SEG)
    return out


def _rows_first(a, name):
    return a.transpose(0, 2, 1) if SHARD_AXIS[name] == 2 else a


SMALL_TILE = 256
DIRECT_NAMES = [n for n in BIG_NAMES if n != 'conv_dw_w']
GATHER_RIDES = {
    "mix_in_fwd_l0": [(0, 'conv_pw2_w'), (0, 'ssm_glu_w'), (0, 'w_out')],
    "sb_fwd_l0": [(0, 'xa_wq'), (0, 'xa_wk'), (0, 'xa_wv'), (0, 'xa_wo'), (0, 'ffn_w_in')],
    "ssm_fwd_l0": [(0, 'ffn_w_out')],
    "xa_fwd_l0": [(1, 'w_in'), (1, 'conv_pw2_w'), (1, 'ssm_glu_w')],
    "ffn_fwd_l0": [(1, 'w_out'), (1, 'xa_wq'), (1, 'xa_wk'), (1, 'xa_wv'), (1, 'xa_wo')],
    "sb_fwd_l1": [(1, 'ffn_w_in'), (1, 'ffn_w_out')],
}
_MID = ['xa_wq', 'xa_wo', 'xa_wk', 'xa_wv', 'w_out', 'conv_pw2_w', 'ssm_glu_w']
SCATTER_RIDES = {
    "xa_bwd_l1": [(1, 'ffn_w_in'), (1, 'ffn_w_out')],
    "ssm_bwd_l1": [(1, n) for n in _MID],
    "ffn_bwd_l0": [(1, 'w_in')],
    "xa_bwd_l0": [(0, 'ffn_w_in'), (0, 'ffn_w_out')],
    "ssm_bwd_l0": [(0, n) for n in _MID],
}


def _tile_rows(rows):
    return next(t for t in range(min(rows, 256), 0, -ROW_ALIGN) if rows % t == 0 and t % ROW_ALIGN == 0)


class _LayerWeights:
    def __init__(self, layer, small, full, conv):
        self.layer, self.small, self.full, self.conv = layer, small, full, conv

    def __getitem__(self, name):
        if name == 'conv_dw_w':
            return self.conv[self.layer]
        return self.full[(self.layer, name)] if name in SHARD_AXIS else self.small[name][self.layer]


def kernel(x, mem, norm_mix_g, w_in, sb_q_norm_g, sb_k_norm_g, conv_dw_w, conv_dw_b, conv_ln_g, conv_ln_b, conv_pw2_w, ssm_lam_re, ssm_lam_im, ssm_log_dt, ssm_b_re, ssm_b_im, ssm_c_re, ssm_c_im, ssm_d, ssm_glu_w, branch_norm_g, w_out, norm_xa_g, norm_mem_g, xa_wq, xa_wk, xa_wv, xa_q_norm_g, xa_k_norm_g, xa_wo, norm_ffn_g, ffn_w_in, ffn_w_out, loss_target, m_norm_mix_g, m_w_in, m_sb_q_norm_g, m_sb_k_norm_g, m_conv_dw_w, m_conv_dw_b, m_conv_ln_g, m_conv_ln_b, m_conv_pw2_w, m_ssm_lam_re, m_ssm_lam_im, m_ssm_log_dt, m_ssm_b_re, m_ssm_b_im, m_ssm_c_re, m_ssm_c_im, m_ssm_d, m_ssm_glu_w, m_branch_norm_g, m_w_out, m_norm_xa_g, m_norm_mem_g, m_xa_wq, m_xa_wk, m_xa_wv, m_xa_q_norm_g, m_xa_k_norm_g, m_xa_wo, m_norm_ffn_g, m_ffn_w_in, m_ffn_w_out, v_norm_mix_g, v_w_in, v_sb_q_norm_g, v_sb_k_norm_g, v_conv_dw_w, v_conv_dw_b, v_conv_ln_g, v_conv_ln_b, v_conv_pw2_w, v_ssm_lam_re, v_ssm_lam_im, v_ssm_log_dt, v_ssm_b_re, v_ssm_b_im, v_ssm_c_re, v_ssm_c_im, v_ssm_d, v_ssm_glu_w, v_branch_norm_g, v_w_out, v_norm_xa_g, v_norm_mem_g, v_xa_wq, v_xa_wk, v_xa_wv, v_xa_q_norm_g, v_xa_k_norm_g, v_xa_wo, v_norm_ffn_g, v_ffn_w_in, v_ffn_w_out):
    args = locals()
    w_loc = {n: args[n] for n in WEIGHT_NAMES}
    m_loc = {n: args["m_" + n] for n in WEIGHT_NAMES}
    v_loc = {n: args["v_" + n] for n in WEIGHT_NAMES}
    me = _my_index()
    shard = {(l, n): _rows_first(w_loc[n], n)[l].astype(BF16) for l in range(DEPTH) for n in DIRECT_NAMES}
    conv_shape = _rows_first(w_loc['conv_dw_w'], 'conv_dw_w').shape
    conv_rows = _pack([_rows_first(w_loc['conv_dw_w'], 'conv_dw_w')], F32, SUBLANES)
    full = {}
    recv = {}
    grads = [dict() for _ in range(DEPTH)]

    def ride(kernel_name):
        if kernel_name in GATHER_RIDES:
            keys = GATHER_RIDES[kernel_name]
            return _Rider(gathers=[shard[k] for k in keys], done=lambda res: full.update(zip(keys, res)))
        if kernel_name in SCATTER_RIDES:
            keys = SCATTER_RIDES[kernel_name]
            return _Rider(scatters=[grads[l][n].astype(BF16) for (l, n) in keys],
                          done=lambda res: recv.update(zip(keys, res)))
        return None

    first = []
    _exchange(_Rider(gathers=[shard[(0, 'w_in')], conv_rows], done=first.extend), "gather_first")
    full[(0, 'w_in')] = first[0]
    conv_all = first[1].reshape(N_DEV, -1)[:, :int(np.prod(conv_shape))].reshape((N_DEV,) + conv_shape)
    conv_full = conv_all.transpose(1, 0, 2, 3).reshape(DEPTH, N_DEV * conv_shape[1], conv_shape[2])
    weights = [_LayerWeights(l, w_loc, full, conv_full) for l in range(DEPTH)]

    consts = _layer_consts()
    h, saved = x[0], []
    for l in range(DEPTH):
        h, sv = _layer_fwd(h, mem[0], weights[l], consts, f"l{l}", ride)
        saved.append(sv)
    loss_part, dh = _loss_head(h, loss_target[0])
    for l in reversed(range(DEPTH)):
        dh, _ = _layer_bwd(dh, mem[0], weights[l], consts, saved[l], f"l{l}", ride, grads[l])
    grad_x = dh
    loss = lax.psum(loss_part[0, 0], MESH_AXES)

    small_shapes = [w_loc[n].shape for n in SMALL_NAMES]
    conv_nat = (DEPTH,) + grads[0]['conv_dw_w'].shape[::-1]
    small_send = _pack([jnp.stack([grads[l][n] for l in range(DEPTH)]) for n in SMALL_NAMES]
                       + [jnp.stack([grads[l]['conv_dw_w'].T for l in range(DEPTH)])], F32, SMALL_TILE)
    last = []
    _exchange(_Rider(gathers=[small_send], scatters=[grads[0]['w_in'].astype(BF16)], done=last.extend), "exchange_last")
    recv[(0, 'w_in')] = last[1]
    small_sum = _reduce_slots(last[0].reshape((N_DEV,) + small_send.shape), SMALL_TILE, "reduce_replicated")
    small_g = _unpack(small_sum, small_shapes + [conv_nat])
    conv_cols = w_loc['conv_dw_w'].shape[2]
    conv_g = lax.dynamic_slice_in_dim(small_g[-1], me * conv_cols, conv_cols, axis=2)

    result = [{}, {}, {}, {}]
    for n in DIRECT_NAMES:
        parts = [recv[(l, n)] for l in range(DEPTH)]
        if SHARD_AXIS[n] == 2:
            parts = [_reduce_slots(p, _tile_rows(p.shape[1]), f"reduce_{n}_l{l}").T[None] for l, p in enumerate(parts)]
        outs = _adamw_layers(parts, w_loc[n], m_loc[n], v_loc[n], _tile_rows(w_loc[n].shape[1]), f"adamw_{n}")
        for kind in range(4):
            result[kind][n] = outs[kind]
    packed_names = SMALL_NAMES + ['conv_dw_w']
    pk = lambda d: _pack([d[n] for n in packed_names], F32, SMALL_TILE)
    outs = _adamw(_pack(small_g[:-1] + [conv_g], F32, SMALL_TILE)[None], pk(w_loc), pk(m_loc), pk(v_loc), SMALL_TILE,
                  "adamw_packed")
    for kind in range(4):
        for n, a in zip(packed_names, _unpack(outs[kind], [w_loc[n].shape for n in packed_names])):
            result[kind][n] = a
    return (loss, grad_x[None], *[result[0][n] for n in WEIGHT_NAMES], *[result[1][n] for n in WEIGHT_NAMES],
            *[result[2][n] for n in WEIGHT_NAMES], *[result[3][n] for n in WEIGHT_NAMES])
```

```python
import functools

import numpy as np
import jax
import jax.numpy as jnp
from jax import lax
from jax.experimental import pallas as pl
from jax.experimental.pallas import tpu as pltpu

F32 = jnp.float32
BF16 = jnp.bfloat16
EPS = 1e-6
D_MODEL = 1024
DEPTH = 2
N_DEV = 8
SB_WIDTH = 512
SB_HEAD_DIM = 64
CONV_CH = 256
CONV_WIDTH = 31
SSM_CH = 256
SSM_GROUP = 16
SSM_GROUPS = 16
SSM_STATE = 64
SSM_LANES = SSM_GROUPS * SSM_STATE
XA_HEADS = 4
XA_HEAD_DIM = 256
FFN_HIDDEN = 2816
ADAM_LR = 0.001
ADAM_B1 = 0.9
ADAM_B2 = 0.999
ADAM_EPS = 1e-08
ADAM_WD = 0.01
ADAM_STEP = 10

LANES = 128
SUBLANES = 8
TL = 512
SB_TQ = 256
SB_TK = 128
SB_PAIRS = 2
SB_SCALE = SB_HEAD_DIM ** -0.5
SB_DEAD = -120.0
SSM_T = 256
CONV_HALO = 32
CONV_SUB = 64
VMEM_MB = 48

MESH_AXES = ("x", "y", "c")
WEIGHT_NAMES = ['norm_mix_g', 'w_in', 'sb_q_norm_g', 'sb_k_norm_g', 'conv_dw_w', 'conv_dw_b', 'conv_ln_g',
                'conv_ln_b', 'conv_pw2_w', 'ssm_lam_re', 'ssm_lam_im', 'ssm_log_dt', 'ssm_b_re', 'ssm_b_im',
                'ssm_c_re', 'ssm_c_im', 'ssm_d', 'ssm_glu_w', 'branch_norm_g', 'w_out', 'norm_xa_g',
                'norm_mem_g', 'xa_wq', 'xa_wk', 'xa_wv', 'xa_q_norm_g', 'xa_k_norm_g', 'xa_wo', 'norm_ffn_g',
                'ffn_w_in', 'ffn_w_out']
SHARD_AXIS = {'w_in': 2, 'conv_dw_w': 2, 'conv_pw2_w': 1, 'ssm_glu_w': 2, 'w_out': 1, 'xa_wq': 1, 'xa_wk': 1,
              'xa_wv': 1, 'xa_wo': 1, 'ffn_w_in': 2, 'ffn_w_out': 1}
BIG_NAMES = [n for n in WEIGHT_NAMES if n in SHARD_AXIS]
SMALL_NAMES = [n for n in WEIGHT_NAMES if n not in SHARD_AXIS]


def _nn(a, b):
    return jnp.dot(a.astype(BF16), b.astype(BF16), preferred_element_type=F32)


def _nt(a, b):
    return lax.dot_general(a.astype(BF16), b.astype(BF16), (((1,), (1,)), ((), ())), preferred_element_type=F32)


def _tn(a, b):
    return lax.dot_general(a.astype(BF16), b.astype(BF16), (((0,), (0,)), ((), ())), preferred_element_type=F32)


def _rms(x, g):
    return x * lax.rsqrt(jnp.mean(x * x, axis=-1, keepdims=True) + EPS) * g


def _sigmoid(x):
    return 1.0 / (1.0 + jnp.exp(-x))


def _silu(x):
    return x * _sigmoid(x)


def _layer_norm(x, g, b):
    mu = jnp.mean(x, axis=-1, keepdims=True)
    xc = x - mu
    var = jnp.mean(xc * xc, axis=-1, keepdims=True)
    return xc * lax.rsqrt(var + EPS) * g + b


def _head_rms64(p, g, mavg):
    ms = jnp.dot(p * p, mavg, preferred_element_type=F32)
    return p * lax.rsqrt(ms + EPS) * g


def _params(n_grid, vmem_mb=VMEM_MB):
    return pltpu.CompilerParams(dimension_semantics=("arbitrary",) * n_grid, vmem_limit_bytes=vmem_mb << 20)


def _rows(cols, tl=TL):
    return pl.BlockSpec((tl, cols), lambda i: (i, 0))


def _res(shape):
    nd = len(shape)
    return pl.BlockSpec(tuple(shape), lambda *_: (0,) * nd)


def _sds(shape, dtype=F32):
    return jax.ShapeDtypeStruct(tuple(shape), dtype)


def _accumulate(i, ref, val):
    @pl.when(i == 0)
    def _():
        ref[...] = val

    @pl.when(i > 0)
    def _():
        ref[...] += val


HBM = pl.BlockSpec(memory_space=pltpu.HBM)
ROW_ALIGN = 16


def _my_index():
    return lax.axis_index("x") * 4 + lax.axis_index("y") * 2 + lax.axis_index("c")


def _peer(k):
    x, y, c = lax.axis_index("x"), lax.axis_index("y"), lax.axis_index("c")
    return (x ^ ((k >> 2) & 1), y ^ ((k >> 1) & 1), c ^ (k & 1))


class _Rider:
    def __init__(self, gathers=(), scatters=(), done=None):
        self.gathers, self.scatters, self.done = list(gathers), list(scatters), done

    @property
    def inputs(self):
        return self.gathers + self.scatters

    def out_shapes(self):
        return ([_sds((N_DEV * a.shape[0],) + a.shape[1:], a.dtype) for a in self.gathers]
                + [_sds((N_DEV, a.shape[0] // N_DEV) + a.shape[1:], a.dtype) for a in self.scatters])


def _rider_copies(rider, in_refs, out_refs, send_sems, recv_sems, local_sems):
    me = _my_index()
    local, sends, recvs = [], [], []
    for t, (src, dst) in enumerate(zip(in_refs, out_refs)):
        gather = t < len(rider.gathers)
        rows = src.shape[0] if gather else src.shape[0] // N_DEV

        def block(ref, d, rows=rows):
            return ref.at[pl.ds(pl.multiple_of(d * rows, ROW_ALIGN), rows)]

        src_for = (lambda p, src=src: src) if gather else (lambda p, src=src: block(src, p))
        dst_for = (lambda d, dst=dst: block(dst, d)) if gather else (lambda d, dst=dst: dst.at[d])
        local.append(pltpu.make_async_copy(src_for(me), dst_for(me), local_sems.at[t]))
        for k in range(1, N_DEV):
            args = dict(send_sem=send_sems.at[t * N_DEV + k], recv_sem=recv_sems.at[t * N_DEV + k], device_id=_peer(k),
                        device_id_type=pl.DeviceIdType.MESH)
            sends.append(pltpu.make_async_remote_copy(src_ref=src_for(me ^ k), dst_ref=dst_for(me), **args))
            recvs.append(pltpu.make_async_remote_copy(src_ref=src_for(me ^ k), dst_ref=dst_for(me ^ k), **args))
    return local, sends, recvs


def _pcall(body, *, name, out_shape, grid=(), in_specs=None, out_specs=None, scratch_shapes=(), compiler_params=None,
           rider=None):
    if rider is None or not rider.inputs:
        return pl.pallas_call(body, name=name, grid=grid, in_specs=in_specs, out_specs=out_specs, out_shape=out_shape,
                              scratch_shapes=list(scratch_shapes), compiler_params=compiler_params)
    single = not isinstance(out_shape, (list, tuple))
    outs = [out_shape] if single else list(out_shape)
    ospecs = [out_specs] if single else list(out_specs)
    n_in, n_out, n_scr, n_r = len(in_specs), len(outs), len(scratch_shapes), len(rider.inputs)

    def wrapped(*refs):
        ins, rin = refs[:n_in], refs[n_in:n_in + n_r]
        own_out = refs[n_in + n_r:n_in + n_r + n_out]
        rout = refs[n_in + n_r + n_out:n_in + 2 * n_r + n_out]
        scratch = refs[n_in + 2 * n_r + n_out:n_in + 2 * n_r + n_out + n_scr]
        local, sends, recvs = _rider_copies(rider, rin, rout, *refs[-3:])

        def start():
            for cp in local + sends:
                cp.start()

        def wait():
            for cp in recvs:
                cp.wait_recv()
            for cp in sends:
                cp.wait_send()
            for cp in local:
                cp.wait()

        if grid:
            ids = [pl.program_id(a) for a in range(len(grid))]
            first = functools.reduce(jnp.logical_and, [i == 0 for i in ids])
            last = functools.reduce(jnp.logical_and, [i == n - 1 for i, n in zip(ids, grid)])
            pl.when(first)(start)
            body(*ins, *own_out, *scratch)
            pl.when(last)(wait)
        else:
            start()
            body(*ins, *own_out, *scratch)
            wait()

    call = pl.pallas_call(
        wrapped, name=name, grid=grid, in_specs=list(in_specs) + [HBM] * n_r, out_specs=ospecs + [HBM] * n_r,
        out_shape=outs + rider.out_shapes(),
        scratch_shapes=list(scratch_shapes) + [pltpu.SemaphoreType.DMA((n_r * N_DEV,)),
                                               pltpu.SemaphoreType.DMA((n_r * N_DEV,)), pltpu.SemaphoreType.DMA((n_r,))],
        compiler_params=compiler_params)

    def run(*args):
        res = call(*args, *rider.inputs)
        if rider.done is not None:
            rider.done(list(res[n_out:]))
        return res[0] if single else list(res[:n_out])

    return run


def _exchange(rider, name):
    def body():
        pass

    _pcall(body, name=name, out_shape=[], in_specs=[], out_specs=[], rider=rider)()


def _mixin_post(pq, pk, a, b, gq, gk, mavg):
    return _head_rms64(pq, gq, mavg), _head_rms64(pk, gk, mavg), a * _sigmoid(b)


def _mix_in_fwd(x, g_mix, w_in, gq, gk, mavg, tag, rider=None):
    L = x.shape[0]

    def body(x_ref, g_ref, w_ref, gq_ref, gk_ref, mavg_ref, q_ref, k_ref, v_ref, hg_ref, u_ref):
        h = _rms(x_ref[...], g_ref[...])
        p = _nt(h, w_ref[...])
        q, k, hg = _mixin_post(p[:, 0:512], p[:, 512:1024], p[:, 1536:1792], p[:, 1792:2048],
                               gq_ref[...], gk_ref[...], mavg_ref[...])
        q_ref[...] = (q * SB_SCALE).astype(BF16)
        k_ref[...] = k.astype(BF16)
        v_ref[...] = p[:, 1024:1536].astype(BF16)
        hg_ref[...] = hg
        u_ref[...] = p[:, 2048:2304]

    return _pcall(
        body, name=f"mix_in_fwd_{tag}", grid=(L // TL,),
        in_specs=[_rows(D_MODEL), _res((1, D_MODEL)), _res(w_in.shape), _res((1, 512)), _res((1, 512)), _res((512, 512))],
        out_specs=[_rows(512), _rows(512), _rows(512), _rows(256), _rows(256)],
        out_shape=[_sds((L, 512), BF16), _sds((L, 512), BF16), _sds((L, 512), BF16), _sds((L, 256)), _sds((L, 256))],
        compiler_params=_params(1), rider=rider,
    )(x, g_mix, w_in, gq, gk, mavg)


def _mix_in_bwd(x, dres, dq, dk, dv, dhg, du, g_mix, w_in, gq, gk, mavg, tag):
    L = x.shape[0]
    tl = 256

    def body(x_ref, dres_ref, dq_ref, dk_ref, dv_ref, dhg_ref, du_ref, g_ref, w_ref, gq_ref, gk_ref, mavg_ref,
             dx_ref, dw_ref, dg_ref, dgq_ref, dgk_ref):
        i = pl.program_id(0)
        xx = x_ref[...]
        g = g_ref[...]
        mavg_v = mavg_ref[...]
        h, vjp_n = jax.vjp(_rms, xx, g)
        p = _nt(h, w_ref[...])
        _, vjp_p = jax.vjp(lambda pq, pk, a, b, gq_, gk_: _mixin_post(pq, pk, a, b, gq_, gk_, mavg_v),
                           p[:, 0:512], p[:, 512:1024], p[:, 1536:1792], p[:, 1792:2048], gq_ref[...], gk_ref[...])
        dpq, dpk, da, db, dgq, dgk = vjp_p((dq_ref[...], dk_ref[...], dhg_ref[...]))
        dp = jnp.concatenate([dpq, dpk, dv_ref[...], da, db, du_ref[...]], axis=1)
        dh = _nn(dp, w_ref[...])
        dxn, dg = vjp_n(dh)
        dx_ref[...] = dres_ref[...] + dxn
        _accumulate(i, dw_ref, _tn(dp, h))
        _accumulate(i, dg_ref, dg)
        _accumulate(i, dgq_ref, dgq)
        _accumulate(i, dgk_ref, dgk)

    r = lambda c: _rows(c, tl)
    return pl.pallas_call(
        body, name=f"mix_in_bwd_{tag}", grid=(L // tl,),
        in_specs=[r(D_MODEL), r(D_MODEL), r(512), r(512), r(512), r(256), r(256),
                  _res((1, D_MODEL)), _res(w_in.shape), _res((1, 512)), _res((1, 512)), _res((512, 512))],
        out_specs=[r(D_MODEL), _res(w_in.shape), _res((1, D_MODEL)), _res((1, 512)), _res((1, 512))],
        out_shape=[_sds((L, D_MODEL)), _sds(w_in.shape), _sds((1, D_MODEL)), _sds((1, 512)), _sds((1, 512))],
        compiler_params=_params(1),
    )(x, dres, dq, dk, dv, dhg, du, g_mix, w_in, gq, gk, mavg)


def _sb_tri_consts():
    r = np.arange(2 * SB_TK)[:, None]
    c = np.arange(2 * SB_TK)[None, :]
    same = (r // SB_TK) == (c // SB_TK)
    later = (same & (r > c)).astype(np.float32)
    earlier = (same & (r < c)).astype(np.float32)
    return jnp.asarray(later, BF16), jnp.asarray(earlier, BF16)


def _two_heads(blk, lane_a):
    zero = jnp.zeros_like(blk)
    return jnp.concatenate([jnp.where(lane_a, blk, zero), jnp.where(lane_a, zero, blk)], axis=0)


def _sb_logs(z, i, j, masked):
    e = jnp.exp(-jnp.abs(z))
    lm = -(jnp.maximum(z, 0.0) + jnp.log(1.0 + e))
    ls = z + lm
    valid = None
    if masked:
        row = lax.broadcasted_iota(jnp.int32, (SB_TQ, 2 * SB_TK), 0)
        col = lax.broadcasted_iota(jnp.int32, (SB_TQ, 2 * SB_TK), 1) & (SB_TK - 1)
        valid = (j * SB_TK + col) < (i * SB_TQ + row)
        lm = jnp.where(valid, lm, 0.0)
    return lm, ls, lm.astype(BF16), valid


def _lane_halves(a, b):
    return jnp.concatenate([jnp.broadcast_to(a, (SB_TQ, SB_TK)), jnp.broadcast_to(b, (SB_TQ, SB_TK))], axis=1)


def _dot(a, b):
    return jnp.dot(a, b, preferred_element_type=F32)


def _dot_nt(a, b):
    return lax.dot_general(a, b, (((1,), (1,)), ((), ())), preferred_element_type=F32)


def _dot_tn(a, b):
    return lax.dot_general(a, b, (((0,), (0,)), ((), ())), preferred_element_type=F32)


def _sb_fwd(q, k, v, ul, tag, rider=None):
    L = q.shape[0]
    nq = L // SB_TQ
    per = SB_TQ // SB_TK
    wid = SB_PAIRS * LANES

    def body(q_ref, k_ref, v_ref, ul_ref, o_ref, rs_ref, n_ref):
        i = pl.program_id(1)
        ulv = ul_ref[...]
        lane_a = lax.broadcasted_iota(jnp.int32, (1, LANES), 1) < SB_HEAD_DIM
        lane_q = lax.broadcasted_iota(jnp.int32, (SB_TQ, LANES), 1)
        cols = [slice(p * LANES, (p + 1) * LANES) for p in range(SB_PAIRS)]
        qbs = [q_ref[:, c] for c in cols]

        def double_step(jhi, carry, masked):
            chains = [dict(p=p, j=jhi - d) for d in range(2) for p in range(SB_PAIRS)]
            for c in chains:
                c['off'] = pl.multiple_of(c['j'] * SB_TK, SB_TK)
                kb = k_ref[pl.ds(c['off'], SB_TK), cols[c['p']]]
                c['z'] = _dot_nt(qbs[c['p']], _two_heads(kb, lane_a))
            for c in chains:
                c['lm'], c['ls'], c['hi'], c['valid'] = _sb_logs(c.pop('z'), i, c['j'], masked)
            for c in chains:
                c['lb'] = _dot(c.pop('hi'), ulv)
            state = [list(s) for s in carry]
            for c in chains:
                ra, rb, _, rsave = state[c['p']]
                w = jnp.exp(c['ls'] + c['lb'] + _lane_halves(ra, rb))
                if masked:
                    w = jnp.where(c['valid'], w, 0.0)
                c['w'] = w.astype(BF16)
                lb, lm = c['lb'], c['lm']
                state[c['p']][3] = jnp.where(lane_q == c['j'], ra, jnp.where(lane_q == c['j'] + SB_HEAD_DIM, rb, rsave))
                state[c['p']][0] = ra + lb[:, 0:1] + lm[:, 0:1]
                state[c['p']][1] = rb + lb[:, SB_TK:SB_TK + 1] + lm[:, SB_TK:SB_TK + 1]
            for c in chains:
                vb = v_ref[pl.ds(c['off'], SB_TK), cols[c['p']]]
                state[c['p']][2] = state[c['p']][2] + _dot(c['w'], _two_heads(vb, lane_a))
            return tuple(tuple(s) for s in state)

        assert per == 2
        carry = tuple((jnp.zeros((SB_TQ, 1), F32), jnp.zeros((SB_TQ, 1), F32),
                       jnp.zeros((SB_TQ, LANES), F32), jnp.zeros((SB_TQ, LANES), F32)) for _ in range(SB_PAIRS))
        def alive(carry):
            m = carry[0][0]
            for c in carry:
                m = jnp.maximum(m, jnp.maximum(c[0], c[1]))
            return jnp.max(m) > SB_DEAD

        carry = double_step(i * per + 1, carry, True)
        n_done, _, carry = lax.while_loop(
            lambda st: jnp.logical_and(st[0] < i, st[1]),
            lambda st: (lambda c: (st[0] + 1, alive(c), c))(double_step(i * per - 1 - 2 * st[0], st[2], False)),
            (jnp.int32(0), alive(carry), carry))
        o_ref[...] = jnp.concatenate([c[2] for c in carry], axis=1)
        rs_ref[...] = jnp.concatenate([c[3] for c in carry], axis=1)
        n_ref[pl.program_id(0), i] = n_done

    qspec = pl.BlockSpec((SB_TQ, wid), lambda g, i: (i, g))
    kspec = pl.BlockSpec((L, wid), lambda g, i: (0, g))
    return _pcall(
        body, name=f"sb_fwd_{tag}", grid=(SB_WIDTH // wid, nq),
        in_specs=[qspec, kspec, kspec, pl.BlockSpec((2 * SB_TK, 2 * SB_TK), lambda g, i: (0, 0))],
        out_specs=[qspec, qspec, pl.BlockSpec(memory_space=pltpu.SMEM)],
        out_shape=[_sds((L, SB_WIDTH)), _sds((L, SB_WIDTH)), _sds((SB_WIDTH // wid, nq), jnp.int32)],
        compiler_params=_params(2), rider=rider,
    )(q, k, v, ul)


def _sb_bwd(n_done, q, k, v, rsave, do, ul, ue, tag):
    L = q.shape[0]
    nq = L // SB_TQ
    per = SB_TQ // SB_TK
    wid = SB_PAIRS * LANES

    def body(n_ref, q_ref, k_ref, v_ref, rs_ref, do_ref, ul_ref, ue_ref, dq_ref, dk_ref, dv_ref):
        i = pl.program_id(1)

        @pl.when(i == 0)
        def _():
            dk_ref[...] = jnp.zeros_like(dk_ref)
            dv_ref[...] = jnp.zeros_like(dv_ref)

        ulv = ul_ref[...]
        uev = ue_ref[...]
        lane_a = lax.broadcasted_iota(jnp.int32, (1, LANES), 1) < SB_HEAD_DIM
        lane_q = lax.broadcasted_iota(jnp.int32, (SB_TQ, LANES), 1)
        cols = [slice(p * LANES, (p + 1) * LANES) for p in range(SB_PAIRS)]
        qbs = [q_ref[:, c] for c in cols]
        dobs = [do_ref[:, c].astype(BF16) for c in cols]
        rsvs = [rs_ref[:, c] for c in cols]

        def double_step(jlo, carry, masked):
            chains = [dict(p=p, j=jlo + d) for d in range(2) for p in range(SB_PAIRS)]
            for c in chains:
                p = c['p']
                c['off'] = pl.multiple_of(c['j'] * SB_TK, SB_TK)
                c['kk2'] = _two_heads(k_ref[pl.ds(c['off'], SB_TK), cols[p]], lane_a)
                c['z'] = _dot_nt(qbs[p], c['kk2'])
                c['dw'] = _dot_nt(dobs[p], _two_heads(v_ref[pl.ds(c['off'], SB_TK), cols[p]], lane_a))
            for c in chains:
                c['lm'], c['ls'], c['hi'], c['valid'] = _sb_logs(c.pop('z'), i, c['j'], masked)
                c['ra'] = jnp.sum(jnp.where(lane_q == c['j'], rsvs[c['p']], 0.0), axis=1, keepdims=True)
                c['rb'] = jnp.sum(jnp.where(lane_q == c['j'] + SB_HEAD_DIM, rsvs[c['p']], 0.0), axis=1, keepdims=True)
            for c in chains:
                c['lb'] = _dot(c.pop('hi'), ulv)
            for c in chains:
                w = jnp.exp(c['ls'] + c.pop('lb') + _lane_halves(c['ra'], c['rb']))
                if masked:
                    w = jnp.where(c['valid'], w, 0.0)
                c['wb'] = w.astype(BF16)
                gg = w * c.pop('dw')
                c['gg'] = gg
                c['beta'] = jnp.exp(c['ls'])
            for c in chains:
                c['cb'] = _dot(c['gg'].astype(BF16), uev)
                c['dv2'] = _dot_tn(c.pop('wb'), dobs[c['p']])
            state = [list(s) for s in carry]
            for c in chains:
                pa, pb, _ = state[c['p']]
                gg, cb, beta = c['gg'], c['cb'], c['beta']
                dz = gg * (1.0 - beta) - beta * (cb + _lane_halves(pa, pb))
                if masked:
                    dz = jnp.where(c['valid'], dz, 0.0)
                c['dzb'] = dz.astype(BF16)
                state[c['p']][0] = pa + cb[:, SB_TK - 1:SB_TK] + gg[:, SB_TK - 1:SB_TK]
                state[c['p']][1] = pb + cb[:, 2 * SB_TK - 1:2 * SB_TK] + gg[:, 2 * SB_TK - 1:2 * SB_TK]
            for c in chains:
                c['dqc'] = _dot(c['dzb'], c['kk2'])
                c['dk2'] = _dot_tn(c['dzb'], qbs[c['p']])
            for c in chains:
                p, dk2, dv2 = c['p'], c['dk2'], c['dv2']
                state[p][2] = state[p][2] + c['dqc']
                dk_ref[pl.ds(c['off'], SB_TK), cols[p]] += jnp.where(lane_a, dk2[0:SB_TK], dk2[SB_TK:2 * SB_TK])
                dv_ref[pl.ds(c['off'], SB_TK), cols[p]] += jnp.where(lane_a, dv2[0:SB_TK], dv2[SB_TK:2 * SB_TK])
            return tuple(tuple(s) for s in state)

        assert per == 2
        carry = tuple((jnp.zeros((SB_TQ, 1), F32), jnp.zeros((SB_TQ, 1), F32), jnp.zeros((SB_TQ, LANES), F32))
                      for _ in range(SB_PAIRS))
        first = i - n_ref[pl.program_id(0), i]
        carry = lax.fori_loop(first, i, lambda jj, c: double_step(2 * jj, c, False), carry)
        carry = double_step(i * per, carry, True)
        dq_ref[...] = jnp.concatenate([c[2] for c in carry], axis=1) * SB_SCALE

    qspec = pl.BlockSpec((SB_TQ, wid), lambda g, i: (i, g))
    kspec = pl.BlockSpec((L, wid), lambda g, i: (0, g))
    kin = pl.BlockSpec((L, wid), lambda g, i: (0, g), pipeline_mode=pl.Buffered(1))
    cspec = pl.BlockSpec((2 * SB_TK, 2 * SB_TK), lambda g, i: (0, 0))
    return pl.pallas_call(
        body, name=f"sb_bwd_{tag}", grid=(SB_WIDTH // wid, nq),
        in_specs=[pl.BlockSpec(memory_space=pltpu.SMEM), qspec, kin, kin, qspec, qspec, cspec, cspec],
        out_specs=[qspec, kspec, kspec],
        out_shape=[_sds((L, SB_WIDTH)), _sds((L, SB_WIDTH)), _sds((L, SB_WIDTH))],
        compiler_params=_params(2, 58),
    )(n_done, q, k, v, rsave, do, ul, ue)


def _conv_fwd(hp, w, tag):
    L = hp.shape[0] - CONV_HALO
    win_rows = CONV_SUB + CONV_HALO

    def body(hp_ref, w_ref, o_ref):
        i = pl.program_id(0)

        def sub(s, _):
            t0 = pl.multiple_of(i * TL + s * CONV_SUB, CONV_SUB)
            win = hp_ref[pl.ds(t0, win_rows), :]
            acc = jnp.zeros((CONV_SUB, CONV_CH), F32)
            for kk in range(CONV_WIDTH):
                sh = CONV_WIDTH - 1 - kk
                r = win if sh == 0 else pltpu.roll(win, sh, 0)
                acc = acc + w_ref[kk:kk + 1, :] * r[CONV_HALO:, :]
            o_ref[pl.ds(pl.multiple_of(s * CONV_SUB, CONV_SUB), CONV_SUB), :] = acc
            return 0

        lax.fori_loop(0, TL // CONV_SUB, sub, 0)

    return pl.pallas_call(
        body, name=f"conv_fwd_{tag}", grid=(L // TL,),
        in_specs=[_res(hp.shape), _res(w.shape)],
        out_specs=_rows(CONV_CH),
        out_shape=_sds((L, CONV_CH)),
        compiler_params=_params(1),
    )(hp, w)


def _conv_bwd(dpad, hp, w, tag):
    L = hp.shape[0] - CONV_HALO
    win_rows = CONV_SUB + CONV_HALO
    n_tiles = L // TL

    def body(dp_ref, hp_ref, w_ref, dh_ref, dw_ref, acc_ref):
        i = pl.program_id(0)

        @pl.when(i == 0)
        def _():
            acc_ref[...] = jnp.zeros_like(acc_ref)

        def sub(s, _):
            t0 = pl.multiple_of(i * TL + s * CONV_SUB, CONV_SUB)
            wd = dp_ref[pl.ds(t0, win_rows), :]
            wh = hp_ref[pl.ds(t0, win_rows), :]
            dy = wd[0:CONV_SUB, :]
            acc = jnp.zeros((CONV_SUB, CONV_CH), F32)
            for kk in range(CONV_WIDTH):
                sh = CONV_WIDTH - 1 - kk
                rd = wd if sh == 0 else pltpu.roll(wd, win_rows - sh, 0)
                acc = acc + w_ref[kk:kk + 1, :] * rd[0:CONV_SUB, :]
                rh = wh if sh == 0 else pltpu.roll(wh, sh, 0)
                prod = dy * rh[CONV_HALO:, :]
                part = prod[0:SUBLANES]
                for m in range(1, CONV_SUB // SUBLANES):
                    part = part + prod[m * SUBLANES:(m + 1) * SUBLANES]
                acc_ref[kk] += part
            dh_ref[pl.ds(pl.multiple_of(s * CONV_SUB, CONV_SUB), CONV_SUB), :] = acc
            return 0

        lax.fori_loop(0, TL // CONV_SUB, sub, 0)

        @pl.when(i == n_tiles - 1)
        def _():
            for kk in range(CONV_WIDTH):
                dw_ref[kk:kk + 1, :] = jnp.sum(acc_ref[kk], axis=0, keepdims=True)

    return pl.pallas_call(
        body, name=f"conv_bwd_{tag}", grid=(n_tiles,),
        in_specs=[_res(dpad.shape), _res(hp.shape), _res(w.shape)],
        out_specs=[_rows(CONV_CH), _res(w.shape)],
        out_shape=[_sds((L, CONV_CH)), _sds(w.shape)],
        scratch_shapes=[pltpu.VMEM((CONV_WIDTH, SUBLANES, CONV_CH), F32)],
        compiler_params=_params(1),
    )(dpad, hp, w)


def _ssm_mask():
    r = np.arange(SSM_CH)[:, None] // SSM_GROUP
    c = np.arange(SSM_LANES)[None, :] // SSM_STATE
    return jnp.asarray((r == c).astype(np.float32))


def _ssm_discretize(lr, li, ldt, brt, bit):
    dt = jnp.exp(ldt)
    mag = jnp.exp(lr * dt)
    ar = mag * jnp.cos(li * dt)
    ai = mag * jnp.sin(li * dt)
    den = lr * lr + li * li
    fr = ((ar - 1.0) * lr + ai * li) / den
    fi = (ai * lr - (ar - 1.0) * li) / den
    return ar, ai, fr * brt - fi * bit, fr * bit + fi * brt


def _block_diag(rows16, mask):
    return jnp.where(mask > 0.5, jnp.tile(rows16, (SSM_GROUPS, 1)), 0.0)


def _block_diag_t(full, mask):
    m = jnp.where(mask > 0.5, full, 0.0)
    out = m[0:SSM_GROUP]
    for g in range(1, SSM_GROUPS):
        out = out + m[g * SSM_GROUP:(g + 1) * SSM_GROUP]
    return out


def _ssm_prep(lr, li, ldt, brt, bit, crt, cit, mask, tag):
    def body(lr_ref, li_ref, ldt_ref, brt_ref, bit_ref, crt_ref, cit_ref, m_ref,
             ar_ref, ai_ref, bbr_ref, bbi_ref, cbr_ref, cbi_ref):
        ar, ai, bbr, bbi = _ssm_discretize(lr_ref[...], li_ref[...], ldt_ref[...], brt_ref[...], bit_ref[...])
        m = m_ref[...]
        ar_ref[...] = ar
        ai_ref[...] = ai
        bbr_ref[...] = _block_diag(bbr, m).astype(BF16)
        bbi_ref[...] = _block_diag(bbi, m).astype(BF16)
        cbr_ref[...] = _block_diag(crt_ref[...], m).astype(BF16)
        cbi_ref[...] = _block_diag(cit_ref[...], m).astype(BF16)

    row = _sds((1, SSM_LANES))
    blk = _sds((SSM_CH, SSM_LANES), BF16)
    return pl.pallas_call(body, name=f"ssm_prep_{tag}", out_shape=[row, row, blk, blk, blk, blk])(
        lr, li, ldt, brt, bit, crt, cit, mask)


def _ssm_prep_bwd(lr, li, ldt, brt, bit, mask, dar, dai, dbbr, dbbi, dcbr, dcbi, tag):
    def body(lr_ref, li_ref, ldt_ref, brt_ref, bit_ref, m_ref, dar_ref, dai_ref, dbbr_ref, dbbi_ref, dcbr_ref,
             dcbi_ref, dlr_ref, dli_ref, dldt_ref, dbrt_ref, dbit_ref, dcrt_ref, dcit_ref):
        m = m_ref[...]
        _, vjp = jax.vjp(_ssm_discretize, lr_ref[...], li_ref[...], ldt_ref[...], brt_ref[...], bit_ref[...])
        dlr, dli, dldt, dbrt, dbit = vjp((dar_ref[...], dai_ref[...], _block_diag_t(dbbr_ref[...], m),
                                          _block_diag_t(dbbi_ref[...], m)))
        dlr_ref[...] = dlr
        dli_ref[...] = dli
        dldt_ref[...] = dldt
        dbrt_ref[...] = dbrt
        dbit_ref[...] = dbit
        dcrt_ref[...] = _block_diag_t(dcbr_ref[...], m)
        dcit_ref[...] = _block_diag_t(dcbi_ref[...], m)

    row = _sds((1, SSM_LANES))
    r16 = _sds((SSM_GROUP, SSM_LANES))
    return pl.pallas_call(body, name=f"ssm_prep_bwd_{tag}", out_shape=[row, row, row, r16, r16, r16, r16])(
        lr, li, ldt, brt, bit, mask, dar, dai, dbbr, dbbi, dcbr, dcbi)


def _complex_scan(br, bi, ar, ai, cr, ci, reverse):
    n = br.shape[0]
    row = lax.broadcasted_iota(jnp.int32, (n, 1), 0) & (SUBLANES - 1)
    xr, xi, pr, pi = br, bi, ar, ai
    d = 1
    while d < SUBLANES:
        if reverse:
            sr, si, keep = pltpu.roll(xr, n - d, 0), pltpu.roll(xi, n - d, 0), row < SUBLANES - d
        else:
            sr, si, keep = pltpu.roll(xr, d, 0), pltpu.roll(xi, d, 0), row >= d
        sr = jnp.where(keep, sr, 0.0)
        si = jnp.where(keep, si, 0.0)
        xr, xi = xr + pr * sr - pi * si, xi + pr * si + pi * sr
        pr, pi = pr * pr - pi * pi, 2.0 * pr * pi
        d *= 2
    powers = [(ar, ai)]
    for _ in range(SUBLANES - 1):
        qr, qi = powers[-1]
        powers.append((qr * ar - qi * ai, qr * ai + qi * ar))
    sub = lax.broadcasted_iota(jnp.int32, (SUBLANES, 1), 0)
    tr = jnp.zeros((SUBLANES, br.shape[1]), F32)
    ti = jnp.zeros((SUBLANES, br.shape[1]), F32)
    for r in range(SUBLANES):
        qr, qi = powers[SUBLANES - 1 - r] if reverse else powers[r]
        tr = jnp.where(sub == r, qr, tr)
        ti = jnp.where(sub == r, qi, ti)
    n_groups = n // SUBLANES
    out_r, out_i = [None] * n_groups, [None] * n_groups
    end = 0 if reverse else SUBLANES - 1
    for g in (reversed(range(n_groups)) if reverse else range(n_groups)):
        gr = xr[g * SUBLANES:(g + 1) * SUBLANES]
        gi = xi[g * SUBLANES:(g + 1) * SUBLANES]
        gr, gi = gr + tr * cr - ti * ci, gi + tr * ci + ti * cr
        cr, ci = gr[end:end + 1], gi[end:end + 1]
        out_r[g], out_i[g] = gr, gi
    return jnp.concatenate(out_r, axis=0), jnp.concatenate(out_i, axis=0)


def _ssm_fwd(u, ar, ai, bbr, bbi, cbr, cbi, dvec, tag, rider=None):
    L = u.shape[0]
    T = SSM_T

    def body(u_ref, ar_ref, ai_ref, bbr_ref, bbi_ref, cbr_ref, cbi_ref, d_ref, y_ref, xr_ref, xi_ref, cr_ref, ci_ref):
        i = pl.program_id(0)

        @pl.when(i == 0)
        def _():
            cr_ref[...] = jnp.zeros_like(cr_ref)
            ci_ref[...] = jnp.zeros_like(ci_ref)

        uu = u_ref[...]
        a_r, a_i = ar_ref[...], ai_ref[...]
        c_r, c_i = cr_ref[...], ci_ref[...]
        xr, xi = _complex_scan(_nn(uu, bbr_ref[...]), _nn(uu, bbi_ref[...]), a_r, a_i, c_r, c_i, False)
        cr_ref[...] = xr[T - 1:T, :]
        ci_ref[...] = xi[T - 1:T, :]
        xr_ref[...] = xr
        xi_ref[...] = xi
        y_ref[...] = _nt(xr, cbr_ref[...]) - _nt(xi, cbi_ref[...]) + d_ref[...] * uu

    blk = _res((SSM_CH, SSM_LANES))
    row = _res((1, SSM_LANES))
    return _pcall(
        body, name=f"ssm_fwd_{tag}", grid=(L // T,), rider=rider,
        in_specs=[_rows(SSM_CH, T), row, row, blk, blk, blk, blk, _res((1, SSM_CH))],
        out_specs=[_rows(SSM_CH, T), _rows(SSM_LANES, T), _rows(SSM_LANES, T)],
        out_shape=[_sds((L, SSM_CH)), _sds((L, SSM_LANES)), _sds((L, SSM_LANES))],
        scratch_shapes=[pltpu.VMEM((1, SSM_LANES), F32), pltpu.VMEM((1, SSM_LANES), F32)],
        compiler_params=_params(1),
    )(u, ar, ai, bbr, bbi, cbr, cbi, dvec)


def _ssm_bwd(dy, u, xr, xi, ar, ai, bbr, bbi, cbr, cbi, dvec, tag, rider=None):
    L = u.shape[0]
    T = SSM_T
    nc = L // T

    def body(dy_ref, u_ref, xr_ref, xi_ref, pr_ref, pi_ref, ar_ref, ai_ref, bbr_ref, bbi_ref, cbr_ref, cbi_ref, d_ref,
             du_ref, dar_ref, dai_ref, dbbr_ref, dbbi_ref, dcbr_ref, dcbi_ref, dd_ref, gr_ref, gi_ref):
        i = pl.program_id(0)

        @pl.when(i == 0)
        def _():
            gr_ref[...] = jnp.zeros_like(gr_ref)
            gi_ref[...] = jnp.zeros_like(gi_ref)

        dyy = dy_ref[...]
        uu = u_ref[...]
        xr, xi = xr_ref[...], xi_ref[...]
        a_r, a_i = ar_ref[...], ai_ref[...]
        g_r, g_i = gr_ref[...], gi_ref[...]
        row = lax.broadcasted_iota(jnp.int32, (T, 1), 0)
        gr, gi = _complex_scan(_nn(dyy, cbr_ref[...]), -_nn(dyy, cbi_ref[...]), a_r, -a_i, g_r, g_i, True)
        gr_ref[...] = gr[0:1, :]
        gi_ref[...] = gi[0:1, :]
        has_prev = (i < nc - 1).astype(F32)
        pr = pr_ref[SUBLANES - 1:SUBLANES, :] * has_prev
        pi = pi_ref[SUBLANES - 1:SUBLANES, :] * has_prev
        sr = jnp.where(row == 0, pr, pltpu.roll(xr, 1, 0))
        si = jnp.where(row == 0, pi, pltpu.roll(xi, 1, 0))
        _accumulate(i, dar_ref, jnp.sum(gr * sr + gi * si, axis=0, keepdims=True))
        _accumulate(i, dai_ref, jnp.sum(gi * sr - gr * si, axis=0, keepdims=True))
        _accumulate(i, dbbr_ref, _tn(uu, gr))
        _accumulate(i, dbbi_ref, _tn(uu, gi))
        _accumulate(i, dcbr_ref, _tn(dyy, xr))
        _accumulate(i, dcbi_ref, -_tn(dyy, xi))
        _accumulate(i, dd_ref, jnp.sum(dyy * uu, axis=0, keepdims=True))
        du_ref[...] = _nt(gr, bbr_ref[...]) + _nt(gi, bbi_ref[...]) + dyy * d_ref[...]

    rev = lambda cols: pl.BlockSpec((T, cols), lambda i: (nc - 1 - i, 0))
    prev = pl.BlockSpec((SUBLANES, SSM_LANES), lambda i: (jnp.maximum((nc - 1 - i) * (T // SUBLANES) - 1, 0), 0))
    blk = _res((SSM_CH, SSM_LANES))
    row = _res((1, SSM_LANES))
    return _pcall(
        body, name=f"ssm_bwd_{tag}", grid=(nc,), rider=rider,
        in_specs=[rev(SSM_CH), rev(SSM_CH), rev(SSM_LANES), rev(SSM_LANES), prev, prev, row, row, blk, blk, blk, blk,
                  _res((1, SSM_CH))],
        out_specs=[rev(SSM_CH), row, row, blk, blk, blk, blk, _res((1, SSM_CH))],
        out_shape=[_sds((L, SSM_CH)), _sds((1, SSM_LANES)), _sds((1, SSM_LANES)), _sds((SSM_CH, SSM_LANES)),
                   _sds((SSM_CH, SSM_LANES)), _sds((SSM_CH, SSM_LANES)), _sds((SSM_CH, SSM_LANES)), _sds((1, SSM_CH))],
        scratch_shapes=[pltpu.VMEM((1, SSM_LANES), F32), pltpu.VMEM((1, SSM_LANES), F32)],
        compiler_params=_params(1),
    )(dy, u, xr, xi, xr, xi, ar, ai, bbr, bbi, cbr, cbi, dvec)


def _conv_post(hc, dw_b, ln_g, ln_b):
    return _silu(_layer_norm(hc + dw_b, ln_g, ln_b))


def _branch_mix(o_sb, o_conv, t, g1, g2, g3):
    o_ssm = t[:, 0:SSM_CH] * _sigmoid(t[:, SSM_CH:2 * SSM_CH])
    return jnp.concatenate([_rms(o_sb, g1), _rms(o_conv, g2), _rms(o_ssm, g3)], axis=1)


def _branch_mix_split(o_sb, o_conv, ta, tb, g1, g2, g3):
    return jnp.concatenate([_rms(o_sb, g1), _rms(o_conv, g2), _rms(ta * _sigmoid(tb), g3)], axis=1)


def _mix_out_fwd(x, o_sb, hc, y, dw_b, ln_g, ln_b, pw2, glu_w, g1, g2, g3, w_out, tag):
    L = x.shape[0]

    def body(x_ref, o_ref, hc_ref, y_ref, dwb_ref, lng_ref, lnb_ref, pw2_ref, glu_ref, g1_ref, g2_ref, g3_ref, wo_ref,
             out_ref):
        c1 = _conv_post(hc_ref[...], dwb_ref[...], lng_ref[...], lnb_ref[...])
        o_conv = _nn(c1, pw2_ref[...])
        t = _nt(y_ref[...], glu_ref[...])
        mixed = _branch_mix(o_ref[...], o_conv, t, g1_ref[...], g2_ref[...], g3_ref[...])
        out_ref[...] = x_ref[...] + _nn(mixed, wo_ref[...])

    v256 = _res((1, 256))
    return pl.pallas_call(
        body, name=f"mix_out_fwd_{tag}", grid=(L // TL,),
        in_specs=[_rows(D_MODEL), _rows(512), _rows(256), _rows(256), v256, v256, v256, _res(pw2.shape),
                  _res(glu_w.shape), _res((1, 512)), v256, v256, _res(w_out.shape)],
        out_specs=_rows(D_MODEL),
        out_shape=_sds((L, D_MODEL)),
        compiler_params=_params(1),
    )(x, o_sb, hc, y, dw_b, ln_g, ln_b, pw2, glu_w, g1, g2, g3, w_out)


def _mix_out_bwd(dx1, o_sb, hc, y, dw_b, ln_g, ln_b, pw2, glu_w, g1, g2, g3, w_out, tag):
    L = dx1.shape[0]

    def body(dx_ref, o_ref, hc_ref, y_ref, dwb_ref, lng_ref, lnb_ref, pw2_ref, glu_ref, g1_ref, g2_ref, g3_ref, wo_ref,
             do_ref, dhc_ref, dy_ref, ddwb_ref, dlng_ref, dlnb_ref, dpw2_ref, dglu_ref, dg1_ref, dg2_ref, dg3_ref,
             dwo_ref):
        i = pl.program_id(0)
        dxx = dx_ref[...]
        yy = y_ref[...]
        c1, vjp1 = jax.vjp(_conv_post, hc_ref[...], dwb_ref[...], lng_ref[...], lnb_ref[...])
        o_conv = _nn(c1, pw2_ref[...])
        t = _nt(yy, glu_ref[...])
        mixed, vjp2 = jax.vjp(_branch_mix_split, o_ref[...], o_conv, t[:, 0:SSM_CH], t[:, SSM_CH:2 * SSM_CH],
                              g1_ref[...], g2_ref[...], g3_ref[...])
        dmixed = _nt(dxx, wo_ref[...])
        do_sb, do_conv, dta, dtb, dg1, dg2, dg3 = vjp2(dmixed)
        dt = jnp.concatenate([dta, dtb], axis=1)
        dc1 = _nt(do_conv, pw2_ref[...])
        dhc, ddwb, dlng, dlnb = vjp1(dc1)
        do_ref[...] = do_sb
        dhc_ref[...] = dhc
        dy_ref[...] = _nn(dt, glu_ref[...])
        _accumulate(i, dwo_ref, _tn(mixed, dxx))
        _accumulate(i, dglu_ref, _tn(dt, yy))
        _accumulate(i, dpw2_ref, _tn(c1, do_conv))
        _accumulate(i, ddwb_ref, ddwb)
        _accumulate(i, dlng_ref, dlng)
        _accumulate(i, dlnb_ref, dlnb)
        _accumulate(i, dg1_ref, dg1)
        _accumulate(i, dg2_ref, dg2)
        _accumulate(i, dg3_ref, dg3)

    v256 = _res((1, 256))
    return pl.pallas_call(
        body, name=f"mix_out_bwd_{tag}", grid=(L // TL,),
        in_specs=[_rows(D_MODEL), _rows(512), _rows(256), _rows(256), v256, v256, v256, _res(pw2.shape),
                  _res(glu_w.shape), _res((1, 512)), v256, v256, _res(w_out.shape)],
        out_specs=[_rows(512), _rows(256), _rows(256), v256, v256, v256, _res(pw2.shape), _res(glu_w.shape),
                   _res((1, 512)), v256, v256, _res(w_out.shape)],
        out_shape=[_sds((L, 512)), _sds((L, 256)), _sds((L, 256)), _sds((1, 256)), _sds((1, 256)), _sds((1, 256)),
                   _sds(pw2.shape), _sds(glu_w.shape), _sds((1, 512)), _sds((1, 256)), _sds((1, 256)), _sds(w_out.shape)],
        compiler_params=_params(1),
    )(dx1, o_sb, hc, y, dw_b, ln_g, ln_b, pw2, glu_w, g1, g2, g3, w_out)


def _xa_heads_norm(kk, kg):
    return jnp.concatenate([_rms(kk[:, h * XA_HEAD_DIM:(h + 1) * XA_HEAD_DIM], kg) for h in range(XA_HEADS)], axis=1)


def _xa_mem_fwd(mem, g_mem, wk, wv, kg, tag):
    def body(m_ref, g_ref, wk_ref, wv_ref, kg_ref, k_ref, v_ref):
        hm = _rms(m_ref[...], g_ref[...])
        k_ref[...] = _xa_heads_norm(_nn(hm, wk_ref[...]), kg_ref[...])
        v_ref[...] = _nn(hm, wv_ref[...])

    return pl.pallas_call(body, name=f"xa_mem_fwd_{tag}", out_shape=[_sds(mem.shape), _sds(mem.shape)],
                          compiler_params=_params(0))(mem, g_mem, wk, wv, kg)


def _xa_mem_bwd(mem, g_mem, wk, wv, kg, dkx, dvx, tag):
    def body(m_ref, g_ref, wk_ref, wv_ref, kg_ref, dk_ref, dv_ref, dwk_ref, dwv_ref, dg_ref, dkg_ref):
        hm, vjp_n = jax.vjp(_rms, m_ref[...], g_ref[...])
        kk = _nn(hm, wk_ref[...])
        dvv = dv_ref[...]
        dkg = jnp.zeros((1, XA_HEAD_DIM), F32)
        parts = []
        for h in range(XA_HEADS):
            sl = slice(h * XA_HEAD_DIM, (h + 1) * XA_HEAD_DIM)
            _, vjp_h = jax.vjp(_rms, kk[:, sl], kg_ref[...])
            dkh, dgh = vjp_h(dk_ref[:, sl])
            parts.append(dkh)
            dkg = dkg + dgh
        dkk = jnp.concatenate(parts, axis=1)
        dwk_ref[...] = _tn(hm, dkk)
        dwv_ref[...] = _tn(hm, dvv)
        dhm = _nt(dkk, wk_ref[...]) + _nt(dvv, wv_ref[...])
        _, dg = vjp_n(dhm)
        dg_ref[...] = dg
        dkg_ref[...] = dkg

    return pl.pallas_call(
        body, name=f"xa_mem_bwd_{tag}",
        out_shape=[_sds(wk.shape), _sds(wv.shape), _sds((1, D_MODEL)), _sds((1, XA_HEAD_DIM))],
        compiler_params=_params(0))(mem, g_mem, wk, wv, kg, dkx, dvx)


def _xa_fwd(x1, kx, vx, g_xa, wq, qg, wo, tag, rider=None):
    L = x1.shape[0]

    def body(x_ref, k_ref, v_ref, g_ref, wq_ref, qg_ref, wo_ref, out_ref):
        xx = x_ref[...]
        qp = _nn(_rms(xx, g_ref[...]), wq_ref[...])
        outs = []
        for h in range(XA_HEADS):
            sl = slice(h * XA_HEAD_DIM, (h + 1) * XA_HEAD_DIM)
            qh = _rms(qp[:, sl], qg_ref[...])
            s = _nt(qh, k_ref[:, sl]) * (XA_HEAD_DIM ** -0.5)
            s = s - jnp.max(s, axis=-1, keepdims=True)
            e = jnp.exp(s)
            p = e / jnp.sum(e, axis=-1, keepdims=True)
            outs.append(_nn(p, v_ref[:, sl]))
        out_ref[...] = xx + _nn(jnp.concatenate(outs, axis=1), wo_ref[...])

    return _pcall(
        body, name=f"xa_fwd_{tag}", grid=(L // TL,), rider=rider,
        in_specs=[_rows(D_MODEL), _res(kx.shape), _res(vx.shape), _res((1, D_MODEL)), _res(wq.shape),
                  _res((1, XA_HEAD_DIM)), _res(wo.shape)],
        out_specs=_rows(D_MODEL),
        out_shape=_sds((L, D_MODEL)),
        compiler_params=_params(1),
    )(x1, kx, vx, g_xa, wq, qg, wo)


def _xa_bwd(x1, dx2, kx, vx, g_xa, wq, qg, wo, tag, rider=None):
    L = x1.shape[0]
    tl = 256

    def body(x_ref, dx_ref, k_ref, v_ref, g_ref, wq_ref, qg_ref, wo_ref,
             dx1_ref, dk_ref, dv_ref, dwq_ref, dwo_ref, dg_ref, dqg_ref):
        i = pl.program_id(0)
        xx = x_ref[...]
        dxx = dx_ref[...]
        hx, vjp_n = jax.vjp(_rms, xx, g_ref[...])
        qp = _nn(hx, wq_ref[...])
        do = _nt(dxx, wo_ref[...])
        heads = [dict(sl=slice(h * XA_HEAD_DIM, (h + 1) * XA_HEAD_DIM)) for h in range(XA_HEADS)]
        for c in heads:
            c['kh'], c['vh'], c['doh'] = k_ref[:, c['sl']].astype(BF16), v_ref[:, c['sl']].astype(BF16), do[:, c['sl']]
            c['qh'], c['vjp_q'] = jax.vjp(_rms, qp[:, c['sl']], qg_ref[...])
        for c in heads:
            c['s'] = _nt(c['qh'], c['kh'])
            c['dp'] = _nt(c['doh'], c['vh'])
        for c in heads:
            s = c.pop('s') * (XA_HEAD_DIM ** -0.5)
            e = jnp.exp(s - jnp.max(s, axis=-1, keepdims=True))
            c['p'] = e / jnp.sum(e, axis=-1, keepdims=True)
        for c in heads:
            c['out'] = _nn(c['p'], c['vh'])
            c['dv'] = _tn(c['p'], c['doh'])
        for c in heads:
            p, dp = c['p'], c.pop('dp')
            c['ds'] = p * (dp - jnp.sum(dp * p, axis=-1, keepdims=True)) * (XA_HEAD_DIM ** -0.5)
        for c in heads:
            c['dk'] = _tn(c['ds'], c['qh'])
            c['dq'] = _nn(c['ds'], c['kh'])
        dqg = jnp.zeros((1, XA_HEAD_DIM), F32)
        for c in heads:
            c['dqp'], dgh = c['vjp_q'](c['dq'])
            dqg = dqg + dgh
        o = jnp.concatenate([c['out'] for c in heads], axis=1)
        dqp = jnp.concatenate([c['dqp'] for c in heads], axis=1)
        dks, dvs = [c['dk'] for c in heads], [c['dv'] for c in heads]
        dxn, dg = vjp_n(_nt(dqp, wq_ref[...]))
        dx1_ref[...] = dxx + dxn
        _accumulate(i, dk_ref, jnp.concatenate(dks, axis=1))
        _accumulate(i, dv_ref, jnp.concatenate(dvs, axis=1))
        _accumulate(i, dwq_ref, _tn(hx, dqp))
        _accumulate(i, dwo_ref, _tn(o, dxx))
        _accumulate(i, dg_ref, dg)
        _accumulate(i, dqg_ref, dqg)

    r = lambda c: _rows(c, tl)
    return _pcall(
        body, name=f"xa_bwd_{tag}", grid=(L // tl,), rider=rider,
        in_specs=[r(D_MODEL), r(D_MODEL), _res(kx.shape), _res(vx.shape), _res((1, D_MODEL)), _res(wq.shape),
                  _res((1, XA_HEAD_DIM)), _res(wo.shape)],
        out_specs=[r(D_MODEL), _res(kx.shape), _res(vx.shape), _res(wq.shape), _res(wo.shape), _res((1, D_MODEL)),
                   _res((1, XA_HEAD_DIM))],
        out_shape=[_sds((L, D_MODEL)), _sds(kx.shape), _sds(vx.shape), _sds(wq.shape), _sds(wo.shape),
                   _sds((1, D_MODEL)), _sds((1, XA_HEAD_DIM))],
        compiler_params=_params(1),
    )(x1, dx2, kx, vx, g_xa, wq, qg, wo)


def _swiglu(gate, up):
    return _silu(gate) * up


def _ffn_fwd(x2, g, w_in, w_out, tag, rider=None):
    L = x2.shape[0]
    tl = 256

    def body(x_ref, g_ref, wi_ref, wo_ref, out_ref, hf_ref, gu_ref, act_ref):
        xx = x_ref[...]
        hf = _rms(xx, g_ref[...]).astype(BF16)
        gu = _nt(hf, wi_ref[...])
        act = _swiglu(gu[:, 0:FFN_HIDDEN], gu[:, FFN_HIDDEN:2 * FFN_HIDDEN]).astype(BF16)
        out_ref[...] = xx + _nn(act, wo_ref[...])
        hf_ref[...] = hf
        gu_ref[...] = gu.astype(BF16)
        act_ref[...] = act

    r = lambda c: _rows(c, tl)
    return _pcall(
        body, name=f"ffn_fwd_{tag}", grid=(L // tl,), rider=rider,
        in_specs=[r(D_MODEL), _res((1, D_MODEL)), _res(w_in.shape), _res(w_out.shape)],
        out_specs=[r(D_MODEL), r(D_MODEL), r(2 * FFN_HIDDEN), r(FFN_HIDDEN)],
        out_shape=[_sds((L, D_MODEL)), _sds((L, D_MODEL), BF16), _sds((L, 2 * FFN_HIDDEN), BF16),
                   _sds((L, FFN_HIDDEN), BF16)],
        compiler_params=_params(1, 56),
    )(x2, g, w_in, w_out)


def _ffn_bwd(x2, dx3, gu, g, w_in, w_out, tag, rider=None):
    L = x2.shape[0]
    tl = 256

    def body(x_ref, dx_ref, gu_ref, g_ref, wi_ref, wo_ref, dx2_ref, dgu_ref, dg_ref):
        i = pl.program_id(0)
        dxx = dx_ref[...]
        _, vjp_n = jax.vjp(_rms, x_ref[...], g_ref[...])
        _, vjp_a = jax.vjp(_swiglu, gu_ref[:, 0:FFN_HIDDEN].astype(F32), gu_ref[:, FFN_HIDDEN:2 * FFN_HIDDEN].astype(F32))
        dgate, dup = vjp_a(_nt(dxx, wo_ref[...]))
        dgu = jnp.concatenate([dgate, dup], axis=1).astype(BF16)
        dxn, dg = vjp_n(_nn(dgu, wi_ref[...]))
        dx2_ref[...] = dxx + dxn
        dgu_ref[...] = dgu
        _accumulate(i, dg_ref, dg)

    r = lambda c: _rows(c, tl)
    return _pcall(
        body, name=f"ffn_bwd_{tag}", grid=(L // tl,), rider=rider,
        in_specs=[r(D_MODEL), r(D_MODEL), r(2 * FFN_HIDDEN), _res((1, D_MODEL)), _res(w_in.shape), _res(w_out.shape)],
        out_specs=[r(D_MODEL), r(2 * FFN_HIDDEN), _res((1, D_MODEL))],
        out_shape=[_sds((L, D_MODEL)), _sds((L, 2 * FFN_HIDDEN), BF16), _sds((1, D_MODEL))],
        compiler_params=_params(1, 56),
    )(x2, dx3, gu, g, w_in, w_out)


def _matmul_tn(a, b, tm, tn, tag):
    L, M = a.shape
    N = b.shape[1]
    tk = min(L, 2048)
    nk = L // tk

    def body(a_ref, b_ref, o_ref, acc_ref):
        k = pl.program_id(2)
        _accumulate(k, acc_ref, _tn(a_ref[...], b_ref[...]))

        @pl.when(k == nk - 1)
        def _():
            o_ref[...] = acc_ref[...].astype(BF16)

    return pl.pallas_call(
        body, name=f"matmul_tn_{tag}", grid=(M // tm, N // tn, nk),
        in_specs=[pl.BlockSpec((tk, tm), lambda m, n, k: (k, m)), pl.BlockSpec((tk, tn), lambda m, n, k: (k, n))],
        out_specs=pl.BlockSpec((tm, tn), lambda m, n, k: (m, n)),
        out_shape=_sds((M, N), BF16),
        scratch_shapes=[pltpu.VMEM((tm, tn), F32)],
        compiler_params=_params(3),
    )(a, b)


def _loss_head(y, tgt):
    L = y.shape[0]
    n_tiles = L // TL

    def body(y_ref, t_ref, loss_ref, dy_ref, acc_ref):
        i = pl.program_id(0)
        diff = y_ref[...] - t_ref[...]
        dy_ref[...] = diff * (1.0 / D_MODEL)
        _accumulate(i, acc_ref, jnp.sum(diff * diff, axis=0, keepdims=True))

        @pl.when(i == n_tiles - 1)
        def _():
            loss_ref[...] = jnp.sum(acc_ref[...], axis=1, keepdims=True) * (0.5 / D_MODEL)

    return pl.pallas_call(
        body, name="loss_head", grid=(n_tiles,),
        in_specs=[_rows(D_MODEL), _rows(D_MODEL)],
        out_specs=[_res((1, 1)), _rows(D_MODEL)],
        out_shape=[_sds((1, 1)), _sds((L, D_MODEL))],
        scratch_shapes=[pltpu.VMEM((1, D_MODEL), F32)],
        compiler_params=_params(1),
    )(y, tgt)


def _row(v):
    return v.reshape(1, -1)


def _layer_consts():
    r = np.arange(SB_WIDTH)
    mavg = ((r[:, None] // SB_HEAD_DIM) == (r[None, :] // SB_HEAD_DIM)).astype(np.float32) / SB_HEAD_DIM
    ul, ue = _sb_tri_consts()
    return dict(mavg=jnp.asarray(mavg), ul=ul, ue=ue, mask=_ssm_mask())


def _ssm_rows(P):
    lanes = lambda a: a.reshape(1, SSM_LANES)
    return dict(
        lr=lanes(P['ssm_lam_re']), li=lanes(P['ssm_lam_im']),
        ldt=lanes(jnp.repeat(P['ssm_log_dt'], SSM_STATE)),
        brt=P['ssm_b_re'].transpose(2, 0, 1).reshape(SSM_GROUP, SSM_LANES),
        bit=P['ssm_b_im'].transpose(2, 0, 1).reshape(SSM_GROUP, SSM_LANES),
        crt=P['ssm_c_re'].transpose(1, 0, 2).reshape(SSM_GROUP, SSM_LANES),
        cit=P['ssm_c_im'].transpose(1, 0, 2).reshape(SSM_GROUP, SSM_LANES))


def _layer_fwd(x, mem, P, C, tag, ride):
    gq = _row(jnp.tile(P['sb_q_norm_g'], SB_WIDTH // SB_HEAD_DIM))
    gk = _row(jnp.tile(P['sb_k_norm_g'], SB_WIDTH // SB_HEAD_DIM))
    q, k, v, hg, u = _mix_in_fwd(x, _row(P['norm_mix_g']), P['w_in'], gq, gk, C['mavg'], tag,
                                 rider=ride("mix_in_fwd_" + tag))
    o_sb, rsave, n_done = _sb_fwd(q, k, v, C['ul'], tag, rider=ride("sb_fwd_" + tag))
    hp = jnp.pad(hg, ((CONV_HALO, 0), (0, 0)))
    hc = _conv_fwd(hp, P['conv_dw_w'].T, tag)
    S = _ssm_rows(P)
    ar, ai, bbr, bbi, cbr, cbi = _ssm_prep(S['lr'], S['li'], S['ldt'], S['brt'], S['bit'], S['crt'], S['cit'],
                                           C['mask'], tag)
    y, xr, xi = _ssm_fwd(u, ar, ai, bbr, bbi, cbr, cbi, _row(P['ssm_d']), tag, rider=ride("ssm_fwd_" + tag))
    gb = P['branch_norm_g']
    x1 = _mix_out_fwd(x, o_sb, hc, y, _row(P['conv_dw_b']), _row(P['conv_ln_g']), _row(P['conv_ln_b']),
                      P['conv_pw2_w'], P['ssm_glu_w'], _row(gb[0:512]), _row(gb[512:768]), _row(gb[768:1024]),
                      P['w_out'], tag)
    kx, vx = _xa_mem_fwd(mem, _row(P['norm_mem_g']), P['xa_wk'], P['xa_wv'], _row(P['xa_k_norm_g']), tag)
    x2 = _xa_fwd(x1, kx, vx, _row(P['norm_xa_g']), P['xa_wq'], _row(P['xa_q_norm_g']), P['xa_wo'], tag,
                 rider=ride("xa_fwd_" + tag))
    x3, hf, gu, act = _ffn_fwd(x2, _row(P['norm_ffn_g']), P['ffn_w_in'], P['ffn_w_out'], tag,
                               rider=ride("ffn_fwd_" + tag))
    saved = dict(hf=hf, gu=gu, act=act, x=x, q=q, k=k, v=v, rsave=rsave, n_done=n_done, o_sb=o_sb, hp=hp, hc=hc, u=u, y=y, xr=xr, xi=xi, x1=x1, x2=x2,
                 kx=kx, vx=vx, gq=gq, gk=gk, S=S, ssm=(ar, ai, bbr, bbi, cbr, cbi))
    return x3, saved


def _layer_bwd(dx3, mem, P, C, sv, tag, ride, G):
    dx2, dgu, dg = _ffn_bwd(sv['x2'], dx3, sv['gu'], _row(P['norm_ffn_g']), P['ffn_w_in'], P['ffn_w_out'], tag,
                            rider=ride("ffn_bwd_" + tag))
    G['norm_ffn_g'] = dg.reshape(-1)
    G['ffn_w_in'] = _matmul_tn(dgu, sv['hf'], 2 * FFN_HIDDEN // 4, D_MODEL, "ffn_in_" + tag)
    G['ffn_w_out'] = _matmul_tn(sv['act'], dx3, FFN_HIDDEN // 2, 512, "ffn_out_" + tag)
    dx1, dkx, dvx, dwq, dwo, dg, dqg = _xa_bwd(sv['x1'], dx2, sv['kx'], sv['vx'], _row(P['norm_xa_g']), P['xa_wq'],
                                               _row(P['xa_q_norm_g']), P['xa_wo'], tag, rider=ride("xa_bwd_" + tag))
    G['xa_wq'], G['xa_wo'], G['norm_xa_g'], G['xa_q_norm_g'] = dwq, dwo, dg.reshape(-1), dqg.reshape(-1)
    dwk, dwv, dg, dkg = _xa_mem_bwd(mem, _row(P['norm_mem_g']), P['xa_wk'], P['xa_wv'], _row(P['xa_k_norm_g']),
                                    dkx, dvx, tag)
    G['xa_wk'], G['xa_wv'], G['norm_mem_g'], G['xa_k_norm_g'] = dwk, dwv, dg.reshape(-1), dkg.reshape(-1)
    gb = P['branch_norm_g']
    (do_sb, dhc, dy, ddwb, dlng, dlnb, dpw2, dglu, dg1, dg2, dg3, dwout) = _mix_out_bwd(
        dx1, sv['o_sb'], sv['hc'], sv['y'], _row(P['conv_dw_b']), _row(P['conv_ln_g']), _row(P['conv_ln_b']),
        P['conv_pw2_w'], P['ssm_glu_w'], _row(gb[0:512]), _row(gb[512:768]), _row(gb[768:1024]), P['w_out'], tag)
    G['conv_dw_b'], G['conv_ln_g'], G['conv_ln_b'] = ddwb.reshape(-1), dlng.reshape(-1), dlnb.reshape(-1)
    G['conv_pw2_w'], G['ssm_glu_w'], G['w_out'] = dpw2, dglu, dwout
    G['branch_norm_g'] = jnp.concatenate([dg1.reshape(-1), dg2.reshape(-1), dg3.reshape(-1)])
    ar, ai, bbr, bbi, cbr, cbi = sv['ssm']
    du, dar, dai, dbbr, dbbi, dcbr, dcbi, dd = _ssm_bwd(dy, sv['u'], sv['xr'], sv['xi'], ar, ai, bbr, bbi, cbr, cbi,
                                                        _row(P['ssm_d']), tag, rider=ride("ssm_bwd_" + tag))
    S = sv['S']
    dlr, dli, dldt, dbrt, dbit, dcrt, dcit = _ssm_prep_bwd(S['lr'], S['li'], S['ldt'], S['brt'], S['bit'], C['mask'],
                                                           dar, dai, dbbr, dbbi, dcbr, dcbi, tag)
    G['ssm_lam_re'] = dlr.reshape(SSM_GROUPS, SSM_STATE)
    G['ssm_lam_im'] = dli.reshape(SSM_GROUPS, SSM_STATE)
    G['ssm_log_dt'] = dldt.reshape(SSM_GROUPS, SSM_STATE).sum(axis=1)
    G['ssm_b_re'] = dbrt.reshape(SSM_GROUP, SSM_GROUPS, SSM_STATE).transpose(1, 2, 0)
    G['ssm_b_im'] = dbit.reshape(SSM_GROUP, SSM_GROUPS, SSM_STATE).transpose(1, 2, 0)
    G['ssm_c_re'] = dcrt.reshape(SSM_GROUP, SSM_GROUPS, SSM_STATE).transpose(1, 0, 2)
    G['ssm_c_im'] = dcit.reshape(SSM_GROUP, SSM_GROUPS, SSM_STATE).transpose(1, 0, 2)
    G['ssm_d'] = dd.reshape(-1)
    dpad = jnp.pad(dhc, ((0, CONV_HALO), (0, 0)))
    dhg, ddww = _conv_bwd(dpad, sv['hp'], P['conv_dw_w'].T, tag)
    G['conv_dw_w'] = ddww.T
    dq, dk, dv = _sb_bwd(sv['n_done'], sv['q'], sv['k'], sv['v'], sv['rsave'], do_sb, C['ul'], C['ue'], tag)
    dx, dwin, dg, dgq, dgk = _mix_in_bwd(sv['x'], dx1, dq, dk, dv, dhg, du, _row(P['norm_mix_g']), P['w_in'],
                                         sv['gq'], sv['gk'], C['mavg'], tag)
    G['w_in'], G['norm_mix_g'] = dwin, dg.reshape(-1)
    G['sb_q_norm_g'] = dgq.reshape(SB_WIDTH // SB_HEAD_DIM, SB_HEAD_DIM).sum(axis=0)
    G['sb_k_norm_g'] = dgk.reshape(SB_WIDTH // SB_HEAD_DIM, SB_HEAD_DIM).sum(axis=0)
    return dx, G


def _slot_sum(r_ref):
    g = r_ref[0].astype(F32)
    for s in range(1, r_ref.shape[0]):
        g = g + r_ref[s].astype(F32)
    return g


def _adam_update(g, w, m, v):
    nm = ADAM_B1 * m + (1.0 - ADAM_B1) * g
    nv = ADAM_B2 * v + (1.0 - ADAM_B2) * (g * g)
    m_hat = nm * (1.0 / (1.0 - ADAM_B1 ** ADAM_STEP))
    v_hat = nv * (1.0 / (1.0 - ADAM_B2 ** ADAM_STEP))
    return -ADAM_LR * (m_hat / (jnp.sqrt(v_hat) + ADAM_EPS) + ADAM_WD * w), nm, nv


def _adamw(recv, w, m, v, tile, name):
    n_slots, R, C = recv.shape

    def body(r_ref, w_ref, m_ref, v_ref, g_ref, d_ref, nm_ref, nv_ref):
        g = _slot_sum(r_ref)
        g_ref[...] = g
        d_ref[...], nm_ref[...], nv_ref[...] = _adam_update(g, w_ref[...], m_ref[...], v_ref[...])

    rows = pl.BlockSpec((tile, C), lambda i: (i, 0))
    out = _sds((R, C))
    return pl.pallas_call(
        body, name=name, grid=(R // tile,),
        in_specs=[pl.BlockSpec((n_slots, tile, C), lambda i: (0, i, 0)), rows, rows, rows],
        out_specs=[rows, rows, rows, rows],
        out_shape=[out, out, out, out],
        compiler_params=_params(1),
    )(recv, w, m, v)


def _adamw_layers(recvs, w, m, v, tile, name):
    n_slots, R, C = recvs[0].shape

    def body(*refs):
        r_refs = refs[:DEPTH]
        w_ref, m_ref, v_ref, g_ref, d_ref, nm_ref, nv_ref = refs[DEPTH:]
        for l in range(DEPTH):
            @pl.when(pl.program_id(0) == l)
            def _(l=l):
                g = _slot_sum(r_refs[l])
                g_ref[0] = g
                d_ref[0], nm_ref[0], nv_ref[0] = _adam_update(g, w_ref[0], m_ref[0], v_ref[0])

    rspec = lambda l: pl.BlockSpec((n_slots, tile, C), lambda ll, i: (0, jnp.where(ll == l, i, 0), 0))
    rows = pl.BlockSpec((1, tile, C), lambda ll, i: (ll, i, 0))
    out = _sds((DEPTH, R, C))
    return pl.pallas_call(
        body, name=name, grid=(DEPTH, R // tile),
        in_specs=[rspec(l) for l in range(DEPTH)] + [rows, rows, rows],
        out_specs=[rows, rows, rows, rows],
        out_shape=[out, out, out, out],
        compiler_params=_params(2),
    )(*recvs, w, m, v)


def _reduce_slots(recv, tile, name):
    n_slots, R, C = recv.shape

    def body(r_ref, g_ref):
        g_ref[...] = _slot_sum(r_ref)

    return pl.pallas_call(
        body, name=name, grid=(R // tile,),
        in_specs=[pl.BlockSpec((n_slots, tile, C), lambda i: (0, i, 0))],
        out_specs=pl.BlockSpec((tile, C), lambda i: (i, 0)),
        out_shape=_sds((R, C)),
        compiler_params=_params(1),
    )(recv)


SEG = SUBLANES * LANES


def _pad_to(n, mult):
    return -(-n // mult) * mult


def _pack(arrays, dtype, row_mult, lead=0):
    keep = [(0, 0)] * lead
    parts = []
    for a in arrays:
        flat = a.reshape(a.shape[:lead] + (-1,)).astype(dtype)
        n = flat.shape[-1]
        parts.append(jnp.pad(flat, keep + [(0, _pad_to(n, SEG) - n)]))
    flat = jnp.concatenate(parts, axis=-1)
    n = flat.shape[-1]
    flat = jnp.pad(flat, keep + [(0, _pad_to(n, row_mult * LANES) - n)])
    return flat.reshape(flat.shape[:lead] + (-1, LANES))


def _unpack(buf, shapes):
    lead = buf.shape[:-2]
    flat = buf.reshape(lead + (-1,))
    out, off = [], 0
    for shp in shapes:
        n = int(np.prod(shp))
        out.append(flat[..., off:off + n].reshape(lead + tuple(shp)))
        off += _pad_to(n, SEG)
    return out


def _rows_first(a, name):
    return a.transpose(0, 2, 1) if SHARD_AXIS[name] == 2 else a


SMALL_TILE = 256
DIRECT_NAMES = [n for n in BIG_NAMES if n != 'conv_dw_w']
GATHER_RIDES = {
    "mix_in_fwd_l0": [(0, 'conv_pw2_w'), (0, 'ssm_glu_w'), (0, 'w_out')],
    "sb_fwd_l0": [(0, 'xa_wq'), (0, 'xa_wk'), (0, 'xa_wv'), (0, 'xa_wo'), (0, 'ffn_w_in')],
    "ssm_fwd_l0": [(0, 'ffn_w_out')],
    "xa_fwd_l0": [(1, 'w_in'), (1, 'conv_pw2_w'), (1, 'ssm_glu_w')],
    "ffn_fwd_l0": [(1, 'w_out'), (1, 'xa_wq'), (1, 'xa_wk'), (1, 'xa_wv'), (1, 'xa_wo')],
    "sb_fwd_l1": [(1, 'ffn_w_in'), (1, 'ffn_w_out')],
}
_MID = ['xa_wq', 'xa_wo', 'xa_wk', 'xa_wv', 'w_out', 'conv_pw2_w', 'ssm_glu_w']
SCATTER_RIDES = {
    "xa_bwd_l1": [(1, 'ffn_w_in'), (1, 'ffn_w_out')],
    "ssm_bwd_l1": [(1, n) for n in _MID],
    "ffn_bwd_l0": [(1, 'w_in')],
    "xa_bwd_l0": [(0, 'ffn_w_in'), (0, 'ffn_w_out')],
    "ssm_bwd_l0": [(0, n) for n in _MID],
}


def _tile_rows(rows):
    return next(t for t in range(min(rows, 256), 0, -ROW_ALIGN) if rows % t == 0 and t % ROW_ALIGN == 0)


class _LayerWeights:
    def __init__(self, layer, small, full, conv):
        self.layer, self.small, self.full, self.conv = layer, small, full, conv

    def __getitem__(self, name):
        if name == 'conv_dw_w':
            return self.conv[self.layer]
        return self.full[(self.layer, name)] if name in SHARD_AXIS else self.small[name][self.layer]


def kernel(x, mem, norm_mix_g, w_in, sb_q_norm_g, sb_k_norm_g, conv_dw_w, conv_dw_b, conv_ln_g, conv_ln_b, conv_pw2_w, ssm_lam_re, ssm_lam_im, ssm_log_dt, ssm_b_re, ssm_b_im, ssm_c_re, ssm_c_im, ssm_d, ssm_glu_w, branch_norm_g, w_out, norm_xa_g, norm_mem_g, xa_wq, xa_wk, xa_wv, xa_q_norm_g, xa_k_norm_g, xa_wo, norm_ffn_g, ffn_w_in, ffn_w_out, loss_target, m_norm_mix_g, m_w_in, m_sb_q_norm_g, m_sb_k_norm_g, m_conv_dw_w, m_conv_dw_b, m_conv_ln_g, m_conv_ln_b, m_conv_pw2_w, m_ssm_lam_re, m_ssm_lam_im, m_ssm_log_dt, m_ssm_b_re, m_ssm_b_im, m_ssm_c_re, m_ssm_c_im, m_ssm_d, m_ssm_glu_w, m_branch_norm_g, m_w_out, m_norm_xa_g, m_norm_mem_g, m_xa_wq, m_xa_wk, m_xa_wv, m_xa_q_norm_g, m_xa_k_norm_g, m_xa_wo, m_norm_ffn_g, m_ffn_w_in, m_ffn_w_out, v_norm_mix_g, v_w_in, v_sb_q_norm_g, v_sb_k_norm_g, v_conv_dw_w, v_conv_dw_b, v_conv_ln_g, v_conv_ln_b, v_conv_pw2_w, v_ssm_lam_re, v_ssm_lam_im, v_ssm_log_dt, v_ssm_b_re, v_ssm_b_im, v_ssm_c_re, v_ssm_c_im, v_ssm_d, v_ssm_glu_w, v_branch_norm_g, v_w_out, v_norm_xa_g, v_norm_mem_g, v_xa_wq, v_xa_wk, v_xa_wv, v_xa_q_norm_g, v_xa_k_norm_g, v_xa_wo, v_norm_ffn_g, v_ffn_w_in, v_ffn_w_out):
    args = locals()
    w_loc = {n: args[n] for n in WEIGHT_NAMES}
    m_loc = {n: args["m_" + n] for n in WEIGHT_NAMES}
    v_loc = {n: args["v_" + n] for n in WEIGHT_NAMES}
    me = _my_index()
    shard = {(l, n): _rows_first(w_loc[n], n)[l].astype(BF16) for l in range(DEPTH) for n in DIRECT_NAMES}
    conv_shape = _rows_first(w_loc['conv_dw_w'], 'conv_dw_w').shape
    conv_rows = _pack([_rows_first(w_loc['conv_dw_w'], 'conv_dw_w')], F32, SUBLANES)
    full = {}
    recv = {}
    grads = [dict() for _ in range(DEPTH)]

    def ride(kernel_name):
        if kernel_name in GATHER_RIDES:
            keys = GATHER_RIDES[kernel_name]
            return _Rider(gathers=[shard[k] for k in keys], done=lambda res: full.update(zip(keys, res)))
        if kernel_name in SCATTER_RIDES:
            keys = SCATTER_RIDES[kernel_name]
            return _Rider(scatters=[grads[l][n].astype(BF16) for (l, n) in keys],
                          done=lambda res: recv.update(zip(keys, res)))
        return None

    first = []
    _exchange(_Rider(gathers=[shard[(0, 'w_in')], conv_rows], done=first.extend), "gather_first")
    full[(0, 'w_in')] = first[0]
    conv_all = first[1].reshape(N_DEV, -1)[:, :int(np.prod(conv_shape))].reshape((N_DEV,) + conv_shape)
    conv_full = conv_all.transpose(1, 0, 2, 3).reshape(DEPTH, N_DEV * conv_shape[1], conv_shape[2])
    weights = [_LayerWeights(l, w_loc, full, conv_full) for l in range(DEPTH)]

    consts = _layer_consts()
    h, saved = x[0], []
    for l in range(DEPTH):
        h, sv = _layer_fwd(h, mem[0], weights[l], consts, f"l{l}", ride)
        saved.append(sv)
    loss_part, dh = _loss_head(h, loss_target[0])
    for l in reversed(range(DEPTH)):
        dh, _ = _layer_bwd(dh, mem[0], weights[l], consts, saved[l], f"l{l}", ride, grads[l])
    grad_x = dh
    loss = lax.psum(loss_part[0, 0], MESH_AXES)

    small_shapes = [w_loc[n].shape for n in SMALL_NAMES]
    conv_nat = (DEPTH,) + grads[0]['conv_dw_w'].shape[::-1]
    small_send = _pack([jnp.stack([grads[l][n] for l in range(DEPTH)]) for n in SMALL_NAMES]
                       + [jnp.stack([grads[l]['conv_dw_w'].T for l in range(DEPTH)])], F32, SMALL_TILE)
    last = []
    _exchange(_Rider(gathers=[small_send], scatters=[grads[0]['w_in'].astype(BF16)], done=last.extend), "exchange_last")
    recv[(0, 'w_in')] = last[1]
    small_sum = _reduce_slots(last[0].reshape((N_DEV,) + small_send.shape), SMALL_TILE, "reduce_replicated")
    small_g = _unpack(small_sum, small_shapes + [conv_nat])
    conv_cols = w_loc['conv_dw_w'].shape[2]
    conv_g = lax.dynamic_slice_in_dim(small_g[-1], me * conv_cols, conv_cols, axis=2)

    result = [{}, {}, {}, {}]
    for n in DIRECT_NAMES:
        parts = [recv[(l, n)] for l in range(DEPTH)]
        if SHARD_AXIS[n] == 2:
            parts = [_reduce_slots(p, _tile_rows(p.shape[1]), f"reduce_{n}_l{l}").T[None] for l, p in enumerate(parts)]
        outs = _adamw_layers(parts, w_loc[n], m_loc[n], v_loc[n], _tile_rows(w_loc[n].shape[1]), f"adamw_{n}")
        for kind in range(4):
            result[kind][n] = outs[kind]
    packed_names = SMALL_NAMES + ['conv_dw_w']
    pk = lambda d: _pack([d[n] for n in packed_names], F32, SMALL_TILE)
    outs = _adamw(_pack(small_g[:-1] + [conv_g], F32, SMALL_TILE)[None], pk(w_loc), pk(m_loc), pk(v_loc), SMALL_TILE,
                  "adamw_packed")
    for kind in range(4):
        for n, a in zip(packed_names, _unpack(outs[kind], [w_loc[n].shape for n in packed_names])):
            result[kind][n] = a
    return (loss, grad_x[None], *[result[0][n] for n in WEIGHT_NAMES], *[result[1][n] for n in WEIGHT_NAMES],
            *[result[2][n] for n in WEIGHT_NAMES], *[result[3][n] for n in WEIGHT_NAMES])
```

```python
import functools

import numpy as np
import jax
import jax.numpy as jnp
from jax import lax
from jax.experimental import pallas as pl
from jax.experimental.pallas import tpu as pltpu

F32 = jnp.float32
BF16 = jnp.bfloat16
EPS = 1e-6
D_MODEL = 1024
DEPTH = 2
N_DEV = 8
SB_WIDTH = 512
SB_HEAD_DIM = 64
CONV_CH = 256
CONV_WIDTH = 31
SSM_CH = 256
SSM_GROUP = 16
SSM_GROUPS = 16
SSM_STATE = 64
SSM_LANES = SSM_GROUPS * SSM_STATE
XA_HEADS = 4
XA_HEAD_DIM = 256
FFN_HIDDEN = 2816
ADAM_LR = 0.001
ADAM_B1 = 0.9
ADAM_B2 = 0.999
ADAM_EPS = 1e-08
ADAM_WD = 0.01
ADAM_STEP = 10

LANES = 128
SUBLANES = 8
TL = 512
SB_TQ = 256
SB_TK = 128
SB_PAIRS = 2
SB_SCALE = SB_HEAD_DIM ** -0.5
SB_DEAD = -120.0
SSM_T = 256
CONV_HALO = 32
CONV_SUB = 64
VMEM_MB = 48

MESH_AXES = ("x", "y", "c")
WEIGHT_NAMES = ['norm_mix_g', 'w_in', 'sb_q_norm_g', 'sb_k_norm_g', 'conv_dw_w', 'conv_dw_b', 'conv_ln_g',
                'conv_ln_b', 'conv_pw2_w', 'ssm_lam_re', 'ssm_lam_im', 'ssm_log_dt', 'ssm_b_re', 'ssm_b_im',
                'ssm_c_re', 'ssm_c_im', 'ssm_d', 'ssm_glu_w', 'branch_norm_g', 'w_out', 'norm_xa_g',
                'norm_mem_g', 'xa_wq', 'xa_wk', 'xa_wv', 'xa_q_norm_g', 'xa_k_norm_g', 'xa_wo', 'norm_ffn_g',
                'ffn_w_in', 'ffn_w_out']
SHARD_AXIS = {'w_in': 2, 'conv_dw_w': 2, 'conv_pw2_w': 1, 'ssm_glu_w': 2, 'w_out': 1, 'xa_wq': 1, 'xa_wk': 1,
              'xa_wv': 1, 'xa_wo': 1, 'ffn_w_in': 2, 'ffn_w_out': 1}
BIG_NAMES = [n for n in WEIGHT_NAMES if n in SHARD_AXIS]
SMALL_NAMES = [n for n in WEIGHT_NAMES if n not in SHARD_AXIS]


def _nn(a, b):
    return jnp.dot(a.astype(BF16), b.astype(BF16), preferred_element_type=F32)


def _nt(a, b):
    return lax.dot_general(a.astype(BF16), b.astype(BF16), (((1,), (1,)), ((), ())), preferred_element_type=F32)


def _tn(a, b):
    return lax.dot_general(a.astype(BF16), b.astype(BF16), (((0,), (0,)), ((), ())), preferred_element_type=F32)


def _rms(x, g):
    return x * lax.rsqrt(jnp.mean(x * x, axis=-1, keepdims=True) + EPS) * g


def _sigmoid(x):
    return 1.0 / (1.0 + jnp.exp(-x))


def _silu(x):
    return x * _sigmoid(x)


def _layer_norm(x, g, b):
    mu = jnp.mean(x, axis=-1, keepdims=True)
    xc = x - mu
    var = jnp.mean(xc * xc, axis=-1, keepdims=True)
    return xc * lax.rsqrt(var + EPS) * g + b


def _head_rms64(p, g, mavg):
    ms = jnp.dot(p * p, mavg, preferred_element_type=F32)
    return p * lax.rsqrt(ms + EPS) * g


def _params(n_grid, vmem_mb=VMEM_MB):
    return pltpu.CompilerParams(dimension_semantics=("arbitrary",) * n_grid, vmem_limit_bytes=vmem_mb << 20)


def _rows(cols, tl=TL):
    return pl.BlockSpec((tl, cols), lambda i: (i, 0))


def _res(shape):
    nd = len(shape)
    return pl.BlockSpec(tuple(shape), lambda *_: (0,) * nd)


def _sds(shape, dtype=F32):
    return jax.ShapeDtypeStruct(tuple(shape), dtype)


def _accumulate(i, ref, val):
    @pl.when(i == 0)
    def _():
        ref[...] = val

    @pl.when(i > 0)
    def _():
        ref[...] += val


HBM = pl.BlockSpec(memory_space=pltpu.HBM)
ROW_ALIGN = 16


def _my_index():
    return lax.axis_index("x") * 4 + lax.axis_index("y") * 2 + lax.axis_index("c")


def _peer(k):
    x, y, c = lax.axis_index("x"), lax.axis_index("y"), lax.axis_index("c")
    return (x ^ ((k >> 2) & 1), y ^ ((k >> 1) & 1), c ^ (k & 1))


class _Rider:
    def __init__(self, gathers=(), scatters=(), done=None):
        self.gathers, self.scatters, self.done = list(gathers), list(scatters), done

    @property
    def inputs(self):
        return self.gathers + self.scatters

    def out_shapes(self):
        return ([_sds((N_DEV * a.shape[0],) + a.shape[1:], a.dtype) for a in self.gathers]
                + [_sds((N_DEV, a.shape[0] // N_DEV) + a.shape[1:], a.dtype) for a in self.scatters])


def _rider_copies(rider, in_refs, out_refs, send_sems, recv_sems, local_sems):
    me = _my_index()
    local, sends, recvs = [], [], []
    for t, (src, dst) in enumerate(zip(in_refs, out_refs)):
        gather = t < len(rider.gathers)
        rows = src.shape[0] if gather else src.shape[0] // N_DEV

        def block(ref, d, rows=rows):
            return ref.at[pl.ds(pl.multiple_of(d * rows, ROW_ALIGN), rows)]

        src_for = (lambda p, src=src: src) if gather else (lambda p, src=src: block(src, p))
        dst_for = (lambda d, dst=dst: block(dst, d)) if gather else (lambda d, dst=dst: dst.at[d])
        local.append(pltpu.make_async_copy(src_for(me), dst_for(me), local_sems.at[t]))
        for k in range(1, N_DEV):
            args = dict(send_sem=send_sems.at[t * N_DEV + k], recv_sem=recv_sems.at[t * N_DEV + k], device_id=_peer(k),
                        device_id_type=pl.DeviceIdType.MESH)
            sends.append(pltpu.make_async_remote_copy(src_ref=src_for(me ^ k), dst_ref=dst_for(me), **args))
            recvs.append(pltpu.make_async_remote_copy(src_ref=src_for(me ^ k), dst_ref=dst_for(me ^ k), **args))
    return local, sends, recvs


def _pcall(body, *, name, out_shape, grid=(), in_specs=None, out_specs=None, scratch_shapes=(), compiler_params=None,
           rider=None):
    if rider is None or not rider.inputs:
        return pl.pallas_call(body, name=name, grid=grid, in_specs=in_specs, out_specs=out_specs, out_shape=out_shape,
                              scratch_shapes=list(scratch_shapes), compiler_params=compiler_params)
    single = not isinstance(out_shape, (list, tuple))
    outs = [out_shape] if single else list(out_shape)
    ospecs = [out_specs] if single else list(out_specs)
    n_in, n_out, n_scr, n_r = len(in_specs), len(outs), len(scratch_shapes), len(rider.inputs)

    def wrapped(*refs):
        ins, rin = refs[:n_in], refs[n_in:n_in + n_r]
        own_out = refs[n_in + n_r:n_in + n_r + n_out]
        rout = refs[n_in + n_r + n_out:n_in + 2 * n_r + n_out]
        scratch = refs[n_in + 2 * n_r + n_out:n_in + 2 * n_r + n_out + n_scr]
        local, sends, recvs = _rider_copies(rider, rin, rout, *refs[-3:])

        def start():
            for cp in local + sends:
                cp.start()

        def wait():
            for cp in recvs:
                cp.wait_recv()
            for cp in sends:
                cp.wait_send()
            for cp in local:
                cp.wait()

        if grid:
            ids = [pl.program_id(a) for a in range(len(grid))]
            first = functools.reduce(jnp.logical_and, [i == 0 for i in ids])
            last = functools.reduce(jnp.logical_and, [i == n - 1 for i, n in zip(ids, grid)])
            pl.when(first)(start)
            body(*ins, *own_out, *scratch)
            pl.when(last)(wait)
        else:
            start()
            body(*ins, *own_out, *scratch)
            wait()

    call = pl.pallas_call(
        wrapped, name=name, grid=grid, in_specs=list(in_specs) + [HBM] * n_r, out_specs=ospecs + [HBM] * n_r,
        out_shape=outs + rider.out_shapes(),
        scratch_shapes=list(scratch_shapes) + [pltpu.SemaphoreType.DMA((n_r * N_DEV,)),
                                               pltpu.SemaphoreType.DMA((n_r * N_DEV,)), pltpu.SemaphoreType.DMA((n_r,))],
        compiler_params=compiler_params)

    def run(*args):
        res = call(*args, *rider.inputs)
        if rider.done is not None:
            rider.done(list(res[n_out:]))
        return res[0] if single else list(res[:n_out])

    return run


def _exchange(rider, name):
    def body():
        pass

    _pcall(body, name=name, out_shape=[], in_specs=[], out_specs=[], rider=rider)()


def _mixin_post(pq, pk, a, b, gq, gk, mavg):
    return _head_rms64(pq, gq, mavg), _head_rms64(pk, gk, mavg), a * _sigmoid(b)


def _mix_in_fwd(x, g_mix, w_in, gq, gk, mavg, tag, rider=None):
    L = x.shape[0]

    def body(x_ref, g_ref, w_ref, gq_ref, gk_ref, mavg_ref, q_ref, k_ref, v_ref, hg_ref, u_ref, p_ref):
        h = _rms(x_ref[...], g_ref[...])
        p = _nt(h, w_ref[...])
        q, k, hg = _mixin_post(p[:, 0:512], p[:, 512:1024], p[:, 1536:1792], p[:, 1792:2048],
                               gq_ref[...], gk_ref[...], mavg_ref[...])
        q_ref[...] = (q * SB_SCALE).astype(BF16)
        k_ref[...] = k.astype(BF16)
        v_ref[...] = p[:, 1024:1536].astype(BF16)
        hg_ref[...] = hg
        u_ref[...] = p[:, 2048:2304]
        p_ref[...] = p.astype(BF16)

    return _pcall(
        body, name=f"mix_in_fwd_{tag}", grid=(L // TL,),
        in_specs=[_rows(D_MODEL), _res((1, D_MODEL)), _res(w_in.shape), _res((1, 512)), _res((1, 512)), _res((512, 512))],
        out_specs=[_rows(512), _rows(512), _rows(512), _rows(256), _rows(256), _rows(w_in.shape[0])],
        out_shape=[_sds((L, 512), BF16), _sds((L, 512), BF16), _sds((L, 512), BF16), _sds((L, 256)), _sds((L, 256)),
                   _sds((L, w_in.shape[0]), BF16)],
        compiler_params=_params(1), rider=rider,
    )(x, g_mix, w_in, gq, gk, mavg)


def _mix_in_bwd(x, dres, p_kept, dq, dk, dv, dhg, du, g_mix, w_in, gq, gk, mavg, tag):
    L = x.shape[0]
    tl = 256

    def body(x_ref, dres_ref, p_ref, dq_ref, dk_ref, dv_ref, dhg_ref, du_ref, g_ref, w_ref, gq_ref, gk_ref, mavg_ref,
             dx_ref, dw_ref, dg_ref, dgq_ref, dgk_ref):
        i = pl.program_id(0)
        xx = x_ref[...]
        g = g_ref[...]
        mavg_v = mavg_ref[...]
        h, vjp_n = jax.vjp(_rms, xx, g)
        p = p_ref[...].astype(F32)
        _, vjp_p = jax.vjp(lambda pq, pk, a, b, gq_, gk_: _mixin_post(pq, pk, a, b, gq_, gk_, mavg_v),
                           p[:, 0:512], p[:, 512:1024], p[:, 1536:1792], p[:, 1792:2048], gq_ref[...], gk_ref[...])
        dpq, dpk, da, db, dgq, dgk = vjp_p((dq_ref[...], dk_ref[...], dhg_ref[...]))
        dp = jnp.concatenate([dpq, dpk, dv_ref[...], da, db, du_ref[...]], axis=1)
        dh = _nn(dp, w_ref[...])
        dxn, dg = vjp_n(dh)
        dx_ref[...] = dres_ref[...] + dxn
        _accumulate(i, dw_ref, _tn(dp, h))
        _accumulate(i, dg_ref, dg)
        _accumulate(i, dgq_ref, dgq)
        _accumulate(i, dgk_ref, dgk)

    r = lambda c: _rows(c, tl)
    return pl.pallas_call(
        body, name=f"mix_in_bwd_{tag}", grid=(L // tl,),
        in_specs=[r(D_MODEL), r(D_MODEL), r(w_in.shape[0]), r(512), r(512), r(512), r(256), r(256),
                  _res((1, D_MODEL)), _res(w_in.shape), _res((1, 512)), _res((1, 512)), _res((512, 512))],
        out_specs=[r(D_MODEL), _res(w_in.shape), _res((1, D_MODEL)), _res((1, 512)), _res((1, 512))],
        out_shape=[_sds((L, D_MODEL)), _sds(w_in.shape), _sds((1, D_MODEL)), _sds((1, 512)), _sds((1, 512))],
        compiler_params=_params(1),
    )(x, dres, p_kept, dq, dk, dv, dhg, du, g_mix, w_in, gq, gk, mavg)


def _sb_tri_consts():
    r = np.arange(2 * SB_TK)[:, None]
    c = np.arange(2 * SB_TK)[None, :]
    same = (r // SB_TK) == (c // SB_TK)
    later = (same & (r > c)).astype(np.float32)
    earlier = (same & (r < c)).astype(np.float32)
    return jnp.asarray(later, BF16), jnp.asarray(earlier, BF16)


def _two_heads(blk, lane_a):
    zero = jnp.zeros_like(blk)
    return jnp.concatenate([jnp.where(lane_a, blk, zero), jnp.where(lane_a, zero, blk)], axis=0)


def _sb_logs(z, i, j, masked):
    e = jnp.exp(-jnp.abs(z))
    lm = -(jnp.maximum(z, 0.0) + jnp.log(1.0 + e))
    ls = z + lm
    valid = None
    if masked:
        row = lax.broadcasted_iota(jnp.int32, (SB_TQ, 2 * SB_TK), 0)
        col = lax.broadcasted_iota(jnp.int32, (SB_TQ, 2 * SB_TK), 1) & (SB_TK - 1)
        valid = (j * SB_TK + col) < (i * SB_TQ + row)
        lm = jnp.where(valid, lm, 0.0)
    return lm, ls, lm.astype(BF16), valid


def _lane_halves(a, b):
    return jnp.concatenate([jnp.broadcast_to(a, (SB_TQ, SB_TK)), jnp.broadcast_to(b, (SB_TQ, SB_TK))], axis=1)


def _dot(a, b):
    return jnp.dot(a, b, preferred_element_type=F32)


def _dot_nt(a, b):
    return lax.dot_general(a, b, (((1,), (1,)), ((), ())), preferred_element_type=F32)


def _dot_tn(a, b):
    return lax.dot_general(a, b, (((0,), (0,)), ((), ())), preferred_element_type=F32)


def _sb_fwd(q, k, v, ul, tag, rider=None):
    L = q.shape[0]
    nq = L // SB_TQ
    per = SB_TQ // SB_TK
    wid = SB_PAIRS * LANES

    def body(q_ref, k_ref, v_ref, ul_ref, o_ref, rs_ref, n_ref):
        i = pl.program_id(1)
        ulv = ul_ref[...]
        lane_a = lax.broadcasted_iota(jnp.int32, (1, LANES), 1) < SB_HEAD_DIM
        lane_q = lax.broadcasted_iota(jnp.int32, (SB_TQ, LANES), 1)
        cols = [slice(p * LANES, (p + 1) * LANES) for p in range(SB_PAIRS)]
        qbs = [q_ref[:, c] for c in cols]

        def double_step(jhi, carry, masked):
            chains = [dict(p=p, j=jhi - d) for d in range(2) for p in range(SB_PAIRS)]
            for c in chains:
                c['off'] = pl.multiple_of(c['j'] * SB_TK, SB_TK)
                kb = k_ref[pl.ds(c['off'], SB_TK), cols[c['p']]]
                c['z'] = _dot_nt(qbs[c['p']], _two_heads(kb, lane_a))
            for c in chains:
                c['lm'], c['ls'], c['hi'], c['valid'] = _sb_logs(c.pop('z'), i, c['j'], masked)
            for c in chains:
                c['lb'] = _dot(c.pop('hi'), ulv)
            state = [list(s) for s in carry]
            for c in chains:
                ra, rb, _, rsave = state[c['p']]
                w = jnp.exp(c['ls'] + c['lb'] + _lane_halves(ra, rb))
                if masked:
                    w = jnp.where(c['valid'], w, 0.0)
                c['w'] = w.astype(BF16)
                lb, lm = c['lb'], c['lm']
                state[c['p']][3] = jnp.where(lane_q == c['j'], ra, jnp.where(lane_q == c['j'] + SB_HEAD_DIM, rb, rsave))
                state[c['p']][0] = ra + lb[:, 0:1] + lm[:, 0:1]
                state[c['p']][1] = rb + lb[:, SB_TK:SB_TK + 1] + lm[:, SB_TK:SB_TK + 1]
            for c in chains:
                vb = v_ref[pl.ds(c['off'], SB_TK), cols[c['p']]]
                state[c['p']][2] = state[c['p']][2] + _dot(c['w'], _two_heads(vb, lane_a))
            return tuple(tuple(s) for s in state)

        assert per == 2
        carry = tuple((jnp.zeros((SB_TQ, 1), F32), jnp.zeros((SB_TQ, 1), F32),
                       jnp.zeros((SB_TQ, LANES), F32), jnp.zeros((SB_TQ, LANES), F32)) for _ in range(SB_PAIRS))
        def alive(carry):
            m = carry[0][0]
            for c in carry:
                m = jnp.maximum(m, jnp.maximum(c[0], c[1]))
            return jnp.max(m) > SB_DEAD

        carry = double_step(i * per + 1, carry, True)
        n_done, _, carry = lax.while_loop(
            lambda st: jnp.logical_and(st[0] < i, st[1]),
            lambda st: (lambda c: (st[0] + 1, alive(c), c))(double_step(i * per - 1 - 2 * st[0], st[2], False)),
            (jnp.int32(0), alive(carry), carry))
        o_ref[...] = jnp.concatenate([c[2] for c in carry], axis=1)
        rs_ref[...] = jnp.concatenate([c[3] for c in carry], axis=1)
        n_ref[pl.program_id(0), i] = n_done

    qspec = pl.BlockSpec((SB_TQ, wid), lambda g, i: (i, g))
    kspec = pl.BlockSpec((L, wid), lambda g, i: (0, g))
    return _pcall(
        body, name=f"sb_fwd_{tag}", grid=(SB_WIDTH // wid, nq),
        in_specs=[qspec, kspec, kspec, pl.BlockSpec((2 * SB_TK, 2 * SB_TK), lambda g, i: (0, 0))],
        out_specs=[qspec, qspec, pl.BlockSpec(memory_space=pltpu.SMEM)],
        out_shape=[_sds((L, SB_WIDTH)), _sds((L, SB_WIDTH)), _sds((SB_WIDTH // wid, nq), jnp.int32)],
        compiler_params=_params(2), rider=rider,
    )(q, k, v, ul)


def _sb_bwd(n_done, q, k, v, rsave, do, ul, ue, tag):
    L = q.shape[0]
    nq = L // SB_TQ
    per = SB_TQ // SB_TK
    wid = SB_PAIRS * LANES

    def body(n_ref, q_ref, k_ref, v_ref, rs_ref, do_ref, ul_ref, ue_ref, dq_ref, dk_ref, dv_ref):
        i = pl.program_id(1)

        @pl.when(i == 0)
        def _():
            dk_ref[...] = jnp.zeros_like(dk_ref)
            dv_ref[...] = jnp.zeros_like(dv_ref)

        ulv = ul_ref[...]
        uev = ue_ref[...]
        lane_a = lax.broadcasted_iota(jnp.int32, (1, LANES), 1) < SB_HEAD_DIM
        lane_q = lax.broadcasted_iota(jnp.int32, (SB_TQ, LANES), 1)
        cols = [slice(p * LANES, (p + 1) * LANES) for p in range(SB_PAIRS)]
        qbs = [q_ref[:, c] for c in cols]
        dobs = [do_ref[:, c].astype(BF16) for c in cols]
        rsvs = [rs_ref[:, c] for c in cols]

        def double_step(jlo, carry, masked):
            chains = [dict(p=p, j=jlo + d) for d in range(2) for p in range(SB_PAIRS)]
            for c in chains:
                p = c['p']
                c['off'] = pl.multiple_of(c['j'] * SB_TK, SB_TK)
                c['kk2'] = _two_heads(k_ref[pl.ds(c['off'], SB_TK), cols[p]], lane_a)
                c['z'] = _dot_nt(qbs[p], c['kk2'])
                c['dw'] = _dot_nt(dobs[p], _two_heads(v_ref[pl.ds(c['off'], SB_TK), cols[p]], lane_a))
            for c in chains:
                c['lm'], c['ls'], c['hi'], c['valid'] = _sb_logs(c.pop('z'), i, c['j'], masked)
                c['ra'] = jnp.sum(jnp.where(lane_q == c['j'], rsvs[c['p']], 0.0), axis=1, keepdims=True)
                c['rb'] = jnp.sum(jnp.where(lane_q == c['j'] + SB_HEAD_DIM, rsvs[c['p']], 0.0), axis=1, keepdims=True)
            for c in chains:
                c['lb'] = _dot(c.pop('hi'), ulv)
            for c in chains:
                w = jnp.exp(c['ls'] + c.pop('lb') + _lane_halves(c['ra'], c['rb']))
                if masked:
                    w = jnp.where(c['valid'], w, 0.0)
                c['wb'] = w.astype(BF16)
                gg = w * c.pop('dw')
                c['gg'] = gg
                c['beta'] = jnp.exp(c['ls'])
            for c in chains:
                c['cb'] = _dot(c['gg'].astype(BF16), uev)
                c['dv2'] = _dot_tn(c.pop('wb'), dobs[c['p']])
            state = [list(s) for s in carry]
            for c in chains:
                pa, pb, _ = state[c['p']]
                gg, cb, beta = c['gg'], c['cb'], c['beta']
                dz = gg * (1.0 - beta) - beta * (cb + _lane_halves(pa, pb))
                if masked:
                    dz = jnp.where(c['valid'], dz, 0.0)
                c['dzb'] = dz.astype(BF16)
                state[c['p']][0] = pa + cb[:, SB_TK - 1:SB_TK] + gg[:, SB_TK - 1:SB_TK]
                state[c['p']][1] = pb + cb[:, 2 * SB_TK - 1:2 * SB_TK] + gg[:, 2 * SB_TK - 1:2 * SB_TK]
            for c in chains:
                c['dqc'] = _dot(c['dzb'], c['kk2'])
                c['dk2'] = _dot_tn(c['dzb'], qbs[c['p']])
            for c in chains:
                p, dk2, dv2 = c['p'], c['dk2'], c['dv2']
                state[p][2] = state[p][2] + c['dqc']
                dk_ref[pl.ds(c['off'], SB_TK), cols[p]] += jnp.where(lane_a, dk2[0:SB_TK], dk2[SB_TK:2 * SB_TK])
                dv_ref[pl.ds(c['off'], SB_TK), cols[p]] += jnp.where(lane_a, dv2[0:SB_TK], dv2[SB_TK:2 * SB_TK])
            return tuple(tuple(s) for s in state)

        assert per == 2
        carry = tuple((jnp.zeros((SB_TQ, 1), F32), jnp.zeros((SB_TQ, 1), F32), jnp.zeros((SB_TQ, LANES), F32))
                      for _ in range(SB_PAIRS))
        first = i - n_ref[pl.program_id(0), i]
        carry = lax.fori_loop(first, i, lambda jj, c: double_step(2 * jj, c, False), carry)
        carry = double_step(i * per, carry, True)
        dq_ref[...] = jnp.concatenate([c[2] for c in carry], axis=1) * SB_SCALE

    qspec = pl.BlockSpec((SB_TQ, wid), lambda g, i: (i, g))
    kspec = pl.BlockSpec((L, wid), lambda g, i: (0, g))
    kin = pl.BlockSpec((L, wid), lambda g, i: (0, g), pipeline_mode=pl.Buffered(1))
    cspec = pl.BlockSpec((2 * SB_TK, 2 * SB_TK), lambda g, i: (0, 0))
    return pl.pallas_call(
        body, name=f"sb_bwd_{tag}", grid=(SB_WIDTH // wid, nq),
        in_specs=[pl.BlockSpec(memory_space=pltpu.SMEM), qspec, kin, kin, qspec, qspec, cspec, cspec],
        out_specs=[qspec, kspec, kspec],
        out_shape=[_sds((L, SB_WIDTH)), _sds((L, SB_WIDTH)), _sds((L, SB_WIDTH))],
        compiler_params=_params(2, 58),
    )(n_done, q, k, v, rsave, do, ul, ue)


def _conv_fwd(hp, w, tag):
    L = hp.shape[0] - CONV_HALO
    win_rows = CONV_SUB + CONV_HALO

    def body(hp_ref, w_ref, o_ref):
        i = pl.program_id(0)

        def sub(s, _):
            t0 = pl.multiple_of(i * TL + s * CONV_SUB, CONV_SUB)
            win = hp_ref[pl.ds(t0, win_rows), :]
            acc = jnp.zeros((CONV_SUB, CONV_CH), F32)
            for kk in range(CONV_WIDTH):
                sh = CONV_WIDTH - 1 - kk
                r = win if sh == 0 else pltpu.roll(win, sh, 0)
                acc = acc + w_ref[kk:kk + 1, :] * r[CONV_HALO:, :]
            o_ref[pl.ds(pl.multiple_of(s * CONV_SUB, CONV_SUB), CONV_SUB), :] = acc
            return 0

        lax.fori_loop(0, TL // CONV_SUB, sub, 0)

    return pl.pallas_call(
        body, name=f"conv_fwd_{tag}", grid=(L // TL,),
        in_specs=[_res(hp.shape), _res(w.shape)],
        out_specs=_rows(CONV_CH),
        out_shape=_sds((L, CONV_CH)),
        compiler_params=_params(1),
    )(hp, w)


def _conv_bwd(dpad, hp, w, tag):
    L = hp.shape[0] - CONV_HALO
    win_rows = CONV_SUB + CONV_HALO
    n_tiles = L // TL

    def body(dp_ref, hp_ref, w_ref, dh_ref, dw_ref, acc_ref):
        i = pl.program_id(0)

        @pl.when(i == 0)
        def _():
            acc_ref[...] = jnp.zeros_like(acc_ref)

        def sub(s, _):
            t0 = pl.multiple_of(i * TL + s * CONV_SUB, CONV_SUB)
            wd = dp_ref[pl.ds(t0, win_rows), :]
            wh = hp_ref[pl.ds(t0, win_rows), :]
            dy = wd[0:CONV_SUB, :]
            acc = jnp.zeros((CONV_SUB, CONV_CH), F32)
            for kk in range(CONV_WIDTH):
                sh = CONV_WIDTH - 1 - kk
                rd = wd if sh == 0 else pltpu.roll(wd, win_rows - sh, 0)
                acc = acc + w_ref[kk:kk + 1, :] * rd[0:CONV_SUB, :]
                rh = wh if sh == 0 else pltpu.roll(wh, sh, 0)
                prod = dy * rh[CONV_HALO:, :]
                part = prod[0:SUBLANES]
                for m in range(1, CONV_SUB // SUBLANES):
                    part = part + prod[m * SUBLANES:(m + 1) * SUBLANES]
                acc_ref[kk] += part
            dh_ref[pl.ds(pl.multiple_of(s * CONV_SUB, CONV_SUB), CONV_SUB), :] = acc
            return 0

        lax.fori_loop(0, TL // CONV_SUB, sub, 0)

        @pl.when(i == n_tiles - 1)
        def _():
            for kk in range(CONV_WIDTH):
                dw_ref[kk:kk + 1, :] = jnp.sum(acc_ref[kk], axis=0, keepdims=True)

    return pl.pallas_call(
        body, name=f"conv_bwd_{tag}", grid=(n_tiles,),
        in_specs=[_res(dpad.shape), _res(hp.shape), _res(w.shape)],
        out_specs=[_rows(CONV_CH), _res(w.shape)],
        out_shape=[_sds((L, CONV_CH)), _sds(w.shape)],
        scratch_shapes=[pltpu.VMEM((CONV_WIDTH, SUBLANES, CONV_CH), F32)],
        compiler_params=_params(1),
    )(dpad, hp, w)


def _ssm_mask():
    r = np.arange(SSM_CH)[:, None] // SSM_GROUP
    c = np.arange(SSM_LANES)[None, :] // SSM_STATE
    return jnp.asarray((r == c).astype(np.float32))


def _ssm_discretize(lr, li, ldt, brt, bit):
    dt = jnp.exp(ldt)
    mag = jnp.exp(lr * dt)
    ar = mag * jnp.cos(li * dt)
    ai = mag * jnp.sin(li * dt)
    den = lr * lr + li * li
    fr = ((ar - 1.0) * lr + ai * li) / den
    fi = (ai * lr - (ar - 1.0) * li) / den
    return ar, ai, fr * brt - fi * bit, fr * bit + fi * brt


def _block_diag(rows16, mask):
    return jnp.where(mask > 0.5, jnp.tile(rows16, (SSM_GROUPS, 1)), 0.0)


def _block_diag_t(full, mask):
    m = jnp.where(mask > 0.5, full, 0.0)
    out = m[0:SSM_GROUP]
    for g in range(1, SSM_GROUPS):
        out = out + m[g * SSM_GROUP:(g + 1) * SSM_GROUP]
    return out


def _ssm_prep(lr, li, ldt, brt, bit, crt, cit, mask, tag):
    def body(lr_ref, li_ref, ldt_ref, brt_ref, bit_ref, crt_ref, cit_ref, m_ref,
             ar_ref, ai_ref, bbr_ref, bbi_ref, cbr_ref, cbi_ref):
        ar, ai, bbr, bbi = _ssm_discretize(lr_ref[...], li_ref[...], ldt_ref[...], brt_ref[...], bit_ref[...])
        m = m_ref[...]
        ar_ref[...] = ar
        ai_ref[...] = ai
        bbr_ref[...] = _block_diag(bbr, m).astype(BF16)
        bbi_ref[...] = _block_diag(bbi, m).astype(BF16)
        cbr_ref[...] = _block_diag(crt_ref[...], m).astype(BF16)
        cbi_ref[...] = _block_diag(cit_ref[...], m).astype(BF16)

    row = _sds((1, SSM_LANES))
    blk = _sds((SSM_CH, SSM_LANES), BF16)
    return pl.pallas_call(body, name=f"ssm_prep_{tag}", out_shape=[row, row, blk, blk, blk, blk])(
        lr, li, ldt, brt, bit, crt, cit, mask)


def _ssm_prep_bwd(lr, li, ldt, brt, bit, mask, dar, dai, dbbr, dbbi, dcbr, dcbi, tag):
    def body(lr_ref, li_ref, ldt_ref, brt_ref, bit_ref, m_ref, dar_ref, dai_ref, dbbr_ref, dbbi_ref, dcbr_ref,
             dcbi_ref, dlr_ref, dli_ref, dldt_ref, dbrt_ref, dbit_ref, dcrt_ref, dcit_ref):
        m = m_ref[...]
        _, vjp = jax.vjp(_ssm_discretize, lr_ref[...], li_ref[...], ldt_ref[...], brt_ref[...], bit_ref[...])
        dlr, dli, dldt, dbrt, dbit = vjp((dar_ref[...], dai_ref[...], _block_diag_t(dbbr_ref[...], m),
                                          _block_diag_t(dbbi_ref[...], m)))
        dlr_ref[...] = dlr
        dli_ref[...] = dli
        dldt_ref[...] = dldt
        dbrt_ref[...] = dbrt
        dbit_ref[...] = dbit
        dcrt_ref[...] = _block_diag_t(dcbr_ref[...], m)
        dcit_ref[...] = _block_diag_t(dcbi_ref[...], m)

    row = _sds((1, SSM_LANES))
    r16 = _sds((SSM_GROUP, SSM_LANES))
    return pl.pallas_call(body, name=f"ssm_prep_bwd_{tag}", out_shape=[row, row, row, r16, r16, r16, r16])(
        lr, li, ldt, brt, bit, mask, dar, dai, dbbr, dbbi, dcbr, dcbi)


def _complex_scan(br, bi, ar, ai, cr, ci, reverse):
    n = br.shape[0]
    row = lax.broadcasted_iota(jnp.int32, (n, 1), 0) & (SUBLANES - 1)
    xr, xi, pr, pi = br, bi, ar, ai
    d = 1
    while d < SUBLANES:
        if reverse:
            sr, si, keep = pltpu.roll(xr, n - d, 0), pltpu.roll(xi, n - d, 0), row < SUBLANES - d
        else:
            sr, si, keep = pltpu.roll(xr, d, 0), pltpu.roll(xi, d, 0), row >= d
        sr = jnp.where(keep, sr, 0.0)
        si = jnp.where(keep, si, 0.0)
        xr, xi = xr + pr * sr - pi * si, xi + pr * si + pi * sr
        pr, pi = pr * pr - pi * pi, 2.0 * pr * pi
        d *= 2
    powers = [(ar, ai)]
    for _ in range(SUBLANES - 1):
        qr, qi = powers[-1]
        powers.append((qr * ar - qi * ai, qr * ai + qi * ar))
    sub = lax.broadcasted_iota(jnp.int32, (SUBLANES, 1), 0)
    tr = jnp.zeros((SUBLANES, br.shape[1]), F32)
    ti = jnp.zeros((SUBLANES, br.shape[1]), F32)
    for r in range(SUBLANES):
        qr, qi = powers[SUBLANES - 1 - r] if reverse else powers[r]
        tr = jnp.where(sub == r, qr, tr)
        ti = jnp.where(sub == r, qi, ti)
    n_groups = n // SUBLANES
    out_r, out_i = [None] * n_groups, [None] * n_groups
    end = 0 if reverse else SUBLANES - 1
    for g in (reversed(range(n_groups)) if reverse else range(n_groups)):
        gr = xr[g * SUBLANES:(g + 1) * SUBLANES]
        gi = xi[g * SUBLANES:(g + 1) * SUBLANES]
        gr, gi = gr + tr * cr - ti * ci, gi + tr * ci + ti * cr
        cr, ci = gr[end:end + 1], gi[end:end + 1]
        out_r[g], out_i[g] = gr, gi
    return jnp.concatenate(out_r, axis=0), jnp.concatenate(out_i, axis=0)


def _ssm_fwd(u, ar, ai, bbr, bbi, cbr, cbi, dvec, tag, rider=None):
    L = u.shape[0]
    T = SSM_T

    def body(u_ref, ar_ref, ai_ref, bbr_ref, bbi_ref, cbr_ref, cbi_ref, d_ref, y_ref, xr_ref, xi_ref, cr_ref, ci_ref):
        i = pl.program_id(0)

        @pl.when(i == 0)
        def _():
            cr_ref[...] = jnp.zeros_like(cr_ref)
            ci_ref[...] = jnp.zeros_like(ci_ref)

        uu = u_ref[...]
        a_r, a_i = ar_ref[...], ai_ref[...]
        c_r, c_i = cr_ref[...], ci_ref[...]
        xr, xi = _complex_scan(_nn(uu, bbr_ref[...]), _nn(uu, bbi_ref[...]), a_r, a_i, c_r, c_i, False)
        cr_ref[...] = xr[T - 1:T, :]
        ci_ref[...] = xi[T - 1:T, :]
        xr_ref[...] = xr
        xi_ref[...] = xi
        y_ref[...] = _nt(xr, cbr_ref[...]) - _nt(xi, cbi_ref[...]) + d_ref[...] * uu

    blk = _res((SSM_CH, SSM_LANES))
    row = _res((1, SSM_LANES))
    return _pcall(
        body, name=f"ssm_fwd_{tag}", grid=(L // T,), rider=rider,
        in_specs=[_rows(SSM_CH, T), row, row, blk, blk, blk, blk, _res((1, SSM_CH))],
        out_specs=[_rows(SSM_CH, T), _rows(SSM_LANES, T), _rows(SSM_LANES, T)],
        out_shape=[_sds((L, SSM_CH)), _sds((L, SSM_LANES)), _sds((L, SSM_LANES))],
        scratch_shapes=[pltpu.VMEM((1, SSM_LANES), F32), pltpu.VMEM((1, SSM_LANES), F32)],
        compiler_params=_params(1),
    )(u, ar, ai, bbr, bbi, cbr, cbi, dvec)


def _ssm_bwd(dy, u, xr, xi, ar, ai, bbr, bbi, cbr, cbi, dvec, tag, rider=None):
    L = u.shape[0]
    T = SSM_T
    nc = L // T

    def body(dy_ref, u_ref, xr_ref, xi_ref, pr_ref, pi_ref, ar_ref, ai_ref, bbr_ref, bbi_ref, cbr_ref, cbi_ref, d_ref,
             du_ref, dar_ref, dai_ref, dbbr_ref, dbbi_ref, dcbr_ref, dcbi_ref, dd_ref, gr_ref, gi_ref):
        i = pl.program_id(0)

        @pl.when(i == 0)
        def _():
            gr_ref[...] = jnp.zeros_like(gr_ref)
            gi_ref[...] = jnp.zeros_like(gi_ref)

        dyy = dy_ref[...]
        uu = u_ref[...]
        xr, xi = xr_ref[...], xi_ref[...]
        a_r, a_i = ar_ref[...], ai_ref[...]
        g_r, g_i = gr_ref[...], gi_ref[...]
        row = lax.broadcasted_iota(jnp.int32, (T, 1), 0)
        gr, gi = _complex_scan(_nn(dyy, cbr_ref[...]), -_nn(dyy, cbi_ref[...]), a_r, -a_i, g_r, g_i, True)
        gr_ref[...] = gr[0:1, :]
        gi_ref[...] = gi[0:1, :]
        has_prev = (i < nc - 1).astype(F32)
        pr = pr_ref[SUBLANES - 1:SUBLANES, :] * has_prev
        pi = pi_ref[SUBLANES - 1:SUBLANES, :] * has_prev
        sr = jnp.where(row == 0, pr, pltpu.roll(xr, 1, 0))
        si = jnp.where(row == 0, pi, pltpu.roll(xi, 1, 0))
        _accumulate(i, dar_ref, jnp.sum(gr * sr + gi * si, axis=0, keepdims=True))
        _accumulate(i, dai_ref, jnp.sum(gi * sr - gr * si, axis=0, keepdims=True))
        _accumulate(i, dbbr_ref, _tn(uu, gr))
        _accumulate(i, dbbi_ref, _tn(uu, gi))
        _accumulate(i, dcbr_ref, _tn(dyy, xr))
        _accumulate(i, dcbi_ref, -_tn(dyy, xi))
        _accumulate(i, dd_ref, jnp.sum(dyy * uu, axis=0, keepdims=True))
        du_ref[...] = _nt(gr, bbr_ref[...]) + _nt(gi, bbi_ref[...]) + dyy * d_ref[...]

    rev = lambda cols: pl.BlockSpec((T, cols), lambda i: (nc - 1 - i, 0))
    prev = pl.BlockSpec((SUBLANES, SSM_LANES), lambda i: (jnp.maximum((nc - 1 - i) * (T // SUBLANES) - 1, 0), 0))
    blk = _res((SSM_CH, SSM_LANES))
    row = _res((1, SSM_LANES))
    return _pcall(
        body, name=f"ssm_bwd_{tag}", grid=(nc,), rider=rider,
        in_specs=[rev(SSM_CH), rev(SSM_CH), rev(SSM_LANES), rev(SSM_LANES), prev, prev, row, row, blk, blk, blk, blk,
                  _res((1, SSM_CH))],
        out_specs=[rev(SSM_CH), row, row, blk, blk, blk, blk, _res((1, SSM_CH))],
        out_shape=[_sds((L, SSM_CH)), _sds((1, SSM_LANES)), _sds((1, SSM_LANES)), _sds((SSM_CH, SSM_LANES)),
                   _sds((SSM_CH, SSM_LANES)), _sds((SSM_CH, SSM_LANES)), _sds((SSM_CH, SSM_LANES)), _sds((1, SSM_CH))],
        scratch_shapes=[pltpu.VMEM((1, SSM_LANES), F32), pltpu.VMEM((1, SSM_LANES), F32)],
        compiler_params=_params(1),
    )(dy, u, xr, xi, xr, xi, ar, ai, bbr, bbi, cbr, cbi, dvec)


def _conv_post(hc, dw_b, ln_g, ln_b):
    return _silu(_layer_norm(hc + dw_b, ln_g, ln_b))


def _branch_mix(o_sb, o_conv, t, g1, g2, g3):
    o_ssm = t[:, 0:SSM_CH] * _sigmoid(t[:, SSM_CH:2 * SSM_CH])
    return jnp.concatenate([_rms(o_sb, g1), _rms(o_conv, g2), _rms(o_ssm, g3)], axis=1)


def _branch_mix_split(o_sb, o_conv, ta, tb, g1, g2, g3):
    return jnp.concatenate([_rms(o_sb, g1), _rms(o_conv, g2), _rms(ta * _sigmoid(tb), g3)], axis=1)


def _mix_out_fwd(x, o_sb, hc, y, dw_b, ln_g, ln_b, pw2, glu_w, g1, g2, g3, w_out, tag):
    L = x.shape[0]

    def body(x_ref, o_ref, hc_ref, y_ref, dwb_ref, lng_ref, lnb_ref, pw2_ref, glu_ref, g1_ref, g2_ref, g3_ref, wo_ref,
             out_ref):
        c1 = _conv_post(hc_ref[...], dwb_ref[...], lng_ref[...], lnb_ref[...])
        o_conv = _nn(c1, pw2_ref[...])
        t = _nt(y_ref[...], glu_ref[...])
        mixed = _branch_mix(o_ref[...], o_conv, t, g1_ref[...], g2_ref[...], g3_ref[...])
        out_ref[...] = x_ref[...] + _nn(mixed, wo_ref[...])

    v256 = _res((1, 256))
    return pl.pallas_call(
        body, name=f"mix_out_fwd_{tag}", grid=(L // TL,),
        in_specs=[_rows(D_MODEL), _rows(512), _rows(256), _rows(256), v256, v256, v256, _res(pw2.shape),
                  _res(glu_w.shape), _res((1, 512)), v256, v256, _res(w_out.shape)],
        out_specs=_rows(D_MODEL),
        out_shape=_sds((L, D_MODEL)),
        compiler_params=_params(1),
    )(x, o_sb, hc, y, dw_b, ln_g, ln_b, pw2, glu_w, g1, g2, g3, w_out)


def _mix_out_bwd(dx1, o_sb, hc, y, dw_b, ln_g, ln_b, pw2, glu_w, g1, g2, g3, w_out, tag):
    L = dx1.shape[0]

    def body(dx_ref, o_ref, hc_ref, y_ref, dwb_ref, lng_ref, lnb_ref, pw2_ref, glu_ref, g1_ref, g2_ref, g3_ref, wo_ref,
             do_ref, dhc_ref, dy_ref, ddwb_ref, dlng_ref, dlnb_ref, dpw2_ref, dglu_ref, dg1_ref, dg2_ref, dg3_ref,
             dwo_ref):
        i = pl.program_id(0)
        dxx = dx_ref[...]
        yy = y_ref[...]
        c1, vjp1 = jax.vjp(_conv_post, hc_ref[...], dwb_ref[...], lng_ref[...], lnb_ref[...])
        o_conv = _nn(c1, pw2_ref[...])
        t = _nt(yy, glu_ref[...])
        mixed, vjp2 = jax.vjp(_branch_mix_split, o_ref[...], o_conv, t[:, 0:SSM_CH], t[:, SSM_CH:2 * SSM_CH],
                              g1_ref[...], g2_ref[...], g3_ref[...])
        dmixed = _nt(dxx, wo_ref[...])
        do_sb, do_conv, dta, dtb, dg1, dg2, dg3 = vjp2(dmixed)
        dt = jnp.concatenate([dta, dtb], axis=1)
        dc1 = _nt(do_conv, pw2_ref[...])
        dhc, ddwb, dlng, dlnb = vjp1(dc1)
        do_ref[...] = do_sb
        dhc_ref[...] = dhc
        dy_ref[...] = _nn(dt, glu_ref[...])
        _accumulate(i, dwo_ref, _tn(mixed, dxx))
        _accumulate(i, dglu_ref, _tn(dt, yy))
        _accumulate(i, dpw2_ref, _tn(c1, do_conv))
        _accumulate(i, ddwb_ref, ddwb)
        _accumulate(i, dlng_ref, dlng)
        _accumulate(i, dlnb_ref, dlnb)
        _accumulate(i, dg1_ref, dg1)
        _accumulate(i, dg2_ref, dg2)
        _accumulate(i, dg3_ref, dg3)

    v256 = _res((1, 256))
    return pl.pallas_call(
        body, name=f"mix_out_bwd_{tag}", grid=(L // TL,),
        in_specs=[_rows(D_MODEL), _rows(512), _rows(256), _rows(256), v256, v256, v256, _res(pw2.shape),
                  _res(glu_w.shape), _res((1, 512)), v256, v256, _res(w_out.shape)],
        out_specs=[_rows(512), _rows(256), _rows(256), v256, v256, v256, _res(pw2.shape), _res(glu_w.shape),
                   _res((1, 512)), v256, v256, _res(w_out.shape)],
        out_shape=[_sds((L, 512)), _sds((L, 256)), _sds((L, 256)), _sds((1, 256)), _sds((1, 256)), _sds((1, 256)),
                   _sds(pw2.shape), _sds(glu_w.shape), _sds((1, 512)), _sds((1, 256)), _sds((1, 256)), _sds(w_out.shape)],
        compiler_params=_params(1),
    )(dx1, o_sb, hc, y, dw_b, ln_g, ln_b, pw2, glu_w, g1, g2, g3, w_out)


def _xa_heads_norm(kk, kg):
    return jnp.concatenate([_rms(kk[:, h * XA_HEAD_DIM:(h + 1) * XA_HEAD_DIM], kg) for h in range(XA_HEADS)], axis=1)


def _xa_mem_fwd(mem, g_mem, wk, wv, kg, tag):
    def body(m_ref, g_ref, wk_ref, wv_ref, kg_ref, k_ref, v_ref):
        hm = _rms(m_ref[...], g_ref[...])
        k_ref[...] = _xa_heads_norm(_nn(hm, wk_ref[...]), kg_ref[...])
        v_ref[...] = _nn(hm, wv_ref[...])

    return pl.pallas_call(body, name=f"xa_mem_fwd_{tag}", out_shape=[_sds(mem.shape), _sds(mem.shape)],
                          compiler_params=_params(0))(mem, g_mem, wk, wv, kg)


def _xa_mem_bwd(mem, g_mem, wk, wv, kg, dkx, dvx, tag):
    def body(m_ref, g_ref, wk_ref, wv_ref, kg_ref, dk_ref, dv_ref, dwk_ref, dwv_ref, dg_ref, dkg_ref):
        hm, vjp_n = jax.vjp(_rms, m_ref[...], g_ref[...])
        kk = _nn(hm, wk_ref[...])
        dvv = dv_ref[...]
        dkg = jnp.zeros((1, XA_HEAD_DIM), F32)
        parts = []
        for h in range(XA_HEADS):
            sl = slice(h * XA_HEAD_DIM, (h + 1) * XA_HEAD_DIM)
            _, vjp_h = jax.vjp(_rms, kk[:, sl], kg_ref[...])
            dkh, dgh = vjp_h(dk_ref[:, sl])
            parts.append(dkh)
            dkg = dkg + dgh
        dkk = jnp.concatenate(parts, axis=1)
        dwk_ref[...] = _tn(hm, dkk)
        dwv_ref[...] = _tn(hm, dvv)
        dhm = _nt(dkk, wk_ref[...]) + _nt(dvv, wv_ref[...])
        _, dg = vjp_n(dhm)
        dg_ref[...] = dg
        dkg_ref[...] = dkg

    return pl.pallas_call(
        body, name=f"xa_mem_bwd_{tag}",
        out_shape=[_sds(wk.shape), _sds(wv.shape), _sds((1, D_MODEL)), _sds((1, XA_HEAD_DIM))],
        compiler_params=_params(0))(mem, g_mem, wk, wv, kg, dkx, dvx)


def _xa_fwd(x1, kx, vx, g_xa, wq, qg, wo, tag, rider=None):
    L = x1.shape[0]

    def body(x_ref, k_ref, v_ref, g_ref, wq_ref, qg_ref, wo_ref, out_ref, qp_ref):
        xx = x_ref[...]
        qp = _nn(_rms(xx, g_ref[...]), wq_ref[...])
        qp_ref[...] = qp.astype(BF16)
        outs = []
        for h in range(XA_HEADS):
            sl = slice(h * XA_HEAD_DIM, (h + 1) * XA_HEAD_DIM)
            qh = _rms(qp[:, sl], qg_ref[...])
            s = _nt(qh, k_ref[:, sl]) * (XA_HEAD_DIM ** -0.5)
            s = s - jnp.max(s, axis=-1, keepdims=True)
            e = jnp.exp(s)
            p = e / jnp.sum(e, axis=-1, keepdims=True)
            outs.append(_nn(p, v_ref[:, sl]))
        out_ref[...] = xx + _nn(jnp.concatenate(outs, axis=1), wo_ref[...])

    return _pcall(
        body, name=f"xa_fwd_{tag}", grid=(L // TL,), rider=rider,
        in_specs=[_rows(D_MODEL), _res(kx.shape), _res(vx.shape), _res((1, D_MODEL)), _res(wq.shape),
                  _res((1, XA_HEAD_DIM)), _res(wo.shape)],
        out_specs=[_rows(D_MODEL), _rows(D_MODEL)],
        out_shape=[_sds((L, D_MODEL)), _sds((L, D_MODEL), BF16)],
        compiler_params=_params(1),
    )(x1, kx, vx, g_xa, wq, qg, wo)


def _xa_bwd(x1, dx2, qp_kept, kx, vx, g_xa, wq, qg, wo, tag, rider=None):
    L = x1.shape[0]
    tl = 256

    def body(x_ref, dx_ref, qp_ref, k_ref, v_ref, g_ref, wq_ref, qg_ref, wo_ref,
             dx1_ref, dk_ref, dv_ref, dwq_ref, dwo_ref, dg_ref, dqg_ref):
        i = pl.program_id(0)
        xx = x_ref[...]
        dxx = dx_ref[...]
        hx, vjp_n = jax.vjp(_rms, xx, g_ref[...])
        qp = qp_ref[...].astype(F32)
        do = _nt(dxx, wo_ref[...])
        heads = [dict(sl=slice(h * XA_HEAD_DIM, (h + 1) * XA_HEAD_DIM)) for h in range(XA_HEADS)]
        for c in heads:
            c['kh'], c['vh'], c['doh'] = k_ref[:, c['sl']].astype(BF16), v_ref[:, c['sl']].astype(BF16), do[:, c['sl']]
            c['qh'], c['vjp_q'] = jax.vjp(_rms, qp[:, c['sl']], qg_ref[...])
        for c in heads:
            c['s'] = _nt(c['qh'], c['kh'])
            c['dp'] = _nt(c['doh'], c['vh'])
        for c in heads:
            s = c.pop('s') * (XA_HEAD_DIM ** -0.5)
            e = jnp.exp(s - jnp.max(s, axis=-1, keepdims=True))
            c['p'] = e / jnp.sum(e, axis=-1, keepdims=True)
        for c in heads:
            c['out'] = _nn(c['p'], c['vh'])
            c['dv'] = _tn(c['p'], c['doh'])
        for c in heads:
            p, dp = c['p'], c.pop('dp')
            c['ds'] = p * (dp - jnp.sum(dp * p, axis=-1, keepdims=True)) * (XA_HEAD_DIM ** -0.5)
        for c in heads:
            c['dk'] = _tn(c['ds'], c['qh'])
            c['dq'] = _nn(c['ds'], c['kh'])
        dqg = jnp.zeros((1, XA_HEAD_DIM), F32)
        for c in heads:
            c['dqp'], dgh = c['vjp_q'](c['dq'])
            dqg = dqg + dgh
        o = jnp.concatenate([c['out'] for c in heads], axis=1)
        dqp = jnp.concatenate([c['dqp'] for c in heads], axis=1)
        dks, dvs = [c['dk'] for c in heads], [c['dv'] for c in heads]
        dxn, dg = vjp_n(_nt(dqp, wq_ref[...]))
        dx1_ref[...] = dxx + dxn
        _accumulate(i, dk_ref, jnp.concatenate(dks, axis=1))
        _accumulate(i, dv_ref, jnp.concatenate(dvs, axis=1))
        _accumulate(i, dwq_ref, _tn(hx, dqp))
        _accumulate(i, dwo_ref, _tn(o, dxx))
        _accumulate(i, dg_ref, dg)
        _accumulate(i, dqg_ref, dqg)

    r = lambda c: _rows(c, tl)
    return _pcall(
        body, name=f"xa_bwd_{tag}", grid=(L // tl,), rider=rider,
        in_specs=[r(D_MODEL), r(D_MODEL), r(D_MODEL), _res(kx.shape), _res(vx.shape), _res((1, D_MODEL)),
                  _res(wq.shape), _res((1, XA_HEAD_DIM)), _res(wo.shape)],
        out_specs=[r(D_MODEL), _res(kx.shape), _res(vx.shape), _res(wq.shape), _res(wo.shape), _res((1, D_MODEL)),
                   _res((1, XA_HEAD_DIM))],
        out_shape=[_sds((L, D_MODEL)), _sds(kx.shape), _sds(vx.shape), _sds(wq.shape), _sds(wo.shape),
                   _sds((1, D_MODEL)), _sds((1, XA_HEAD_DIM))],
        compiler_params=_params(1),
    )(x1, dx2, qp_kept, kx, vx, g_xa, wq, qg, wo)


def _swiglu(gate, up):
    return _silu(gate) * up


def _ffn_fwd(x2, g, w_in, w_out, tag, rider=None):
    L = x2.shape[0]
    tl = 256

    def body(x_ref, g_ref, wi_ref, wo_ref, out_ref, hf_ref, gu_ref, act_ref):
        xx = x_ref[...]
        hf = _rms(xx, g_ref[...]).astype(BF16)
        gu = _nt(hf, wi_ref[...])
        act = _swiglu(gu[:, 0:FFN_HIDDEN], gu[:, FFN_HIDDEN:2 * FFN_HIDDEN]).astype(BF16)
        out_ref[...] = xx + _nn(act, wo_ref[...])
        hf_ref[...] = hf
        gu_ref[...] = gu.astype(BF16)
        act_ref[...] = act

    r = lambda c: _rows(c, tl)
    return _pcall(
        body, name=f"ffn_fwd_{tag}", grid=(L // tl,), rider=rider,
        in_specs=[r(D_MODEL), _res((1, D_MODEL)), _res(w_in.shape), _res(w_out.shape)],
        out_specs=[r(D_MODEL), r(D_MODEL), r(2 * FFN_HIDDEN), r(FFN_HIDDEN)],
        out_shape=[_sds((L, D_MODEL)), _sds((L, D_MODEL), BF16), _sds((L, 2 * FFN_HIDDEN), BF16),
                   _sds((L, FFN_HIDDEN), BF16)],
        compiler_params=_params(1, 56),
    )(x2, g, w_in, w_out)


def _ffn_bwd(x2, dx3, gu, g, w_in, w_out, tag, rider=None):
    L = x2.shape[0]
    tl = 256

    def body(x_ref, dx_ref, gu_ref, g_ref, wi_ref, wo_ref, dx2_ref, dgu_ref, dg_ref):
        i = pl.program_id(0)
        dxx = dx_ref[...]
        _, vjp_n = jax.vjp(_rms, x_ref[...], g_ref[...])
        _, vjp_a = jax.vjp(_swiglu, gu_ref[:, 0:FFN_HIDDEN].astype(F32), gu_ref[:, FFN_HIDDEN:2 * FFN_HIDDEN].astype(F32))
        dgate, dup = vjp_a(_nt(dxx, wo_ref[...]))
        dgu = jnp.concatenate([dgate, dup], axis=1).astype(BF16)
        dxn, dg = vjp_n(_nn(dgu, wi_ref[...]))
        dx2_ref[...] = dxx + dxn
        dgu_ref[...] = dgu
        _accumulate(i, dg_ref, dg)

    r = lambda c: _rows(c, tl)
    return _pcall(
        body, name=f"ffn_bwd_{tag}", grid=(L // tl,), rider=rider,
        in_specs=[r(D_MODEL), r(D_MODEL), r(2 * FFN_HIDDEN), _res((1, D_MODEL)), _res(w_in.shape), _res(w_out.shape)],
        out_specs=[r(D_MODEL), r(2 * FFN_HIDDEN), _res((1, D_MODEL))],
        out_shape=[_sds((L, D_MODEL)), _sds((L, 2 * FFN_HIDDEN), BF16), _sds((1, D_MODEL))],
        compiler_params=_params(1, 56),
    )(x2, dx3, gu, g, w_in, w_out)


def _matmul_tn(a, b, tm, tn, tag):
    L, M = a.shape
    N = b.shape[1]
    tk = min(L, 2048)
    nk = L // tk

    def body(a_ref, b_ref, o_ref, acc_ref):
        k = pl.program_id(2)
        _accumulate(k, acc_ref, _tn(a_ref[...], b_ref[...]))

        @pl.when(k == nk - 1)
        def _():
            o_ref[...] = acc_ref[...].astype(BF16)

    return pl.pallas_call(
        body, name=f"matmul_tn_{tag}", grid=(M // tm, N // tn, nk),
        in_specs=[pl.BlockSpec((tk, tm), lambda m, n, k: (k, m)), pl.BlockSpec((tk, tn), lambda m, n, k: (k, n))],
        out_specs=pl.BlockSpec((tm, tn), lambda m, n, k: (m, n)),
        out_shape=_sds((M, N), BF16),
        scratch_shapes=[pltpu.VMEM((tm, tn), F32)],
        compiler_params=_params(3),
    )(a, b)


def _loss_head(y, tgt):
    L = y.shape[0]
    n_tiles = L // TL

    def body(y_ref, t_ref, loss_ref, dy_ref, acc_ref):
        i = pl.program_id(0)
        diff = y_ref[...] - t_ref[...]
        dy_ref[...] = diff * (1.0 / D_MODEL)
        _accumulate(i, acc_ref, jnp.sum(diff * diff, axis=0, keepdims=True))

        @pl.when(i == n_tiles - 1)
        def _():
            loss_ref[...] = jnp.sum(acc_ref[...], axis=1, keepdims=True) * (0.5 / D_MODEL)

    return pl.pallas_call(
        body, name="loss_head", grid=(n_tiles,),
        in_specs=[_rows(D_MODEL), _rows(D_MODEL)],
        out_specs=[_res((1, 1)), _rows(D_MODEL)],
        out_shape=[_sds((1, 1)), _sds((L, D_MODEL))],
        scratch_shapes=[pltpu.VMEM((1, D_MODEL), F32)],
        compiler_params=_params(1),
    )(y, tgt)


def _row(v):
    return v.reshape(1, -1)


def _layer_consts():
    r = np.arange(SB_WIDTH)
    mavg = ((r[:, None] // SB_HEAD_DIM) == (r[None, :] // SB_HEAD_DIM)).astype(np.float32) / SB_HEAD_DIM
    ul, ue = _sb_tri_consts()
    return dict(mavg=jnp.asarray(mavg), ul=ul, ue=ue, mask=_ssm_mask())


def _ssm_rows(P):
    lanes = lambda a: a.reshape(1, SSM_LANES)
    return dict(
        lr=lanes(P['ssm_lam_re']), li=lanes(P['ssm_lam_im']),
        ldt=lanes(jnp.repeat(P['ssm_log_dt'], SSM_STATE)),
        brt=P['ssm_b_re'].transpose(2, 0, 1).reshape(SSM_GROUP, SSM_LANES),
        bit=P['ssm_b_im'].transpose(2, 0, 1).reshape(SSM_GROUP, SSM_LANES),
        crt=P['ssm_c_re'].transpose(1, 0, 2).reshape(SSM_GROUP, SSM_LANES),
        cit=P['ssm_c_im'].transpose(1, 0, 2).reshape(SSM_GROUP, SSM_LANES))


def _layer_fwd(x, mem, P, C, tag, ride):
    gq = _row(jnp.tile(P['sb_q_norm_g'], SB_WIDTH // SB_HEAD_DIM))
    gk = _row(jnp.tile(P['sb_k_norm_g'], SB_WIDTH // SB_HEAD_DIM))
    q, k, v, hg, u, p = _mix_in_fwd(x, _row(P['norm_mix_g']), P['w_in'], gq, gk, C['mavg'], tag,
                                 rider=ride("mix_in_fwd_" + tag))
    o_sb, rsave, n_done = _sb_fwd(q, k, v, C['ul'], tag, rider=ride("sb_fwd_" + tag))
    hp = jnp.pad(hg, ((CONV_HALO, 0), (0, 0)))
    hc = _conv_fwd(hp, P['conv_dw_w'].T, tag)
    S = _ssm_rows(P)
    ar, ai, bbr, bbi, cbr, cbi = _ssm_prep(S['lr'], S['li'], S['ldt'], S['brt'], S['bit'], S['crt'], S['cit'],
                                           C['mask'], tag)
    y, xr, xi = _ssm_fwd(u, ar, ai, bbr, bbi, cbr, cbi, _row(P['ssm_d']), tag, rider=ride("ssm_fwd_" + tag))
    gb = P['branch_norm_g']
    x1 = _mix_out_fwd(x, o_sb, hc, y, _row(P['conv_dw_b']), _row(P['conv_ln_g']), _row(P['conv_ln_b']),
                      P['conv_pw2_w'], P['ssm_glu_w'], _row(gb[0:512]), _row(gb[512:768]), _row(gb[768:1024]),
                      P['w_out'], tag)
    kx, vx = _xa_mem_fwd(mem, _row(P['norm_mem_g']), P['xa_wk'], P['xa_wv'], _row(P['xa_k_norm_g']), tag)
    x2, qp = _xa_fwd(x1, kx, vx, _row(P['norm_xa_g']), P['xa_wq'], _row(P['xa_q_norm_g']), P['xa_wo'], tag,
                 rider=ride("xa_fwd_" + tag))
    x3, hf, gu, act = _ffn_fwd(x2, _row(P['norm_ffn_g']), P['ffn_w_in'], P['ffn_w_out'], tag,
                               rider=ride("ffn_fwd_" + tag))
    saved = dict(p=p, qp=qp, hf=hf, gu=gu, act=act, x=x, q=q, k=k, v=v, rsave=rsave, n_done=n_done, o_sb=o_sb, hp=hp, hc=hc, u=u, y=y, xr=xr, xi=xi, x1=x1, x2=x2,
                 kx=kx, vx=vx, gq=gq, gk=gk, S=S, ssm=(ar, ai, bbr, bbi, cbr, cbi))
    return x3, saved


def _layer_bwd(dx3, mem, P, C, sv, tag, ride, G):
    dx2, dgu, dg = _ffn_bwd(sv['x2'], dx3, sv['gu'], _row(P['norm_ffn_g']), P['ffn_w_in'], P['ffn_w_out'], tag,
                            rider=ride("ffn_bwd_" + tag))
    G['norm_ffn_g'] = dg.reshape(-1)
    G['ffn_w_in'] = _matmul_tn(dgu, sv['hf'], 2 * FFN_HIDDEN // 4, D_MODEL, "ffn_in_" + tag)
    G['ffn_w_out'] = _matmul_tn(sv['act'], dx3, FFN_HIDDEN // 2, 512, "ffn_out_" + tag)
    dx1, dkx, dvx, dwq, dwo, dg, dqg = _xa_bwd(sv['x1'], dx2, sv['qp'], sv['kx'], sv['vx'], _row(P['norm_xa_g']), P['xa_wq'],
                                               _row(P['xa_q_norm_g']), P['xa_wo'], tag, rider=ride("xa_bwd_" + tag))
    G['xa_wq'], G['xa_wo'], G['norm_xa_g'], G['xa_q_norm_g'] = dwq, dwo, dg.reshape(-1), dqg.reshape(-1)
    dwk, dwv, dg, dkg = _xa_mem_bwd(mem, _row(P['norm_mem_g']), P['xa_wk'], P['xa_wv'], _row(P['xa_k_norm_g']),
                                    dkx, dvx, tag)
    G['xa_wk'], G['xa_wv'], G['norm_mem_g'], G['xa_k_norm_g'] = dwk, dwv, dg.reshape(-1), dkg.reshape(-1)
    gb = P['branch_norm_g']
    (do_sb, dhc, dy, ddwb, dlng, dlnb, dpw2, dglu, dg1, dg2, dg3, dwout) = _mix_out_bwd(
        dx1, sv['o_sb'], sv['hc'], sv['y'], _row(P['conv_dw_b']), _row(P['conv_ln_g']), _row(P['conv_ln_b']),
        P['conv_pw2_w'], P['ssm_glu_w'], _row(gb[0:512]), _row(gb[512:768]), _row(gb[768:1024]), P['w_out'], tag)
    G['conv_dw_b'], G['conv_ln_g'], G['conv_ln_b'] = ddwb.reshape(-1), dlng.reshape(-1), dlnb.reshape(-1)
    G['conv_pw2_w'], G['ssm_glu_w'], G['w_out'] = dpw2, dglu, dwout
    G['branch_norm_g'] = jnp.concatenate([dg1.reshape(-1), dg2.reshape(-1), dg3.reshape(-1)])
    ar, ai, bbr, bbi, cbr, cbi = sv['ssm']
    du, dar, dai, dbbr, dbbi, dcbr, dcbi, dd = _ssm_bwd(dy, sv['u'], sv['xr'], sv['xi'], ar, ai, bbr, bbi, cbr, cbi,
                                                        _row(P['ssm_d']), tag, rider=ride("ssm_bwd_" + tag))
    S = sv['S']
    dlr, dli, dldt, dbrt, dbit, dcrt, dcit = _ssm_prep_bwd(S['lr'], S['li'], S['ldt'], S['brt'], S['bit'], C['mask'],
                                                           dar, dai, dbbr, dbbi, dcbr, dcbi, tag)
    G['ssm_lam_re'] = dlr.reshape(SSM_GROUPS, SSM_STATE)
    G['ssm_lam_im'] = dli.reshape(SSM_GROUPS, SSM_STATE)
    G['ssm_log_dt'] = dldt.reshape(SSM_GROUPS, SSM_STATE).sum(axis=1)
    G['ssm_b_re'] = dbrt.reshape(SSM_GROUP, SSM_GROUPS, SSM_STATE).transpose(1, 2, 0)
    G['ssm_b_im'] = dbit.reshape(SSM_GROUP, SSM_GROUPS, SSM_STATE).transpose(1, 2, 0)
    G['ssm_c_re'] = dcrt.reshape(SSM_GROUP, SSM_GROUPS, SSM_STATE).transpose(1, 0, 2)
    G['ssm_c_im'] = dcit.reshape(SSM_GROUP, SSM_GROUPS, SSM_STATE).transpose(1, 0, 2)
    G['ssm_d'] = dd.reshape(-1)
    dpad = jnp.pad(dhc, ((0, CONV_HALO), (0, 0)))
    dhg, ddww = _conv_bwd(dpad, sv['hp'], P['conv_dw_w'].T, tag)
    G['conv_dw_w'] = ddww.T
    dq, dk, dv = _sb_bwd(sv['n_done'], sv['q'], sv['k'], sv['v'], sv['rsave'], do_sb, C['ul'], C['ue'], tag)
    dx, dwin, dg, dgq, dgk = _mix_in_bwd(sv['x'], dx1, sv['p'], dq, dk, dv, dhg, du, _row(P['norm_mix_g']), P['w_in'],
                                         sv['gq'], sv['gk'], C['mavg'], tag)
    G['w_in'], G['norm_mix_g'] = dwin, dg.reshape(-1)
    G['sb_q_norm_g'] = dgq.reshape(SB_WIDTH // SB_HEAD_DIM, SB_HEAD_DIM).sum(axis=0)
    G['sb_k_norm_g'] = dgk.reshape(SB_WIDTH // SB_HEAD_DIM, SB_HEAD_DIM).sum(axis=0)
    return dx, G


def _slot_sum(r_ref):
    g = r_ref[0].astype(F32)
    for s in range(1, r_ref.shape[0]):
        g = g + r_ref[s].astype(F32)
    return g


def _adam_update(g, w, m, v):
    nm = ADAM_B1 * m + (1.0 - ADAM_B1) * g
    nv = ADAM_B2 * v + (1.0 - ADAM_B2) * (g * g)
    m_hat = nm * (1.0 / (1.0 - ADAM_B1 ** ADAM_STEP))
    v_hat = nv * (1.0 / (1.0 - ADAM_B2 ** ADAM_STEP))
    return -ADAM_LR * (m_hat / (jnp.sqrt(v_hat) + ADAM_EPS) + ADAM_WD * w), nm, nv


def _adamw(recv, w, m, v, tile, name):
    n_slots, R, C = recv.shape

    def body(r_ref, w_ref, m_ref, v_ref, g_ref, d_ref, nm_ref, nv_ref):
        g = _slot_sum(r_ref)
        g_ref[...] = g
        d_ref[...], nm_ref[...], nv_ref[...] = _adam_update(g, w_ref[...], m_ref[...], v_ref[...])

    rows = pl.BlockSpec((tile, C), lambda i: (i, 0))
    out = _sds((R, C))
    return pl.pallas_call(
        body, name=name, grid=(R // tile,),
        in_specs=[pl.BlockSpec((n_slots, tile, C), lambda i: (0, i, 0)), rows, rows, rows],
        out_specs=[rows, rows, rows, rows],
        out_shape=[out, out, out, out],
        compiler_params=_params(1),
    )(recv, w, m, v)


def _adamw_layers(recvs, w, m, v, tile, name):
    n_slots, R, C = recvs[0].shape

    def body(*refs):
        r_refs = refs[:DEPTH]
        w_ref, m_ref, v_ref, g_ref, d_ref, nm_ref, nv_ref = refs[DEPTH:]
        for l in range(DEPTH):
            @pl.when(pl.program_id(0) == l)
            def _(l=l):
                g = _slot_sum(r_refs[l])
                g_ref[0] = g
                d_ref[0], nm_ref[0], nv_ref[0] = _adam_update(g, w_ref[0], m_ref[0], v_ref[0])

    rspec = lambda l: pl.BlockSpec((n_slots, tile, C), lambda ll, i: (0, jnp.where(ll == l, i, 0), 0))
    rows = pl.BlockSpec((1, tile, C), lambda ll, i: (ll, i, 0))
    out = _sds((DEPTH, R, C))
    return pl.pallas_call(
        body, name=name, grid=(DEPTH, R // tile),
        in_specs=[rspec(l) for l in range(DEPTH)] + [rows, rows, rows],
        out_specs=[rows, rows, rows, rows],
        out_shape=[out, out, out, out],
        compiler_params=_params(2),
    )(*recvs, w, m, v)


def _reduce_slots(recv, tile, name):
    n_slots, R, C = recv.shape

    def body(r_ref, g_ref):
        g_ref[...] = _slot_sum(r_ref)

    return pl.pallas_call(
        body, name=name, grid=(R // tile,),
        in_specs=[pl.BlockSpec((n_slots, tile, C), lambda i: (0, i, 0))],
        out_specs=pl.BlockSpec((tile, C), lambda i: (i, 0)),
        out_shape=_sds((R, C)),
        compiler_params=_params(1),
    )(recv)


SEG = SUBLANES * LANES


def _pad_to(n, mult):
    return -(-n // mult) * mult


def _pack(arrays, dtype, row_mult, lead=0):
    keep = [(0, 0)] * lead
    parts = []
    for a in arrays:
        flat = a.reshape(a.shape[:lead] + (-1,)).astype(dtype)
        n = flat.shape[-1]
        parts.append(jnp.pad(flat, keep + [(0, _pad_to(n, SEG) - n)]))
    flat = jnp.concatenate(parts, axis=-1)
    n = flat.shape[-1]
    flat = jnp.pad(flat, keep + [(0, _pad_to(n, row_mult * LANES) - n)])
    return flat.reshape(flat.shape[:lead] + (-1, LANES))


def _unpack(buf, shapes):
    lead = buf.shape[:-2]
    flat = buf.reshape(lead + (-1,))
    out, off = [], 0
    for shp in shapes:
        n = int(np.prod(shp))
        out.append(flat[..., off:off + n].reshape(lead + tuple(shp)))
        off += _pad_to(n, SEG)
    return out


def _rows_first(a, name):
    return a.transpose(0, 2, 1) if SHARD_AXIS[name] == 2 else a


SMALL_TILE = 256
DIRECT_NAMES = [n for n in BIG_NAMES if n != 'conv_dw_w']
GATHER_RIDES = {
    "mix_in_fwd_l0": [(0, 'conv_pw2_w'), (0, 'ssm_glu_w'), (0, 'w_out'), (0, 'xa_wq')],
    "sb_fwd_l0": [(0, 'xa_wk'), (0, 'xa_wv'), (0, 'xa_wo'), (0, 'ffn_w_in')],
    "ssm_fwd_l0": [(0, 'ffn_w_out')],
    "xa_fwd_l0": [(1, 'w_in'), (1, 'conv_pw2_w'), (1, 'ssm_glu_w')],
    "ffn_fwd_l0": [(1, 'w_out'), (1, 'xa_wq'), (1, 'xa_wk'), (1, 'xa_wv'), (1, 'xa_wo')],
    "sb_fwd_l1": [(1, 'ffn_w_in'), (1, 'ffn_w_out')],
}
_MID = ['xa_wq', 'xa_wo', 'xa_wk', 'xa_wv', 'w_out', 'conv_pw2_w', 'ssm_glu_w']
SCATTER_RIDES = {
    "xa_bwd_l1": [(1, 'ffn_w_in'), (1, 'ffn_w_out')],
    "ssm_bwd_l1": [(1, n) for n in _MID],
    "ffn_bwd_l0": [(1, 'w_in')],
    "xa_bwd_l0": [(0, 'ffn_w_in'), (0, 'ffn_w_out')],
    "ssm_bwd_l0": [(0, n) for n in _MID],
}


def _tile_rows(rows):
    return next(t for t in range(min(rows, 256), 0, -ROW_ALIGN) if rows % t == 0 and t % ROW_ALIGN == 0)


class _LayerWeights:
    def __init__(self, layer, small, full, conv):
        self.layer, self.small, self.full, self.conv = layer, small, full, conv

    def __getitem__(self, name):
        if name == 'conv_dw_w':
            return self.conv[self.layer]
        return self.full[(self.layer, name)] if name in SHARD_AXIS else self.small[name][self.layer]


def kernel(x, mem, norm_mix_g, w_in, sb_q_norm_g, sb_k_norm_g, conv_dw_w, conv_dw_b, conv_ln_g, conv_ln_b, conv_pw2_w, ssm_lam_re, ssm_lam_im, ssm_log_dt, ssm_b_re, ssm_b_im, ssm_c_re, ssm_c_im, ssm_d, ssm_glu_w, branch_norm_g, w_out, norm_xa_g, norm_mem_g, xa_wq, xa_wk, xa_wv, xa_q_norm_g, xa_k_norm_g, xa_wo, norm_ffn_g, ffn_w_in, ffn_w_out, loss_target, m_norm_mix_g, m_w_in, m_sb_q_norm_g, m_sb_k_norm_g, m_conv_dw_w, m_conv_dw_b, m_conv_ln_g, m_conv_ln_b, m_conv_pw2_w, m_ssm_lam_re, m_ssm_lam_im, m_ssm_log_dt, m_ssm_b_re, m_ssm_b_im, m_ssm_c_re, m_ssm_c_im, m_ssm_d, m_ssm_glu_w, m_branch_norm_g, m_w_out, m_norm_xa_g, m_norm_mem_g, m_xa_wq, m_xa_wk, m_xa_wv, m_xa_q_norm_g, m_xa_k_norm_g, m_xa_wo, m_norm_ffn_g, m_ffn_w_in, m_ffn_w_out, v_norm_mix_g, v_w_in, v_sb_q_norm_g, v_sb_k_norm_g, v_conv_dw_w, v_conv_dw_b, v_conv_ln_g, v_conv_ln_b, v_conv_pw2_w, v_ssm_lam_re, v_ssm_lam_im, v_ssm_log_dt, v_ssm_b_re, v_ssm_b_im, v_ssm_c_re, v_ssm_c_im, v_ssm_d, v_ssm_glu_w, v_branch_norm_g, v_w_out, v_norm_xa_g, v_norm_mem_g, v_xa_wq, v_xa_wk, v_xa_wv, v_xa_q_norm_g, v_xa_k_norm_g, v_xa_wo, v_norm_ffn_g, v_ffn_w_in, v_ffn_w_out):
    args = locals()
    w_loc = {n: args[n] for n in WEIGHT_NAMES}
    m_loc = {n: args["m_" + n] for n in WEIGHT_NAMES}
    v_loc = {n: args["v_" + n] for n in WEIGHT_NAMES}
    me = _my_index()
    shard = {(l, n): _rows_first(w_loc[n], n)[l].astype(BF16) for l in range(DEPTH) for n in DIRECT_NAMES}
    conv_shape = _rows_first(w_loc['conv_dw_w'], 'conv_dw_w').shape
    conv_rows = _pack([_rows_first(w_loc['conv_dw_w'], 'conv_dw_w')], F32, SUBLANES)
    full = {}
    recv = {}
    grads = [dict() for _ in range(DEPTH)]

    def ride(kernel_name):
        if kernel_name in GATHER_RIDES:
            keys = GATHER_RIDES[kernel_name]
            return _Rider(gathers=[shard[k] for k in keys], done=lambda res: full.update(zip(keys, res)))
        if kernel_name in SCATTER_RIDES:
            keys = SCATTER_RIDES[kernel_name]
            return _Rider(scatters=[grads[l][n].astype(BF16) for (l, n) in keys],
                          done=lambda res: recv.update(zip(keys, res)))
        return None

    first = []
    _exchange(_Rider(gathers=[shard[(0, 'w_in')], conv_rows], done=first.extend), "gather_first")
    full[(0, 'w_in')] = first[0]
    conv_all = first[1].reshape(N_DEV, -1)[:, :int(np.prod(conv_shape))].reshape((N_DEV,) + conv_shape)
    conv_full = conv_all.transpose(1, 0, 2, 3).reshape(DEPTH, N_DEV * conv_shape[1], conv_shape[2])
    weights = [_LayerWeights(l, w_loc, full, conv_full) for l in range(DEPTH)]

    consts = _layer_consts()
    h, saved = x[0], []
    for l in range(DEPTH):
        h, sv = _layer_fwd(h, mem[0], weights[l], consts, f"l{l}", ride)
        saved.append(sv)
    loss_part, dh = _loss_head(h, loss_target[0])
    for l in reversed(range(DEPTH)):
        dh, _ = _layer_bwd(dh, mem[0], weights[l], consts, saved[l], f"l{l}", ride, grads[l])
    grad_x = dh
    loss = lax.psum(loss_part[0, 0], MESH_AXES)

    small_shapes = [w_loc[n].shape for n in SMALL_NAMES]
    conv_nat = (DEPTH,) + grads[0]['conv_dw_w'].shape[::-1]
    small_send = _pack([jnp.stack([grads[l][n] for l in range(DEPTH)]) for n in SMALL_NAMES]
                       + [jnp.stack([grads[l]['conv_dw_w'].T for l in range(DEPTH)])], F32, SMALL_TILE)
    last = []
    _exchange(_Rider(gathers=[small_send], scatters=[grads[0]['w_in'].astype(BF16)], done=last.extend), "exchange_last")
    recv[(0, 'w_in')] = last[1]
    small_sum = _reduce_slots(last[0].reshape((N_DEV,) + small_send.shape), SMALL_TILE, "reduce_replicated")
    small_g = _unpack(small_sum, small_shapes + [conv_nat])
    conv_cols = w_loc['conv_dw_w'].shape[2]
    conv_g = lax.dynamic_slice_in_dim(small_g[-1], me * conv_cols, conv_cols, axis=2)

    result = [{}, {}, {}, {}]
    for n in DIRECT_NAMES:
        parts = [recv[(l, n)] for l in range(DEPTH)]
        if SHARD_AXIS[n] == 2:
            parts = [_reduce_slots(p, _tile_rows(p.shape[1]), f"reduce_{n}_l{l}").T[None] for l, p in enumerate(parts)]
        outs = _adamw_layers(parts, w_loc[n], m_loc[n], v_loc[n], _tile_rows(w_loc[n].shape[1]), f"adamw_{n}")
        for kind in range(4):
            result[kind][n] = outs[kind]
    packed_names = SMALL_NAMES + ['conv_dw_w']
    pk = lambda d: _pack([d[n] for n in packed_names], F32, SMALL_TILE)
    outs = _adamw(_pack(small_g[:-1] + [conv_g], F32, SMALL_TILE)[None], pk(w_loc), pk(m_loc), pk(v_loc), SMALL_TILE,
                  "adamw_packed")
    for kind in range(4):
        for n, a in zip(packed_names, _unpack(outs[kind], [w_loc[n].shape for n in packed_names])):
            result[kind][n] = a
    return (loss, grad_x[None], *[result[0][n] for n in WEIGHT_NAMES], *[result[1][n] for n in WEIGHT_NAMES],
            *[result[2][n] for n in WEIGHT_NAMES], *[result[3][n] for n in WEIGHT_NAMES])
```

```python
import functools

import numpy as np
import jax
import jax.numpy as jnp
from jax import lax
from jax.experimental import pallas as pl
from jax.experimental.pallas import tpu as pltpu

F32 = jnp.float32
BF16 = jnp.bfloat16
EPS = 1e-6
D_MODEL = 1024
DEPTH = 2
N_DEV = 8
SB_WIDTH = 512
SB_HEAD_DIM = 64
CONV_CH = 256
CONV_WIDTH = 31
SSM_CH = 256
SSM_GROUP = 16
SSM_GROUPS = 16
SSM_STATE = 64
SSM_LANES = SSM_GROUPS * SSM_STATE
XA_HEADS = 4
XA_HEAD_DIM = 256
FFN_HIDDEN = 2816
ADAM_LR = 0.001
ADAM_B1 = 0.9
ADAM_B2 = 0.999
ADAM_EPS = 1e-08
ADAM_WD = 0.01
ADAM_STEP = 10

LANES = 128
SUBLANES = 8
TL = 512
SB_TQ = 256
SB_TK = 128
SB_PAIRS = 2
SB_SCALE = SB_HEAD_DIM ** -0.5
SB_DEAD = -120.0
SSM_T = 256
CONV_HALO = 32
CONV_SUB = 64
VMEM_MB = 48

MESH_AXES = ("x", "y", "c")
WEIGHT_NAMES = ['norm_mix_g', 'w_in', 'sb_q_norm_g', 'sb_k_norm_g', 'conv_dw_w', 'conv_dw_b', 'conv_ln_g',
                'conv_ln_b', 'conv_pw2_w', 'ssm_lam_re', 'ssm_lam_im', 'ssm_log_dt', 'ssm_b_re', 'ssm_b_im',
                'ssm_c_re', 'ssm_c_im', 'ssm_d', 'ssm_glu_w', 'branch_norm_g', 'w_out', 'norm_xa_g',
                'norm_mem_g', 'xa_wq', 'xa_wk', 'xa_wv', 'xa_q_norm_g', 'xa_k_norm_g', 'xa_wo', 'norm_ffn_g',
                'ffn_w_in', 'ffn_w_out']
SHARD_AXIS = {'w_in': 2, 'conv_dw_w': 2, 'conv_pw2_w': 1, 'ssm_glu_w': 2, 'w_out': 1, 'xa_wq': 1, 'xa_wk': 1,
              'xa_wv': 1, 'xa_wo': 1, 'ffn_w_in': 2, 'ffn_w_out': 1}
BIG_NAMES = [n for n in WEIGHT_NAMES if n in SHARD_AXIS]
SMALL_NAMES = [n for n in WEIGHT_NAMES if n not in SHARD_AXIS]


def _nn(a, b):
    return jnp.dot(a.astype(BF16), b.astype(BF16), preferred_element_type=F32)


def _nt(a, b):
    return lax.dot_general(a.astype(BF16), b.astype(BF16), (((1,), (1,)), ((), ())), preferred_element_type=F32)


def _tn(a, b):
    return lax.dot_general(a.astype(BF16), b.astype(BF16), (((0,), (0,)), ((), ())), preferred_element_type=F32)


def _rms(x, g):
    return x * lax.rsqrt(jnp.mean(x * x, axis=-1, keepdims=True) + EPS) * g


def _sigmoid(x):
    return 1.0 / (1.0 + jnp.exp(-x))


def _silu(x):
    return x * _sigmoid(x)


def _layer_norm(x, g, b):
    mu = jnp.mean(x, axis=-1, keepdims=True)
    xc = x - mu
    var = jnp.mean(xc * xc, axis=-1, keepdims=True)
    return xc * lax.rsqrt(var + EPS) * g + b


def _head_rms64(p, g, mavg):
    ms = jnp.dot(p * p, mavg, preferred_element_type=F32)
    return p * lax.rsqrt(ms + EPS) * g


def _params(n_grid, vmem_mb=VMEM_MB):
    return pltpu.CompilerParams(dimension_semantics=("arbitrary",) * n_grid, vmem_limit_bytes=vmem_mb << 20)


def _rows(cols, tl=TL):
    return pl.BlockSpec((tl, cols), lambda i: (i, 0))


def _res(shape):
    nd = len(shape)
    return pl.BlockSpec(tuple(shape), lambda *_: (0,) * nd)


def _sds(shape, dtype=F32):
    return jax.ShapeDtypeStruct(tuple(shape), dtype)


def _accumulate(i, ref, val):
    @pl.when(i == 0)
    def _():
        ref[...] = val

    @pl.when(i > 0)
    def _():
        ref[...] += val


HBM = pl.BlockSpec(memory_space=pltpu.HBM)
ROW_ALIGN = 16


def _my_index():
    return lax.axis_index("x") * 4 + lax.axis_index("y") * 2 + lax.axis_index("c")


def _peer(k):
    x, y, c = lax.axis_index("x"), lax.axis_index("y"), lax.axis_index("c")
    return (x ^ ((k >> 2) & 1), y ^ ((k >> 1) & 1), c ^ (k & 1))


class _Rider:
    def __init__(self, gathers=(), scatters=(), done=None):
        self.gathers, self.scatters, self.done = list(gathers), list(scatters), done

    @property
    def inputs(self):
        return self.gathers + self.scatters

    def out_shapes(self):
        return ([_sds((N_DEV * a.shape[0],) + a.shape[1:], a.dtype) for a in self.gathers]
                + [_sds((N_DEV, a.shape[0] // N_DEV) + a.shape[1:], a.dtype) for a in self.scatters])


def _rider_copies(rider, in_refs, out_refs, send_sems, recv_sems, local_sems):
    me = _my_index()
    local, sends, recvs = [], [], []
    for t, (src, dst) in enumerate(zip(in_refs, out_refs)):
        gather = t < len(rider.gathers)
        rows = src.shape[0] if gather else src.shape[0] // N_DEV

        def block(ref, d, rows=rows):
            return ref.at[pl.ds(pl.multiple_of(d * rows, ROW_ALIGN), rows)]

        src_for = (lambda p, src=src: src) if gather else (lambda p, src=src: block(src, p))
        dst_for = (lambda d, dst=dst: block(dst, d)) if gather else (lambda d, dst=dst: dst.at[d])
        local.append(pltpu.make_async_copy(src_for(me), dst_for(me), local_sems.at[t]))
        for k in range(1, N_DEV):
            args = dict(send_sem=send_sems.at[t * N_DEV + k], recv_sem=recv_sems.at[t * N_DEV + k], device_id=_peer(k),
                        device_id_type=pl.DeviceIdType.MESH)
            sends.append(pltpu.make_async_remote_copy(src_ref=src_for(me ^ k), dst_ref=dst_for(me), **args))
            recvs.append(pltpu.make_async_remote_copy(src_ref=src_for(me ^ k), dst_ref=dst_for(me ^ k), **args))
    return local, sends, recvs


def _pcall(body, *, name, out_shape, grid=(), in_specs=None, out_specs=None, scratch_shapes=(), compiler_params=None,
           rider=None):
    if rider is None or not rider.inputs:
        return pl.pallas_call(body, name=name, grid=grid, in_specs=in_specs, out_specs=out_specs, out_shape=out_shape,
                              scratch_shapes=list(scratch_shapes), compiler_params=compiler_params)
    single = not isinstance(out_shape, (list, tuple))
    outs = [out_shape] if single else list(out_shape)
    ospecs = [out_specs] if single else list(out_specs)
    n_in, n_out, n_scr, n_r = len(in_specs), len(outs), len(scratch_shapes), len(rider.inputs)

    def wrapped(*refs):
        ins, rin = refs[:n_in], refs[n_in:n_in + n_r]
        own_out = refs[n_in + n_r:n_in + n_r + n_out]
        rout = refs[n_in + n_r + n_out:n_in + 2 * n_r + n_out]
        scratch = refs[n_in + 2 * n_r + n_out:n_in + 2 * n_r + n_out + n_scr]
        local, sends, recvs = _rider_copies(rider, rin, rout, *refs[-3:])

        def start():
            for cp in local + sends:
                cp.start()

        def wait():
            for cp in recvs:
                cp.wait_recv()
            for cp in sends:
                cp.wait_send()
            for cp in local:
                cp.wait()

        if grid:
            ids = [pl.program_id(a) for a in range(len(grid))]
            first = functools.reduce(jnp.logical_and, [i == 0 for i in ids])
            last = functools.reduce(jnp.logical_and, [i == n - 1 for i, n in zip(ids, grid)])
            pl.when(first)(start)
            body(*ins, *own_out, *scratch)
            pl.when(last)(wait)
        else:
            start()
            body(*ins, *own_out, *scratch)
            wait()

    call = pl.pallas_call(
        wrapped, name=name, grid=grid, in_specs=list(in_specs) + [HBM] * n_r, out_specs=ospecs + [HBM] * n_r,
        out_shape=outs + rider.out_shapes(),
        scratch_shapes=list(scratch_shapes) + [pltpu.SemaphoreType.DMA((n_r * N_DEV,)),
                                               pltpu.SemaphoreType.DMA((n_r * N_DEV,)), pltpu.SemaphoreType.DMA((n_r,))],
        compiler_params=compiler_params)

    def run(*args):
        res = call(*args, *rider.inputs)
        if rider.done is not None:
            rider.done(list(res[n_out:]))
        return res[0] if single else list(res[:n_out])

    return run


def _exchange(rider, name):
    def body():
        pass

    _pcall(body, name=name, out_shape=[], in_specs=[], out_specs=[], rider=rider)()


def _mixin_post(pq, pk, a, b, gq, gk, mavg):
    return _head_rms64(pq, gq, mavg), _head_rms64(pk, gk, mavg), a * _sigmoid(b)


def _mix_in_fwd(x, g_mix, w_in, gq, gk, mavg, tag, rider=None):
    L = x.shape[0]

    def body(x_ref, g_ref, w_ref, gq_ref, gk_ref, mavg_ref, q_ref, k_ref, v_ref, hg_ref, u_ref, p_ref):
        h = _rms(x_ref[...], g_ref[...])
        p = _nt(h, w_ref[...])
        q, k, hg = _mixin_post(p[:, 0:512], p[:, 512:1024], p[:, 1536:1792], p[:, 1792:2048],
                               gq_ref[...], gk_ref[...], mavg_ref[...])
        q_ref[...] = (q * SB_SCALE).astype(BF16)
        k_ref[...] = k.astype(BF16)
        v_ref[...] = p[:, 1024:1536].astype(BF16)
        hg_ref[...] = hg
        u_ref[...] = p[:, 2048:2304]
        p_ref[...] = p.astype(BF16)

    return _pcall(
        body, name=f"mix_in_fwd_{tag}", grid=(L // TL,),
        in_specs=[_rows(D_MODEL), _res((1, D_MODEL)), _res(w_in.shape), _res((1, 512)), _res((1, 512)), _res((512, 512))],
        out_specs=[_rows(512), _rows(512), _rows(512), _rows(256), _rows(256), _rows(w_in.shape[0])],
        out_shape=[_sds((L, 512), BF16), _sds((L, 512), BF16), _sds((L, 512), BF16), _sds((L, 256)), _sds((L, 256)),
                   _sds((L, w_in.shape[0]), BF16)],
        compiler_params=_params(1), rider=rider,
    )(x, g_mix, w_in, gq, gk, mavg)


def _mix_in_bwd(x, dres, p_kept, dq, dk, dv, dhg, du, g_mix, w_in, gq, gk, mavg, tag):
    L = x.shape[0]
    tl = 256

    def body(x_ref, dres_ref, p_ref, dq_ref, dk_ref, dv_ref, dhg_ref, du_ref, g_ref, w_ref, gq_ref, gk_ref, mavg_ref,
             dx_ref, dw_ref, dg_ref, dgq_ref, dgk_ref):
        i = pl.program_id(0)
        xx = x_ref[...]
        g = g_ref[...]
        mavg_v = mavg_ref[...]
        h, vjp_n = jax.vjp(_rms, xx, g)
        p = p_ref[...].astype(F32)
        _, vjp_p = jax.vjp(lambda pq, pk, a, b, gq_, gk_: _mixin_post(pq, pk, a, b, gq_, gk_, mavg_v),
                           p[:, 0:512], p[:, 512:1024], p[:, 1536:1792], p[:, 1792:2048], gq_ref[...], gk_ref[...])
        dpq, dpk, da, db, dgq, dgk = vjp_p((dq_ref[...], dk_ref[...], dhg_ref[...]))
        dp = jnp.concatenate([dpq, dpk, dv_ref[...], da, db, du_ref[...]], axis=1)
        dh = _nn(dp, w_ref[...])
        dxn, dg = vjp_n(dh)
        dx_ref[...] = dres_ref[...] + dxn
        _accumulate(i, dw_ref, _tn(dp, h))
        _accumulate(i, dg_ref, dg)
        _accumulate(i, dgq_ref, dgq)
        _accumulate(i, dgk_ref, dgk)

    r = lambda c: _rows(c, tl)
    return pl.pallas_call(
        body, name=f"mix_in_bwd_{tag}", grid=(L // tl,),
        in_specs=[r(D_MODEL), r(D_MODEL), r(w_in.shape[0]), r(512), r(512), r(512), r(256), r(256),
                  _res((1, D_MODEL)), _res(w_in.shape), _res((1, 512)), _res((1, 512)), _res((512, 512))],
        out_specs=[r(D_MODEL), _res(w_in.shape), _res((1, D_MODEL)), _res((1, 512)), _res((1, 512))],
        out_shape=[_sds((L, D_MODEL)), _sds(w_in.shape), _sds((1, D_MODEL)), _sds((1, 512)), _sds((1, 512))],
        compiler_params=_params(1),
    )(x, dres, p_kept, dq, dk, dv, dhg, du, g_mix, w_in, gq, gk, mavg)


def _sb_tri_consts():
    r = np.arange(2 * SB_TK)[:, None]
    c = np.arange(2 * SB_TK)[None, :]
    same = (r // SB_TK) == (c // SB_TK)
    later = (same & (r > c)).astype(np.float32)
    earlier = (same & (r < c)).astype(np.float32)
    return jnp.asarray(later, BF16), jnp.asarray(earlier, BF16)


def _two_heads(blk, lane_a):
    zero = jnp.zeros_like(blk)
    return jnp.concatenate([jnp.where(lane_a, blk, zero), jnp.where(lane_a, zero, blk)], axis=0)


def _sb_logs(z, i, j, masked):
    e = jnp.exp(-jnp.abs(z))
    lm = -(jnp.maximum(z, 0.0) + jnp.log(1.0 + e))
    ls = z + lm
    valid = None
    if masked:
        row = lax.broadcasted_iota(jnp.int32, (SB_TQ, 2 * SB_TK), 0)
        col = lax.broadcasted_iota(jnp.int32, (SB_TQ, 2 * SB_TK), 1) & (SB_TK - 1)
        valid = (j * SB_TK + col) < (i * SB_TQ + row)
        lm = jnp.where(valid, lm, 0.0)
    return lm, ls, lm.astype(BF16), valid


def _lane_halves(a, b):
    return jnp.concatenate([jnp.broadcast_to(a, (SB_TQ, SB_TK)), jnp.broadcast_to(b, (SB_TQ, SB_TK))], axis=1)


def _dot(a, b):
    return jnp.dot(a, b, preferred_element_type=F32)


def _dot_nt(a, b):
    return lax.dot_general(a, b, (((1,), (1,)), ((), ())), preferred_element_type=F32)


def _dot_tn(a, b):
    return lax.dot_general(a, b, (((0,), (0,)), ((), ())), preferred_element_type=F32)


def _sb_fwd(q, k, v, ul, tag, rider=None):
    L = q.shape[0]
    nq = L // SB_TQ
    per = SB_TQ // SB_TK
    wid = SB_PAIRS * LANES

    def body(q_ref, k_ref, v_ref, ul_ref, o_ref, rs_ref, n_ref):
        i = pl.program_id(1)
        ulv = ul_ref[...]
        lane_a = lax.broadcasted_iota(jnp.int32, (1, LANES), 1) < SB_HEAD_DIM
        lane_q = lax.broadcasted_iota(jnp.int32, (SB_TQ, LANES), 1)
        cols = [slice(p * LANES, (p + 1) * LANES) for p in range(SB_PAIRS)]
        qbs = [q_ref[:, c] for c in cols]

        def double_step(jhi, carry, masked):
            chains = [dict(p=p, j=jhi - d) for d in range(2) for p in range(SB_PAIRS)]
            for c in chains:
                c['off'] = pl.multiple_of(c['j'] * SB_TK, SB_TK)
                kb = k_ref[pl.ds(c['off'], SB_TK), cols[c['p']]]
                c['z'] = _dot_nt(qbs[c['p']], _two_heads(kb, lane_a))
            for c in chains:
                c['lm'], c['ls'], c['hi'], c['valid'] = _sb_logs(c.pop('z'), i, c['j'], masked)
            for c in chains:
                c['lb'] = _dot(c.pop('hi'), ulv)
            state = [list(s) for s in carry]
            for c in chains:
                ra, rb, _, rsave = state[c['p']]
                w = jnp.exp(c['ls'] + c['lb'] + _lane_halves(ra, rb))
                if masked:
                    w = jnp.where(c['valid'], w, 0.0)
                c['w'] = w.astype(BF16)
                lb, lm = c['lb'], c['lm']
                state[c['p']][3] = jnp.where(lane_q == c['j'], ra, jnp.where(lane_q == c['j'] + SB_HEAD_DIM, rb, rsave))
                state[c['p']][0] = ra + lb[:, 0:1] + lm[:, 0:1]
                state[c['p']][1] = rb + lb[:, SB_TK:SB_TK + 1] + lm[:, SB_TK:SB_TK + 1]
            for c in chains:
                vb = v_ref[pl.ds(c['off'], SB_TK), cols[c['p']]]
                state[c['p']][2] = state[c['p']][2] + _dot(c['w'], _two_heads(vb, lane_a))
            return tuple(tuple(s) for s in state)

        assert per == 2
        carry = tuple((jnp.zeros((SB_TQ, 1), F32), jnp.zeros((SB_TQ, 1), F32),
                       jnp.zeros((SB_TQ, LANES), F32), jnp.zeros((SB_TQ, LANES), F32)) for _ in range(SB_PAIRS))
        def alive(carry):
            m = carry[0][0]
            for c in carry:
                m = jnp.maximum(m, jnp.maximum(c[0], c[1]))
            return jnp.max(m) > SB_DEAD

        carry = double_step(i * per + 1, carry, True)
        n_done, _, carry = lax.while_loop(
            lambda st: jnp.logical_and(st[0] < i, st[1]),
            lambda st: (lambda c: (st[0] + 1, alive(c), c))(double_step(i * per - 1 - 2 * st[0], st[2], False)),
            (jnp.int32(0), alive(carry), carry))
        o_ref[...] = jnp.concatenate([c[2] for c in carry], axis=1)
        rs_ref[...] = jnp.concatenate([c[3] for c in carry], axis=1)
        n_ref[pl.program_id(0), i] = n_done

    qspec = pl.BlockSpec((SB_TQ, wid), lambda g, i: (i, g))
    kspec = pl.BlockSpec((L, wid), lambda g, i: (0, g))
    return _pcall(
        body, name=f"sb_fwd_{tag}", grid=(SB_WIDTH // wid, nq),
        in_specs=[qspec, kspec, kspec, pl.BlockSpec((2 * SB_TK, 2 * SB_TK), lambda g, i: (0, 0))],
        out_specs=[qspec, qspec, pl.BlockSpec(memory_space=pltpu.SMEM)],
        out_shape=[_sds((L, SB_WIDTH)), _sds((L, SB_WIDTH)), _sds((SB_WIDTH // wid, nq), jnp.int32)],
        compiler_params=_params(2), rider=rider,
    )(q, k, v, ul)


def _sb_bwd(n_done, q, k, v, rsave, do, ul, ue, tag):
    L = q.shape[0]
    nq = L // SB_TQ
    per = SB_TQ // SB_TK
    wid = SB_PAIRS * LANES

    def body(n_ref, q_ref, k_ref, v_ref, rs_ref, do_ref, ul_ref, ue_ref, dq_ref, dk_ref, dv_ref):
        i = pl.program_id(1)

        @pl.when(i == 0)
        def _():
            dk_ref[...] = jnp.zeros_like(dk_ref)
            dv_ref[...] = jnp.zeros_like(dv_ref)

        ulv = ul_ref[...]
        uev = ue_ref[...]
        lane_a = lax.broadcasted_iota(jnp.int32, (1, LANES), 1) < SB_HEAD_DIM
        lane_q = lax.broadcasted_iota(jnp.int32, (SB_TQ, LANES), 1)
        cols = [slice(p * LANES, (p + 1) * LANES) for p in range(SB_PAIRS)]
        qbs = [q_ref[:, c] for c in cols]
        dobs = [do_ref[:, c].astype(BF16) for c in cols]
        rsvs = [rs_ref[:, c] for c in cols]

        def double_step(jlo, carry, masked):
            chains = [dict(p=p, j=jlo + d) for d in range(2) for p in range(SB_PAIRS)]
            for c in chains:
                p = c['p']
                c['off'] = pl.multiple_of(c['j'] * SB_TK, SB_TK)
                c['kk2'] = _two_heads(k_ref[pl.ds(c['off'], SB_TK), cols[p]], lane_a)
                c['z'] = _dot_nt(qbs[p], c['kk2'])
                c['dw'] = _dot_nt(dobs[p], _two_heads(v_ref[pl.ds(c['off'], SB_TK), cols[p]], lane_a))
            for c in chains:
                c['lm'], c['ls'], c['hi'], c['valid'] = _sb_logs(c.pop('z'), i, c['j'], masked)
                c['ra'] = jnp.sum(jnp.where(lane_q == c['j'], rsvs[c['p']], 0.0), axis=1, keepdims=True)
                c['rb'] = jnp.sum(jnp.where(lane_q == c['j'] + SB_HEAD_DIM, rsvs[c['p']], 0.0), axis=1, keepdims=True)
            for c in chains:
                c['lb'] = _dot(c.pop('hi'), ulv)
            for c in chains:
                w = jnp.exp(c['ls'] + c.pop('lb') + _lane_halves(c['ra'], c['rb']))
                if masked:
                    w = jnp.where(c['valid'], w, 0.0)
                c['wb'] = w.astype(BF16)
                gg = w * c.pop('dw')
                c['gg'] = gg
                c['beta'] = jnp.exp(c['ls'])
            for c in chains:
                c['cb'] = _dot(c['gg'].astype(BF16), uev)
                c['dv2'] = _dot_tn(c.pop('wb'), dobs[c['p']])
            state = [list(s) for s in carry]
            for c in chains:
                pa, pb, _ = state[c['p']]
                gg, cb, beta = c['gg'], c['cb'], c['beta']
                dz = gg * (1.0 - beta) - beta * (cb + _lane_halves(pa, pb))
                if masked:
                    dz = jnp.where(c['valid'], dz, 0.0)
                c['dzb'] = dz.astype(BF16)
                state[c['p']][0] = pa + cb[:, SB_TK - 1:SB_TK] + gg[:, SB_TK - 1:SB_TK]
                state[c['p']][1] = pb + cb[:, 2 * SB_TK - 1:2 * SB_TK] + gg[:, 2 * SB_TK - 1:2 * SB_TK]
            for c in chains:
                c['dqc'] = _dot(c['dzb'], c['kk2'])
                c['dk2'] = _dot_tn(c['dzb'], qbs[c['p']])
            for c in chains:
                p, dk2, dv2 = c['p'], c['dk2'], c['dv2']
                state[p][2] = state[p][2] + c['dqc']
                dk_ref[pl.ds(c['off'], SB_TK), cols[p]] += jnp.where(lane_a, dk2[0:SB_TK], dk2[SB_TK:2 * SB_TK])
                dv_ref[pl.ds(c['off'], SB_TK), cols[p]] += jnp.where(lane_a, dv2[0:SB_TK], dv2[SB_TK:2 * SB_TK])
            return tuple(tuple(s) for s in state)

        assert per == 2
        carry = tuple((jnp.zeros((SB_TQ, 1), F32), jnp.zeros((SB_TQ, 1), F32), jnp.zeros((SB_TQ, LANES), F32))
                      for _ in range(SB_PAIRS))
        first = i - n_ref[pl.program_id(0), i]
        carry = lax.fori_loop(first, i, lambda jj, c: double_step(2 * jj, c, False), carry)
        carry = double_step(i * per, carry, True)
        dq_ref[...] = jnp.concatenate([c[2] for c in carry], axis=1) * SB_SCALE

    qspec = pl.BlockSpec((SB_TQ, wid), lambda g, i: (i, g))
    kspec = pl.BlockSpec((L, wid), lambda g, i: (0, g))
    kin = pl.BlockSpec((L, wid), lambda g, i: (0, g), pipeline_mode=pl.Buffered(1))
    cspec = pl.BlockSpec((2 * SB_TK, 2 * SB_TK), lambda g, i: (0, 0))
    return pl.pallas_call(
        body, name=f"sb_bwd_{tag}", grid=(SB_WIDTH // wid, nq),
        in_specs=[pl.BlockSpec(memory_space=pltpu.SMEM), qspec, kin, kin, qspec, qspec, cspec, cspec],
        out_specs=[qspec, kspec, kspec],
        out_shape=[_sds((L, SB_WIDTH)), _sds((L, SB_WIDTH)), _sds((L, SB_WIDTH))],
        compiler_params=_params(2, 58),
    )(n_done, q, k, v, rsave, do, ul, ue)


def _conv_fwd(hp, w, tag):
    L = hp.shape[0] - CONV_HALO
    win_rows = CONV_SUB + CONV_HALO

    def body(hp_ref, w_ref, o_ref):
        i = pl.program_id(0)

        def sub(s, _):
            t0 = pl.multiple_of(i * TL + s * CONV_SUB, CONV_SUB)
            win = hp_ref[pl.ds(t0, win_rows), :]
            acc = jnp.zeros((CONV_SUB, CONV_CH), F32)
            for kk in range(CONV_WIDTH):
                sh = CONV_WIDTH - 1 - kk
                r = win if sh == 0 else pltpu.roll(win, sh, 0)
                acc = acc + w_ref[kk:kk + 1, :] * r[CONV_HALO:, :]
            o_ref[pl.ds(pl.multiple_of(s * CONV_SUB, CONV_SUB), CONV_SUB), :] = acc
            return 0

        lax.fori_loop(0, TL // CONV_SUB, sub, 0)

    return pl.pallas_call(
        body, name=f"conv_fwd_{tag}", grid=(L // TL,),
        in_specs=[_res(hp.shape), _res(w.shape)],
        out_specs=_rows(CONV_CH),
        out_shape=_sds((L, CONV_CH)),
        compiler_params=_params(1),
    )(hp, w)


def _conv_bwd(dpad, hp, w, tag):
    L = hp.shape[0] - CONV_HALO
    win_rows = CONV_SUB + CONV_HALO
    n_tiles = L // TL

    def body(dp_ref, hp_ref, w_ref, dh_ref, dw_ref, acc_ref):
        i = pl.program_id(0)

        @pl.when(i == 0)
        def _():
            acc_ref[...] = jnp.zeros_like(acc_ref)

        def sub(s, _):
            t0 = pl.multiple_of(i * TL + s * CONV_SUB, CONV_SUB)
            wd = dp_ref[pl.ds(t0, win_rows), :]
            wh = hp_ref[pl.ds(t0, win_rows), :]
            dy = wd[0:CONV_SUB, :]
            acc = jnp.zeros((CONV_SUB, CONV_CH), F32)
            for kk in range(CONV_WIDTH):
                sh = CONV_WIDTH - 1 - kk
                rd = wd if sh == 0 else pltpu.roll(wd, win_rows - sh, 0)
                acc = acc + w_ref[kk:kk + 1, :] * rd[0:CONV_SUB, :]
                rh = wh if sh == 0 else pltpu.roll(wh, sh, 0)
                prod = dy * rh[CONV_HALO:, :]
                part = prod[0:SUBLANES]
                for m in range(1, CONV_SUB // SUBLANES):
                    part = part + prod[m * SUBLANES:(m + 1) * SUBLANES]
                acc_ref[kk] += part
            dh_ref[pl.ds(pl.multiple_of(s * CONV_SUB, CONV_SUB), CONV_SUB), :] = acc
            return 0

        lax.fori_loop(0, TL // CONV_SUB, sub, 0)

        @pl.when(i == n_tiles - 1)
        def _():
            for kk in range(CONV_WIDTH):
                dw_ref[kk:kk + 1, :] = jnp.sum(acc_ref[kk], axis=0, keepdims=True)

    return pl.pallas_call(
        body, name=f"conv_bwd_{tag}", grid=(n_tiles,),
        in_specs=[_res(dpad.shape), _res(hp.shape), _res(w.shape)],
        out_specs=[_rows(CONV_CH), _res(w.shape)],
        out_shape=[_sds((L, CONV_CH)), _sds(w.shape)],
        scratch_shapes=[pltpu.VMEM((CONV_WIDTH, SUBLANES, CONV_CH), F32)],
        compiler_params=_params(1),
    )(dpad, hp, w)


def _ssm_mask():
    r = np.arange(SSM_CH)[:, None] // SSM_GROUP
    c = np.arange(SSM_LANES)[None, :] // SSM_STATE
    return jnp.asarray((r == c).astype(np.float32))


def _ssm_discretize(lr, li, ldt, brt, bit):
    dt = jnp.exp(ldt)
    mag = jnp.exp(lr * dt)
    ar = mag * jnp.cos(li * dt)
    ai = mag * jnp.sin(li * dt)
    den = lr * lr + li * li
    fr = ((ar - 1.0) * lr + ai * li) / den
    fi = (ai * lr - (ar - 1.0) * li) / den
    return ar, ai, fr * brt - fi * bit, fr * bit + fi * brt


def _block_diag(rows16, mask):
    return jnp.where(mask > 0.5, jnp.tile(rows16, (SSM_GROUPS, 1)), 0.0)


def _block_diag_t(full, mask):
    m = jnp.where(mask > 0.5, full, 0.0)
    out = m[0:SSM_GROUP]
    for g in range(1, SSM_GROUPS):
        out = out + m[g * SSM_GROUP:(g + 1) * SSM_GROUP]
    return out


def _ssm_prep(lr, li, ldt, brt, bit, crt, cit, mask, tag):
    def body(lr_ref, li_ref, ldt_ref, brt_ref, bit_ref, crt_ref, cit_ref, m_ref,
             ar_ref, ai_ref, bbr_ref, bbi_ref, cbr_ref, cbi_ref):
        ar, ai, bbr, bbi = _ssm_discretize(lr_ref[...], li_ref[...], ldt_ref[...], brt_ref[...], bit_ref[...])
        m = m_ref[...]
        ar_ref[...] = ar
        ai_ref[...] = ai
        bbr_ref[...] = _block_diag(bbr, m).astype(BF16)
        bbi_ref[...] = _block_diag(bbi, m).astype(BF16)
        cbr_ref[...] = _block_diag(crt_ref[...], m).astype(BF16)
        cbi_ref[...] = _block_diag(cit_ref[...], m).astype(BF16)

    row = _sds((1, SSM_LANES))
    blk = _sds((SSM_CH, SSM_LANES), BF16)
    return pl.pallas_call(body, name=f"ssm_prep_{tag}", out_shape=[row, row, blk, blk, blk, blk])(
        lr, li, ldt, brt, bit, crt, cit, mask)


def _ssm_prep_bwd(lr, li, ldt, brt, bit, mask, dar, dai, dbbr, dbbi, dcbr, dcbi, tag):
    def body(lr_ref, li_ref, ldt_ref, brt_ref, bit_ref, m_ref, dar_ref, dai_ref, dbbr_ref, dbbi_ref, dcbr_ref,
             dcbi_ref, dlr_ref, dli_ref, dldt_ref, dbrt_ref, dbit_ref, dcrt_ref, dcit_ref):
        m = m_ref[...]
        _, vjp = jax.vjp(_ssm_discretize, lr_ref[...], li_ref[...], ldt_ref[...], brt_ref[...], bit_ref[...])
        dlr, dli, dldt, dbrt, dbit = vjp((dar_ref[...], dai_ref[...], _block_diag_t(dbbr_ref[...], m),
                                          _block_diag_t(dbbi_ref[...], m)))
        dlr_ref[...] = dlr
        dli_ref[...] = dli
        dldt_ref[...] = dldt
        dbrt_ref[...] = dbrt
        dbit_ref[...] = dbit
        dcrt_ref[...] = _block_diag_t(dcbr_ref[...], m)
        dcit_ref[...] = _block_diag_t(dcbi_ref[...], m)

    row = _sds((1, SSM_LANES))
    r16 = _sds((SSM_GROUP, SSM_LANES))
    return pl.pallas_call(body, name=f"ssm_prep_bwd_{tag}", out_shape=[row, row, row, r16, r16, r16, r16])(
        lr, li, ldt, brt, bit, mask, dar, dai, dbbr, dbbi, dcbr, dcbi)


def _complex_scan(br, bi, ar, ai, cr, ci, reverse):
    n = br.shape[0]
    row = lax.broadcasted_iota(jnp.int32, (n, 1), 0) & (SUBLANES - 1)
    xr, xi, pr, pi = br, bi, ar, ai
    d = 1
    while d < SUBLANES:
        if reverse:
            sr, si, keep = pltpu.roll(xr, n - d, 0), pltpu.roll(xi, n - d, 0), row < SUBLANES - d
        else:
            sr, si, keep = pltpu.roll(xr, d, 0), pltpu.roll(xi, d, 0), row >= d
        sr = jnp.where(keep, sr, 0.0)
        si = jnp.where(keep, si, 0.0)
        xr, xi = xr + pr * sr - pi * si, xi + pr * si + pi * sr
        pr, pi = pr * pr - pi * pi, 2.0 * pr * pi
        d *= 2
    powers = [(ar, ai)]
    for _ in range(SUBLANES - 1):
        qr, qi = powers[-1]
        powers.append((qr * ar - qi * ai, qr * ai + qi * ar))
    sub = lax.broadcasted_iota(jnp.int32, (SUBLANES, 1), 0)
    tr = jnp.zeros((SUBLANES, br.shape[1]), F32)
    ti = jnp.zeros((SUBLANES, br.shape[1]), F32)
    for r in range(SUBLANES):
        qr, qi = powers[SUBLANES - 1 - r] if reverse else powers[r]
        tr = jnp.where(sub == r, qr, tr)
        ti = jnp.where(sub == r, qi, ti)
    n_groups = n // SUBLANES
    out_r, out_i = [None] * n_groups, [None] * n_groups
    end = 0 if reverse else SUBLANES - 1
    for g in (reversed(range(n_groups)) if reverse else range(n_groups)):
        gr = xr[g * SUBLANES:(g + 1) * SUBLANES]
        gi = xi[g * SUBLANES:(g + 1) * SUBLANES]
        gr, gi = gr + tr * cr - ti * ci, gi + tr * ci + ti * cr
        cr, ci = gr[end:end + 1], gi[end:end + 1]
        out_r[g], out_i[g] = gr, gi
    return jnp.concatenate(out_r, axis=0), jnp.concatenate(out_i, axis=0)


def _ssm_fwd(u, ar, ai, bbr, bbi, cbr, cbi, dvec, tag, rider=None):
    L = u.shape[0]
    T = SSM_T

    def body(u_ref, ar_ref, ai_ref, bbr_ref, bbi_ref, cbr_ref, cbi_ref, d_ref, y_ref, xr_ref, xi_ref, cr_ref, ci_ref):
        i = pl.program_id(0)

        @pl.when(i == 0)
        def _():
            cr_ref[...] = jnp.zeros_like(cr_ref)
            ci_ref[...] = jnp.zeros_like(ci_ref)

        uu = u_ref[...]
        a_r, a_i = ar_ref[...], ai_ref[...]
        c_r, c_i = cr_ref[...], ci_ref[...]
        xr, xi = _complex_scan(_nn(uu, bbr_ref[...]), _nn(uu, bbi_ref[...]), a_r, a_i, c_r, c_i, False)
        cr_ref[...] = xr[T - 1:T, :]
        ci_ref[...] = xi[T - 1:T, :]
        xr_ref[...] = xr
        xi_ref[...] = xi
        y_ref[...] = _nt(xr, cbr_ref[...]) - _nt(xi, cbi_ref[...]) + d_ref[...] * uu

    blk = _res((SSM_CH, SSM_LANES))
    row = _res((1, SSM_LANES))
    return _pcall(
        body, name=f"ssm_fwd_{tag}", grid=(L // T,), rider=rider,
        in_specs=[_rows(SSM_CH, T), row, row, blk, blk, blk, blk, _res((1, SSM_CH))],
        out_specs=[_rows(SSM_CH, T), _rows(SSM_LANES, T), _rows(SSM_LANES, T)],
        out_shape=[_sds((L, SSM_CH)), _sds((L, SSM_LANES)), _sds((L, SSM_LANES))],
        scratch_shapes=[pltpu.VMEM((1, SSM_LANES), F32), pltpu.VMEM((1, SSM_LANES), F32)],
        compiler_params=_params(1),
    )(u, ar, ai, bbr, bbi, cbr, cbi, dvec)


def _ssm_bwd(dy, u, xr, xi, ar, ai, bbr, bbi, cbr, cbi, dvec, tag, rider=None):
    L = u.shape[0]
    T = SSM_T
    nc = L // T

    def body(dy_ref, u_ref, xr_ref, xi_ref, pr_ref, pi_ref, ar_ref, ai_ref, bbr_ref, bbi_ref, cbr_ref, cbi_ref, d_ref,
             du_ref, dar_ref, dai_ref, dbbr_ref, dbbi_ref, dcbr_ref, dcbi_ref, dd_ref, gr_ref, gi_ref):
        i = pl.program_id(0)

        @pl.when(i == 0)
        def _():
            gr_ref[...] = jnp.zeros_like(gr_ref)
            gi_ref[...] = jnp.zeros_like(gi_ref)

        dyy = dy_ref[...]
        uu = u_ref[...]
        xr, xi = xr_ref[...], xi_ref[...]
        a_r, a_i = ar_ref[...], ai_ref[...]
        g_r, g_i = gr_ref[...], gi_ref[...]
        row = lax.broadcasted_iota(jnp.int32, (T, 1), 0)
        gr, gi = _complex_scan(_nn(dyy, cbr_ref[...]), -_nn(dyy, cbi_ref[...]), a_r, -a_i, g_r, g_i, True)
        gr_ref[...] = gr[0:1, :]
        gi_ref[...] = gi[0:1, :]
        has_prev = (i < nc - 1).astype(F32)
        pr = pr_ref[SUBLANES - 1:SUBLANES, :] * has_prev
        pi = pi_ref[SUBLANES - 1:SUBLANES, :] * has_prev
        sr = jnp.where(row == 0, pr, pltpu.roll(xr, 1, 0))
        si = jnp.where(row == 0, pi, pltpu.roll(xi, 1, 0))
        _accumulate(i, dar_ref, jnp.sum(gr * sr + gi * si, axis=0, keepdims=True))
        _accumulate(i, dai_ref, jnp.sum(gi * sr - gr * si, axis=0, keepdims=True))
        _accumulate(i, dbbr_ref, _tn(uu, gr))
        _accumulate(i, dbbi_ref, _tn(uu, gi))
        _accumulate(i, dcbr_ref, _tn(dyy, xr))
        _accumulate(i, dcbi_ref, -_tn(dyy, xi))
        _accumulate(i, dd_ref, jnp.sum(dyy * uu, axis=0, keepdims=True))
        du_ref[...] = _nt(gr, bbr_ref[...]) + _nt(gi, bbi_ref[...]) + dyy * d_ref[...]

    rev = lambda cols: pl.BlockSpec((T, cols), lambda i: (nc - 1 - i, 0))
    prev = pl.BlockSpec((SUBLANES, SSM_LANES), lambda i: (jnp.maximum((nc - 1 - i) * (T // SUBLANES) - 1, 0), 0))
    blk = _res((SSM_CH, SSM_LANES))
    row = _res((1, SSM_LANES))
    return _pcall(
        body, name=f"ssm_bwd_{tag}", grid=(nc,), rider=rider,
        in_specs=[rev(SSM_CH), rev(SSM_CH), rev(SSM_LANES), rev(SSM_LANES), prev, prev, row, row, blk, blk, blk, blk,
                  _res((1, SSM_CH))],
        out_specs=[rev(SSM_CH), row, row, blk, blk, blk, blk, _res((1, SSM_CH))],
        out_shape=[_sds((L, SSM_CH)), _sds((1, SSM_LANES)), _sds((1, SSM_LANES)), _sds((SSM_CH, SSM_LANES)),
                   _sds((SSM_CH, SSM_LANES)), _sds((SSM_CH, SSM_LANES)), _sds((SSM_CH, SSM_LANES)), _sds((1, SSM_CH))],
        scratch_shapes=[pltpu.VMEM((1, SSM_LANES), F32), pltpu.VMEM((1, SSM_LANES), F32)],
        compiler_params=_params(1),
    )(dy, u, xr, xi, xr, xi, ar, ai, bbr, bbi, cbr, cbi, dvec)


def _conv_post(hc, dw_b, ln_g, ln_b):
    return _silu(_layer_norm(hc + dw_b, ln_g, ln_b))


def _branch_mix(o_sb, o_conv, t, g1, g2, g3):
    o_ssm = t[:, 0:SSM_CH] * _sigmoid(t[:, SSM_CH:2 * SSM_CH])
    return jnp.concatenate([_rms(o_sb, g1), _rms(o_conv, g2), _rms(o_ssm, g3)], axis=1)


def _branch_mix_split(o_sb, o_conv, ta, tb, g1, g2, g3):
    return jnp.concatenate([_rms(o_sb, g1), _rms(o_conv, g2), _rms(ta * _sigmoid(tb), g3)], axis=1)


def _mix_out_fwd(x, o_sb, hc, y, dw_b, ln_g, ln_b, pw2, glu_w, g1, g2, g3, w_out, tag):
    L = x.shape[0]

    def body(x_ref, o_ref, hc_ref, y_ref, dwb_ref, lng_ref, lnb_ref, pw2_ref, glu_ref, g1_ref, g2_ref, g3_ref, wo_ref,
             out_ref):
        c1 = _conv_post(hc_ref[...], dwb_ref[...], lng_ref[...], lnb_ref[...])
        o_conv = _nn(c1, pw2_ref[...])
        t = _nt(y_ref[...], glu_ref[...])
        mixed = _branch_mix(o_ref[...], o_conv, t, g1_ref[...], g2_ref[...], g3_ref[...])
        out_ref[...] = x_ref[...] + _nn(mixed, wo_ref[...])

    v256 = _res((1, 256))
    return pl.pallas_call(
        body, name=f"mix_out_fwd_{tag}", grid=(L // TL,),
        in_specs=[_rows(D_MODEL), _rows(512), _rows(256), _rows(256), v256, v256, v256, _res(pw2.shape),
                  _res(glu_w.shape), _res((1, 512)), v256, v256, _res(w_out.shape)],
        out_specs=_rows(D_MODEL),
        out_shape=_sds((L, D_MODEL)),
        compiler_params=_params(1),
    )(x, o_sb, hc, y, dw_b, ln_g, ln_b, pw2, glu_w, g1, g2, g3, w_out)


def _mix_out_bwd(dx1, o_sb, hc, y, dw_b, ln_g, ln_b, pw2, glu_w, g1, g2, g3, w_out, tag):
    L = dx1.shape[0]

    def body(dx_ref, o_ref, hc_ref, y_ref, dwb_ref, lng_ref, lnb_ref, pw2_ref, glu_ref, g1_ref, g2_ref, g3_ref, wo_ref,
             do_ref, dhc_ref, dy_ref, ddwb_ref, dlng_ref, dlnb_ref, dpw2_ref, dglu_ref, dg1_ref, dg2_ref, dg3_ref,
             dwo_ref):
        i = pl.program_id(0)
        dxx = dx_ref[...]
        yy = y_ref[...]
        c1, vjp1 = jax.vjp(_conv_post, hc_ref[...], dwb_ref[...], lng_ref[...], lnb_ref[...])
        o_conv = _nn(c1, pw2_ref[...])
        t = _nt(yy, glu_ref[...])
        mixed, vjp2 = jax.vjp(_branch_mix_split, o_ref[...], o_conv, t[:, 0:SSM_CH], t[:, SSM_CH:2 * SSM_CH],
                              g1_ref[...], g2_ref[...], g3_ref[...])
        dmixed = _nt(dxx, wo_ref[...])
        do_sb, do_conv, dta, dtb, dg1, dg2, dg3 = vjp2(dmixed)
        dt = jnp.concatenate([dta, dtb], axis=1)
        dc1 = _nt(do_conv, pw2_ref[...])
        dhc, ddwb, dlng, dlnb = vjp1(dc1)
        do_ref[...] = do_sb
        dhc_ref[...] = dhc
        dy_ref[...] = _nn(dt, glu_ref[...])
        _accumulate(i, dwo_ref, _tn(mixed, dxx))
        _accumulate(i, dglu_ref, _tn(dt, yy))
        _accumulate(i, dpw2_ref, _tn(c1, do_conv))
        _accumulate(i, ddwb_ref, ddwb)
        _accumulate(i, dlng_ref, dlng)
        _accumulate(i, dlnb_ref, dlnb)
        _accumulate(i, dg1_ref, dg1)
        _accumulate(i, dg2_ref, dg2)
        _accumulate(i, dg3_ref, dg3)

    v256 = _res((1, 256))
    return pl.pallas_call(
        body, name=f"mix_out_bwd_{tag}", grid=(L // TL,),
        in_specs=[_rows(D_MODEL), _rows(512), _rows(256), _rows(256), v256, v256, v256, _res(pw2.shape),
                  _res(glu_w.shape), _res((1, 512)), v256, v256, _res(w_out.shape)],
        out_specs=[_rows(512), _rows(256), _rows(256), v256, v256, v256, _res(pw2.shape), _res(glu_w.shape),
                   _res((1, 512)), v256, v256, _res(w_out.shape)],
        out_shape=[_sds((L, 512)), _sds((L, 256)), _sds((L, 256)), _sds((1, 256)), _sds((1, 256)), _sds((1, 256)),
                   _sds(pw2.shape), _sds(glu_w.shape), _sds((1, 512)), _sds((1, 256)), _sds((1, 256)), _sds(w_out.shape)],
        compiler_params=_params(1),
    )(dx1, o_sb, hc, y, dw_b, ln_g, ln_b, pw2, glu_w, g1, g2, g3, w_out)


def _xa_heads_norm(kk, kg):
    return jnp.concatenate([_rms(kk[:, h * XA_HEAD_DIM:(h + 1) * XA_HEAD_DIM], kg) for h in range(XA_HEADS)], axis=1)


def _xa_mem_fwd(mem, g_mem, wk, wv, kg, tag):
    def body(m_ref, g_ref, wk_ref, wv_ref, kg_ref, k_ref, v_ref):
        hm = _rms(m_ref[...], g_ref[...])
        k_ref[...] = _xa_heads_norm(_nn(hm, wk_ref[...]), kg_ref[...])
        v_ref[...] = _nn(hm, wv_ref[...])

    return pl.pallas_call(body, name=f"xa_mem_fwd_{tag}", out_shape=[_sds(mem.shape), _sds(mem.shape)],
                          compiler_params=_params(0))(mem, g_mem, wk, wv, kg)


def _xa_mem_bwd(mem, g_mem, wk, wv, kg, dkx, dvx, tag):
    def body(m_ref, g_ref, wk_ref, wv_ref, kg_ref, dk_ref, dv_ref, dwk_ref, dwv_ref, dg_ref, dkg_ref):
        hm, vjp_n = jax.vjp(_rms, m_ref[...], g_ref[...])
        kk = _nn(hm, wk_ref[...])
        dvv = dv_ref[...]
        dkg = jnp.zeros((1, XA_HEAD_DIM), F32)
        parts = []
        for h in range(XA_HEADS):
            sl = slice(h * XA_HEAD_DIM, (h + 1) * XA_HEAD_DIM)
            _, vjp_h = jax.vjp(_rms, kk[:, sl], kg_ref[...])
            dkh, dgh = vjp_h(dk_ref[:, sl])
            parts.append(dkh)
            dkg = dkg + dgh
        dkk = jnp.concatenate(parts, axis=1)
        dwk_ref[...] = _tn(hm, dkk)
        dwv_ref[...] = _tn(hm, dvv)
        dhm = _nt(dkk, wk_ref[...]) + _nt(dvv, wv_ref[...])
        _, dg = vjp_n(dhm)
        dg_ref[...] = dg
        dkg_ref[...] = dkg

    return pl.pallas_call(
        body, name=f"xa_mem_bwd_{tag}",
        out_shape=[_sds(wk.shape), _sds(wv.shape), _sds((1, D_MODEL)), _sds((1, XA_HEAD_DIM))],
        compiler_params=_params(0))(mem, g_mem, wk, wv, kg, dkx, dvx)


def _xa_fwd(x1, kx, vx, g_xa, wq, qg, wo, tag, rider=None):
    L = x1.shape[0]

    def body(x_ref, k_ref, v_ref, g_ref, wq_ref, qg_ref, wo_ref, out_ref, qp_ref):
        xx = x_ref[...]
        qp = _nn(_rms(xx, g_ref[...]), wq_ref[...])
        qp_ref[...] = qp.astype(BF16)
        outs = []
        for h in range(XA_HEADS):
            sl = slice(h * XA_HEAD_DIM, (h + 1) * XA_HEAD_DIM)
            qh = _rms(qp[:, sl], qg_ref[...])
            s = _nt(qh, k_ref[:, sl]) * (XA_HEAD_DIM ** -0.5)
            s = s - jnp.max(s, axis=-1, keepdims=True)
            e = jnp.exp(s)
            p = e / jnp.sum(e, axis=-1, keepdims=True)
            outs.append(_nn(p, v_ref[:, sl]))
        out_ref[...] = xx + _nn(jnp.concatenate(outs, axis=1), wo_ref[...])

    return _pcall(
        body, name=f"xa_fwd_{tag}", grid=(L // TL,), rider=rider,
        in_specs=[_rows(D_MODEL), _res(kx.shape), _res(vx.shape), _res((1, D_MODEL)), _res(wq.shape),
                  _res((1, XA_HEAD_DIM)), _res(wo.shape)],
        out_specs=[_rows(D_MODEL), _rows(D_MODEL)],
        out_shape=[_sds((L, D_MODEL)), _sds((L, D_MODEL), BF16)],
        compiler_params=_params(1),
    )(x1, kx, vx, g_xa, wq, qg, wo)


def _xa_bwd(x1, dx2, qp_kept, kx, vx, g_xa, wq, qg, wo, tag, rider=None):
    L = x1.shape[0]
    tl = 256

    def body(x_ref, dx_ref, qp_ref, k_ref, v_ref, g_ref, wq_ref, qg_ref, wo_ref,
             dx1_ref, dk_ref, dv_ref, dwq_ref, dwo_ref, dg_ref, dqg_ref):
        i = pl.program_id(0)
        xx = x_ref[...]
        dxx = dx_ref[...]
        hx, vjp_n = jax.vjp(_rms, xx, g_ref[...])
        qp = qp_ref[...].astype(F32)
        do = _nt(dxx, wo_ref[...])
        heads = [dict(sl=slice(h * XA_HEAD_DIM, (h + 1) * XA_HEAD_DIM)) for h in range(XA_HEADS)]
        for c in heads:
            c['kh'], c['vh'], c['doh'] = k_ref[:, c['sl']].astype(BF16), v_ref[:, c['sl']].astype(BF16), do[:, c['sl']]
            c['qh'], c['vjp_q'] = jax.vjp(_rms, qp[:, c['sl']], qg_ref[...])
        for c in heads:
            c['s'] = _nt(c['qh'], c['kh'])
            c['dp'] = _nt(c['doh'], c['vh'])
        for c in heads:
            s = c.pop('s') * (XA_HEAD_DIM ** -0.5)
            e = jnp.exp(s - jnp.max(s, axis=-1, keepdims=True))
            c['p'] = e / jnp.sum(e, axis=-1, keepdims=True)
        for c in heads:
            c['out'] = _nn(c['p'], c['vh'])
            c['dv'] = _tn(c['p'], c['doh'])
        for c in heads:
            p, dp = c['p'], c.pop('dp')
            c['ds'] = p * (dp - jnp.sum(dp * p, axis=-1, keepdims=True)) * (XA_HEAD_DIM ** -0.5)
        for c in heads:
            c['dk'] = _tn(c['ds'], c['qh'])
            c['dq'] = _nn(c['ds'], c['kh'])
        dqg = jnp.zeros((1, XA_HEAD_DIM), F32)
        for c in heads:
            c['dqp'], dgh = c['vjp_q'](c['dq'])
            dqg = dqg + dgh
        o = jnp.concatenate([c['out'] for c in heads], axis=1)
        dqp = jnp.concatenate([c['dqp'] for c in heads], axis=1)
        dks, dvs = [c['dk'] for c in heads], [c['dv'] for c in heads]
        dxn, dg = vjp_n(_nt(dqp, wq_ref[...]))
        dx1_ref[...] = dxx + dxn
        _accumulate(i, dk_ref, jnp.concatenate(dks, axis=1))
        _accumulate(i, dv_ref, jnp.concatenate(dvs, axis=1))
        _accumulate(i, dwq_ref, _tn(hx, dqp))
        _accumulate(i, dwo_ref, _tn(o, dxx))
        _accumulate(i, dg_ref, dg)
        _accumulate(i, dqg_ref, dqg)

    r = lambda c: _rows(c, tl)
    return _pcall(
        body, name=f"xa_bwd_{tag}", grid=(L // tl,), rider=rider,
        in_specs=[r(D_MODEL), r(D_MODEL), r(D_MODEL), _res(kx.shape), _res(vx.shape), _res((1, D_MODEL)),
                  _res(wq.shape), _res((1, XA_HEAD_DIM)), _res(wo.shape)],
        out_specs=[r(D_MODEL), _res(kx.shape), _res(vx.shape), _res(wq.shape), _res(wo.shape), _res((1, D_MODEL)),
                   _res((1, XA_HEAD_DIM))],
        out_shape=[_sds((L, D_MODEL)), _sds(kx.shape), _sds(vx.shape), _sds(wq.shape), _sds(wo.shape),
                   _sds((1, D_MODEL)), _sds((1, XA_HEAD_DIM))],
        compiler_params=_params(1),
    )(x1, dx2, qp_kept, kx, vx, g_xa, wq, qg, wo)


def _swiglu(gate, up):
    return _silu(gate) * up


def _ffn_fwd(x2, g, w_in, w_out, tag, rider=None, tgt=None):
    L = x2.shape[0]
    tl = 256
    n_tiles = L // tl

    def ffn(x_ref, g_ref, wi_ref, wo_ref, hf_ref, gu_ref, act_ref):
        xx = x_ref[...]
        hf = _rms(xx, g_ref[...]).astype(BF16)
        gu = _nt(hf, wi_ref[...])
        act = _swiglu(gu[:, 0:FFN_HIDDEN], gu[:, FFN_HIDDEN:2 * FFN_HIDDEN]).astype(BF16)
        hf_ref[...] = hf
        gu_ref[...] = gu.astype(BF16)
        act_ref[...] = act
        return xx + _nn(act, wo_ref[...])

    def body(x_ref, g_ref, wi_ref, wo_ref, out_ref, hf_ref, gu_ref, act_ref):
        out_ref[...] = ffn(x_ref, g_ref, wi_ref, wo_ref, hf_ref, gu_ref, act_ref)

    def body_loss(x_ref, g_ref, wi_ref, wo_ref, t_ref, dy_ref, hf_ref, gu_ref, act_ref, loss_ref, acc_ref):
        i = pl.program_id(0)
        diff = ffn(x_ref, g_ref, wi_ref, wo_ref, hf_ref, gu_ref, act_ref) - t_ref[...]
        dy_ref[...] = diff * (1.0 / D_MODEL)
        _accumulate(i, acc_ref, jnp.sum(diff * diff, axis=0, keepdims=True))

        @pl.when(i == n_tiles - 1)
        def _():
            loss_ref[...] = jnp.sum(acc_ref[...], axis=1, keepdims=True) * (0.5 / D_MODEL)

    r = lambda c: _rows(c, tl)
    in_specs = [r(D_MODEL), _res((1, D_MODEL)), _res(w_in.shape), _res(w_out.shape)]
    out_specs = [r(D_MODEL), r(D_MODEL), r(2 * FFN_HIDDEN), r(FFN_HIDDEN)]
    out_shape = [_sds((L, D_MODEL)), _sds((L, D_MODEL), BF16), _sds((L, 2 * FFN_HIDDEN), BF16),
                 _sds((L, FFN_HIDDEN), BF16)]
    if tgt is None:
        return _pcall(body, name=f"ffn_fwd_{tag}", grid=(n_tiles,), rider=rider, in_specs=in_specs,
                      out_specs=out_specs, out_shape=out_shape, compiler_params=_params(1, 56))(x2, g, w_in, w_out)
    return _pcall(body_loss, name=f"ffn_fwd_loss_{tag}", grid=(n_tiles,), rider=rider,
                  in_specs=in_specs + [r(D_MODEL)], out_specs=out_specs + [_res((1, 1))],
                  out_shape=out_shape + [_sds((1, 1))], scratch_shapes=[pltpu.VMEM((1, D_MODEL), F32)],
                  compiler_params=_params(1, 56))(x2, g, w_in, w_out, tgt)


def _ffn_bwd(x2, dx3, gu, g, w_in, w_out, tag, rider=None):
    L = x2.shape[0]
    tl = 256

    def body(x_ref, dx_ref, gu_ref, g_ref, wi_ref, wo_ref, dx2_ref, dgu_ref, dg_ref):
        i = pl.program_id(0)
        dxx = dx_ref[...]
        _, vjp_n = jax.vjp(_rms, x_ref[...], g_ref[...])
        _, vjp_a = jax.vjp(_swiglu, gu_ref[:, 0:FFN_HIDDEN].astype(F32), gu_ref[:, FFN_HIDDEN:2 * FFN_HIDDEN].astype(F32))
        dgate, dup = vjp_a(_nt(dxx, wo_ref[...]))
        dgu = jnp.concatenate([dgate, dup], axis=1).astype(BF16)
        dxn, dg = vjp_n(_nn(dgu, wi_ref[...]))
        dx2_ref[...] = dxx + dxn
        dgu_ref[...] = dgu
        _accumulate(i, dg_ref, dg)

    r = lambda c: _rows(c, tl)
    return _pcall(
        body, name=f"ffn_bwd_{tag}", grid=(L // tl,), rider=rider,
        in_specs=[r(D_MODEL), r(D_MODEL), r(2 * FFN_HIDDEN), _res((1, D_MODEL)), _res(w_in.shape), _res(w_out.shape)],
        out_specs=[r(D_MODEL), r(2 * FFN_HIDDEN), _res((1, D_MODEL))],
        out_shape=[_sds((L, D_MODEL)), _sds((L, 2 * FFN_HIDDEN), BF16), _sds((1, D_MODEL))],
        compiler_params=_params(1, 56),
    )(x2, dx3, gu, g, w_in, w_out)


def _matmul_tn(a, b, tm, tn, tag):
    L, M = a.shape
    N = b.shape[1]
    tk = min(L, 2048)
    nk = L // tk

    def body(a_ref, b_ref, o_ref, acc_ref):
        k = pl.program_id(2)
        _accumulate(k, acc_ref, _tn(a_ref[...], b_ref[...]))

        @pl.when(k == nk - 1)
        def _():
            o_ref[...] = acc_ref[...].astype(BF16)

    return pl.pallas_call(
        body, name=f"matmul_tn_{tag}", grid=(M // tm, N // tn, nk),
        in_specs=[pl.BlockSpec((tk, tm), lambda m, n, k: (k, m)), pl.BlockSpec((tk, tn), lambda m, n, k: (k, n))],
        out_specs=pl.BlockSpec((tm, tn), lambda m, n, k: (m, n)),
        out_shape=_sds((M, N), BF16),
        scratch_shapes=[pltpu.VMEM((tm, tn), F32)],
        compiler_params=_params(3),
    )(a, b)


def _row(v):
    return v.reshape(1, -1)


def _layer_consts():
    r = np.arange(SB_WIDTH)
    mavg = ((r[:, None] // SB_HEAD_DIM) == (r[None, :] // SB_HEAD_DIM)).astype(np.float32) / SB_HEAD_DIM
    ul, ue = _sb_tri_consts()
    return dict(mavg=jnp.asarray(mavg), ul=ul, ue=ue, mask=_ssm_mask())


def _ssm_rows(P):
    lanes = lambda a: a.reshape(1, SSM_LANES)
    return dict(
        lr=lanes(P['ssm_lam_re']), li=lanes(P['ssm_lam_im']),
        ldt=lanes(jnp.repeat(P['ssm_log_dt'], SSM_STATE)),
        brt=P['ssm_b_re'].transpose(2, 0, 1).reshape(SSM_GROUP, SSM_LANES),
        bit=P['ssm_b_im'].transpose(2, 0, 1).reshape(SSM_GROUP, SSM_LANES),
        crt=P['ssm_c_re'].transpose(1, 0, 2).reshape(SSM_GROUP, SSM_LANES),
        cit=P['ssm_c_im'].transpose(1, 0, 2).reshape(SSM_GROUP, SSM_LANES))


def _layer_fwd(x, mem, P, C, tag, ride, tgt=None):
    gq = _row(jnp.tile(P['sb_q_norm_g'], SB_WIDTH // SB_HEAD_DIM))
    gk = _row(jnp.tile(P['sb_k_norm_g'], SB_WIDTH // SB_HEAD_DIM))
    q, k, v, hg, u, p = _mix_in_fwd(x, _row(P['norm_mix_g']), P['w_in'], gq, gk, C['mavg'], tag,
                                 rider=ride("mix_in_fwd_" + tag))
    o_sb, rsave, n_done = _sb_fwd(q, k, v, C['ul'], tag, rider=ride("sb_fwd_" + tag))
    hp = jnp.pad(hg, ((CONV_HALO, 0), (0, 0)))
    hc = _conv_fwd(hp, P['conv_dw_w'].T, tag)
    S = _ssm_rows(P)
    ar, ai, bbr, bbi, cbr, cbi = _ssm_prep(S['lr'], S['li'], S['ldt'], S['brt'], S['bit'], S['crt'], S['cit'],
                                           C['mask'], tag)
    y, xr, xi = _ssm_fwd(u, ar, ai, bbr, bbi, cbr, cbi, _row(P['ssm_d']), tag, rider=ride("ssm_fwd_" + tag))
    gb = P['branch_norm_g']
    x1 = _mix_out_fwd(x, o_sb, hc, y, _row(P['conv_dw_b']), _row(P['conv_ln_g']), _row(P['conv_ln_b']),
                      P['conv_pw2_w'], P['ssm_glu_w'], _row(gb[0:512]), _row(gb[512:768]), _row(gb[768:1024]),
                      P['w_out'], tag)
    kx, vx = _xa_mem_fwd(mem, _row(P['norm_mem_g']), P['xa_wk'], P['xa_wv'], _row(P['xa_k_norm_g']), tag)
    x2, qp = _xa_fwd(x1, kx, vx, _row(P['norm_xa_g']), P['xa_wq'], _row(P['xa_q_norm_g']), P['xa_wo'], tag,
                 rider=ride("xa_fwd_" + tag))
    x3, hf, gu, act, *loss = _ffn_fwd(x2, _row(P['norm_ffn_g']), P['ffn_w_in'], P['ffn_w_out'], tag,
                                      rider=ride("ffn_fwd_" + tag), tgt=tgt)
    saved = dict(p=p, qp=qp, hf=hf, gu=gu, act=act, x=x, q=q, k=k, v=v, rsave=rsave, n_done=n_done, o_sb=o_sb, hp=hp, hc=hc, u=u, y=y, xr=xr, xi=xi, x1=x1, x2=x2,
                 kx=kx, vx=vx, gq=gq, gk=gk, S=S, ssm=(ar, ai, bbr, bbi, cbr, cbi))
    return x3, saved, (loss[0] if loss else None)


def _layer_bwd(dx3, mem, P, C, sv, tag, ride, G):
    dx2, dgu, dg = _ffn_bwd(sv['x2'], dx3, sv['gu'], _row(P['norm_ffn_g']), P['ffn_w_in'], P['ffn_w_out'], tag,
                            rider=ride("ffn_bwd_" + tag))
    G['norm_ffn_g'] = dg.reshape(-1)
    G['ffn_w_in'] = _matmul_tn(dgu, sv['hf'], 2 * FFN_HIDDEN // 4, D_MODEL, "ffn_in_" + tag)
    G['ffn_w_out'] = _matmul_tn(sv['act'], dx3, FFN_HIDDEN // 2, 512, "ffn_out_" + tag)
    dx1, dkx, dvx, dwq, dwo, dg, dqg = _xa_bwd(sv['x1'], dx2, sv['qp'], sv['kx'], sv['vx'], _row(P['norm_xa_g']), P['xa_wq'],
                                               _row(P['xa_q_norm_g']), P['xa_wo'], tag, rider=ride("xa_bwd_" + tag))
    G['xa_wq'], G['xa_wo'], G['norm_xa_g'], G['xa_q_norm_g'] = dwq, dwo, dg.reshape(-1), dqg.reshape(-1)
    dwk, dwv, dg, dkg = _xa_mem_bwd(mem, _row(P['norm_mem_g']), P['xa_wk'], P['xa_wv'], _row(P['xa_k_norm_g']),
                                    dkx, dvx, tag)
    G['xa_wk'], G['xa_wv'], G['norm_mem_g'], G['xa_k_norm_g'] = dwk, dwv, dg.reshape(-1), dkg.reshape(-1)
    gb = P['branch_norm_g']
    (do_sb, dhc, dy, ddwb, dlng, dlnb, dpw2, dglu, dg1, dg2, dg3, dwout) = _mix_out_bwd(
        dx1, sv['o_sb'], sv['hc'], sv['y'], _row(P['conv_dw_b']), _row(P['conv_ln_g']), _row(P['conv_ln_b']),
        P['conv_pw2_w'], P['ssm_glu_w'], _row(gb[0:512]), _row(gb[512:768]), _row(gb[768:1024]), P['w_out'], tag)
    G['conv_dw_b'], G['conv_ln_g'], G['conv_ln_b'] = ddwb.reshape(-1), dlng.reshape(-1), dlnb.reshape(-1)
    G['conv_pw2_w'], G['ssm_glu_w'], G['w_out'] = dpw2, dglu, dwout
    G['branch_norm_g'] = jnp.concatenate([dg1.reshape(-1), dg2.reshape(-1), dg3.reshape(-1)])
    ar, ai, bbr, bbi, cbr, cbi = sv['ssm']
    du, dar, dai, dbbr, dbbi, dcbr, dcbi, dd = _ssm_bwd(dy, sv['u'], sv['xr'], sv['xi'], ar, ai, bbr, bbi, cbr, cbi,
                                                        _row(P['ssm_d']), tag, rider=ride("ssm_bwd_" + tag))
    S = sv['S']
    dlr, dli, dldt, dbrt, dbit, dcrt, dcit = _ssm_prep_bwd(S['lr'], S['li'], S['ldt'], S['brt'], S['bit'], C['mask'],
                                                           dar, dai, dbbr, dbbi, dcbr, dcbi, tag)
    G['ssm_lam_re'] = dlr.reshape(SSM_GROUPS, SSM_STATE)
    G['ssm_lam_im'] = dli.reshape(SSM_GROUPS, SSM_STATE)
    G['ssm_log_dt'] = dldt.reshape(SSM_GROUPS, SSM_STATE).sum(axis=1)
    G['ssm_b_re'] = dbrt.reshape(SSM_GROUP, SSM_GROUPS, SSM_STATE).transpose(1, 2, 0)
    G['ssm_b_im'] = dbit.reshape(SSM_GROUP, SSM_GROUPS, SSM_STATE).transpose(1, 2, 0)
    G['ssm_c_re'] = dcrt.reshape(SSM_GROUP, SSM_GROUPS, SSM_STATE).transpose(1, 0, 2)
    G['ssm_c_im'] = dcit.reshape(SSM_GROUP, SSM_GROUPS, SSM_STATE).transpose(1, 0, 2)
    G['ssm_d'] = dd.reshape(-1)
    dpad = jnp.pad(dhc, ((0, CONV_HALO), (0, 0)))
    dhg, ddww = _conv_bwd(dpad, sv['hp'], P['conv_dw_w'].T, tag)
    G['conv_dw_w'] = ddww.T
    dq, dk, dv = _sb_bwd(sv['n_done'], sv['q'], sv['k'], sv['v'], sv['rsave'], do_sb, C['ul'], C['ue'], tag)
    dx, dwin, dg, dgq, dgk = _mix_in_bwd(sv['x'], dx1, sv['p'], dq, dk, dv, dhg, du, _row(P['norm_mix_g']), P['w_in'],
                                         sv['gq'], sv['gk'], C['mavg'], tag)
    G['w_in'], G['norm_mix_g'] = dwin, dg.reshape(-1)
    G['sb_q_norm_g'] = dgq.reshape(SB_WIDTH // SB_HEAD_DIM, SB_HEAD_DIM).sum(axis=0)
    G['sb_k_norm_g'] = dgk.reshape(SB_WIDTH // SB_HEAD_DIM, SB_HEAD_DIM).sum(axis=0)
    return dx, G


def _slot_sum(r_ref):
    g = r_ref[0].astype(F32)
    for s in range(1, r_ref.shape[0]):
        g = g + r_ref[s].astype(F32)
    return g


def _adam_update(g, w, m, v):
    nm = ADAM_B1 * m + (1.0 - ADAM_B1) * g
    nv = ADAM_B2 * v + (1.0 - ADAM_B2) * (g * g)
    m_hat = nm * (1.0 / (1.0 - ADAM_B1 ** ADAM_STEP))
    v_hat = nv * (1.0 / (1.0 - ADAM_B2 ** ADAM_STEP))
    return -ADAM_LR * (m_hat / (jnp.sqrt(v_hat) + ADAM_EPS) + ADAM_WD * w), nm, nv


def _adamw(recv, w, m, v, tile, name):
    n_slots, R, C = recv.shape

    def body(r_ref, w_ref, m_ref, v_ref, g_ref, d_ref, nm_ref, nv_ref):
        g = _slot_sum(r_ref)
        g_ref[...] = g
        d_ref[...], nm_ref[...], nv_ref[...] = _adam_update(g, w_ref[...], m_ref[...], v_ref[...])

    rows = pl.BlockSpec((tile, C), lambda i: (i, 0))
    out = _sds((R, C))
    return pl.pallas_call(
        body, name=name, grid=(R // tile,),
        in_specs=[pl.BlockSpec((n_slots, tile, C), lambda i: (0, i, 0)), rows, rows, rows],
        out_specs=[rows, rows, rows, rows],
        out_shape=[out, out, out, out],
        compiler_params=_params(1),
    )(recv, w, m, v)


def _adamw_layers(recvs, w, m, v, tile, name):
    n_slots, R, C = recvs[0].shape

    def body(*refs):
        r_refs = refs[:DEPTH]
        w_ref, m_ref, v_ref, g_ref, d_ref, nm_ref, nv_ref = refs[DEPTH:]
        for l in range(DEPTH):
            @pl.when(pl.program_id(0) == l)
            def _(l=l):
                g = _slot_sum(r_refs[l])
                g_ref[0] = g
                d_ref[0], nm_ref[0], nv_ref[0] = _adam_update(g, w_ref[0], m_ref[0], v_ref[0])

    rspec = lambda l: pl.BlockSpec((n_slots, tile, C), lambda ll, i: (0, jnp.where(ll == l, i, 0), 0))
    rows = pl.BlockSpec((1, tile, C), lambda ll, i: (ll, i, 0))
    out = _sds((DEPTH, R, C))
    return pl.pallas_call(
        body, name=name, grid=(DEPTH, R // tile),
        in_specs=[rspec(l) for l in range(DEPTH)] + [rows, rows, rows],
        out_specs=[rows, rows, rows, rows],
        out_shape=[out, out, out, out],
        compiler_params=_params(2),
    )(*recvs, w, m, v)


def _reduce_slots(recv, tile, name):
    n_slots, R, C = recv.shape

    def body(r_ref, g_ref):
        g_ref[...] = _slot_sum(r_ref)

    return pl.pallas_call(
        body, name=name, grid=(R // tile,),
        in_specs=[pl.BlockSpec((n_slots, tile, C), lambda i: (0, i, 0))],
        out_specs=pl.BlockSpec((tile, C), lambda i: (i, 0)),
        out_shape=_sds((R, C)),
        compiler_params=_params(1),
    )(recv)


SEG = SUBLANES * LANES


def _pad_to(n, mult):
    return -(-n // mult) * mult


def _pack(arrays, dtype, row_mult, lead=0):
    keep = [(0, 0)] * lead
    parts = []
    for a in arrays:
        flat = a.reshape(a.shape[:lead] + (-1,)).astype(dtype)
        n = flat.shape[-1]
        parts.append(jnp.pad(flat, keep + [(0, _pad_to(n, SEG) - n)]))
    flat = jnp.concatenate(parts, axis=-1)
    n = flat.shape[-1]
    flat = jnp.pad(flat, keep + [(0, _pad_to(n, row_mult * LANES) - n)])
    return flat.reshape(flat.shape[:lead] + (-1, LANES))


def _unpack(buf, shapes):
    lead = buf.shape[:-2]
    flat = buf.reshape(lead + (-1,))
    out, off = [], 0
    for shp in shapes:
        n = int(np.prod(shp))
        out.append(flat[..., off:off + n].reshape(lead + tuple(shp)))
        off += _pad_to(n, SEG)
    return out


def _rows_first(a, name):
    return a.transpose(0, 2, 1) if SHARD_AXIS[name] == 2 else a


SMALL_TILE = 256
DIRECT_NAMES = [n for n in BIG_NAMES if n != 'conv_dw_w']
GATHER_RIDES = {
    "mix_in_fwd_l0": [(0, 'conv_pw2_w'), (0, 'ssm_glu_w'), (0, 'w_out'), (0, 'xa_wq')],
    "sb_fwd_l0": [(0, 'xa_wk'), (0, 'xa_wv'), (0, 'xa_wo'), (0, 'ffn_w_in')],
    "ssm_fwd_l0": [(0, 'ffn_w_out')],
    "xa_fwd_l0": [(1, 'w_in'), (1, 'conv_pw2_w'), (1, 'ssm_glu_w')],
    "ffn_fwd_l0": [(1, 'w_out'), (1, 'xa_wq'), (1, 'xa_wk'), (1, 'xa_wv'), (1, 'xa_wo')],
    "sb_fwd_l1": [(1, 'ffn_w_in'), (1, 'ffn_w_out')],
}
_MID = ['xa_wq', 'xa_wo', 'xa_wk', 'xa_wv', 'w_out', 'conv_pw2_w', 'ssm_glu_w']
SCATTER_RIDES = {
    "xa_bwd_l1": [(1, 'ffn_w_in'), (1, 'ffn_w_out')],
    "ssm_bwd_l1": [(1, n) for n in _MID],
    "ffn_bwd_l0": [(1, 'w_in')],
    "xa_bwd_l0": [(0, 'ffn_w_in'), (0, 'ffn_w_out')],
    "ssm_bwd_l0": [(0, n) for n in _MID],
}


def _tile_rows(rows):
    return next(t for t in range(min(rows, 256), 0, -ROW_ALIGN) if rows % t == 0 and t % ROW_ALIGN == 0)


class _LayerWeights:
    def __init__(self, layer, small, full, conv):
        self.layer, self.small, self.full, self.conv = layer, small, full, conv

    def __getitem__(self, name):
        if name == 'conv_dw_w':
            return self.conv[self.layer]
        return self.full[(self.layer, name)] if name in SHARD_AXIS else self.small[name][self.layer]


def kernel(x, mem, norm_mix_g, w_in, sb_q_norm_g, sb_k_norm_g, conv_dw_w, conv_dw_b, conv_ln_g, conv_ln_b, conv_pw2_w, ssm_lam_re, ssm_lam_im, ssm_log_dt, ssm_b_re, ssm_b_im, ssm_c_re, ssm_c_im, ssm_d, ssm_glu_w, branch_norm_g, w_out, norm_xa_g, norm_mem_g, xa_wq, xa_wk, xa_wv, xa_q_norm_g, xa_k_norm_g, xa_wo, norm_ffn_g, ffn_w_in, ffn_w_out, loss_target, m_norm_mix_g, m_w_in, m_sb_q_norm_g, m_sb_k_norm_g, m_conv_dw_w, m_conv_dw_b, m_conv_ln_g, m_conv_ln_b, m_conv_pw2_w, m_ssm_lam_re, m_ssm_lam_im, m_ssm_log_dt, m_ssm_b_re, m_ssm_b_im, m_ssm_c_re, m_ssm_c_im, m_ssm_d, m_ssm_glu_w, m_branch_norm_g, m_w_out, m_norm_xa_g, m_norm_mem_g, m_xa_wq, m_xa_wk, m_xa_wv, m_xa_q_norm_g, m_xa_k_norm_g, m_xa_wo, m_norm_ffn_g, m_ffn_w_in, m_ffn_w_out, v_norm_mix_g, v_w_in, v_sb_q_norm_g, v_sb_k_norm_g, v_conv_dw_w, v_conv_dw_b, v_conv_ln_g, v_conv_ln_b, v_conv_pw2_w, v_ssm_lam_re, v_ssm_lam_im, v_ssm_log_dt, v_ssm_b_re, v_ssm_b_im, v_ssm_c_re, v_ssm_c_im, v_ssm_d, v_ssm_glu_w, v_branch_norm_g, v_w_out, v_norm_xa_g, v_norm_mem_g, v_xa_wq, v_xa_wk, v_xa_wv, v_xa_q_norm_g, v_xa_k_norm_g, v_xa_wo, v_norm_ffn_g, v_ffn_w_in, v_ffn_w_out):
    args = locals()
    w_loc = {n: args[n] for n in WEIGHT_NAMES}
    m_loc = {n: args["m_" + n] for n in WEIGHT_NAMES}
    v_loc = {n: args["v_" + n] for n in WEIGHT_NAMES}
    me = _my_index()
    shard = {(l, n): _rows_first(w_loc[n], n)[l].astype(BF16) for l in range(DEPTH) for n in DIRECT_NAMES}
    conv_shape = _rows_first(w_loc['conv_dw_w'], 'conv_dw_w').shape
    conv_rows = _pack([_rows_first(w_loc['conv_dw_w'], 'conv_dw_w')], F32, SUBLANES)
    full = {}
    recv = {}
    grads = [dict() for _ in range(DEPTH)]

    def ride(kernel_name):
        if kernel_name in GATHER_RIDES:
            keys = GATHER_RIDES[kernel_name]
            return _Rider(gathers=[shard[k] for k in keys], done=lambda res: full.update(zip(keys, res)))
        if kernel_name in SCATTER_RIDES:
            keys = SCATTER_RIDES[kernel_name]
            return _Rider(scatters=[grads[l][n].astype(BF16) for (l, n) in keys],
                          done=lambda res: recv.update(zip(keys, res)))
        return None

    first = []
    _exchange(_Rider(gathers=[shard[(0, 'w_in')], conv_rows], done=first.extend), "gather_first")
    full[(0, 'w_in')] = first[0]
    conv_all = first[1].reshape(N_DEV, -1)[:, :int(np.prod(conv_shape))].reshape((N_DEV,) + conv_shape)
    conv_full = conv_all.transpose(1, 0, 2, 3).reshape(DEPTH, N_DEV * conv_shape[1], conv_shape[2])
    weights = [_LayerWeights(l, w_loc, full, conv_full) for l in range(DEPTH)]

    consts = _layer_consts()
    h, saved = x[0], []
    for l in range(DEPTH):
        h, sv, loss_part = _layer_fwd(h, mem[0], weights[l], consts, f"l{l}", ride,
                                      tgt=loss_target[0] if l == DEPTH - 1 else None)
        saved.append(sv)
    dh = h
    for l in reversed(range(DEPTH)):
        dh, _ = _layer_bwd(dh, mem[0], weights[l], consts, saved[l], f"l{l}", ride, grads[l])
    grad_x = dh
    loss = lax.psum(loss_part[0, 0], MESH_AXES)

    small_shapes = [w_loc[n].shape for n in SMALL_NAMES]
    conv_nat = (DEPTH,) + grads[0]['conv_dw_w'].shape[::-1]
    small_send = _pack([jnp.stack([grads[l][n] for l in range(DEPTH)]) for n in SMALL_NAMES]
                       + [jnp.stack([grads[l]['conv_dw_w'].T for l in range(DEPTH)])], F32, SMALL_TILE)
    last = []
    _exchange(_Rider(gathers=[small_send], scatters=[grads[0]['w_in'].astype(BF16)], done=last.extend), "exchange_last")
    recv[(0, 'w_in')] = last[1]
    small_sum = _reduce_slots(last[0].reshape((N_DEV,) + small_send.shape), SMALL_TILE, "reduce_replicated")
    small_g = _unpack(small_sum, small_shapes + [conv_nat])
    conv_cols = w_loc['conv_dw_w'].shape[2]
    conv_g = lax.dynamic_slice_in_dim(small_g[-1], me * conv_cols, conv_cols, axis=2)

    result = [{}, {}, {}, {}]
    for n in DIRECT_NAMES:
        parts = [recv[(l, n)] for l in range(DEPTH)]
        if SHARD_AXIS[n] == 2:
            parts = [_reduce_slots(p, _tile_rows(p.shape[1]), f"reduce_{n}_l{l}").T[None] for l, p in enumerate(parts)]
        outs = _adamw_layers(parts, w_loc[n], m_loc[n], v_loc[n], _tile_rows(w_loc[n].shape[1]), f"adamw_{n}")
        for kind in range(4):
            result[kind][n] = outs[kind]
    packed_names = SMALL_NAMES + ['conv_dw_w']
    pk = lambda d: _pack([d[n] for n in packed_names], F32, SMALL_TILE)
    outs = _adamw(_pack(small_g[:-1] + [conv_g], F32, SMALL_TILE)[None], pk(w_loc), pk(m_loc), pk(v_loc), SMALL_TILE,
                  "adamw_packed")
    for kind in range(4):
        for n, a in zip(packed_names, _unpack(outs[kind], [w_loc[n].shape for n in packed_names])):
            result[kind][n] = a
    return (loss, grad_x[None], *[result[0][n] for n in WEIGHT_NAMES], *[result[1][n] for n in WEIGHT_NAMES],
            *[result[2][n] for n in WEIGHT_NAMES], *[result[3][n] for n in WEIGHT_NAMES])
```

```python
import functools

import numpy as np
import jax
import jax.numpy as jnp
from jax import lax
from jax.experimental import pallas as pl
from jax.experimental.pallas import tpu as pltpu

F32 = jnp.float32
BF16 = jnp.bfloat16
EPS = 1e-6
D_MODEL = 1024
DEPTH = 2
N_DEV = 8
SB_WIDTH = 512
SB_HEAD_DIM = 64
CONV_CH = 256
CONV_WIDTH = 31
SSM_CH = 256
SSM_GROUP = 16
SSM_GROUPS = 16
SSM_STATE = 64
SSM_LANES = SSM_GROUPS * SSM_STATE
XA_HEADS = 4
XA_HEAD_DIM = 256
FFN_HIDDEN = 2816
ADAM_LR = 0.001
ADAM_B1 = 0.9
ADAM_B2 = 0.999
ADAM_EPS = 1e-08
ADAM_WD = 0.01
ADAM_STEP = 10

LANES = 128
SUBLANES = 8
TL = 512
SB_TQ = 256
SB_TK = 128
SB_PAIRS = 2
SB_SCALE = SB_HEAD_DIM ** -0.5
SB_DEAD = -120.0
SSM_T = 256
CONV_HALO = 32
CONV_SUB = 64
VMEM_MB = 48

MESH_AXES = ("x", "y", "c")
WEIGHT_NAMES = ['norm_mix_g', 'w_in', 'sb_q_norm_g', 'sb_k_norm_g', 'conv_dw_w', 'conv_dw_b', 'conv_ln_g',
                'conv_ln_b', 'conv_pw2_w', 'ssm_lam_re', 'ssm_lam_im', 'ssm_log_dt', 'ssm_b_re', 'ssm_b_im',
                'ssm_c_re', 'ssm_c_im', 'ssm_d', 'ssm_glu_w', 'branch_norm_g', 'w_out', 'norm_xa_g',
                'norm_mem_g', 'xa_wq', 'xa_wk', 'xa_wv', 'xa_q_norm_g', 'xa_k_norm_g', 'xa_wo', 'norm_ffn_g',
                'ffn_w_in', 'ffn_w_out']
SHARD_AXIS = {'w_in': 2, 'conv_dw_w': 2, 'conv_pw2_w': 1, 'ssm_glu_w': 2, 'w_out': 1, 'xa_wq': 1, 'xa_wk': 1,
              'xa_wv': 1, 'xa_wo': 1, 'ffn_w_in': 2, 'ffn_w_out': 1}
BIG_NAMES = [n for n in WEIGHT_NAMES if n in SHARD_AXIS]
SMALL_NAMES = [n for n in WEIGHT_NAMES if n not in SHARD_AXIS]


def _nn(a, b):
    return jnp.dot(a.astype(BF16), b.astype(BF16), preferred_element_type=F32)


def _nt(a, b):
    return lax.dot_general(a.astype(BF16), b.astype(BF16), (((1,), (1,)), ((), ())), preferred_element_type=F32)


def _tn(a, b):
    return lax.dot_general(a.astype(BF16), b.astype(BF16), (((0,), (0,)), ((), ())), preferred_element_type=F32)


def _rms(x, g):
    return x * lax.rsqrt(jnp.mean(x * x, axis=-1, keepdims=True) + EPS) * g


def _sigmoid(x):
    return 1.0 / (1.0 + jnp.exp(-x))


def _silu(x):
    return x * _sigmoid(x)


def _layer_norm(x, g, b):
    mu = jnp.mean(x, axis=-1, keepdims=True)
    xc = x - mu
    var = jnp.mean(xc * xc, axis=-1, keepdims=True)
    return xc * lax.rsqrt(var + EPS) * g + b


def _head_rms64(p, g, mavg):
    ms = jnp.dot(p * p, mavg, preferred_element_type=F32)
    return p * lax.rsqrt(ms + EPS) * g


def _params(n_grid, vmem_mb=VMEM_MB):
    return pltpu.CompilerParams(dimension_semantics=("arbitrary",) * n_grid, vmem_limit_bytes=vmem_mb << 20)


def _rows(cols, tl=TL):
    return pl.BlockSpec((tl, cols), lambda i: (i, 0))


def _res(shape):
    nd = len(shape)
    return pl.BlockSpec(tuple(shape), lambda *_: (0,) * nd)


def _sds(shape, dtype=F32):
    return jax.ShapeDtypeStruct(tuple(shape), dtype)


def _accumulate(i, ref, val):
    @pl.when(i == 0)
    def _():
        ref[...] = val

    @pl.when(i > 0)
    def _():
        ref[...] += val


HBM = pl.BlockSpec(memory_space=pltpu.HBM)
ROW_ALIGN = 16


def _my_index():
    return lax.axis_index("x") * 4 + lax.axis_index("y") * 2 + lax.axis_index("c")


def _peer(k):
    x, y, c = lax.axis_index("x"), lax.axis_index("y"), lax.axis_index("c")
    return (x ^ ((k >> 2) & 1), y ^ ((k >> 1) & 1), c ^ (k & 1))


class _Rider:
    def __init__(self, gathers=(), scatters=(), done=None):
        self.gathers, self.scatters, self.done = list(gathers), list(scatters), done

    @property
    def inputs(self):
        return self.gathers + self.scatters

    def out_shapes(self):
        return ([_sds((N_DEV * a.shape[0],) + a.shape[1:], a.dtype) for a in self.gathers]
                + [_sds((N_DEV, a.shape[0] // N_DEV) + a.shape[1:], a.dtype) for a in self.scatters])


def _rider_copies(rider, in_refs, out_refs, send_sems, recv_sems, local_sems):
    me = _my_index()
    local, sends, recvs = [], [], []
    for t, (src, dst) in enumerate(zip(in_refs, out_refs)):
        gather = t < len(rider.gathers)
        rows = src.shape[0] if gather else src.shape[0] // N_DEV

        def block(ref, d, rows=rows):
            return ref.at[pl.ds(pl.multiple_of(d * rows, ROW_ALIGN), rows)]

        src_for = (lambda p, src=src: src) if gather else (lambda p, src=src: block(src, p))
        dst_for = (lambda d, dst=dst: block(dst, d)) if gather else (lambda d, dst=dst: dst.at[d])
        local.append(pltpu.make_async_copy(src_for(me), dst_for(me), local_sems.at[t]))
        for k in range(1, N_DEV):
            args = dict(send_sem=send_sems.at[t * N_DEV + k], recv_sem=recv_sems.at[t * N_DEV + k], device_id=_peer(k),
                        device_id_type=pl.DeviceIdType.MESH)
            sends.append(pltpu.make_async_remote_copy(src_ref=src_for(me ^ k), dst_ref=dst_for(me), **args))
            recvs.append(pltpu.make_async_remote_copy(src_ref=src_for(me ^ k), dst_ref=dst_for(me ^ k), **args))
    return local, sends, recvs


def _pcall(body, *, name, out_shape, grid=(), in_specs=None, out_specs=None, scratch_shapes=(), compiler_params=None,
           rider=None):
    if rider is None or not rider.inputs:
        return pl.pallas_call(body, name=name, grid=grid, in_specs=in_specs, out_specs=out_specs, out_shape=out_shape,
                              scratch_shapes=list(scratch_shapes), compiler_params=compiler_params)
    single = not isinstance(out_shape, (list, tuple))
    outs = [out_shape] if single else list(out_shape)
    ospecs = [out_specs] if single else list(out_specs)
    n_in, n_out, n_scr, n_r = len(in_specs), len(outs), len(scratch_shapes), len(rider.inputs)

    def wrapped(*refs):
        ins, rin = refs[:n_in], refs[n_in:n_in + n_r]
        own_out = refs[n_in + n_r:n_in + n_r + n_out]
        rout = refs[n_in + n_r + n_out:n_in + 2 * n_r + n_out]
        scratch = refs[n_in + 2 * n_r + n_out:n_in + 2 * n_r + n_out + n_scr]
        local, sends, recvs = _rider_copies(rider, rin, rout, *refs[-3:])

        def start():
            for cp in local + sends:
                cp.start()

        def wait():
            for cp in recvs:
                cp.wait_recv()
            for cp in sends:
                cp.wait_send()
            for cp in local:
                cp.wait()

        if grid:
            ids = [pl.program_id(a) for a in range(len(grid))]
            first = functools.reduce(jnp.logical_and, [i == 0 for i in ids])
            last = functools.reduce(jnp.logical_and, [i == n - 1 for i, n in zip(ids, grid)])
            pl.when(first)(start)
            body(*ins, *own_out, *scratch)
            pl.when(last)(wait)
        else:
            start()
            body(*ins, *own_out, *scratch)
            wait()

    call = pl.pallas_call(
        wrapped, name=name, grid=grid, in_specs=list(in_specs) + [HBM] * n_r, out_specs=ospecs + [HBM] * n_r,
        out_shape=outs + rider.out_shapes(),
        scratch_shapes=list(scratch_shapes) + [pltpu.SemaphoreType.DMA((n_r * N_DEV,)),
                                               pltpu.SemaphoreType.DMA((n_r * N_DEV,)), pltpu.SemaphoreType.DMA((n_r,))],
        compiler_params=compiler_params)

    def run(*args):
        res = call(*args, *rider.inputs)
        if rider.done is not None:
            rider.done(list(res[n_out:]))
        return res[0] if single else list(res[:n_out])

    return run


def _exchange(rider, name):
    def body():
        pass

    _pcall(body, name=name, out_shape=[], in_specs=[], out_specs=[], rider=rider)()


def _mixin_post(pq, pk, a, b, gq, gk, mavg):
    return _head_rms64(pq, gq, mavg), _head_rms64(pk, gk, mavg), a * _sigmoid(b)


def _mix_in_fwd(x, g_mix, w_in, gq, gk, mavg, tag, rider=None):
    L = x.shape[0]

    def body(x_ref, g_ref, w_ref, gq_ref, gk_ref, mavg_ref, q_ref, k_ref, v_ref, hg_ref, u_ref, p_ref):
        h = _rms(x_ref[...], g_ref[...])
        p = _nt(h, w_ref[...])
        q, k, hg = _mixin_post(p[:, 0:512], p[:, 512:1024], p[:, 1536:1792], p[:, 1792:2048],
                               gq_ref[...], gk_ref[...], mavg_ref[...])
        q_ref[...] = (q * SB_SCALE).astype(BF16)
        k_ref[...] = k.astype(BF16)
        v_ref[...] = p[:, 1024:1536].astype(BF16)
        hg_ref[...] = hg
        u_ref[...] = p[:, 2048:2304]
        p_ref[...] = p.astype(BF16)

    return _pcall(
        body, name=f"mix_in_fwd_{tag}", grid=(L // TL,),
        in_specs=[_rows(D_MODEL), _res((1, D_MODEL)), _res(w_in.shape), _res((1, 512)), _res((1, 512)), _res((512, 512))],
        out_specs=[_rows(512), _rows(512), _rows(512), _rows(256), _rows(256), _rows(w_in.shape[0])],
        out_shape=[_sds((L, 512), BF16), _sds((L, 512), BF16), _sds((L, 512), BF16), _sds((L, 256)), _sds((L, 256)),
                   _sds((L, w_in.shape[0]), BF16)],
        compiler_params=_params(1), rider=rider,
    )(x, g_mix, w_in, gq, gk, mavg)


def _mix_in_bwd(x, dres, p_kept, dq, dk, dv, dhg, du, g_mix, w_in, gq, gk, mavg, tag):
    L = x.shape[0]
    tl = 256

    def body(x_ref, dres_ref, p_ref, dq_ref, dk_ref, dv_ref, dhg_ref, du_ref, g_ref, w_ref, gq_ref, gk_ref, mavg_ref,
             dx_ref, dw_ref, dg_ref, dgq_ref, dgk_ref):
        i = pl.program_id(0)
        xx = x_ref[...]
        g = g_ref[...]
        mavg_v = mavg_ref[...]
        h, vjp_n = jax.vjp(_rms, xx, g)
        p = p_ref[...].astype(F32)
        _, vjp_p = jax.vjp(lambda pq, pk, a, b, gq_, gk_: _mixin_post(pq, pk, a, b, gq_, gk_, mavg_v),
                           p[:, 0:512], p[:, 512:1024], p[:, 1536:1792], p[:, 1792:2048], gq_ref[...], gk_ref[...])
        dpq, dpk, da, db, dgq, dgk = vjp_p((dq_ref[...], dk_ref[...], dhg_ref[...]))
        dp = jnp.concatenate([dpq, dpk, dv_ref[...], da, db, du_ref[...]], axis=1)
        dh = _nn(dp, w_ref[...])
        dxn, dg = vjp_n(dh)
        dx_ref[...] = dres_ref[...] + dxn
        _accumulate(i, dw_ref, _tn(dp, h))
        _accumulate(i, dg_ref, dg)
        _accumulate(i, dgq_ref, dgq)
        _accumulate(i, dgk_ref, dgk)

    r = lambda c: _rows(c, tl)
    return pl.pallas_call(
        body, name=f"mix_in_bwd_{tag}", grid=(L // tl,),
        in_specs=[r(D_MODEL), r(D_MODEL), r(w_in.shape[0]), r(512), r(512), r(512), r(256), r(256),
                  _res((1, D_MODEL)), _res(w_in.shape), _res((1, 512)), _res((1, 512)), _res((512, 512))],
        out_specs=[r(D_MODEL), _res(w_in.shape), _res((1, D_MODEL)), _res((1, 512)), _res((1, 512))],
        out_shape=[_sds((L, D_MODEL)), _sds(w_in.shape), _sds((1, D_MODEL)), _sds((1, 512)), _sds((1, 512))],
        compiler_params=_params(1),
    )(x, dres, p_kept, dq, dk, dv, dhg, du, g_mix, w_in, gq, gk, mavg)


def _sb_tri_consts():
    r = np.arange(2 * SB_TK)[:, None]
    c = np.arange(2 * SB_TK)[None, :]
    same = (r // SB_TK) == (c // SB_TK)
    later = (same & (r > c)).astype(np.float32)
    earlier = (same & (r < c)).astype(np.float32)
    return jnp.asarray(later, BF16), jnp.asarray(earlier, BF16)


def _two_heads(blk, lane_a):
    zero = jnp.zeros_like(blk)
    return jnp.concatenate([jnp.where(lane_a, blk, zero), jnp.where(lane_a, zero, blk)], axis=0)


def _sb_logs(z, i, j, masked):
    e = jnp.exp(-jnp.abs(z))
    lm = -(jnp.maximum(z, 0.0) + jnp.log(1.0 + e))
    ls = z + lm
    valid = None
    if masked:
        row = lax.broadcasted_iota(jnp.int32, (SB_TQ, 2 * SB_TK), 0)
        col = lax.broadcasted_iota(jnp.int32, (SB_TQ, 2 * SB_TK), 1) & (SB_TK - 1)
        valid = (j * SB_TK + col) < (i * SB_TQ + row)
        lm = jnp.where(valid, lm, 0.0)
    return lm, ls, lm.astype(BF16), valid


def _lane_halves(a, b):
    return jnp.concatenate([jnp.broadcast_to(a, (SB_TQ, SB_TK)), jnp.broadcast_to(b, (SB_TQ, SB_TK))], axis=1)


def _dot(a, b):
    return jnp.dot(a, b, preferred_element_type=F32)


def _dot_nt(a, b):
    return lax.dot_general(a, b, (((1,), (1,)), ((), ())), preferred_element_type=F32)


def _dot_tn(a, b):
    return lax.dot_general(a, b, (((0,), (0,)), ((), ())), preferred_element_type=F32)


def _sb_fwd(q, k, v, ul, tag, rider=None):
    L = q.shape[0]
    nq = L // SB_TQ
    per = SB_TQ // SB_TK
    wid = SB_PAIRS * LANES

    def body(q_ref, k_ref, v_ref, ul_ref, o_ref, rs_ref, n_ref):
        i = pl.program_id(1)
        ulv = ul_ref[...]
        lane_a = lax.broadcasted_iota(jnp.int32, (1, LANES), 1) < SB_HEAD_DIM
        lane_q = lax.broadcasted_iota(jnp.int32, (SB_TQ, LANES), 1)
        cols = [slice(p * LANES, (p + 1) * LANES) for p in range(SB_PAIRS)]
        qbs = [q_ref[:, c] for c in cols]

        def double_step(jhi, carry, masked):
            chains = [dict(p=p, j=jhi - d) for d in range(2) for p in range(SB_PAIRS)]
            for c in chains:
                c['off'] = pl.multiple_of(c['j'] * SB_TK, SB_TK)
                kb = k_ref[pl.ds(c['off'], SB_TK), cols[c['p']]]
                c['z'] = _dot_nt(qbs[c['p']], _two_heads(kb, lane_a))
            for c in chains:
                c['lm'], c['ls'], c['hi'], c['valid'] = _sb_logs(c.pop('z'), i, c['j'], masked)
            for c in chains:
                c['lb'] = _dot(c.pop('hi'), ulv)
            state = [list(s) for s in carry]
            for c in chains:
                ra, rb, _, rsave = state[c['p']]
                w = jnp.exp(c['ls'] + c['lb'] + _lane_halves(ra, rb))
                if masked:
                    w = jnp.where(c['valid'], w, 0.0)
                c['w'] = w.astype(BF16)
                lb, lm = c['lb'], c['lm']
                state[c['p']][3] = jnp.where(lane_q == c['j'], ra, jnp.where(lane_q == c['j'] + SB_HEAD_DIM, rb, rsave))
                state[c['p']][0] = ra + lb[:, 0:1] + lm[:, 0:1]
                state[c['p']][1] = rb + lb[:, SB_TK:SB_TK + 1] + lm[:, SB_TK:SB_TK + 1]
            for c in chains:
                vb = v_ref[pl.ds(c['off'], SB_TK), cols[c['p']]]
                state[c['p']][2] = state[c['p']][2] + _dot(c['w'], _two_heads(vb, lane_a))
            return tuple(tuple(s) for s in state)

        assert per == 2
        carry = tuple((jnp.zeros((SB_TQ, 1), F32), jnp.zeros((SB_TQ, 1), F32),
                       jnp.zeros((SB_TQ, LANES), F32), jnp.zeros((SB_TQ, LANES), F32)) for _ in range(SB_PAIRS))
        def alive(carry):
            m = carry[0][0]
            for c in carry:
                m = jnp.maximum(m, jnp.maximum(c[0], c[1]))
            return jnp.max(m) > SB_DEAD

        carry = double_step(i * per + 1, carry, True)
        n_done, _, carry = lax.while_loop(
            lambda st: jnp.logical_and(st[0] < i, st[1]),
            lambda st: (lambda c: (st[0] + 1, alive(c), c))(double_step(i * per - 1 - 2 * st[0], st[2], False)),
            (jnp.int32(0), alive(carry), carry))
        o_ref[...] = jnp.concatenate([c[2] for c in carry], axis=1)
        rs_ref[...] = jnp.concatenate([c[3] for c in carry], axis=1)
        n_ref[pl.program_id(0), i] = n_done

    qspec = pl.BlockSpec((SB_TQ, wid), lambda g, i: (i, g))
    kspec = pl.BlockSpec((L, wid), lambda g, i: (0, g))
    return _pcall(
        body, name=f"sb_fwd_{tag}", grid=(SB_WIDTH // wid, nq),
        in_specs=[qspec, kspec, kspec, pl.BlockSpec((2 * SB_TK, 2 * SB_TK), lambda g, i: (0, 0))],
        out_specs=[qspec, qspec, pl.BlockSpec(memory_space=pltpu.SMEM)],
        out_shape=[_sds((L, SB_WIDTH)), _sds((L, SB_WIDTH)), _sds((SB_WIDTH // wid, nq), jnp.int32)],
        compiler_params=_params(2), rider=rider,
    )(q, k, v, ul)


def _sb_bwd(n_done, q, k, v, rsave, do, ul, ue, tag):
    L = q.shape[0]
    nq = L // SB_TQ
    per = SB_TQ // SB_TK
    wid = SB_PAIRS * LANES

    def body(n_ref, q_ref, k_ref, v_ref, rs_ref, do_ref, ul_ref, ue_ref, dq_ref, dk_ref, dv_ref):
        i = pl.program_id(1)

        @pl.when(i == 0)
        def _():
            dk_ref[...] = jnp.zeros_like(dk_ref)
            dv_ref[...] = jnp.zeros_like(dv_ref)

        ulv = ul_ref[...]
        uev = ue_ref[...]
        lane_a = lax.broadcasted_iota(jnp.int32, (1, LANES), 1) < SB_HEAD_DIM
        lane_q = lax.broadcasted_iota(jnp.int32, (SB_TQ, LANES), 1)
        cols = [slice(p * LANES, (p + 1) * LANES) for p in range(SB_PAIRS)]
        qbs = [q_ref[:, c] for c in cols]
        dobs = [do_ref[:, c].astype(BF16) for c in cols]
        rsvs = [rs_ref[:, c] for c in cols]

        def double_step(jlo, carry, masked):
            chains = [dict(p=p, j=jlo + d) for d in range(2) for p in range(SB_PAIRS)]
            for c in chains:
                p = c['p']
                c['off'] = pl.multiple_of(c['j'] * SB_TK, SB_TK)
                c['kk2'] = _two_heads(k_ref[pl.ds(c['off'], SB_TK), cols[p]], lane_a)
                c['z'] = _dot_nt(qbs[p], c['kk2'])
                c['dw'] = _dot_nt(dobs[p], _two_heads(v_ref[pl.ds(c['off'], SB_TK), cols[p]], lane_a))
            for c in chains:
                c['lm'], c['ls'], c['hi'], c['valid'] = _sb_logs(c.pop('z'), i, c['j'], masked)
                c['ra'] = jnp.sum(jnp.where(lane_q == c['j'], rsvs[c['p']], 0.0), axis=1, keepdims=True)
                c['rb'] = jnp.sum(jnp.where(lane_q == c['j'] + SB_HEAD_DIM, rsvs[c['p']], 0.0), axis=1, keepdims=True)
            for c in chains:
                c['lb'] = _dot(c.pop('hi'), ulv)
            for c in chains:
                w = jnp.exp(c['ls'] + c.pop('lb') + _lane_halves(c['ra'], c['rb']))
                if masked:
                    w = jnp.where(c['valid'], w, 0.0)
                c['wb'] = w.astype(BF16)
                gg = w * c.pop('dw')
                c['gg'] = gg
                c['beta'] = jnp.exp(c['ls'])
            for c in chains:
                c['cb'] = _dot(c['gg'].astype(BF16), uev)
                c['dv2'] = _dot_tn(c.pop('wb'), dobs[c['p']])
            state = [list(s) for s in carry]
            for c in chains:
                pa, pb, _ = state[c['p']]
                gg, cb, beta = c['gg'], c['cb'], c['beta']
                dz = gg * (1.0 - beta) - beta * (cb + _lane_halves(pa, pb))
                if masked:
                    dz = jnp.where(c['valid'], dz, 0.0)
                c['dzb'] = dz.astype(BF16)
                state[c['p']][0] = pa + cb[:, SB_TK - 1:SB_TK] + gg[:, SB_TK - 1:SB_TK]
                state[c['p']][1] = pb + cb[:, 2 * SB_TK - 1:2 * SB_TK] + gg[:, 2 * SB_TK - 1:2 * SB_TK]
            for c in chains:
                c['dqc'] = _dot(c['dzb'], c['kk2'])
                c['dk2'] = _dot_tn(c['dzb'], qbs[c['p']])
            for c in chains:
                p, dk2, dv2 = c['p'], c['dk2'], c['dv2']
                state[p][2] = state[p][2] + c['dqc']
                dk_ref[pl.ds(c['off'], SB_TK), cols[p]] += jnp.where(lane_a, dk2[0:SB_TK], dk2[SB_TK:2 * SB_TK])
                dv_ref[pl.ds(c['off'], SB_TK), cols[p]] += jnp.where(lane_a, dv2[0:SB_TK], dv2[SB_TK:2 * SB_TK])
            return tuple(tuple(s) for s in state)

        assert per == 2
        carry = tuple((jnp.zeros((SB_TQ, 1), F32), jnp.zeros((SB_TQ, 1), F32), jnp.zeros((SB_TQ, LANES), F32))
                      for _ in range(SB_PAIRS))
        first = i - n_ref[pl.program_id(0), i]
        carry = lax.fori_loop(first, i, lambda jj, c: double_step(2 * jj, c, False), carry)
        carry = double_step(i * per, carry, True)
        dq_ref[...] = jnp.concatenate([c[2] for c in carry], axis=1) * SB_SCALE

    qspec = pl.BlockSpec((SB_TQ, wid), lambda g, i: (i, g))
    kspec = pl.BlockSpec((L, wid), lambda g, i: (0, g))
    kin = pl.BlockSpec((L, wid), lambda g, i: (0, g), pipeline_mode=pl.Buffered(1))
    cspec = pl.BlockSpec((2 * SB_TK, 2 * SB_TK), lambda g, i: (0, 0))
    return pl.pallas_call(
        body, name=f"sb_bwd_{tag}", grid=(SB_WIDTH // wid, nq),
        in_specs=[pl.BlockSpec(memory_space=pltpu.SMEM), qspec, kin, kin, qspec, qspec, cspec, cspec],
        out_specs=[qspec, kspec, kspec],
        out_shape=[_sds((L, SB_WIDTH)), _sds((L, SB_WIDTH)), _sds((L, SB_WIDTH))],
        compiler_params=_params(2, 58),
    )(n_done, q, k, v, rsave, do, ul, ue)


def _conv_fwd(hp, w, tag):
    L = hp.shape[0] - CONV_HALO
    win_rows = CONV_SUB + CONV_HALO

    def body(hp_ref, w_ref, o_ref):
        i = pl.program_id(0)

        def sub(s, _):
            t0 = pl.multiple_of(i * TL + s * CONV_SUB, CONV_SUB)
            win = hp_ref[pl.ds(t0, win_rows), :]
            acc = jnp.zeros((CONV_SUB, CONV_CH), F32)
            for kk in range(CONV_WIDTH):
                sh = CONV_WIDTH - 1 - kk
                r = win if sh == 0 else pltpu.roll(win, sh, 0)
                acc = acc + w_ref[kk:kk + 1, :] * r[CONV_HALO:, :]
            o_ref[pl.ds(pl.multiple_of(s * CONV_SUB, CONV_SUB), CONV_SUB), :] = acc
            return 0

        lax.fori_loop(0, TL // CONV_SUB, sub, 0)

    return pl.pallas_call(
        body, name=f"conv_fwd_{tag}", grid=(L // TL,),
        in_specs=[_res(hp.shape), _res(w.shape)],
        out_specs=_rows(CONV_CH),
        out_shape=_sds((L, CONV_CH)),
        compiler_params=_params(1),
    )(hp, w)


def _conv_bwd(dpad, hp, w, tag):
    L = hp.shape[0] - CONV_HALO
    win_rows = CONV_SUB + CONV_HALO
    n_tiles = L // TL

    def body(dp_ref, hp_ref, w_ref, dh_ref, dw_ref, acc_ref):
        i = pl.program_id(0)

        @pl.when(i == 0)
        def _():
            acc_ref[...] = jnp.zeros_like(acc_ref)

        def sub(s, _):
            t0 = pl.multiple_of(i * TL + s * CONV_SUB, CONV_SUB)
            wd = dp_ref[pl.ds(t0, win_rows), :]
            wh = hp_ref[pl.ds(t0, win_rows), :]
            dy = wd[0:CONV_SUB, :]
            acc = jnp.zeros((CONV_SUB, CONV_CH), F32)
            for kk in range(CONV_WIDTH):
                sh = CONV_WIDTH - 1 - kk
                rd = wd if sh == 0 else pltpu.roll(wd, win_rows - sh, 0)
                acc = acc + w_ref[kk:kk + 1, :] * rd[0:CONV_SUB, :]
                rh = wh if sh == 0 else pltpu.roll(wh, sh, 0)
                prod = dy * rh[CONV_HALO:, :]
                part = prod[0:SUBLANES]
                for m in range(1, CONV_SUB // SUBLANES):
                    part = part + prod[m * SUBLANES:(m + 1) * SUBLANES]
                acc_ref[kk] += part
            dh_ref[pl.ds(pl.multiple_of(s * CONV_SUB, CONV_SUB), CONV_SUB), :] = acc
            return 0

        lax.fori_loop(0, TL // CONV_SUB, sub, 0)

        @pl.when(i == n_tiles - 1)
        def _():
            for kk in range(CONV_WIDTH):
                dw_ref[kk:kk + 1, :] = jnp.sum(acc_ref[kk], axis=0, keepdims=True)

    return pl.pallas_call(
        body, name=f"conv_bwd_{tag}", grid=(n_tiles,),
        in_specs=[_res(dpad.shape), _res(hp.shape), _res(w.shape)],
        out_specs=[_rows(CONV_CH), _res(w.shape)],
        out_shape=[_sds((L, CONV_CH)), _sds(w.shape)],
        scratch_shapes=[pltpu.VMEM((CONV_WIDTH, SUBLANES, CONV_CH), F32)],
        compiler_params=_params(1),
    )(dpad, hp, w)


def _ssm_mask():
    r = np.arange(SSM_CH)[:, None] // SSM_GROUP
    c = np.arange(SSM_LANES)[None, :] // SSM_STATE
    return jnp.asarray((r == c).astype(np.float32))


def _ssm_discretize(lr, li, ldt, brt, bit):
    dt = jnp.exp(ldt)
    mag = jnp.exp(lr * dt)
    ar = mag * jnp.cos(li * dt)
    ai = mag * jnp.sin(li * dt)
    den = lr * lr + li * li
    fr = ((ar - 1.0) * lr + ai * li) / den
    fi = (ai * lr - (ar - 1.0) * li) / den
    return ar, ai, fr * brt - fi * bit, fr * bit + fi * brt


def _block_diag(rows16, mask):
    return jnp.where(mask > 0.5, jnp.tile(rows16, (SSM_GROUPS, 1)), 0.0)


def _block_diag_t(full, mask):
    m = jnp.where(mask > 0.5, full, 0.0)
    out = m[0:SSM_GROUP]
    for g in range(1, SSM_GROUPS):
        out = out + m[g * SSM_GROUP:(g + 1) * SSM_GROUP]
    return out


def _ssm_prep(lr, li, ldt, brt, bit, crt, cit, mask, tag):
    def body(lr_ref, li_ref, ldt_ref, brt_ref, bit_ref, crt_ref, cit_ref, m_ref,
             ar_ref, ai_ref, bbr_ref, bbi_ref, cbr_ref, cbi_ref):
        ar, ai, bbr, bbi = _ssm_discretize(lr_ref[...], li_ref[...], ldt_ref[...], brt_ref[...], bit_ref[...])
        m = m_ref[...]
        ar_ref[...] = ar
        ai_ref[...] = ai
        bbr_ref[...] = _block_diag(bbr, m).astype(BF16)
        bbi_ref[...] = _block_diag(bbi, m).astype(BF16)
        cbr_ref[...] = _block_diag(crt_ref[...], m).astype(BF16)
        cbi_ref[...] = _block_diag(cit_ref[...], m).astype(BF16)

    row = _sds((1, SSM_LANES))
    blk = _sds((SSM_CH, SSM_LANES), BF16)
    return pl.pallas_call(body, name=f"ssm_prep_{tag}", out_shape=[row, row, blk, blk, blk, blk])(
        lr, li, ldt, brt, bit, crt, cit, mask)


def _ssm_prep_bwd(lr, li, ldt, brt, bit, mask, dar, dai, dbbr, dbbi, dcbr, dcbi, tag):
    def body(lr_ref, li_ref, ldt_ref, brt_ref, bit_ref, m_ref, dar_ref, dai_ref, dbbr_ref, dbbi_ref, dcbr_ref,
             dcbi_ref, dlr_ref, dli_ref, dldt_ref, dbrt_ref, dbit_ref, dcrt_ref, dcit_ref):
        m = m_ref[...]
        _, vjp = jax.vjp(_ssm_discretize, lr_ref[...], li_ref[...], ldt_ref[...], brt_ref[...], bit_ref[...])
        dlr, dli, dldt, dbrt, dbit = vjp((dar_ref[...], dai_ref[...], _block_diag_t(dbbr_ref[...], m),
                                          _block_diag_t(dbbi_ref[...], m)))
        dlr_ref[...] = dlr
        dli_ref[...] = dli
        dldt_ref[...] = dldt
        dbrt_ref[...] = dbrt
        dbit_ref[...] = dbit
        dcrt_ref[...] = _block_diag_t(dcbr_ref[...], m)
        dcit_ref[...] = _block_diag_t(dcbi_ref[...], m)

    row = _sds((1, SSM_LANES))
    r16 = _sds((SSM_GROUP, SSM_LANES))
    return pl.pallas_call(body, name=f"ssm_prep_bwd_{tag}", out_shape=[row, row, row, r16, r16, r16, r16])(
        lr, li, ldt, brt, bit, mask, dar, dai, dbbr, dbbi, dcbr, dcbi)


def _complex_scan(br, bi, ar, ai, cr, ci, reverse):
    n = br.shape[0]
    row = lax.broadcasted_iota(jnp.int32, (n, 1), 0) & (SUBLANES - 1)
    xr, xi, pr, pi = br, bi, ar, ai
    d = 1
    while d < SUBLANES:
        if reverse:
            sr, si, keep = pltpu.roll(xr, n - d, 0), pltpu.roll(xi, n - d, 0), row < SUBLANES - d
        else:
            sr, si, keep = pltpu.roll(xr, d, 0), pltpu.roll(xi, d, 0), row >= d
        sr = jnp.where(keep, sr, 0.0)
        si = jnp.where(keep, si, 0.0)
        xr, xi = xr + pr * sr - pi * si, xi + pr * si + pi * sr
        pr, pi = pr * pr - pi * pi, 2.0 * pr * pi
        d *= 2
    powers = [(ar, ai)]
    for _ in range(SUBLANES - 1):
        qr, qi = powers[-1]
        powers.append((qr * ar - qi * ai, qr * ai + qi * ar))
    sub = lax.broadcasted_iota(jnp.int32, (SUBLANES, 1), 0)
    tr = jnp.zeros((SUBLANES, br.shape[1]), F32)
    ti = jnp.zeros((SUBLANES, br.shape[1]), F32)
    for r in range(SUBLANES):
        qr, qi = powers[SUBLANES - 1 - r] if reverse else powers[r]
        tr = jnp.where(sub == r, qr, tr)
        ti = jnp.where(sub == r, qi, ti)
    n_groups = n // SUBLANES
    out_r, out_i = [None] * n_groups, [None] * n_groups
    end = 0 if reverse else SUBLANES - 1
    for g in (reversed(range(n_groups)) if reverse else range(n_groups)):
        gr = xr[g * SUBLANES:(g + 1) * SUBLANES]
        gi = xi[g * SUBLANES:(g + 1) * SUBLANES]
        gr, gi = gr + tr * cr - ti * ci, gi + tr * ci + ti * cr
        cr, ci = gr[end:end + 1], gi[end:end + 1]
        out_r[g], out_i[g] = gr, gi
    return jnp.concatenate(out_r, axis=0), jnp.concatenate(out_i, axis=0)


def _ssm_fwd(u, ar, ai, bbr, bbi, cbr, cbi, dvec, tag, rider=None):
    L = u.shape[0]
    T = SSM_T

    def body(u_ref, ar_ref, ai_ref, bbr_ref, bbi_ref, cbr_ref, cbi_ref, d_ref, y_ref, xr_ref, xi_ref, cr_ref, ci_ref):
        i = pl.program_id(0)

        @pl.when(i == 0)
        def _():
            cr_ref[...] = jnp.zeros_like(cr_ref)
            ci_ref[...] = jnp.zeros_like(ci_ref)

        uu = u_ref[...]
        a_r, a_i = ar_ref[...], ai_ref[...]
        c_r, c_i = cr_ref[...], ci_ref[...]
        xr, xi = _complex_scan(_nn(uu, bbr_ref[...]), _nn(uu, bbi_ref[...]), a_r, a_i, c_r, c_i, False)
        cr_ref[...] = xr[T - 1:T, :]
        ci_ref[...] = xi[T - 1:T, :]
        xr_ref[...] = xr
        xi_ref[...] = xi
        y_ref[...] = _nt(xr, cbr_ref[...]) - _nt(xi, cbi_ref[...]) + d_ref[...] * uu

    blk = _res((SSM_CH, SSM_LANES))
    row = _res((1, SSM_LANES))
    return _pcall(
        body, name=f"ssm_fwd_{tag}", grid=(L // T,), rider=rider,
        in_specs=[_rows(SSM_CH, T), row, row, blk, blk, blk, blk, _res((1, SSM_CH))],
        out_specs=[_rows(SSM_CH, T), _rows(SSM_LANES, T), _rows(SSM_LANES, T)],
        out_shape=[_sds((L, SSM_CH)), _sds((L, SSM_LANES)), _sds((L, SSM_LANES))],
        scratch_shapes=[pltpu.VMEM((1, SSM_LANES), F32), pltpu.VMEM((1, SSM_LANES), F32)],
        compiler_params=_params(1),
    )(u, ar, ai, bbr, bbi, cbr, cbi, dvec)


def _ssm_bwd(dy, u, xr, xi, ar, ai, bbr, bbi, cbr, cbi, dvec, tag, rider=None):
    L = u.shape[0]
    T = SSM_T
    nc = L // T

    def body(dy_ref, u_ref, xr_ref, xi_ref, pr_ref, pi_ref, ar_ref, ai_ref, bbr_ref, bbi_ref, cbr_ref, cbi_ref, d_ref,
             du_ref, dar_ref, dai_ref, dbbr_ref, dbbi_ref, dcbr_ref, dcbi_ref, dd_ref, gr_ref, gi_ref):
        i = pl.program_id(0)

        @pl.when(i == 0)
        def _():
            gr_ref[...] = jnp.zeros_like(gr_ref)
            gi_ref[...] = jnp.zeros_like(gi_ref)

        dyy = dy_ref[...]
        uu = u_ref[...]
        xr, xi = xr_ref[...], xi_ref[...]
        a_r, a_i = ar_ref[...], ai_ref[...]
        g_r, g_i = gr_ref[...], gi_ref[...]
        row = lax.broadcasted_iota(jnp.int32, (T, 1), 0)
        gr, gi = _complex_scan(_nn(dyy, cbr_ref[...]), -_nn(dyy, cbi_ref[...]), a_r, -a_i, g_r, g_i, True)
        gr_ref[...] = gr[0:1, :]
        gi_ref[...] = gi[0:1, :]
        has_prev = (i < nc - 1).astype(F32)
        pr = pr_ref[SUBLANES - 1:SUBLANES, :] * has_prev
        pi = pi_ref[SUBLANES - 1:SUBLANES, :] * has_prev
        sr = jnp.where(row == 0, pr, pltpu.roll(xr, 1, 0))
        si = jnp.where(row == 0, pi, pltpu.roll(xi, 1, 0))
        _accumulate(i, dar_ref, jnp.sum(gr * sr + gi * si, axis=0, keepdims=True))
        _accumulate(i, dai_ref, jnp.sum(gi * sr - gr * si, axis=0, keepdims=True))
        _accumulate(i, dbbr_ref, _tn(uu, gr))
        _accumulate(i, dbbi_ref, _tn(uu, gi))
        _accumulate(i, dcbr_ref, _tn(dyy, xr))
        _accumulate(i, dcbi_ref, -_tn(dyy, xi))
        _accumulate(i, dd_ref, jnp.sum(dyy * uu, axis=0, keepdims=True))
        du_ref[...] = _nt(gr, bbr_ref[...]) + _nt(gi, bbi_ref[...]) + dyy * d_ref[...]

    rev = lambda cols: pl.BlockSpec((T, cols), lambda i: (nc - 1 - i, 0))
    prev = pl.BlockSpec((SUBLANES, SSM_LANES), lambda i: (jnp.maximum((nc - 1 - i) * (T // SUBLANES) - 1, 0), 0))
    blk = _res((SSM_CH, SSM_LANES))
    row = _res((1, SSM_LANES))
    return _pcall(
        body, name=f"ssm_bwd_{tag}", grid=(nc,), rider=rider,
        in_specs=[rev(SSM_CH), rev(SSM_CH), rev(SSM_LANES), rev(SSM_LANES), prev, prev, row, row, blk, blk, blk, blk,
                  _res((1, SSM_CH))],
        out_specs=[rev(SSM_CH), row, row, blk, blk, blk, blk, _res((1, SSM_CH))],
        out_shape=[_sds((L, SSM_CH)), _sds((1, SSM_LANES)), _sds((1, SSM_LANES)), _sds((SSM_CH, SSM_LANES)),
                   _sds((SSM_CH, SSM_LANES)), _sds((SSM_CH, SSM_LANES)), _sds((SSM_CH, SSM_LANES)), _sds((1, SSM_CH))],
        scratch_shapes=[pltpu.VMEM((1, SSM_LANES), F32), pltpu.VMEM((1, SSM_LANES), F32)],
        compiler_params=_params(1),
    )(dy, u, xr, xi, xr, xi, ar, ai, bbr, bbi, cbr, cbi, dvec)


def _conv_post(hc, dw_b, ln_g, ln_b):
    return _silu(_layer_norm(hc + dw_b, ln_g, ln_b))


def _branch_mix(o_sb, o_conv, t, g1, g2, g3):
    o_ssm = t[:, 0:SSM_CH] * _sigmoid(t[:, SSM_CH:2 * SSM_CH])
    return jnp.concatenate([_rms(o_sb, g1), _rms(o_conv, g2), _rms(o_ssm, g3)], axis=1)


def _branch_mix_split(o_sb, o_conv, ta, tb, g1, g2, g3):
    return jnp.concatenate([_rms(o_sb, g1), _rms(o_conv, g2), _rms(ta * _sigmoid(tb), g3)], axis=1)


def _mix_out_fwd(x, o_sb, hc, y, dw_b, ln_g, ln_b, pw2, glu_w, g1, g2, g3, w_out, tag):
    L = x.shape[0]

    def body(x_ref, o_ref, hc_ref, y_ref, dwb_ref, lng_ref, lnb_ref, pw2_ref, glu_ref, g1_ref, g2_ref, g3_ref, wo_ref,
             out_ref):
        c1 = _conv_post(hc_ref[...], dwb_ref[...], lng_ref[...], lnb_ref[...])
        o_conv = _nn(c1, pw2_ref[...])
        t = _nt(y_ref[...], glu_ref[...])
        mixed = _branch_mix(o_ref[...], o_conv, t, g1_ref[...], g2_ref[...], g3_ref[...])
        out_ref[...] = x_ref[...] + _nn(mixed, wo_ref[...])

    v256 = _res((1, 256))
    return pl.pallas_call(
        body, name=f"mix_out_fwd_{tag}", grid=(L // TL,),
        in_specs=[_rows(D_MODEL), _rows(512), _rows(256), _rows(256), v256, v256, v256, _res(pw2.shape),
                  _res(glu_w.shape), _res((1, 512)), v256, v256, _res(w_out.shape)],
        out_specs=_rows(D_MODEL),
        out_shape=_sds((L, D_MODEL)),
        compiler_params=_params(1),
    )(x, o_sb, hc, y, dw_b, ln_g, ln_b, pw2, glu_w, g1, g2, g3, w_out)


def _mix_out_bwd(dx1, o_sb, hc, y, dw_b, ln_g, ln_b, pw2, glu_w, g1, g2, g3, w_out, tag):
    L = dx1.shape[0]

    def body(dx_ref, o_ref, hc_ref, y_ref, dwb_ref, lng_ref, lnb_ref, pw2_ref, glu_ref, g1_ref, g2_ref, g3_ref, wo_ref,
             do_ref, dhc_ref, dy_ref, ddwb_ref, dlng_ref, dlnb_ref, dpw2_ref, dglu_ref, dg1_ref, dg2_ref, dg3_ref,
             dwo_ref):
        i = pl.program_id(0)
        dxx = dx_ref[...]
        yy = y_ref[...]
        c1, vjp1 = jax.vjp(_conv_post, hc_ref[...], dwb_ref[...], lng_ref[...], lnb_ref[...])
        o_conv = _nn(c1, pw2_ref[...])
        t = _nt(yy, glu_ref[...])
        mixed, vjp2 = jax.vjp(_branch_mix_split, o_ref[...], o_conv, t[:, 0:SSM_CH], t[:, SSM_CH:2 * SSM_CH],
                              g1_ref[...], g2_ref[...], g3_ref[...])
        dmixed = _nt(dxx, wo_ref[...])
        do_sb, do_conv, dta, dtb, dg1, dg2, dg3 = vjp2(dmixed)
        dt = jnp.concatenate([dta, dtb], axis=1)
        dc1 = _nt(do_conv, pw2_ref[...])
        dhc, ddwb, dlng, dlnb = vjp1(dc1)
        do_ref[...] = do_sb
        dhc_ref[...] = dhc
        dy_ref[...] = _nn(dt, glu_ref[...])
        _accumulate(i, dwo_ref, _tn(mixed, dxx))
        _accumulate(i, dglu_ref, _tn(dt, yy))
        _accumulate(i, dpw2_ref, _tn(c1, do_conv))
        _accumulate(i, ddwb_ref, ddwb)
        _accumulate(i, dlng_ref, dlng)
        _accumulate(i, dlnb_ref, dlnb)
        _accumulate(i, dg1_ref, dg1)
        _accumulate(i, dg2_ref, dg2)
        _accumulate(i, dg3_ref, dg3)

    v256 = _res((1, 256))
    return pl.pallas_call(
        body, name=f"mix_out_bwd_{tag}", grid=(L // TL,),
        in_specs=[_rows(D_MODEL), _rows(512), _rows(256), _rows(256), v256, v256, v256, _res(pw2.shape),
                  _res(glu_w.shape), _res((1, 512)), v256, v256, _res(w_out.shape)],
        out_specs=[_rows(512), _rows(256), _rows(256), v256, v256, v256, _res(pw2.shape), _res(glu_w.shape),
                   _res((1, 512)), v256, v256, _res(w_out.shape)],
        out_shape=[_sds((L, 512)), _sds((L, 256)), _sds((L, 256)), _sds((1, 256)), _sds((1, 256)), _sds((1, 256)),
                   _sds(pw2.shape), _sds(glu_w.shape), _sds((1, 512)), _sds((1, 256)), _sds((1, 256)), _sds(w_out.shape)],
        compiler_params=_params(1),
    )(dx1, o_sb, hc, y, dw_b, ln_g, ln_b, pw2, glu_w, g1, g2, g3, w_out)


def _xa_heads_norm(kk, kg):
    return jnp.concatenate([_rms(kk[:, h * XA_HEAD_DIM:(h + 1) * XA_HEAD_DIM], kg) for h in range(XA_HEADS)], axis=1)


def _xa_mem_fwd(mem, g_mem, wk, wv, kg, tag):
    def body(m_ref, g_ref, wk_ref, wv_ref, kg_ref, k_ref, v_ref):
        hm = _rms(m_ref[...], g_ref[...])
        k_ref[...] = _xa_heads_norm(_nn(hm, wk_ref[...]), kg_ref[...])
        v_ref[...] = _nn(hm, wv_ref[...])

    return pl.pallas_call(body, name=f"xa_mem_fwd_{tag}", out_shape=[_sds(mem.shape), _sds(mem.shape)],
                          compiler_params=_params(0))(mem, g_mem, wk, wv, kg)


def _xa_mem_bwd(mem, g_mem, wk, wv, kg, dkx, dvx, tag):
    def body(m_ref, g_ref, wk_ref, wv_ref, kg_ref, dk_ref, dv_ref, dwk_ref, dwv_ref, dg_ref, dkg_ref):
        hm, vjp_n = jax.vjp(_rms, m_ref[...], g_ref[...])
        kk = _nn(hm, wk_ref[...])
        dvv = dv_ref[...]
        dkg = jnp.zeros((1, XA_HEAD_DIM), F32)
        parts = []
        for h in range(XA_HEADS):
            sl = slice(h * XA_HEAD_DIM, (h + 1) * XA_HEAD_DIM)
            _, vjp_h = jax.vjp(_rms, kk[:, sl], kg_ref[...])
            dkh, dgh = vjp_h(dk_ref[:, sl])
            parts.append(dkh)
            dkg = dkg + dgh
        dkk = jnp.concatenate(parts, axis=1)
        dwk_ref[...] = _tn(hm, dkk)
        dwv_ref[...] = _tn(hm, dvv)
        dhm = _nt(dkk, wk_ref[...]) + _nt(dvv, wv_ref[...])
        _, dg = vjp_n(dhm)
        dg_ref[...] = dg
        dkg_ref[...] = dkg

    return pl.pallas_call(
        body, name=f"xa_mem_bwd_{tag}",
        out_shape=[_sds(wk.shape), _sds(wv.shape), _sds((1, D_MODEL)), _sds((1, XA_HEAD_DIM))],
        compiler_params=_params(0))(mem, g_mem, wk, wv, kg, dkx, dvx)


def _xa_fwd(x1, kx, vx, g_xa, wq, qg, wo, tag, rider=None):
    L = x1.shape[0]

    def body(x_ref, k_ref, v_ref, g_ref, wq_ref, qg_ref, wo_ref, out_ref, qp_ref):
        xx = x_ref[...]
        qp = _nn(_rms(xx, g_ref[...]), wq_ref[...])
        qp_ref[...] = qp.astype(BF16)
        heads = [slice(h * XA_HEAD_DIM, (h + 1) * XA_HEAD_DIM) for h in range(XA_HEADS)]
        qhs = [_rms(qp[:, sl], qg_ref[...]) for sl in heads]
        ss = [_nt(qh, k_ref[:, sl]) * (XA_HEAD_DIM ** -0.5) for qh, sl in zip(qhs, heads)]
        es = [jnp.exp(s - jnp.max(s, axis=-1, keepdims=True)) for s in ss]
        ps = [e / jnp.sum(e, axis=-1, keepdims=True) for e in es]
        outs = [_nn(p, v_ref[:, sl]) for p, sl in zip(ps, heads)]
        out_ref[...] = xx + _nn(jnp.concatenate(outs, axis=1), wo_ref[...])

    return _pcall(
        body, name=f"xa_fwd_{tag}", grid=(L // TL,), rider=rider,
        in_specs=[_rows(D_MODEL), _res(kx.shape), _res(vx.shape), _res((1, D_MODEL)), _res(wq.shape),
                  _res((1, XA_HEAD_DIM)), _res(wo.shape)],
        out_specs=[_rows(D_MODEL), _rows(D_MODEL)],
        out_shape=[_sds((L, D_MODEL)), _sds((L, D_MODEL), BF16)],
        compiler_params=_params(1),
    )(x1, kx, vx, g_xa, wq, qg, wo)


def _xa_bwd(x1, dx2, qp_kept, kx, vx, g_xa, wq, qg, wo, tag, rider=None):
    L = x1.shape[0]
    tl = 256

    def body(x_ref, dx_ref, qp_ref, k_ref, v_ref, g_ref, wq_ref, qg_ref, wo_ref,
             dx1_ref, dk_ref, dv_ref, dwq_ref, dwo_ref, dg_ref, dqg_ref):
        i = pl.program_id(0)
        xx = x_ref[...]
        dxx = dx_ref[...]
        hx, vjp_n = jax.vjp(_rms, xx, g_ref[...])
        qp = qp_ref[...].astype(F32)
        do = _nt(dxx, wo_ref[...])
        heads = [dict(sl=slice(h * XA_HEAD_DIM, (h + 1) * XA_HEAD_DIM)) for h in range(XA_HEADS)]
        for c in heads:
            c['kh'], c['vh'], c['doh'] = k_ref[:, c['sl']].astype(BF16), v_ref[:, c['sl']].astype(BF16), do[:, c['sl']]
            c['qh'], c['vjp_q'] = jax.vjp(_rms, qp[:, c['sl']], qg_ref[...])
        for c in heads:
            c['s'] = _nt(c['qh'], c['kh'])
            c['dp'] = _nt(c['doh'], c['vh'])
        for c in heads:
            s = c.pop('s') * (XA_HEAD_DIM ** -0.5)
            e = jnp.exp(s - jnp.max(s, axis=-1, keepdims=True))
            c['p'] = e / jnp.sum(e, axis=-1, keepdims=True)
        for c in heads:
            c['out'] = _nn(c['p'], c['vh'])
            c['dv'] = _tn(c['p'], c['doh'])
        for c in heads:
            p, dp = c['p'], c.pop('dp')
            c['ds'] = p * (dp - jnp.sum(dp * p, axis=-1, keepdims=True)) * (XA_HEAD_DIM ** -0.5)
        for c in heads:
            c['dk'] = _tn(c['ds'], c['qh'])
            c['dq'] = _nn(c['ds'], c['kh'])
        dqg = jnp.zeros((1, XA_HEAD_DIM), F32)
        for c in heads:
            c['dqp'], dgh = c['vjp_q'](c['dq'])
            dqg = dqg + dgh
        o = jnp.concatenate([c['out'] for c in heads], axis=1)
        dqp = jnp.concatenate([c['dqp'] for c in heads], axis=1)
        dks, dvs = [c['dk'] for c in heads], [c['dv'] for c in heads]
        dxn, dg = vjp_n(_nt(dqp, wq_ref[...]))
        dx1_ref[...] = dxx + dxn
        _accumulate(i, dk_ref, jnp.concatenate(dks, axis=1))
        _accumulate(i, dv_ref, jnp.concatenate(dvs, axis=1))
        _accumulate(i, dwq_ref, _tn(hx, dqp))
        _accumulate(i, dwo_ref, _tn(o, dxx))
        _accumulate(i, dg_ref, dg)
        _accumulate(i, dqg_ref, dqg)

    r = lambda c: _rows(c, tl)
    return _pcall(
        body, name=f"xa_bwd_{tag}", grid=(L // tl,), rider=rider,
        in_specs=[r(D_MODEL), r(D_MODEL), r(D_MODEL), _res(kx.shape), _res(vx.shape), _res((1, D_MODEL)),
                  _res(wq.shape), _res((1, XA_HEAD_DIM)), _res(wo.shape)],
        out_specs=[r(D_MODEL), _res(kx.shape), _res(vx.shape), _res(wq.shape), _res(wo.shape), _res((1, D_MODEL)),
                   _res((1, XA_HEAD_DIM))],
        out_shape=[_sds((L, D_MODEL)), _sds(kx.shape), _sds(vx.shape), _sds(wq.shape), _sds(wo.shape),
                   _sds((1, D_MODEL)), _sds((1, XA_HEAD_DIM))],
        compiler_params=_params(1),
    )(x1, dx2, qp_kept, kx, vx, g_xa, wq, qg, wo)


def _swiglu(gate, up):
    return _silu(gate) * up


def _ffn_fwd(x2, g, w_in, w_out, tag, rider=None, tgt=None):
    L = x2.shape[0]
    tl = 256
    n_tiles = L // tl

    def ffn(x_ref, g_ref, wi_ref, wo_ref, hf_ref, gu_ref, act_ref):
        xx = x_ref[...]
        hf = _rms(xx, g_ref[...]).astype(BF16)
        gu = _nt(hf, wi_ref[...])
        act = _swiglu(gu[:, 0:FFN_HIDDEN], gu[:, FFN_HIDDEN:2 * FFN_HIDDEN]).astype(BF16)
        hf_ref[...] = hf
        gu_ref[...] = gu.astype(BF16)
        act_ref[...] = act
        return xx + _nn(act, wo_ref[...])

    def body(x_ref, g_ref, wi_ref, wo_ref, out_ref, hf_ref, gu_ref, act_ref):
        out_ref[...] = ffn(x_ref, g_ref, wi_ref, wo_ref, hf_ref, gu_ref, act_ref)

    def body_loss(x_ref, g_ref, wi_ref, wo_ref, t_ref, dy_ref, hf_ref, gu_ref, act_ref, loss_ref, acc_ref):
        i = pl.program_id(0)
        diff = ffn(x_ref, g_ref, wi_ref, wo_ref, hf_ref, gu_ref, act_ref) - t_ref[...]
        dy_ref[...] = diff * (1.0 / D_MODEL)
        _accumulate(i, acc_ref, jnp.sum(diff * diff, axis=0, keepdims=True))

        @pl.when(i == n_tiles - 1)
        def _():
            loss_ref[...] = jnp.sum(acc_ref[...], axis=1, keepdims=True) * (0.5 / D_MODEL)

    r = lambda c: _rows(c, tl)
    in_specs = [r(D_MODEL), _res((1, D_MODEL)), _res(w_in.shape), _res(w_out.shape)]
    out_specs = [r(D_MODEL), r(D_MODEL), r(2 * FFN_HIDDEN), r(FFN_HIDDEN)]
    out_shape = [_sds((L, D_MODEL)), _sds((L, D_MODEL), BF16), _sds((L, 2 * FFN_HIDDEN), BF16),
                 _sds((L, FFN_HIDDEN), BF16)]
    if tgt is None:
        return _pcall(body, name=f"ffn_fwd_{tag}", grid=(n_tiles,), rider=rider, in_specs=in_specs,
                      out_specs=out_specs, out_shape=out_shape, compiler_params=_params(1, 56))(x2, g, w_in, w_out)
    return _pcall(body_loss, name=f"ffn_fwd_loss_{tag}", grid=(n_tiles,), rider=rider,
                  in_specs=in_specs + [r(D_MODEL)], out_specs=out_specs + [_res((1, 1))],
                  out_shape=out_shape + [_sds((1, 1))], scratch_shapes=[pltpu.VMEM((1, D_MODEL), F32)],
                  compiler_params=_params(1, 56))(x2, g, w_in, w_out, tgt)


def _ffn_bwd(x2, dx3, gu, g, w_in, w_out, tag, rider=None):
    L = x2.shape[0]
    tl = 256

    def body(x_ref, dx_ref, gu_ref, g_ref, wi_ref, wo_ref, dx2_ref, dgu_ref, dg_ref):
        i = pl.program_id(0)
        dxx = dx_ref[...]
        _, vjp_n = jax.vjp(_rms, x_ref[...], g_ref[...])
        _, vjp_a = jax.vjp(_swiglu, gu_ref[:, 0:FFN_HIDDEN].astype(F32), gu_ref[:, FFN_HIDDEN:2 * FFN_HIDDEN].astype(F32))
        dgate, dup = vjp_a(_nt(dxx, wo_ref[...]))
        dgu = jnp.concatenate([dgate, dup], axis=1).astype(BF16)
        dxn, dg = vjp_n(_nn(dgu, wi_ref[...]))
        dx2_ref[...] = dxx + dxn
        dgu_ref[...] = dgu
        _accumulate(i, dg_ref, dg)

    r = lambda c: _rows(c, tl)
    return _pcall(
        body, name=f"ffn_bwd_{tag}", grid=(L // tl,), rider=rider,
        in_specs=[r(D_MODEL), r(D_MODEL), r(2 * FFN_HIDDEN), _res((1, D_MODEL)), _res(w_in.shape), _res(w_out.shape)],
        out_specs=[r(D_MODEL), r(2 * FFN_HIDDEN), _res((1, D_MODEL))],
        out_shape=[_sds((L, D_MODEL)), _sds((L, 2 * FFN_HIDDEN), BF16), _sds((1, D_MODEL))],
        compiler_params=_params(1, 56),
    )(x2, dx3, gu, g, w_in, w_out)


def _matmul_tn(a, b, tm, tn, tag):
    L, M = a.shape
    N = b.shape[1]
    tk = min(L, 2048)
    nk = L // tk

    def body(a_ref, b_ref, o_ref, acc_ref):
        k = pl.program_id(2)
        _accumulate(k, acc_ref, _tn(a_ref[...], b_ref[...]))

        @pl.when(k == nk - 1)
        def _():
            o_ref[...] = acc_ref[...].astype(BF16)

    return pl.pallas_call(
        body, name=f"matmul_tn_{tag}", grid=(M // tm, N // tn, nk),
        in_specs=[pl.BlockSpec((tk, tm), lambda m, n, k: (k, m)), pl.BlockSpec((tk, tn), lambda m, n, k: (k, n))],
        out_specs=pl.BlockSpec((tm, tn), lambda m, n, k: (m, n)),
        out_shape=_sds((M, N), BF16),
        scratch_shapes=[pltpu.VMEM((tm, tn), F32)],
        compiler_params=_params(3),
    )(a, b)


def _row(v):
    return v.reshape(1, -1)


def _layer_consts():
    r = np.arange(SB_WIDTH)
    mavg = ((r[:, None] // SB_HEAD_DIM) == (r[None, :] // SB_HEAD_DIM)).astype(np.float32) / SB_HEAD_DIM
    ul, ue = _sb_tri_consts()
    return dict(mavg=jnp.asarray(mavg), ul=ul, ue=ue, mask=_ssm_mask())


def _ssm_rows(P):
    lanes = lambda a: a.reshape(1, SSM_LANES)
    return dict(
        lr=lanes(P['ssm_lam_re']), li=lanes(P['ssm_lam_im']),
        ldt=lanes(jnp.repeat(P['ssm_log_dt'], SSM_STATE)),
        brt=P['ssm_b_re'].transpose(2, 0, 1).reshape(SSM_GROUP, SSM_LANES),
        bit=P['ssm_b_im'].transpose(2, 0, 1).reshape(SSM_GROUP, SSM_LANES),
        crt=P['ssm_c_re'].transpose(1, 0, 2).reshape(SSM_GROUP, SSM_LANES),
        cit=P['ssm_c_im'].transpose(1, 0, 2).reshape(SSM_GROUP, SSM_LANES))


def _layer_fwd(x, mem, P, C, tag, ride, tgt=None):
    gq = _row(jnp.tile(P['sb_q_norm_g'], SB_WIDTH // SB_HEAD_DIM))
    gk = _row(jnp.tile(P['sb_k_norm_g'], SB_WIDTH // SB_HEAD_DIM))
    q, k, v, hg, u, p = _mix_in_fwd(x, _row(P['norm_mix_g']), P['w_in'], gq, gk, C['mavg'], tag,
                                 rider=ride("mix_in_fwd_" + tag))
    o_sb, rsave, n_done = _sb_fwd(q, k, v, C['ul'], tag, rider=ride("sb_fwd_" + tag))
    hp = jnp.pad(hg, ((CONV_HALO, 0), (0, 0)))
    hc = _conv_fwd(hp, P['conv_dw_w'].T, tag)
    S = _ssm_rows(P)
    ar, ai, bbr, bbi, cbr, cbi = _ssm_prep(S['lr'], S['li'], S['ldt'], S['brt'], S['bit'], S['crt'], S['cit'],
                                           C['mask'], tag)
    y, xr, xi = _ssm_fwd(u, ar, ai, bbr, bbi, cbr, cbi, _row(P['ssm_d']), tag, rider=ride("ssm_fwd_" + tag))
    gb = P['branch_norm_g']
    x1 = _mix_out_fwd(x, o_sb, hc, y, _row(P['conv_dw_b']), _row(P['conv_ln_g']), _row(P['conv_ln_b']),
                      P['conv_pw2_w'], P['ssm_glu_w'], _row(gb[0:512]), _row(gb[512:768]), _row(gb[768:1024]),
                      P['w_out'], tag)
    kx, vx = _xa_mem_fwd(mem, _row(P['norm_mem_g']), P['xa_wk'], P['xa_wv'], _row(P['xa_k_norm_g']), tag)
    x2, qp = _xa_fwd(x1, kx, vx, _row(P['norm_xa_g']), P['xa_wq'], _row(P['xa_q_norm_g']), P['xa_wo'], tag,
                 rider=ride("xa_fwd_" + tag))
    x3, hf, gu, act, *loss = _ffn_fwd(x2, _row(P['norm_ffn_g']), P['ffn_w_in'], P['ffn_w_out'], tag,
                                      rider=ride("ffn_fwd_" + tag), tgt=tgt)
    saved = dict(p=p, qp=qp, hf=hf, gu=gu, act=act, x=x, q=q, k=k, v=v, rsave=rsave, n_done=n_done, o_sb=o_sb, hp=hp, hc=hc, u=u, y=y, xr=xr, xi=xi, x1=x1, x2=x2,
                 kx=kx, vx=vx, gq=gq, gk=gk, S=S, ssm=(ar, ai, bbr, bbi, cbr, cbi))
    return x3, saved, (loss[0] if loss else None)


def _layer_bwd(dx3, mem, P, C, sv, tag, ride, G):
    dx2, dgu, dg = _ffn_bwd(sv['x2'], dx3, sv['gu'], _row(P['norm_ffn_g']), P['ffn_w_in'], P['ffn_w_out'], tag,
                            rider=ride("ffn_bwd_" + tag))
    G['norm_ffn_g'] = dg.reshape(-1)
    G['ffn_w_in'] = _matmul_tn(dgu, sv['hf'], 2 * FFN_HIDDEN // 4, D_MODEL, "ffn_in_" + tag)
    G['ffn_w_out'] = _matmul_tn(sv['act'], dx3, FFN_HIDDEN // 2, 512, "ffn_out_" + tag)
    dx1, dkx, dvx, dwq, dwo, dg, dqg = _xa_bwd(sv['x1'], dx2, sv['qp'], sv['kx'], sv['vx'], _row(P['norm_xa_g']), P['xa_wq'],
                                               _row(P['xa_q_norm_g']), P['xa_wo'], tag, rider=ride("xa_bwd_" + tag))
    G['xa_wq'], G['xa_wo'], G['norm_xa_g'], G['xa_q_norm_g'] = dwq, dwo, dg.reshape(-1), dqg.reshape(-1)
    dwk, dwv, dg, dkg = _xa_mem_bwd(mem, _row(P['norm_mem_g']), P['xa_wk'], P['xa_wv'], _row(P['xa_k_norm_g']),
                                    dkx, dvx, tag)
    G['xa_wk'], G['xa_wv'], G['norm_mem_g'], G['xa_k_norm_g'] = dwk, dwv, dg.reshape(-1), dkg.reshape(-1)
    gb = P['branch_norm_g']
    (do_sb, dhc, dy, ddwb, dlng, dlnb, dpw2, dglu, dg1, dg2, dg3, dwout) = _mix_out_bwd(
        dx1, sv['o_sb'], sv['hc'], sv['y'], _row(P['conv_dw_b']), _row(P['conv_ln_g']), _row(P['conv_ln_b']),
        P['conv_pw2_w'], P['ssm_glu_w'], _row(gb[0:512]), _row(gb[512:768]), _row(gb[768:1024]), P['w_out'], tag)
    G['conv_dw_b'], G['conv_ln_g'], G['conv_ln_b'] = ddwb.reshape(-1), dlng.reshape(-1), dlnb.reshape(-1)
    G['conv_pw2_w'], G['ssm_glu_w'], G['w_out'] = dpw2, dglu, dwout
    G['branch_norm_g'] = jnp.concatenate([dg1.reshape(-1), dg2.reshape(-1), dg3.reshape(-1)])
    ar, ai, bbr, bbi, cbr, cbi = sv['ssm']
    du, dar, dai, dbbr, dbbi, dcbr, dcbi, dd = _ssm_bwd(dy, sv['u'], sv['xr'], sv['xi'], ar, ai, bbr, bbi, cbr, cbi,
                                                        _row(P['ssm_d']), tag, rider=ride("ssm_bwd_" + tag))
    S = sv['S']
    dlr, dli, dldt, dbrt, dbit, dcrt, dcit = _ssm_prep_bwd(S['lr'], S['li'], S['ldt'], S['brt'], S['bit'], C['mask'],
                                                           dar, dai, dbbr, dbbi, dcbr, dcbi, tag)
    G['ssm_lam_re'] = dlr.reshape(SSM_GROUPS, SSM_STATE)
    G['ssm_lam_im'] = dli.reshape(SSM_GROUPS, SSM_STATE)
    G['ssm_log_dt'] = dldt.reshape(SSM_GROUPS, SSM_STATE).sum(axis=1)
    G['ssm_b_re'] = dbrt.reshape(SSM_GROUP, SSM_GROUPS, SSM_STATE).transpose(1, 2, 0)
    G['ssm_b_im'] = dbit.reshape(SSM_GROUP, SSM_GROUPS, SSM_STATE).transpose(1, 2, 0)
    G['ssm_c_re'] = dcrt.reshape(SSM_GROUP, SSM_GROUPS, SSM_STATE).transpose(1, 0, 2)
    G['ssm_c_im'] = dcit.reshape(SSM_GROUP, SSM_GROUPS, SSM_STATE).transpose(1, 0, 2)
    G['ssm_d'] = dd.reshape(-1)
    dpad = jnp.pad(dhc, ((0, CONV_HALO), (0, 0)))
    dhg, ddww = _conv_bwd(dpad, sv['hp'], P['conv_dw_w'].T, tag)
    G['conv_dw_w'] = ddww.T
    dq, dk, dv = _sb_bwd(sv['n_done'], sv['q'], sv['k'], sv['v'], sv['rsave'], do_sb, C['ul'], C['ue'], tag)
    dx, dwin, dg, dgq, dgk = _mix_in_bwd(sv['x'], dx1, sv['p'], dq, dk, dv, dhg, du, _row(P['norm_mix_g']), P['w_in'],
                                         sv['gq'], sv['gk'], C['mavg'], tag)
    G['w_in'], G['norm_mix_g'] = dwin, dg.reshape(-1)
    G['sb_q_norm_g'] = dgq.reshape(SB_WIDTH // SB_HEAD_DIM, SB_HEAD_DIM).sum(axis=0)
    G['sb_k_norm_g'] = dgk.reshape(SB_WIDTH // SB_HEAD_DIM, SB_HEAD_DIM).sum(axis=0)
    return dx, G


def _slot_sum(r_ref):
    g = r_ref[0].astype(F32)
    for s in range(1, r_ref.shape[0]):
        g = g + r_ref[s].astype(F32)
    return g


def _adam_update(g, w, m, v):
    nm = ADAM_B1 * m + (1.0 - ADAM_B1) * g
    nv = ADAM_B2 * v + (1.0 - ADAM_B2) * (g * g)
    m_hat = nm * (1.0 / (1.0 - ADAM_B1 ** ADAM_STEP))
    v_hat = nv * (1.0 / (1.0 - ADAM_B2 ** ADAM_STEP))
    return -ADAM_LR * (m_hat / (jnp.sqrt(v_hat) + ADAM_EPS) + ADAM_WD * w), nm, nv


def _adamw(recv, w, m, v, tile, name):
    n_slots, R, C = recv.shape

    def body(r_ref, w_ref, m_ref, v_ref, g_ref, d_ref, nm_ref, nv_ref):
        g = _slot_sum(r_ref)
        g_ref[...] = g
        d_ref[...], nm_ref[...], nv_ref[...] = _adam_update(g, w_ref[...], m_ref[...], v_ref[...])

    rows = pl.BlockSpec((tile, C), lambda i: (i, 0))
    out = _sds((R, C))
    return pl.pallas_call(
        body, name=name, grid=(R // tile,),
        in_specs=[pl.BlockSpec((n_slots, tile, C), lambda i: (0, i, 0)), rows, rows, rows],
        out_specs=[rows, rows, rows, rows],
        out_shape=[out, out, out, out],
        compiler_params=_params(1),
    )(recv, w, m, v)


def _adamw_layers(recvs, w, m, v, tile, name):
    n_slots, R, C = recvs[0].shape

    def body(*refs):
        r_refs = refs[:DEPTH]
        w_ref, m_ref, v_ref, g_ref, d_ref, nm_ref, nv_ref = refs[DEPTH:]
        for l in range(DEPTH):
            @pl.when(pl.program_id(0) == l)
            def _(l=l):
                g = _slot_sum(r_refs[l])
                g_ref[0] = g
                d_ref[0], nm_ref[0], nv_ref[0] = _adam_update(g, w_ref[0], m_ref[0], v_ref[0])

    rspec = lambda l: pl.BlockSpec((n_slots, tile, C), lambda ll, i: (0, jnp.where(ll == l, i, 0), 0))
    rows = pl.BlockSpec((1, tile, C), lambda ll, i: (ll, i, 0))
    out = _sds((DEPTH, R, C))
    return pl.pallas_call(
        body, name=name, grid=(DEPTH, R // tile),
        in_specs=[rspec(l) for l in range(DEPTH)] + [rows, rows, rows],
        out_specs=[rows, rows, rows, rows],
        out_shape=[out, out, out, out],
        compiler_params=_params(2),
    )(*recvs, w, m, v)


def _reduce_slots(recv, tile, name):
    n_slots, R, C = recv.shape

    def body(r_ref, g_ref):
        g_ref[...] = _slot_sum(r_ref)

    return pl.pallas_call(
        body, name=name, grid=(R // tile,),
        in_specs=[pl.BlockSpec((n_slots, tile, C), lambda i: (0, i, 0))],
        out_specs=pl.BlockSpec((tile, C), lambda i: (i, 0)),
        out_shape=_sds((R, C)),
        compiler_params=_params(1),
    )(recv)


SEG = SUBLANES * LANES


def _pad_to(n, mult):
    return -(-n // mult) * mult


def _pack(arrays, dtype, row_mult, lead=0):
    keep = [(0, 0)] * lead
    parts = []
    for a in arrays:
        flat = a.reshape(a.shape[:lead] + (-1,)).astype(dtype)
        n = flat.shape[-1]
        parts.append(jnp.pad(flat, keep + [(0, _pad_to(n, SEG) - n)]))
    flat = jnp.concatenate(parts, axis=-1)
    n = flat.shape[-1]
    flat = jnp.pad(flat, keep + [(0, _pad_to(n, row_mult * LANES) - n)])
    return flat.reshape(flat.shape[:lead] + (-1, LANES))


def _unpack(buf, shapes):
    lead = buf.shape[:-2]
    flat = buf.reshape(lead + (-1,))
    out, off = [], 0
    for shp in shapes:
        n = int(np.prod(shp))
        out.append(flat[..., off:off + n].reshape(lead + tuple(shp)))
        off += _pad_to(n, SEG)
    return out


def _rows_first(a, name):
    return a.transpose(0, 2, 1) if SHARD_AXIS[name] == 2 else a


SMALL_TILE = 256
DIRECT_NAMES = [n for n in BIG_NAMES if n != 'conv_dw_w']
GATHER_RIDES = {
    "mix_in_fwd_l0": [(0, 'conv_pw2_w'), (0, 'ssm_glu_w'), (0, 'w_out'), (0, 'xa_wq')],
    "sb_fwd_l0": [(0, 'xa_wk'), (0, 'xa_wv'), (0, 'xa_wo'), (0, 'ffn_w_in')],
    "ssm_fwd_l0": [(0, 'ffn_w_out')],
    "xa_fwd_l0": [(1, 'w_in'), (1, 'conv_pw2_w'), (1, 'ssm_glu_w')],
    "ffn_fwd_l0": [(1, 'w_out'), (1, 'xa_wq'), (1, 'xa_wk'), (1, 'xa_wv'), (1, 'xa_wo')],
    "sb_fwd_l1": [(1, 'ffn_w_in'), (1, 'ffn_w_out')],
}
_MID = ['xa_wq', 'xa_wo', 'xa_wk', 'xa_wv', 'w_out', 'conv_pw2_w', 'ssm_glu_w']
SCATTER_RIDES = {
    "xa_bwd_l1": [(1, 'ffn_w_in'), (1, 'ffn_w_out')],
    "ssm_bwd_l1": [(1, n) for n in _MID],
    "ffn_bwd_l0": [(1, 'w_in')],
    "xa_bwd_l0": [(0, 'ffn_w_in'), (0, 'ffn_w_out')],
    "ssm_bwd_l0": [(0, n) for n in _MID],
}


def _tile_rows(rows):
    return next(t for t in range(min(rows, 256), 0, -ROW_ALIGN) if rows % t == 0 and t % ROW_ALIGN == 0)


class _LayerWeights:
    def __init__(self, layer, small, full, conv):
        self.layer, self.small, self.full, self.conv = layer, small, full, conv

    def __getitem__(self, name):
        if name == 'conv_dw_w':
            return self.conv[self.layer]
        return self.full[(self.layer, name)] if name in SHARD_AXIS else self.small[name][self.layer]


def kernel(x, mem, norm_mix_g, w_in, sb_q_norm_g, sb_k_norm_g, conv_dw_w, conv_dw_b, conv_ln_g, conv_ln_b, conv_pw2_w, ssm_lam_re, ssm_lam_im, ssm_log_dt, ssm_b_re, ssm_b_im, ssm_c_re, ssm_c_im, ssm_d, ssm_glu_w, branch_norm_g, w_out, norm_xa_g, norm_mem_g, xa_wq, xa_wk, xa_wv, xa_q_norm_g, xa_k_norm_g, xa_wo, norm_ffn_g, ffn_w_in, ffn_w_out, loss_target, m_norm_mix_g, m_w_in, m_sb_q_norm_g, m_sb_k_norm_g, m_conv_dw_w, m_conv_dw_b, m_conv_ln_g, m_conv_ln_b, m_conv_pw2_w, m_ssm_lam_re, m_ssm_lam_im, m_ssm_log_dt, m_ssm_b_re, m_ssm_b_im, m_ssm_c_re, m_ssm_c_im, m_ssm_d, m_ssm_glu_w, m_branch_norm_g, m_w_out, m_norm_xa_g, m_norm_mem_g, m_xa_wq, m_xa_wk, m_xa_wv, m_xa_q_norm_g, m_xa_k_norm_g, m_xa_wo, m_norm_ffn_g, m_ffn_w_in, m_ffn_w_out, v_norm_mix_g, v_w_in, v_sb_q_norm_g, v_sb_k_norm_g, v_conv_dw_w, v_conv_dw_b, v_conv_ln_g, v_conv_ln_b, v_conv_pw2_w, v_ssm_lam_re, v_ssm_lam_im, v_ssm_log_dt, v_ssm_b_re, v_ssm_b_im, v_ssm_c_re, v_ssm_c_im, v_ssm_d, v_ssm_glu_w, v_branch_norm_g, v_w_out, v_norm_xa_g, v_norm_mem_g, v_xa_wq, v_xa_wk, v_xa_wv, v_xa_q_norm_g, v_xa_k_norm_g, v_xa_wo, v_norm_ffn_g, v_ffn_w_in, v_ffn_w_out):
    args = locals()
    w_loc = {n: args[n] for n in WEIGHT_NAMES}
    m_loc = {n: args["m_" + n] for n in WEIGHT_NAMES}
    v_loc = {n: args["v_" + n] for n in WEIGHT_NAMES}
    me = _my_index()
    shard = {(l, n): _rows_first(w_loc[n], n)[l].astype(BF16) for l in range(DEPTH) for n in DIRECT_NAMES}
    conv_shape = _rows_first(w_loc['conv_dw_w'], 'conv_dw_w').shape
    conv_rows = _pack([_rows_first(w_loc['conv_dw_w'], 'conv_dw_w')], F32, SUBLANES)
    full = {}
    recv = {}
    grads = [dict() for _ in range(DEPTH)]

    def ride(kernel_name):
        if kernel_name in GATHER_RIDES:
            keys = GATHER_RIDES[kernel_name]
            return _Rider(gathers=[shard[k] for k in keys], done=lambda res: full.update(zip(keys, res)))
        if kernel_name in SCATTER_RIDES:
            keys = SCATTER_RIDES[kernel_name]
            return _Rider(scatters=[grads[l][n].astype(BF16) for (l, n) in keys],
                          done=lambda res: recv.update(zip(keys, res)))
        return None

    first = []
    _exchange(_Rider(gathers=[shard[(0, 'w_in')], conv_rows], done=first.extend), "gather_first")
    full[(0, 'w_in')] = first[0]
    conv_all = first[1].reshape(N_DEV, -1)[:, :int(np.prod(conv_shape))].reshape((N_DEV,) + conv_shape)
    conv_full = conv_all.transpose(1, 0, 2, 3).reshape(DEPTH, N_DEV * conv_shape[1], conv_shape[2])
    weights = [_LayerWeights(l, w_loc, full, conv_full) for l in range(DEPTH)]

    consts = _layer_consts()
    h, saved = x[0], []
    for l in range(DEPTH):
        h, sv, loss_part = _layer_fwd(h, mem[0], weights[l], consts, f"l{l}", ride,
                                      tgt=loss_target[0] if l == DEPTH - 1 else None)
        saved.append(sv)
    dh = h
    for l in reversed(range(DEPTH)):
        dh, _ = _layer_bwd(dh, mem[0], weights[l], consts, saved[l], f"l{l}", ride, grads[l])
    grad_x = dh
    loss = lax.psum(loss_part[0, 0], MESH_AXES)

    small_shapes = [w_loc[n].shape for n in SMALL_NAMES]
    conv_nat = (DEPTH,) + grads[0]['conv_dw_w'].shape[::-1]
    small_send = _pack([jnp.stack([grads[l][n] for l in range(DEPTH)]) for n in SMALL_NAMES]
                       + [jnp.stack([grads[l]['conv_dw_w'].T for l in range(DEPTH)])], F32, SMALL_TILE)
    last = []
    _exchange(_Rider(gathers=[small_send], scatters=[grads[0]['w_in'].astype(BF16)], done=last.extend), "exchange_last")
    recv[(0, 'w_in')] = last[1]
    small_sum = _reduce_slots(last[0].reshape((N_DEV,) + small_send.shape), SMALL_TILE, "reduce_replicated")
    small_g = _unpack(small_sum, small_shapes + [conv_nat])
    conv_cols = w_loc['conv_dw_w'].shape[2]
    conv_g = lax.dynamic_slice_in_dim(small_g[-1], me * conv_cols, conv_cols, axis=2)

    result = [{}, {}, {}, {}]
    for n in DIRECT_NAMES:
        parts = [recv[(l, n)] for l in range(DEPTH)]
        if SHARD_AXIS[n] == 2:
            parts = [_reduce_slots(p, _tile_rows(p.shape[1]), f"reduce_{n}_l{l}").T[None] for l, p in enumerate(parts)]
        outs = _adamw_layers(parts, w_loc[n], m_loc[n], v_loc[n], _tile_rows(w_loc[n].shape[1]), f"adamw_{n}")
        for kind in range(4):
            result[kind][n] = outs[kind]
    packed_names = SMALL_NAMES + ['conv_dw_w']
    pk = lambda d: _pack([d[n] for n in packed_names], F32, SMALL_TILE)
    outs = _adamw(_pack(small_g[:-1] + [conv_g], F32, SMALL_TILE)[None], pk(w_loc), pk(m_loc), pk(v_loc), SMALL_TILE,
                  "adamw_packed")
    for kind in range(4):
        for n, a in zip(packed_names, _unpack(outs[kind], [w_loc[n].shape for n in packed_names])):
            result[kind][n] = a
    return (loss, grad_x[None], *[result[0][n] for n in WEIGHT_NAMES], *[result[1][n] for n in WEIGHT_NAMES],
            *[result[2][n] for n in WEIGHT_NAMES], *[result[3][n] for n in WEIGHT_NAMES])
```
